```python
import math
import jax
import jax.numpy as jnp
from jax import lax
import numpy as np

D_MODEL = 2048
BATCH = 2
SEQ = 4096
DEPTH = 2
DEC_BATCH = 8
DEC_SEQ = 8
PAST_LEN = 16384
PAGE_SIZE = 128

F32 = jnp.float32
EPS = 1e-6
CONV_WIDTH = 4
BLOCK = 128
SSD_HEADS = 16
SSD_HEAD_DIM = 64
SSD_INNER = SSD_HEADS * SSD_HEAD_DIM
SSD_GROUPS = 2
SSD_HPG = SSD_HEADS // SSD_GROUPS
SSD_STATE = 128
SSD_CONV_DIM = SSD_INNER + 2 * SSD_GROUPS * SSD_STATE
LRU_WIDTH = 1024
LRU_BLOCKS = 8
LRU_BLOCK_W = LRU_WIDTH // LRU_BLOCKS
LRU_C = 8.0
ATTN_GROUPS = ((128, 1), (512, 4), (2048, 16))
HEADS_PER_GROUP = 4
ATTN_HEADS = HEADS_PER_GROUP * len(ATTN_GROUPS)
ATTN_HEAD_DIM = 128
ATTN_WIDTH = ATTN_HEADS * ATTN_HEAD_DIM
ATTN_OUT = HEADS_PER_GROUP * ATTN_HEAD_DIM
T5_BUCKETS = 32
T5_MAX_DIST = 2048
N_EXPERT_GROUPS = 4
EXPERTS_PER_GROUP = 4
N_EXPERTS = N_EXPERT_GROUPS * EXPERTS_PER_GROUP
D_EXPERT = 512
TOP_K = 2
IN_PARTS = (SSD_INNER, SSD_CONV_DIM, SSD_HEADS, LRU_WIDTH, LRU_WIDTH, ATTN_WIDTH, ATTN_WIDTH, ATTN_WIDTH)
IN_WIDTH = sum(IN_PARTS)

kernel_name = 'hybrid_ssd_rglru_dilated_hmoe_step'


def rmsnorm(x, w):
    x32 = x.astype(F32)
    y = x32 * lax.rsqrt(jnp.mean(x32 * x32, axis=-1, keepdims=True) + EPS)
    return (y * w.astype(F32)).astype(x.dtype)


def causal_conv(x, buf, w, b):
    xp = jnp.concatenate([buf.astype(x.dtype), x], axis=1)
    y = lax.conv_general_dilated(xp, w[:, None, :].astype(x.dtype), window_strides=(1,), padding='VALID',
                                 dimension_numbers=('NWC', 'WIO', 'NWC'), feature_group_count=x.shape[-1])
    return y + b.astype(x.dtype), xp[:, xp.shape[1] - (CONV_WIDTH - 1):]


def ssd_chunked(x, dt, a, bm, cm, h0):
    bsz, t = x.shape[:2]
    cl = BLOCK if t % BLOCK == 0 else t
    nc = t // cl
    x = x.reshape(bsz, nc, cl, SSD_GROUPS, SSD_HPG, SSD_HEAD_DIM)
    dt = dt.reshape(bsz, nc, cl, SSD_GROUPS, SSD_HPG)
    bm = bm.reshape(bsz, nc, cl, SSD_GROUPS, SSD_STATE)
    cm = cm.reshape(bsz, nc, cl, SSD_GROUPS, SSD_STATE)
    acs = jnp.cumsum(dt * a, axis=2)
    xdt = x * dt[..., None]
    causal = np.tril(np.ones((cl, cl), dtype=bool))[:, :, None, None]
    seg = acs[:, :, :, None] - acs[:, :, None, :]
    decay = jnp.exp(jnp.where(causal, seg, -jnp.inf))
    cb = jnp.einsum('bclgn,bcsgn->bclsg', cm, bm)
    y_diag = jnp.einsum('bclsgh,bcsghp->bclghp', cb[..., None] * decay, xdt)
    to_end = jnp.exp(acs[:, :, -1:] - acs)
    states = jnp.einsum('bclgn,bclghp->bcghpn', bm, xdt * to_end[..., None])
    chunk_decay = jnp.exp(acs[:, :, -1])

    def step(h, inp):
        s_c, d_c = inp
        return d_c[..., None, None] * h + s_c, h

    h_final, h_in = lax.scan(step, h0, (jnp.moveaxis(states, 1, 0), jnp.moveaxis(chunk_decay, 1, 0)))
    h_in = jnp.moveaxis(h_in, 0, 1)
    y_off = jnp.einsum('bclgn,bcghpn->bclghp', cm, h_in) * jnp.exp(acs)[..., None]
    y = (y_diag + y_off).reshape(bsz, t, SSD_GROUPS, SSD_HPG, SSD_HEAD_DIM)
    return y, h_final


def ssd_branch(z, xbc, dt_raw, conv_buf, h0, conv_w, conv_b, dt_bias, a_log, d_skip, norm_w):
    bsz, t, _ = z.shape
    xbc, new_buf = causal_conv(xbc, conv_buf, conv_w, conv_b)
    xbc = jax.nn.silu(xbc.astype(F32))
    xs, bm, cm = jnp.split(xbc, [SSD_INNER, SSD_INNER + SSD_GROUPS * SSD_STATE], axis=-1)
    xs = xs.reshape(bsz, t, SSD_GROUPS, SSD_HPG, SSD_HEAD_DIM)
    bm = bm.reshape(bsz, t, SSD_GROUPS, SSD_STATE)
    cm = cm.reshape(bsz, t, SSD_GROUPS, SSD_STATE)
    dt = jax.nn.softplus(dt_raw.astype(F32) + dt_bias.astype(F32)).reshape(bsz, t, SSD_GROUPS, SSD_HPG)
    a = -jnp.exp(a_log.astype(F32)).reshape(SSD_GROUPS, SSD_HPG)
    h0 = h0.astype(F32).reshape(bsz, SSD_GROUPS, SSD_HPG, SSD_HEAD_DIM, SSD_STATE)
    y, h_final = ssd_chunked(xs, dt, a, bm, cm, h0)
    y = y + d_skip.astype(F32).reshape(SSD_GROUPS, SSD_HPG, 1) * xs
    gw = SSD_INNER // SSD_GROUPS
    yg = y.reshape(bsz, t, SSD_GROUPS, gw) * jax.nn.silu(z.astype(F32).reshape(bsz, t, SSD_GROUPS, gw))
    yg = yg * lax.rsqrt(jnp.mean(yg * yg, axis=-1, keepdims=True) + EPS) * norm_w.astype(F32).reshape(SSD_GROUPS, gw)
    h_final = h_final.reshape(bsz, SSD_HEADS, SSD_HEAD_DIM, SSD_STATE)
    return yg.reshape(bsz, t, SSD_INNER).astype(z.dtype), new_buf, h_final.astype(z.dtype)


def lru_branch(xr, gate, conv_buf, h0, conv_w, conv_b, wr, br, wi, bi, lam):
    xr, new_buf = causal_conv(xr, conv_buf, conv_w, conv_b)
    bsz, t, _ = xr.shape
    x32 = xr.astype(F32)
    xb = x32.reshape(bsz, t, LRU_BLOCKS, LRU_BLOCK_W)
    r_gate = jax.nn.sigmoid(jnp.einsum('btnk,nkj->btnj', xb, wr.astype(F32)).reshape(bsz, t, LRU_WIDTH) + br.astype(F32))
    i_gate = jax.nn.sigmoid(jnp.einsum('btnk,nkj->btnj', xb, wi.astype(F32)).reshape(bsz, t, LRU_WIDTH) + bi.astype(F32))
    log_a = -LRU_C * r_gate * jax.nn.softplus(-lam.astype(F32))
    a = jnp.exp(log_a)
    b = jnp.sqrt(-jnp.expm1(2.0 * log_a)) * (i_gate * x32)
    b = b.at[:, 0].add(a[:, 0] * h0.astype(F32))

    def combine(lhs, rhs):
        a1, b1 = lhs
        a2, b2 = rhs
        return a1 * a2, a2 * b1 + b2

    _, h = lax.associative_scan(combine, (a, b), axis=1)
    y = h * jax.nn.gelu(gate.astype(F32))
    return y.astype(xr.dtype), new_buf, h[:, -1].astype(xr.dtype)


def t5_buckets(dist):
    max_exact = T5_BUCKETS // 2
    large = max_exact + (np.log(np.maximum(dist, 1) / max_exact) / np.log(T5_MAX_DIST / max_exact)
                         * (T5_BUCKETS - max_exact)).astype(np.int32)
    large = np.minimum(large, T5_BUCKETS - 1)
    return np.where(dist < max_exact, dist, large).astype(np.int32)


def dilated_attn(q, k_all, v_all, q_start, key_start, dil, bias):
    bsz, t, nh, hd = q.shape
    tk = k_all.shape[1]
    nk = bias.shape[-1]
    qb = BLOCK if t % BLOCK == 0 else t
    nb = t // qb
    offs = np.arange(nk) * dil
    scale = hd ** -0.5
    bias32 = bias.astype(F32)

    def block(i):
        qi = lax.dynamic_slice_in_dim(q, i * qb, qb, axis=1)
        qpos = q_start + i * qb + jnp.arange(qb)
        kpos = qpos[:, None] - offs[None, :]
        idx = kpos - key_start
        valid = (kpos >= 0) & (idx >= 0)
        idx = jnp.clip(idx, 0, tk - 1)
        kg = jnp.take(k_all, idx, axis=1)
        vg = jnp.take(v_all, idx, axis=1)
        s = jnp.einsum('bqhd,bqjhd->bhqj', qi, kg).astype(F32) * scale + bias32[None, :, None, :]
        s = jnp.where(valid[None, None], s, -jnp.inf)
        m = jnp.max(s, axis=-1, keepdims=True)
        p = jnp.exp(s - m)
        l = jnp.sum(p, axis=-1, keepdims=True)
        o = jnp.einsum('bhqj,bqjhd->bqhd', (p / l).astype(v_all.dtype), vg)
        lse = jnp.transpose((m + jnp.log(l))[..., 0], (0, 2, 1))
        return o, lse

    o, lse = lax.map(block, jnp.arange(nb))
    o = jnp.moveaxis(o, 0, 1).reshape(bsz, t, nh, hd)
    lse = jnp.moveaxis(lse, 0, 1).reshape(bsz, t, nh)
    return o, lse


def attn_branch(q, k, v, kv_bufs, q_start, t5):
    bsz, t = q.shape[:2]
    outs, lses, rows = [], [], []
    for g, (window, dil) in enumerate(ATTN_GROUPS):
        hs = slice(g * HEADS_PER_GROUP, (g + 1) * HEADS_PER_GROUP)
        kg, vg = k[:, :, hs], v[:, :, hs]
        buf = kv_bufs[g].astype(k.dtype)
        k_all = jnp.concatenate([buf[:, :, 0], kg], axis=1)
        v_all = jnp.concatenate([buf[:, :, 1], vg], axis=1)
        nk = window // dil + 1
        bias = t5[t5_buckets(np.arange(nk) * dil)][:, hs].T
        o, lse = dilated_attn(q[:, :, hs], k_all, v_all, q_start, q_start - buf.shape[1], dil, bias)
        outs.append(o)
        lses.append(lse)
        n_rows = min(window, t)
        rows.append(jnp.stack([kg, vg], axis=2)[:, t - n_rows:])
    wts = jax.nn.softmax(jnp.stack(lses, axis=2), axis=2)
    o = jnp.sum(jnp.stack(outs, axis=2).astype(F32) * wts[..., None], axis=2)
    return o.reshape(bsz, t, ATTN_OUT).astype(q.dtype), rows


def moe(h, w_rg, b_rg, w_re, b_re, w1, w3, w2):
    shp = h.shape
    tok = h.reshape(-1, shp[-1])
    n = tok.shape[0]
    gl = (tok @ w_rg).astype(F32) + b_rg.astype(F32)
    gp = jax.nn.softmax(gl, axis=-1)
    gsel = jnp.argmax(gl, axis=-1)
    el = ((tok @ w_re).astype(F32) + b_re.astype(F32)).reshape(n, N_EXPERT_GROUPS, EXPERTS_PER_GROUP)
    el = jnp.take_along_axis(el, gsel[:, None, None], axis=1)[:, 0]
    tv, ti = lax.top_k(el, TOP_K)
    wts = jax.nn.softmax(tv, axis=-1) * jnp.take_along_axis(gp, gsel[:, None], axis=1)
    eid = gsel[:, None] * EXPERTS_PER_GROUP + ti
    comb = jnp.sum(jax.nn.one_hot(eid, N_EXPERTS, dtype=F32) * wts[..., None], axis=1)
    a = jnp.einsum('nd,edf->nef', tok, w1)
    b = jnp.einsum('nd,edf->nef', tok, w3)
    act = jax.nn.silu(a) * b * comb[..., None].astype(tok.dtype)
    y = jnp.einsum('nef,efd->nd', act, w2)
    return y.reshape(shp)


def layer(x, q_start, conv_ssd, st_ssd, conv_lru, st_lru, kv_bufs, t5, lp):
    bsz, t, _ = x.shape
    h = rmsnorm(x, lp['norm1'])
    proj = h @ lp['w_in']
    z, xbc, dt_raw, xr, gr, q, k, v = jnp.split(proj, list(np.cumsum(IN_PARTS)[:-1]), axis=-1)
    y_ssd, conv_ssd, st_ssd = ssd_branch(z, xbc, dt_raw, conv_ssd, st_ssd, lp['conv_ssd_w'], lp['conv_ssd_b'],
                                         lp['ssd_dt_bias'], lp['ssd_a_log'], lp['ssd_d'], lp['ssd_norm_w'])
    y_lru, conv_lru, st_lru = lru_branch(xr, gr, conv_lru, st_lru, lp['conv_lru_w'], lp['conv_lru_b'],
                                         lp['lru_wr'], lp['lru_br'], lp['lru_wi'], lp['lru_bi'], lp['lru_lambda'])
    shp = (bsz, t, ATTN_HEADS, ATTN_HEAD_DIM)
    y_attn, kv_rows = attn_branch(q.reshape(shp), k.reshape(shp), v.reshape(shp), kv_bufs, q_start, t5)
    gates = jax.nn.sigmoid((h @ lp['w_gate']).astype(F32) + lp['b_gate'].astype(F32)).astype(x.dtype)
    g_ssd, g_lru, g_attn = jnp.split(gates, 3, axis=-1)
    mixed = (g_ssd * (y_ssd @ lp['w_br_ssd']) + g_lru * (y_lru @ lp['w_br_lru'])
             + g_attn * (y_attn @ lp['w_br_attn']))
    x = x + mixed @ lp['w_o']
    x = x + moe(rmsnorm(x, lp['norm2']), lp['w_rg'], lp['b_rg'], lp['w_re'], lp['b_re'], lp['w1'], lp['w3'], lp['w2'])
    return x, (conv_ssd, st_ssd, conv_lru, st_lru, kv_rows[0], kv_rows[1], kv_rows[2])


def setup_inputs(seed: int = 0) -> dict:
    key = jax.random.key(seed)
    ks = iter(jax.random.split(key, 64))
    D, L = D_MODEL, DEPTH

    def nrm(shape, scale):
        return jax.random.normal(next(ks), shape, F32) * scale

    def gain(shape):
        return 1.0 + nrm(shape, 0.05)

    x_prompt = nrm((BATCH, SEQ, D), 1.0)
    x_sample = nrm((DEC_BATCH, DEC_SEQ, D), 1.0)
    cache_conv_ssd = nrm((L, DEC_BATCH, CONV_WIDTH - 1, SSD_CONV_DIM), 1.0)
    state_ssd = nrm((L, DEC_BATCH, SSD_HEADS, SSD_HEAD_DIM, SSD_STATE), 0.1)
    cache_conv_lru = nrm((L, DEC_BATCH, CONV_WIDTH - 1, LRU_WIDTH), 1.0)
    state_lru = nrm((L, DEC_BATCH, LRU_WIDTH), 0.5)
    cache_kv_w128 = nrm((L, DEC_BATCH, min(ATTN_GROUPS[0][0], PAST_LEN), 2, HEADS_PER_GROUP, ATTN_HEAD_DIM), 1.0)
    cache_kv_w512 = nrm((L, DEC_BATCH, min(ATTN_GROUPS[1][0], PAST_LEN), 2, HEADS_PER_GROUP, ATTN_HEAD_DIM), 1.0)
    cache_kv_w2048 = nrm((L, DEC_BATCH, min(ATTN_GROUPS[2][0], PAST_LEN), 2, HEADS_PER_GROUP, ATTN_HEAD_DIM), 1.0)
    norm1 = gain((L, D))
    w_in = nrm((L, D, IN_WIDTH), D ** -0.5)
    conv_ssd_w = nrm((L, CONV_WIDTH, SSD_CONV_DIM), CONV_WIDTH ** -0.5)
    conv_ssd_b = nrm((L, SSD_CONV_DIM), 0.02)
    dt = jnp.exp(jax.random.uniform(next(ks), (L, SSD_HEADS), F32, math.log(1e-3), math.log(1e-1)))
    ssd_dt_bias = dt + jnp.log(-jnp.expm1(-dt))
    ssd_a_log = jnp.log(jax.random.uniform(next(ks), (L, SSD_HEADS), F32, 1.0, 16.0))
    ssd_d = gain((L, SSD_HEADS))
    ssd_norm_w = gain((L, SSD_INNER))
    conv_lru_w = nrm((L, CONV_WIDTH, LRU_WIDTH), CONV_WIDTH ** -0.5)
    conv_lru_b = nrm((L, LRU_WIDTH), 0.02)
    lru_wr = nrm((L, LRU_BLOCKS, LRU_BLOCK_W, LRU_BLOCK_W), LRU_BLOCK_W ** -0.5)
    lru_br = nrm((L, LRU_WIDTH), 0.02)
    lru_wi = nrm((L, LRU_BLOCKS, LRU_BLOCK_W, LRU_BLOCK_W), LRU_BLOCK_W ** -0.5)
    lru_bi = nrm((L, LRU_WIDTH), 0.02)
    a0 = jax.random.uniform(next(ks), (L, LRU_WIDTH), F32, 0.9, 0.999) ** (1.0 / LRU_C)
    lru_lambda = jnp.log(a0) - jnp.log1p(-a0)
    t5_bias = nrm((T5_BUCKETS, ATTN_HEADS), 0.5)
    w_br_ssd = nrm((L, SSD_INNER, D), SSD_INNER ** -0.5)
    w_br_lru = nrm((L, LRU_WIDTH, D), LRU_WIDTH ** -0.5)
    w_br_attn = nrm((L, ATTN_OUT, D), ATTN_OUT ** -0.5)
    w_gate = nrm((L, D, 3 * D), D ** -0.5)
    b_gate = nrm((L, 3 * D), 0.02)
    w_o = nrm((L, D, D), D ** -0.5)
    norm2 = gain((L, D))
    w_router_group = nrm((L, D, N_EXPERT_GROUPS), D ** -0.5)
    b_router_group = nrm((L, N_EXPERT_GROUPS), 0.01)
    w_router_expert = nrm((L, D, N_EXPERTS), D ** -0.5)
    b_router_expert = nrm((L, N_EXPERTS), 0.01)
    w1 = nrm((L, N_EXPERTS, D, D_EXPERT), D ** -0.5)
    w3 = nrm((L, N_EXPERTS, D, D_EXPERT), D ** -0.5)
    w2 = nrm((L, N_EXPERTS, D_EXPERT, D), D_EXPERT ** -0.5)
    final_norm = gain((D,))
    return {'x_prompt': x_prompt, 'x_sample': x_sample,
            'cache_conv_ssd': cache_conv_ssd, 'state_ssd': state_ssd,
            'cache_conv_lru': cache_conv_lru, 'state_lru': state_lru,
            'cache_kv_w128': cache_kv_w128, 'cache_kv_w512': cache_kv_w512, 'cache_kv_w2048': cache_kv_w2048,
            'norm1': norm1, 'w_in': w_in, 'conv_ssd_w': conv_ssd_w, 'conv_ssd_b': conv_ssd_b,
            'ssd_dt_bias': ssd_dt_bias, 'ssd_a_log': ssd_a_log, 'ssd_d': ssd_d, 'ssd_norm_w': ssd_norm_w,
            'conv_lru_w': conv_lru_w, 'conv_lru_b': conv_lru_b, 'lru_wr': lru_wr, 'lru_br': lru_br,
            'lru_wi': lru_wi, 'lru_bi': lru_bi, 'lru_lambda': lru_lambda, 't5_bias': t5_bias,
            'w_br_ssd': w_br_ssd, 'w_br_lru': w_br_lru, 'w_br_attn': w_br_attn, 'w_gate': w_gate, 'b_gate': b_gate,
            'w_o': w_o, 'norm2': norm2, 'w_router_group': w_router_group, 'b_router_group': b_router_group,
            'w_router_expert': w_router_expert, 'b_router_expert': b_router_expert,
            'w1': w1, 'w3': w3, 'w2': w2, 'final_norm': final_norm}


def reference(x_prompt, x_sample, cache_conv_ssd, state_ssd, cache_conv_lru, state_lru,
              cache_kv_w128, cache_kv_w512, cache_kv_w2048,
              norm1, w_in, conv_ssd_w, conv_ssd_b, ssd_dt_bias, ssd_a_log, ssd_d, ssd_norm_w,
              conv_lru_w, conv_lru_b, lru_wr, lru_br, lru_wi, lru_bi, lru_lambda, t5_bias,
              w_br_ssd, w_br_lru, w_br_attn, w_gate, b_gate, w_o, norm2,
              w_router_group, b_router_group, w_router_expert, b_router_expert, w1, w3, w2, final_norm):
    xp, xs = x_prompt, x_sample
    bp = xp.shape[0]
    dtp = xp.dtype
    outs_p, outs_s = [], []
    for l in range(DEPTH):
        lp = {'norm1': norm1[l], 'w_in': w_in[l], 'conv_ssd_w': conv_ssd_w[l], 'conv_ssd_b': conv_ssd_b[l],
              'ssd_dt_bias': ssd_dt_bias[l], 'ssd_a_log': ssd_a_log[l], 'ssd_d': ssd_d[l], 'ssd_norm_w': ssd_norm_w[l],
              'conv_lru_w': conv_lru_w[l], 'conv_lru_b': conv_lru_b[l], 'lru_wr': lru_wr[l], 'lru_br': lru_br[l],
              'lru_wi': lru_wi[l], 'lru_bi': lru_bi[l], 'lru_lambda': lru_lambda[l],
              'w_br_ssd': w_br_ssd[l], 'w_br_lru': w_br_lru[l], 'w_br_attn': w_br_attn[l],
              'w_gate': w_gate[l], 'b_gate': b_gate[l], 'w_o': w_o[l], 'norm2': norm2[l],
              'w_rg': w_router_group[l], 'b_rg': b_router_group[l], 'w_re': w_router_expert[l],
              'b_re': b_router_expert[l], 'w1': w1[l], 'w3': w3[l], 'w2': w2[l]}
        empty_kv = jnp.zeros((bp, 0, 2, HEADS_PER_GROUP, ATTN_HEAD_DIM), dtp)
        xp, sp = layer(xp, 0,
                       jnp.zeros((bp, CONV_WIDTH - 1, SSD_CONV_DIM), dtp),
                       jnp.zeros((bp, SSD_HEADS, SSD_HEAD_DIM, SSD_STATE), dtp),
                       jnp.zeros((bp, CONV_WIDTH - 1, LRU_WIDTH), dtp),
                       jnp.zeros((bp, LRU_WIDTH), dtp),
                       (empty_kv, empty_kv, empty_kv), t5_bias, lp)
        xs, ss = layer(xs, PAST_LEN, cache_conv_ssd[l], state_ssd[l], cache_conv_lru[l], state_lru[l],
                       (cache_kv_w128[l], cache_kv_w512[l], cache_kv_w2048[l]), t5_bias, lp)
        outs_p.append(ss if False else sp)
        outs_s.append(ss)
    y_prompt = rmsnorm(xp, final_norm)
    y_sample = rmsnorm(xs, final_norm)

    def stk(outs, i):
        return jnp.stack([o[i] for o in outs], axis=0)

    return (y_prompt, y_sample,
            stk(outs_p, 0), stk(outs_p, 1), stk(outs_p, 2), stk(outs_p, 3), stk(outs_p, 4), stk(outs_p, 5), stk(outs_p, 6),
            stk(outs_s, 0), stk(outs_s, 1), stk(outs_s, 2), stk(outs_s, 3), stk(outs_s, 4), stk(outs_s, 5), stk(outs_s, 6))
```

```python
import functools

import numpy as np
import jax
import jax.numpy as jnp
from jax import lax
from jax.experimental import pallas as pl
from jax.experimental.pallas import tpu as pltpu

F32 = jnp.float32
BF16 = jnp.bfloat16
EPS = 1e-6
NEG = -1e30

D = 2048
DEPTH = 2
PAST_LEN = 16384
CL = 128
CONV_W = 4
SSD_HEADS = 16
SSD_HEAD_DIM = 64
SSD_INNER = 1024
SSD_STATE = 128
SSD_CONV_DIM = 1536
LRU_W = 1024
LRU_BLOCKS = 8
LRU_C = 8.0
ATTN_GROUPS = ((128, 1), (512, 4), (2048, 16))
HPG = 4
HD = 128
GW = HPG * HD
T5_BUCKETS = 32
T5_MAX_DIST = 2048
N_GROUPS = 4
PER_GROUP = 4
N_EXPERTS = 16
D_EXPERT = 512

TILE = 512
C_GATE = 0
C_Z = 6144
C_XR = 7168
C_GR = 8192
C_XBC = 9216
C_Q = 10752
C_K = 12288
C_V = 13824
C_DT = 15360
PW = 15872
N_GATE_TILES = (3 * D) // TILE
ROUTE_LANES = 128
VMEM_LIMIT = 56 * 1024 * 1024


def _cparams(sem):
    return pltpu.CompilerParams(dimension_semantics=sem, vmem_limit_bytes=VMEM_LIMIT)


def _full(shape):
    nd = len(shape)
    return pl.BlockSpec(shape, lambda *_: (0,) * nd)


def _proj_kernel(x_ref, nw_ref, w_ref, b_ref, o_ref, h_ref):
    j = pl.program_id(1)

    @pl.when(j == 0)
    def _():
        x = x_ref[...]
        ms = jnp.mean(x * x, axis=-1, keepdims=True)
        h_ref[...] = (x * lax.rsqrt(ms + EPS) * nw_ref[...]).astype(BF16)

    acc = jnp.dot(h_ref[...], w_ref[...], preferred_element_type=F32) + b_ref[...]

    @pl.when(j < N_GATE_TILES)
    def _():
        o_ref[...] = jax.nn.sigmoid(acc)

    @pl.when(j >= N_GATE_TILES)
    def _():
        o_ref[...] = acc


def _proj(x, nw, w_all, b_all, tm):
    n = x.shape[0]
    return pl.pallas_call(
        _proj_kernel,
        grid=(n // tm, PW // TILE),
        in_specs=[pl.BlockSpec((tm, D), lambda i, j: (i, 0)),
                  pl.BlockSpec((1, D), lambda i, j: (0, 0)),
                  pl.BlockSpec((D, TILE), lambda i, j: (0, j)),
                  pl.BlockSpec((1, TILE), lambda i, j: (0, j))],
        out_specs=pl.BlockSpec((tm, TILE), lambda i, j: (i, j)),
        out_shape=jax.ShapeDtypeStruct((n, PW), F32),
        scratch_shapes=[pltpu.VMEM((tm, D), BF16)],
        compiler_params=_cparams(("parallel", "arbitrary")),
        name="proj",
    )(x, nw, w_all, b_all)


def _conv_step(x_ref, xp_ref, cw_ref, cb_ref, rows, out_rows):
    xp_ref[8:8 + rows, :] = x_ref[...]
    acc = cb_ref[...] + cw_ref[0:1, :] * xp_ref[5:5 + out_rows, :]
    for j in range(1, CONV_W):
        acc = acc + cw_ref[j:j + 1, :] * xp_ref[5 + j:5 + j + out_rows, :]
    tail = xp_ref[rows:rows + 8, :]
    xp_ref[0:8, :] = tail
    return acc


def _softplus(x):
    return jnp.maximum(x, 0.0) + jnp.log1p(jnp.exp(-jnp.abs(x)))


def _ssd_kernel(z_ref, xbc_ref, dt_ref, tail_ref, h0_ref, cw_ref, cb_ref, dtb_ref, alog_ref, dsk_ref, nw_ref,
                y_ref, hf_ref, xp_ref, act_ref, st_ref, ysc_ref, *, rows, n_chunks):
    c = pl.program_id(1)

    @pl.when(c == 0)
    def _():
        xp_ref[0:8, :] = tail_ref[0]
        st_ref[...] = h0_ref[0]

    if rows < CL:
        xp_ref[8 + rows:, :] = jnp.zeros((CL - rows, SSD_CONV_DIM), F32)
    conv = _conv_step(xbc_ref, xp_ref, cw_ref, cb_ref, rows, CL)
    act_ref[...] = conv * jax.nn.sigmoid(conv)

    row = lax.broadcasted_iota(jnp.int32, (CL, 128), 0)
    lane = lax.broadcasted_iota(jnp.int32, (CL, 128), 1)
    raw = dt_ref[...]
    if rows < CL:
        raw = jnp.concatenate([raw, jnp.zeros((CL - rows, 128), F32)], axis=0)
    dt = _softplus(raw + dtb_ref[...])
    dt = jnp.where((lane < SSD_HEADS) & (row < rows), dt, 0.0)
    da = dt * (-jnp.exp(alog_ref[...]))
    acs = da
    d = 1
    while d < CL:
        acs = acs + jnp.where(row >= d, pltpu.roll(acs, d, 0), 0.0)
        d *= 2
    acs_t = acs.T
    last = acs[CL - 1:CL, :]
    e_acs = jnp.exp(acs)
    to_end = jnp.exp(last - acs)
    cdec = jnp.exp(last)
    causal = row >= lane
    lo_lane = lane < SSD_HEAD_DIM
    lo_row = row < SSD_HEAD_DIM
    dsk = dsk_ref[...]

    def pair_cols(arr, h):
        return jnp.where(lo_lane, arr[:, h:h + 1], arr[:, h + 1:h + 2])

    nt = (((1,), (1,)), ((), ()))
    for g in range(2):
        bm = act_ref[:, SSD_INNER + g * SSD_STATE:SSD_INNER + (g + 1) * SSD_STATE].astype(BF16)
        cm = act_ref[:, SSD_INNER + 256 + g * SSD_STATE:SSD_INNER + 256 + (g + 1) * SSD_STATE].astype(BF16)
        cb = lax.dot_general(cm, bm, nt, preferred_element_type=F32)
        for pp in range(4):
            h = g * 8 + 2 * pp
            sl = slice(h * SSD_HEAD_DIM, h * SSD_HEAD_DIM + 128)
            xs = act_ref[:, sl]
            xdt = xs * pair_cols(dt, h)
            xdt_b = xdt.astype(BF16)
            ys = []
            for hh in (h, h + 1):
                seg = acs[:, hh:hh + 1] - acs_t[hh:hh + 1, :]
                decay = jnp.exp(jnp.where(causal, seg, -jnp.inf))
                ys.append(jnp.dot((cb * decay).astype(BF16), xdt_b, preferred_element_type=F32))
            y_diag = jnp.where(lo_lane, ys[0], ys[1])
            st = st_ref[sl, :]
            y_off = lax.dot_general(cm, st.astype(BF16), nt, preferred_element_type=F32) * pair_cols(e_acs, h)
            d_pair = jnp.where(lo_lane, dsk[:, h:h + 1], dsk[:, h + 1:h + 2])
            ysc_ref[:, sl] = y_diag + y_off + d_pair * xs
            xdte_t = (xdt * pair_cols(to_end, h)).T.astype(BF16)
            s_new = jnp.dot(xdte_t, bm, preferred_element_type=F32)
            dec = jnp.where(lo_row, cdec[:, h:h + 1], cdec[:, h + 1:h + 2])
            st_ref[sl, :] = dec * st + s_new

    zz = z_ref[...]
    yg = ysc_ref[0:rows, :] * (zz * jax.nn.sigmoid(zz))
    gw = SSD_INNER // 2
    for g in range(2):
        part = yg[:, g * gw:(g + 1) * gw]
        ms = jnp.mean(part * part, axis=-1, keepdims=True)
        y_ref[:, g * gw:(g + 1) * gw] = part * lax.rsqrt(ms + EPS) * nw_ref[:, g * gw:(g + 1) * gw]

    @pl.when(c == n_chunks - 1)
    def _():
        hf_ref[0] = st_ref[...]


def _ssd(P, tail, h0, lw, nb, t, rows):
    nc = t // rows
    kern = functools.partial(_ssd_kernel, rows=rows, n_chunks=nc)
    return pl.pallas_call(
        kern,
        grid=(nb, nc),
        in_specs=[pl.BlockSpec((rows, SSD_INNER), lambda b, c: (b * nc + c, C_Z // SSD_INNER)),
                  pl.BlockSpec((rows, SSD_CONV_DIM), lambda b, c: (b * nc + c, C_XBC // SSD_CONV_DIM)),
                  pl.BlockSpec((rows, 128), lambda b, c: (b * nc + c, C_DT // 128)),
                  pl.BlockSpec((1, 8, SSD_CONV_DIM), lambda b, c: (b, 0, 0)),
                  pl.BlockSpec((1, SSD_INNER, SSD_STATE), lambda b, c: (b, 0, 0)),
                  _full((CONV_W, SSD_CONV_DIM)), _full((1, SSD_CONV_DIM)),
                  _full((1, 128)), _full((1, 128)), _full((1, 128)), _full((1, SSD_INNER))],
        out_specs=[pl.BlockSpec((rows, SSD_INNER), lambda b, c: (b * nc + c, 0)),
                   pl.BlockSpec((1, SSD_INNER, SSD_STATE), lambda b, c: (b, 0, 0))],
        out_shape=[jax.ShapeDtypeStruct((nb * t, SSD_INNER), F32),
                   jax.ShapeDtypeStruct((nb, SSD_INNER, SSD_STATE), F32)],
        scratch_shapes=[pltpu.VMEM((8 + CL, SSD_CONV_DIM), F32),
                        pltpu.VMEM((CL, SSD_CONV_DIM), F32),
                        pltpu.VMEM((SSD_INNER, SSD_STATE), F32),
                        pltpu.VMEM((CL, SSD_INNER), F32)],
        compiler_params=_cparams(("parallel", "arbitrary")),
        name="ssd",
    )(P, P, P, tail, h0, lw['conv_ssd_w'], lw['conv_ssd_b'], lw['dt_bias'], lw['a_log'], lw['d_skip'], lw['ssd_norm_w'])


def _lru_kernel(xr_ref, gr_ref, tail_ref, h0_ref, cw_ref, cb_ref, wr_ref, br_ref, wi_ref, bi_ref, lam_ref,
                y_ref, hl_ref, xp_ref, h_ref, *, rows, n_chunks):
    c = pl.program_id(1)

    @pl.when(c == 0)
    def _():
        xp_ref[0:8, :] = tail_ref[0]
        h_ref[...] = h0_ref[0]

    x = _conv_step(xr_ref, xp_ref, cw_ref, cb_ref, rows, rows)
    xb = x.astype(BF16)
    rs, is_ = [], []
    for n in range(LRU_BLOCKS):
        blk = xb[:, n * 128:(n + 1) * 128]
        rs.append(jnp.dot(blk, wr_ref[n], preferred_element_type=F32))
        is_.append(jnp.dot(blk, wi_ref[n], preferred_element_type=F32))
    r_gate = jax.nn.sigmoid(jnp.concatenate(rs, axis=1) + br_ref[...])
    i_gate = jax.nn.sigmoid(jnp.concatenate(is_, axis=1) + bi_ref[...])
    log_a = -LRU_C * r_gate * _softplus(-lam_ref[...])
    a = jnp.exp(log_a)
    b = jnp.sqrt(1.0 - jnp.exp(2.0 * log_a)) * (i_gate * x)
    row = lax.broadcasted_iota(jnp.int32, (rows, LRU_W), 0)
    d = 1
    while d < rows:
        a_s = jnp.where(row >= d, pltpu.roll(a, d, 0), 1.0)
        b_s = jnp.where(row >= d, pltpu.roll(b, d, 0), 0.0)
        b = a * b_s + b
        a = a * a_s
        d *= 2
    h = b + a * h_ref[...]
    last = h[rows - 1:rows, :]
    h_ref[...] = last
    y_ref[...] = h * jax.nn.gelu(gr_ref[...])

    @pl.when(c == n_chunks - 1)
    def _():
        hl_ref[0] = last


def _lru(P, tail, h0, lw, nb, t, rows):
    nc = t // rows
    kern = functools.partial(_lru_kernel, rows=rows, n_chunks=nc)
    return pl.pallas_call(
        kern,
        grid=(nb, nc),
        in_specs=[pl.BlockSpec((rows, LRU_W), lambda b, c: (b * nc + c, C_XR // LRU_W)),
                  pl.BlockSpec((rows, LRU_W), lambda b, c: (b * nc + c, C_GR // LRU_W)),
                  pl.BlockSpec((1, 8, LRU_W), lambda b, c: (b, 0, 0)),
                  pl.BlockSpec((1, 1, LRU_W), lambda b, c: (b, 0, 0)),
                  _full((CONV_W, LRU_W)), _full((1, LRU_W)),
                  _full((LRU_BLOCKS, 128, 128)), _full((1, LRU_W)),
                  _full((LRU_BLOCKS, 128, 128)), _full((1, LRU_W)), _full((1, LRU_W))],
        out_specs=[pl.BlockSpec((rows, LRU_W), lambda b, c: (b * nc + c, 0)),
                   pl.BlockSpec((1, 1, LRU_W), lambda b, c: (b, 0, 0))],
        out_shape=[jax.ShapeDtypeStruct((nb * t, LRU_W), F32),
                   jax.ShapeDtypeStruct((nb, 1, LRU_W), F32)],
        scratch_shapes=[pltpu.VMEM((8 + rows, LRU_W), F32), pltpu.VMEM((1, LRU_W), F32)],
        compiler_params=_cparams(("parallel", "arbitrary")),
        name="lru",
    )(P, P, tail, h0, lw['conv_lru_w'], lw['conv_lru_b'], lw['lru_wr'], lw['lru_br'], lw['lru_wi'], lw['lru_bi'],
      lw['lru_lambda'])


def _attn_kernel(q_ref, ka_ref, va_ref, kb_ref, vb_ref, ba_ref, bb_ref, o_ref, lse_ref, *, first_a_invalid):
    scale = HD ** -0.5
    nt = (((1,), (1,)), ((), ()))
    rows = q_ref.shape[0]
    for h in range(HPG):
        sl = slice(h * HD, (h + 1) * HD)
        q = q_ref[:, sl].astype(BF16)
        sa = lax.dot_general(q, ka_ref[:, sl].astype(BF16), nt, preferred_element_type=F32) * scale + ba_ref[h]
        if first_a_invalid:
            sa = jnp.where(pl.program_id(2) > 0, sa, NEG)
        sb = lax.dot_general(q, kb_ref[:, sl].astype(BF16), nt, preferred_element_type=F32) * scale + bb_ref[h]
        m = jnp.maximum(jnp.max(sa, axis=-1, keepdims=True), jnp.max(sb, axis=-1, keepdims=True))
        pa = jnp.exp(sa - m)
        pb = jnp.exp(sb - m)
        l = jnp.sum(pa, axis=-1, keepdims=True) + jnp.sum(pb, axis=-1, keepdims=True)
        o = (jnp.dot(pa.astype(BF16), va_ref[:, sl].astype(BF16), preferred_element_type=F32)
             + jnp.dot(pb.astype(BF16), vb_ref[:, sl].astype(BF16), preferred_element_type=F32))
        o_ref[:, sl] = o / l
        lse_ref[:, sl] = jnp.broadcast_to(m + jnp.log(l), (rows, HD))


def _attn_prompt(P, bias_a, bias_b, g, dil, nb, t):
    n = nb * t
    nblk = t // dil // CL
    tiles = PW // GW
    tq, tk, tv = C_Q // GW + g, C_K // GW + g, C_V // GW + g
    Pv = P.reshape(n // dil, dil * PW)

    def cur(tile):
        return pl.BlockSpec((CL, GW), lambda b, p, i: (b * nblk + i, p * tiles + tile))

    def prev(tile):
        return pl.BlockSpec((CL, GW), lambda b, p, i: (b * nblk + jnp.maximum(i - 1, 0), p * tiles + tile))

    out_spec = pl.BlockSpec((CL, GW), lambda b, p, i: (b * nblk + i, p))
    o, lse = pl.pallas_call(
        functools.partial(_attn_kernel, first_a_invalid=True),
        grid=(nb, dil, nblk),
        in_specs=[cur(tq), prev(tk), prev(tv), cur(tk), cur(tv), _full((HPG, CL, CL)), _full((HPG, CL, CL))],
        out_specs=[out_spec, out_spec],
        out_shape=[jax.ShapeDtypeStruct((n // dil, dil * GW), F32)] * 2,
        compiler_params=_cparams(("parallel", "parallel", "arbitrary")),
        name=f"attn_prompt_d{dil}",
    )(Pv, Pv, Pv, Pv, Pv, bias_a, bias_b)
    return o.reshape(n, GW), lse.reshape(n, GW)


def _attn_sample(Ps, cache2d, bias_a, bias_b, g, layer, nb, t, window):
    tq, tk, tv = C_Q // GW + g, C_K // GW + g, C_V // GW + g
    out_spec = pl.BlockSpec((t, GW), lambda b: (b, 0))
    return pl.pallas_call(
        functools.partial(_attn_kernel, first_a_invalid=False),
        grid=(nb,),
        in_specs=[pl.BlockSpec((t, GW), lambda b: (b, tq)),
                  pl.BlockSpec((window, GW), lambda b: (layer * nb + b, 0)),
                  pl.BlockSpec((window, GW), lambda b: (layer * nb + b, 1)),
                  pl.BlockSpec((t, GW), lambda b: (b, tk)),
                  pl.BlockSpec((t, GW), lambda b: (b, tv)),
                  _full((HPG, t, window)), _full((HPG, t, t))],
        out_specs=[out_spec, out_spec],
        out_shape=[jax.ShapeDtypeStruct((nb * t, GW), F32)] * 2,
        compiler_params=_cparams(("parallel",)),
        name=f"attn_sample_w{window}",
    )(Ps, cache2d, cache2d, Ps, Ps, bias_a, bias_b)


def _mix_kernel(gs_ref, gl_ref, ga_ref, ys_ref, yl_ref, o0_ref, o1_ref, o2_ref, l0_ref, l1_ref, l2_ref,
                wbs_ref, wbl_ref, wba_ref, out_ref):
    l0, l1, l2 = l0_ref[...], l1_ref[...], l2_ref[...]
    m = jnp.maximum(jnp.maximum(l0, l1), l2)
    e0, e1, e2 = jnp.exp(l0 - m), jnp.exp(l1 - m), jnp.exp(l2 - m)
    den = e0 + e1 + e2
    ya = o0_ref[...] * (e0 / den) + o1_ref[...] * (e1 / den) + o2_ref[...] * (e2 / den)
    mixed = (gs_ref[...] * jnp.dot(ys_ref[...].astype(BF16), wbs_ref[...], preferred_element_type=F32)
             + gl_ref[...] * jnp.dot(yl_ref[...].astype(BF16), wbl_ref[...], preferred_element_type=F32)
             + ga_ref[...] * jnp.dot(ya.astype(BF16), wba_ref[...], preferred_element_type=F32))
    out_ref[...] = mixed.astype(BF16)


def _mix(P, y_ssd, y_lru, attn, lw, tm):
    n = P.shape[0]
    row = lambda w: pl.BlockSpec((tm, w), lambda i: (i, 0))
    (o0, s0), (o1, s1), (o2, s2) = attn
    return pl.pallas_call(
        _mix_kernel,
        grid=(n // tm,),
        in_specs=[pl.BlockSpec((tm, D), lambda i: (i, 0)), pl.BlockSpec((tm, D), lambda i: (i, 1)),
                  pl.BlockSpec((tm, D), lambda i: (i, 2)),
                  row(SSD_INNER), row(LRU_W), row(GW), row(GW), row(GW), row(GW), row(GW), row(GW),
                  _full((SSD_INNER, D)), _full((LRU_W, D)), _full((GW, D))],
        out_specs=row(D),
        out_shape=jax.ShapeDtypeStruct((n, D), BF16),
        compiler_params=_cparams(("parallel",)),
        name="mix",
    )(P, P, P, y_ssd, y_lru, o0, o1, o2, s0, s1, s2, lw['w_br_ssd'], lw['w_br_lru'], lw['w_br_attn'])


def _res_kernel(x_ref, mixed_ref, wo_ref, n2_ref, wr_ref, br_ref, x1_ref, h2_ref, comb_ref):
    x1 = x_ref[...] + jnp.dot(mixed_ref[...], wo_ref[...], preferred_element_type=F32)
    x1_ref[...] = x1
    ms = jnp.mean(x1 * x1, axis=-1, keepdims=True)
    h2 = x1 * lax.rsqrt(ms + EPS) * n2_ref[...]
    h2_ref[...] = h2.astype(BF16)
    logits = jnp.dot(h2, wr_ref[...], preferred_element_type=F32, precision=lax.Precision.HIGHEST) + br_ref[...]
    lane = lax.broadcasted_iota(jnp.int32, logits.shape, 1).astype(F32)
    big = float(ROUTE_LANES)

    def first_max(vals, ok):
        v = jnp.where(ok, vals, NEG)
        top = jnp.max(v, axis=-1, keepdims=True)
        idx = jnp.min(jnp.where(ok & (v == top), lane, big), axis=-1, keepdims=True)
        return top, idx

    is_g = lane < N_GROUPS
    gmax, gsel = first_max(logits, is_g)
    gp = 1.0 / jnp.sum(jnp.where(is_g, jnp.exp(logits - gmax), 0.0), axis=-1, keepdims=True)
    lo = N_GROUPS + PER_GROUP * gsel
    is_e = (lane >= lo) & (lane < lo + PER_GROUP)
    t1, i1 = first_max(logits, is_e)
    t2, i2 = first_max(logits, is_e & (lane != i1))
    e2 = jnp.exp(t2 - t1)
    w1 = gp / (1.0 + e2)
    w2 = gp * e2 / (1.0 + e2)
    comb_ref[...] = jnp.where(lane == i1, w1, 0.0) + jnp.where(lane == i2, w2, 0.0)


def _res(x, mixed, lw, tm):
    n = x.shape[0]
    return pl.pallas_call(
        _res_kernel,
        grid=(n // tm,),
        in_specs=[pl.BlockSpec((tm, D), lambda i: (i, 0)), pl.BlockSpec((tm, D), lambda i: (i, 0)),
                  _full((D, D)), _full((1, D)), _full((D, ROUTE_LANES)), _full((1, ROUTE_LANES))],
        out_specs=[pl.BlockSpec((tm, D), lambda i: (i, 0)), pl.BlockSpec((tm, D), lambda i: (i, 0)),
                   pl.BlockSpec((tm, ROUTE_LANES), lambda i: (i, 0))],
        out_shape=[jax.ShapeDtypeStruct((n, D), F32), jax.ShapeDtypeStruct((n, D), BF16),
                   jax.ShapeDtypeStruct((n, ROUTE_LANES), F32)],
        compiler_params=_cparams(("parallel",)),
        name="res_router",
    )(x, mixed, lw['w_o'], lw['norm2'], lw['w_router'], lw['b_router'])


def _moe_kernel(h2_ref, comb_ref, x1_ref, w1_ref, w3_ref, w2_ref, fn_ref, o_ref, *, final):
    e = pl.program_id(1)

    @pl.when(e == 0)
    def _():
        o_ref[...] = x1_ref[...]

    h = h2_ref[...]
    a = jnp.dot(h, w1_ref[0], preferred_element_type=F32)
    b = jnp.dot(h, w3_ref[0], preferred_element_type=F32)
    comb = comb_ref[...]
    lane = lax.broadcasted_iota(jnp.int32, comb.shape, 1)
    w = jnp.sum(jnp.where(lane == e + N_GROUPS, comb, 0.0), axis=-1, keepdims=True)
    act = (a * jax.nn.sigmoid(a)) * b * w
    o_ref[...] += jnp.dot(act.astype(BF16), w2_ref[0], preferred_element_type=F32)

    if final:
        @pl.when(e == N_EXPERTS - 1)
        def _():
            x = o_ref[...]
            ms = jnp.mean(x * x, axis=-1, keepdims=True)
            o_ref[...] = x * lax.rsqrt(ms + EPS) * fn_ref[...]


def _moe(h2, comb, x1, lw, final_norm, tm, final):
    n = x1.shape[0]
    return pl.pallas_call(
        functools.partial(_moe_kernel, final=final),
        grid=(n // tm, N_EXPERTS),
        in_specs=[pl.BlockSpec((tm, D), lambda i, e: (i, 0)),
                  pl.BlockSpec((tm, ROUTE_LANES), lambda i, e: (i, 0)),
                  pl.BlockSpec((tm, D), lambda i, e: (i, 0)),
                  pl.BlockSpec((1, D, D_EXPERT), lambda i, e: (e, 0, 0)),
                  pl.BlockSpec((1, D, D_EXPERT), lambda i, e: (e, 0, 0)),
                  pl.BlockSpec((1, D_EXPERT, D), lambda i, e: (e, 0, 0)),
                  pl.BlockSpec((1, D), lambda i, e: (0, 0))],
        out_specs=pl.BlockSpec((tm, D), lambda i, e: (i, 0)),
        out_shape=jax.ShapeDtypeStruct((n, D), F32),
        compiler_params=_cparams(("parallel", "arbitrary")),
        name="moe",
    )(h2, comb, x1, lw['w1'], lw['w3'], lw['w2'], final_norm)


def _t5_buckets(dist):
    max_exact = T5_BUCKETS // 2
    large = max_exact + (np.log(np.maximum(dist, 1) / max_exact) / np.log(T5_MAX_DIST / max_exact)
                         * (T5_BUCKETS - max_exact)).astype(np.int32)
    large = np.minimum(large, T5_BUCKETS - 1)
    return np.where(dist < max_exact, dist, large).astype(np.int32)


def _bias_tables(t5, g, dil, window, t_sample):
    nk = window // dil + 1
    hs = slice(g * HPG, (g + 1) * HPG)
    bias = t5[_t5_buckets(np.arange(nk) * dil)][:, hs].T

    def table(j, ok):
        return jnp.where(jnp.asarray(ok)[None], bias[:, np.clip(j, 0, nk - 1)], NEG)

    r = np.arange(CL)[:, None]
    c = np.arange(CL)[None, :]
    prev_t = table(CL + r - c, c >= r)
    cur_t = table(r - c, c <= r)
    r = np.arange(t_sample)[:, None]
    m = np.arange(window)[None, :]
    num = window + r - m
    cache_t = table(num // dil, (num % dil == 0) & (num // dil <= nk - 1))
    c = np.arange(t_sample)[None, :]
    new_t = table((r - c) // dil, ((r - c) % dil == 0) & (r >= c))
    return prev_t, cur_t, cache_t, new_t


def _layer_weights(l, norm1, w_in, conv_ssd_w, conv_ssd_b, ssd_dt_bias, ssd_a_log, ssd_d, ssd_norm_w,
                   conv_lru_w, conv_lru_b, lru_wr, lru_br, lru_wi, lru_bi, lru_lambda,
                   w_br_ssd, w_br_lru, w_br_attn, w_gate, b_gate, w_o, norm2,
                   w_router_group, b_router_group, w_router_expert, b_router_expert, w1, w3, w2):
    wi = w_in[l]
    w_all = jnp.concatenate(
        [w_gate[l], wi[:, 0:1024], wi[:, 2576:3600], wi[:, 3600:4624], wi[:, 1024:2560], wi[:, 4624:6160],
         wi[:, 6160:7696], wi[:, 7696:9232], wi[:, 2560:2576], jnp.zeros((D, TILE - SSD_HEADS), F32)],
        axis=1).astype(BF16)
    b_all = jnp.concatenate([b_gate[l], jnp.zeros((PW - 3 * D,), F32)])[None]

    def pad128(v):
        return jnp.concatenate([v, jnp.zeros((128 - v.shape[0],), F32)])[None]

    return {
        'norm1': norm1[l][None], 'w_all': w_all, 'b_all': b_all,
        'conv_ssd_w': conv_ssd_w[l], 'conv_ssd_b': conv_ssd_b[l][None],
        'dt_bias': pad128(ssd_dt_bias[l]), 'a_log': pad128(ssd_a_log[l]), 'd_skip': pad128(ssd_d[l]),
        'ssd_norm_w': ssd_norm_w[l][None],
        'conv_lru_w': conv_lru_w[l], 'conv_lru_b': conv_lru_b[l][None],
        'lru_wr': lru_wr[l].astype(BF16), 'lru_br': lru_br[l][None],
        'lru_wi': lru_wi[l].astype(BF16), 'lru_bi': lru_bi[l][None], 'lru_lambda': lru_lambda[l][None],
        'w_br_ssd': w_br_ssd[l].astype(BF16), 'w_br_lru': w_br_lru[l].astype(BF16),
        'w_br_attn': w_br_attn[l].astype(BF16), 'w_o': w_o[l].astype(BF16), 'norm2': norm2[l][None],
        'w_router': jnp.concatenate([w_router_group[l], w_router_expert[l],
                                     jnp.zeros((D, ROUTE_LANES - N_GROUPS - N_EXPERTS), F32)], axis=1),
        'b_router': pad128(jnp.concatenate([b_router_group[l], b_router_expert[l]])),
        'w1': w1[l].astype(BF16), 'w3': w3[l].astype(BF16), 'w2': w2[l].astype(BF16),
    }


def _front_pad(buf):
    return jnp.pad(buf, ((0, 0), (8 - (CONV_W - 1), 0), (0, 0)))


def _cols(P, nb, t, start, width):
    return P.reshape(nb, t, PW)[:, :, start:start + width]


def _kv_rows(P, nb, t, g, n_rows):
    k = _cols(P, nb, t, C_K + g * GW, GW)[:, t - n_rows:].reshape(nb, n_rows, HPG, HD)
    v = _cols(P, nb, t, C_V + g * GW, GW)[:, t - n_rows:].reshape(nb, n_rows, HPG, HD)
    return jnp.stack([k, v], axis=2)


def _layer(x, lw, tables, layer, nb, t, rows, tm, tm_mix, tm_moe, conv_ssd, st_ssd, conv_lru, st_lru, caches,
           final_norm, final):
    P = _proj(x, lw['norm1'], lw['w_all'], lw['b_all'], tm)
    y_ssd, h_ssd = _ssd(P, _front_pad(conv_ssd), st_ssd.reshape(nb, SSD_INNER, SSD_STATE), lw, nb, t, rows)
    y_lru, h_lru = _lru(P, _front_pad(conv_lru), st_lru.reshape(nb, 1, LRU_W), lw, nb, t, rows)
    attn = []
    for g, (window, dil) in enumerate(ATTN_GROUPS):
        prev_t, cur_t, cache_t, new_t = tables[g]
        if caches is None:
            attn.append(_attn_prompt(P, prev_t, cur_t, g, dil, nb, t))
        else:
            attn.append(_attn_sample(P, caches[g], cache_t, new_t, g, layer, nb, t, window))
    mixed = _mix(P, y_ssd, y_lru, attn, lw, tm_mix)
    x1, h2, comb = _res(x, mixed, lw, tm_mix)
    x2 = _moe(h2, comb, x1, lw, final_norm, tm_moe, final)
    states = (_cols(P, nb, t, C_XBC, SSD_CONV_DIM)[:, t - 3:],
              h_ssd.reshape(nb, SSD_HEADS, SSD_HEAD_DIM, SSD_STATE),
              _cols(P, nb, t, C_XR, LRU_W)[:, t - 3:],
              h_lru.reshape(nb, LRU_W)) + tuple(
                  _kv_rows(P, nb, t, g, min(w, t)) for g, (w, _) in enumerate(ATTN_GROUPS))
    return x2, states


def kernel(x_prompt, x_sample, cache_conv_ssd, state_ssd, cache_conv_lru, state_lru, cache_kv_w128, cache_kv_w512, cache_kv_w2048, norm1, w_in, conv_ssd_w, conv_ssd_b, ssd_dt_bias, ssd_a_log, ssd_d, ssd_norm_w, conv_lru_w, conv_lru_b, lru_wr, lru_br, lru_wi, lru_bi, lru_lambda, t5_bias, w_br_ssd, w_br_lru, w_br_attn, w_gate, b_gate, w_o, norm2, w_router_group, b_router_group, w_router_expert, b_router_expert, w1, w3, w2, final_norm):
    bp, tp, _ = x_prompt.shape
    bs, ts, _ = x_sample.shape
    xp = x_prompt.reshape(bp * tp, D)
    xs = x_sample.reshape(bs * ts, D)
    fn = final_norm[None]
    tables = [_bias_tables(t5_bias, g, dil, window, ts) for g, (window, dil) in enumerate(ATTN_GROUPS)]
    caches = [c.reshape(-1, 2 * GW) for c in (cache_kv_w128, cache_kv_w512, cache_kv_w2048)]
    outs_p, outs_s = [], []
    for l in range(DEPTH):
        lw = _layer_weights(l, norm1, w_in, conv_ssd_w, conv_ssd_b, ssd_dt_bias, ssd_a_log, ssd_d, ssd_norm_w,
                            conv_lru_w, conv_lru_b, lru_wr, lru_br, lru_wi, lru_bi, lru_lambda,
                            w_br_ssd, w_br_lru, w_br_attn, w_gate, b_gate, w_o, norm2,
                            w_router_group, b_router_group, w_router_expert, b_router_expert, w1, w3, w2)
        final = l == DEPTH - 1
        xp, sp = _layer(xp, lw, tables, l, bp, tp, CL, 1024, 256, 512,
                        jnp.zeros((bp, CONV_W - 1, SSD_CONV_DIM), F32),
                        jnp.zeros((bp, SSD_HEADS, SSD_HEAD_DIM, SSD_STATE), F32),
                        jnp.zeros((bp, CONV_W - 1, LRU_W), F32), jnp.zeros((bp, LRU_W), F32),
                        None, fn, final)
        xs, ss = _layer(xs, lw, tables, l, bs, ts, ts, bs * ts, bs * ts, bs * ts,
                        cache_conv_ssd[l], state_ssd[l], cache_conv_lru[l], state_lru[l],
                        caches, fn, final)
        outs_p.append(sp)
        outs_s.append(ss)

    def stk(outs, i):
        return jnp.stack([o[i] for o in outs], axis=0)

    return ((xp.reshape(bp, tp, D), xs.reshape(bs, ts, D))
            + tuple(stk(outs_p, i) for i in range(7)) + tuple(stk(outs_s, i) for i in range(7)))
```

```python
import functools

import numpy as np
import jax
import jax.numpy as jnp
from jax import lax
from jax.experimental import pallas as pl
from jax.experimental.pallas import tpu as pltpu

F32 = jnp.float32
BF16 = jnp.bfloat16
EPS = 1e-6
NEG = -1e30

D = 2048
DEPTH = 2
PAST_LEN = 16384
CL = 128
CONV_W = 4
SSD_HEADS = 16
SSD_HEAD_DIM = 64
SSD_INNER = 1024
SSD_STATE = 128
SSD_CONV_DIM = 1536
LRU_W = 1024
LRU_BLOCKS = 8
LRU_C = 8.0
ATTN_GROUPS = ((128, 1), (512, 4), (2048, 16))
HPG = 4
HD = 128
GW = HPG * HD
T5_BUCKETS = 32
T5_MAX_DIST = 2048
N_GROUPS = 4
PER_GROUP = 4
N_EXPERTS = 16
D_EXPERT = 512

TILE = 512
C_GATE = 0
C_Z = 6144
C_XR = 7168
C_GR = 8192
C_XBC = 9216
C_Q = 10752
C_K = 12288
C_V = 13824
C_DT = 15360
PW = 15872
N_GATE_TILES = (3 * D) // TILE
ROUTE_LANES = 128
VMEM_LIMIT = 56 * 1024 * 1024


def _cparams(sem):
    return pltpu.CompilerParams(dimension_semantics=sem, vmem_limit_bytes=VMEM_LIMIT)


def _full(shape):
    nd = len(shape)
    return pl.BlockSpec(shape, lambda *_: (0,) * nd)


def _proj_kernel(x_ref, nw_ref, w_ref, b_ref, o_ref, *rest, tm, phase_major):
    if phase_major:
        qkv_refs, (h_ref, ph_ref) = rest[:9], rest[9:]
    else:
        (h_ref,) = rest
    j = pl.program_id(1)

    @pl.when(j == 0)
    def _():
        x = x_ref[...]
        ms = jnp.mean(x * x, axis=-1, keepdims=True)
        h_ref[...] = (x * lax.rsqrt(ms + EPS) * nw_ref[...]).astype(BF16)

    acc = jnp.dot(h_ref[...], w_ref[...], preferred_element_type=F32) + b_ref[...]

    @pl.when(j < N_GATE_TILES)
    def _():
        o_ref[...] = jax.nn.sigmoid(acc)

    @pl.when(j >= N_GATE_TILES)
    def _():
        o_ref[...] = acc

    if phase_major:
        for part in range(3):
            for g, (_, dil) in enumerate(ATTN_GROUPS):
                ref = qkv_refs[part * 3 + g]

                @pl.when(j == C_Q // TILE + part * 3 + g)
                def _(ref=ref, dil=dil):
                    if dil == 1:
                        ref[0, 0] = acc.astype(BF16)
                    else:
                        for c in range(TILE // 128):
                            ph_ref[c] = acc[:, c * 128:(c + 1) * 128]
                        for p in range(dil):
                            for c in range(TILE // 128):
                                ref[0, p, :, c * 128:(c + 1) * 128] = (
                                    ph_ref[c, pl.ds(p, tm // dil, stride=dil), :].astype(BF16))


def _proj(x, nw, w_all, b_all, tm, nb=None, t=None):
    n = x.shape[0]
    phase_major = nb is not None
    out_specs = [pl.BlockSpec((tm, TILE), lambda i, j: (i, j))]
    out_shape = [jax.ShapeDtypeStruct((n, PW), F32)]
    scratch = [pltpu.VMEM((tm, D), BF16)]
    if phase_major:
        tpb = t // tm
        for _ in range(3):
            for _, dil in ATTN_GROUPS:
                out_specs.append(pl.BlockSpec((1, dil, tm // dil, GW), lambda i, j: (i // tpb, 0, i % tpb, 0)))
                out_shape.append(jax.ShapeDtypeStruct((nb, dil, t // dil, GW), BF16))
        scratch.append(pltpu.VMEM((TILE // 128, tm, 128), F32))
    return pl.pallas_call(
        functools.partial(_proj_kernel, tm=tm, phase_major=phase_major),
        grid=(n // tm, PW // TILE),
        in_specs=[pl.BlockSpec((tm, D), lambda i, j: (i, 0)),
                  pl.BlockSpec((1, D), lambda i, j: (0, 0)),
                  pl.BlockSpec((D, TILE), lambda i, j: (0, j)),
                  pl.BlockSpec((1, TILE), lambda i, j: (0, j))],
        out_specs=out_specs,
        out_shape=out_shape,
        scratch_shapes=scratch,
        compiler_params=_cparams(("parallel", "arbitrary")),
        name="proj",
    )(x, nw, w_all, b_all)


def _conv_step(x_ref, xp_ref, cw_ref, cb_ref, rows, out_rows):
    xp_ref[8:8 + rows, :] = x_ref[...]
    acc = cb_ref[...] + cw_ref[0:1, :] * xp_ref[5:5 + out_rows, :]
    for j in range(1, CONV_W):
        acc = acc + cw_ref[j:j + 1, :] * xp_ref[5 + j:5 + j + out_rows, :]
    tail = xp_ref[rows:rows + 8, :]
    xp_ref[0:8, :] = tail
    return acc


def _softplus(x):
    return jnp.maximum(x, 0.0) + jnp.log1p(jnp.exp(-jnp.abs(x)))


def _ssd_kernel(z_ref, xbc_ref, dt_ref, tail_ref, h0_ref, cw_ref, cb_ref, dtb_ref, alog_ref, dsk_ref, nw_ref,
                y_ref, hf_ref, xp_ref, act_ref, st_ref, ysc_ref, *, rows, n_chunks):
    c = pl.program_id(1)

    @pl.when(c == 0)
    def _():
        xp_ref[0:8, :] = tail_ref[0]
        st_ref[...] = h0_ref[0]

    if rows < CL:
        xp_ref[8 + rows:, :] = jnp.zeros((CL - rows, SSD_CONV_DIM), F32)
    conv = _conv_step(xbc_ref, xp_ref, cw_ref, cb_ref, rows, CL)
    act_ref[...] = conv * jax.nn.sigmoid(conv)

    row = lax.broadcasted_iota(jnp.int32, (CL, 128), 0)
    lane = lax.broadcasted_iota(jnp.int32, (CL, 128), 1)
    raw = dt_ref[...]
    if rows < CL:
        raw = jnp.concatenate([raw, jnp.zeros((CL - rows, 128), F32)], axis=0)
    dt = _softplus(raw + dtb_ref[...])
    dt = jnp.where((lane < SSD_HEADS) & (row < rows), dt, 0.0)
    da = dt * (-jnp.exp(alog_ref[...]))
    acs = da
    d = 1
    while d < CL:
        acs = acs + jnp.where(row >= d, pltpu.roll(acs, d, 0), 0.0)
        d *= 2
    acs_t = acs.T
    last = acs[CL - 1:CL, :]
    e_acs = jnp.exp(acs)
    to_end = jnp.exp(last - acs)
    cdec = jnp.exp(last)
    causal = row >= lane
    lo_lane = lane < SSD_HEAD_DIM
    lo_row = row < SSD_HEAD_DIM
    dsk = dsk_ref[...]

    def pair_cols(arr, h):
        return jnp.where(lo_lane, arr[:, h:h + 1], arr[:, h + 1:h + 2])

    nt = (((1,), (1,)), ((), ()))
    for g in range(2):
        bm = act_ref[:, SSD_INNER + g * SSD_STATE:SSD_INNER + (g + 1) * SSD_STATE].astype(BF16)
        cm = act_ref[:, SSD_INNER + 256 + g * SSD_STATE:SSD_INNER + 256 + (g + 1) * SSD_STATE].astype(BF16)
        cb = lax.dot_general(cm, bm, nt, preferred_element_type=F32)
        for pp in range(4):
            h = g * 8 + 2 * pp
            sl = slice(h * SSD_HEAD_DIM, h * SSD_HEAD_DIM + 128)
            xs = act_ref[:, sl]
            xdt = xs * pair_cols(dt, h)
            xdt_b = xdt.astype(BF16)
            ys = []
            for hh in (h, h + 1):
                seg = acs[:, hh:hh + 1] - acs_t[hh:hh + 1, :]
                decay = jnp.exp(jnp.where(causal, seg, -jnp.inf))
                ys.append(jnp.dot((cb * decay).astype(BF16), xdt_b, preferred_element_type=F32))
            y_diag = jnp.where(lo_lane, ys[0], ys[1])
            st = st_ref[sl, :]
            y_off = lax.dot_general(cm, st.astype(BF16), nt, preferred_element_type=F32) * pair_cols(e_acs, h)
            d_pair = jnp.where(lo_lane, dsk[:, h:h + 1], dsk[:, h + 1:h + 2])
            ysc_ref[:, sl] = y_diag + y_off + d_pair * xs
            xdte_t = (xdt * pair_cols(to_end, h)).T.astype(BF16)
            s_new = jnp.dot(xdte_t, bm, preferred_element_type=F32)
            dec = jnp.where(lo_row, cdec[:, h:h + 1], cdec[:, h + 1:h + 2])
            st_ref[sl, :] = dec * st + s_new

    zz = z_ref[...]
    yg = ysc_ref[0:rows, :] * (zz * jax.nn.sigmoid(zz))
    gw = SSD_INNER // 2
    for g in range(2):
        part = yg[:, g * gw:(g + 1) * gw]
        ms = jnp.mean(part * part, axis=-1, keepdims=True)
        y_ref[:, g * gw:(g + 1) * gw] = part * lax.rsqrt(ms + EPS) * nw_ref[:, g * gw:(g + 1) * gw]

    @pl.when(c == n_chunks - 1)
    def _():
        hf_ref[0] = st_ref[...]


def _ssd(P, tail, h0, lw, nb, t, rows):
    nc = t // rows
    kern = functools.partial(_ssd_kernel, rows=rows, n_chunks=nc)
    return pl.pallas_call(
        kern,
        grid=(nb, nc),
        in_specs=[pl.BlockSpec((rows, SSD_INNER), lambda b, c: (b * nc + c, C_Z // SSD_INNER)),
                  pl.BlockSpec((rows, SSD_CONV_DIM), lambda b, c: (b * nc + c, C_XBC // SSD_CONV_DIM)),
                  pl.BlockSpec((rows, 128), lambda b, c: (b * nc + c, C_DT // 128)),
                  pl.BlockSpec((1, 8, SSD_CONV_DIM), lambda b, c: (b, 0, 0)),
                  pl.BlockSpec((1, SSD_INNER, SSD_STATE), lambda b, c: (b, 0, 0)),
                  _full((CONV_W, SSD_CONV_DIM)), _full((1, SSD_CONV_DIM)),
                  _full((1, 128)), _full((1, 128)), _full((1, 128)), _full((1, SSD_INNER))],
        out_specs=[pl.BlockSpec((rows, SSD_INNER), lambda b, c: (b * nc + c, 0)),
                   pl.BlockSpec((1, SSD_INNER, SSD_STATE), lambda b, c: (b, 0, 0))],
        out_shape=[jax.ShapeDtypeStruct((nb * t, SSD_INNER), F32),
                   jax.ShapeDtypeStruct((nb, SSD_INNER, SSD_STATE), F32)],
        scratch_shapes=[pltpu.VMEM((8 + CL, SSD_CONV_DIM), F32),
                        pltpu.VMEM((CL, SSD_CONV_DIM), F32),
                        pltpu.VMEM((SSD_INNER, SSD_STATE), F32),
                        pltpu.VMEM((CL, SSD_INNER), F32)],
        compiler_params=_cparams(("parallel", "arbitrary")),
        name="ssd",
    )(P, P, P, tail, h0, lw['conv_ssd_w'], lw['conv_ssd_b'], lw['dt_bias'], lw['a_log'], lw['d_skip'], lw['ssd_norm_w'])


def _lru_kernel(xr_ref, gr_ref, tail_ref, h0_ref, cw_ref, cb_ref, wr_ref, br_ref, wi_ref, bi_ref, lam_ref,
                y_ref, hl_ref, xp_ref, h_ref, *, rows, n_chunks):
    c = pl.program_id(1)

    @pl.when(c == 0)
    def _():
        xp_ref[0:8, :] = tail_ref[0]
        h_ref[...] = h0_ref[0]

    x = _conv_step(xr_ref, xp_ref, cw_ref, cb_ref, rows, rows)
    xb = x.astype(BF16)
    rs, is_ = [], []
    for n in range(LRU_BLOCKS):
        blk = xb[:, n * 128:(n + 1) * 128]
        rs.append(jnp.dot(blk, wr_ref[n], preferred_element_type=F32))
        is_.append(jnp.dot(blk, wi_ref[n], preferred_element_type=F32))
    r_gate = jax.nn.sigmoid(jnp.concatenate(rs, axis=1) + br_ref[...])
    i_gate = jax.nn.sigmoid(jnp.concatenate(is_, axis=1) + bi_ref[...])
    log_a = -LRU_C * r_gate * _softplus(-lam_ref[...])
    a = jnp.exp(log_a)
    b = jnp.sqrt(1.0 - jnp.exp(2.0 * log_a)) * (i_gate * x)
    row = lax.broadcasted_iota(jnp.int32, (rows, LRU_W), 0)
    d = 1
    while d < rows:
        a_s = jnp.where(row >= d, pltpu.roll(a, d, 0), 1.0)
        b_s = jnp.where(row >= d, pltpu.roll(b, d, 0), 0.0)
        b = a * b_s + b
        a = a * a_s
        d *= 2
    h = b + a * h_ref[...]
    last = h[rows - 1:rows, :]
    h_ref[...] = last
    y_ref[...] = h * jax.nn.gelu(gr_ref[...])

    @pl.when(c == n_chunks - 1)
    def _():
        hl_ref[0] = last


def _lru(P, tail, h0, lw, nb, t, rows):
    nc = t // rows
    kern = functools.partial(_lru_kernel, rows=rows, n_chunks=nc)
    return pl.pallas_call(
        kern,
        grid=(nb, nc),
        in_specs=[pl.BlockSpec((rows, LRU_W), lambda b, c: (b * nc + c, C_XR // LRU_W)),
                  pl.BlockSpec((rows, LRU_W), lambda b, c: (b * nc + c, C_GR // LRU_W)),
                  pl.BlockSpec((1, 8, LRU_W), lambda b, c: (b, 0, 0)),
                  pl.BlockSpec((1, 1, LRU_W), lambda b, c: (b, 0, 0)),
                  _full((CONV_W, LRU_W)), _full((1, LRU_W)),
                  _full((LRU_BLOCKS, 128, 128)), _full((1, LRU_W)),
                  _full((LRU_BLOCKS, 128, 128)), _full((1, LRU_W)), _full((1, LRU_W))],
        out_specs=[pl.BlockSpec((rows, LRU_W), lambda b, c: (b * nc + c, 0)),
                   pl.BlockSpec((1, 1, LRU_W), lambda b, c: (b, 0, 0))],
        out_shape=[jax.ShapeDtypeStruct((nb * t, LRU_W), F32),
                   jax.ShapeDtypeStruct((nb, 1, LRU_W), F32)],
        scratch_shapes=[pltpu.VMEM((8 + rows, LRU_W), F32), pltpu.VMEM((1, LRU_W), F32)],
        compiler_params=_cparams(("parallel", "arbitrary")),
        name="lru",
    )(P, P, tail, h0, lw['conv_lru_w'], lw['conv_lru_b'], lw['lru_wr'], lw['lru_br'], lw['lru_wi'], lw['lru_bi'],
      lw['lru_lambda'])


def _attn_kernel(q_ref, ka_ref, va_ref, kb_ref, vb_ref, ba_ref, bb_ref, o_ref, lse_ref, *, first_a_invalid):
    scale = HD ** -0.5
    nt = (((1,), (1,)), ((), ()))
    rows = o_ref.shape[0]
    for h in range(HPG):
        sl = slice(h * HD, (h + 1) * HD)

        def rd(ref):
            return ref[(0,) * (len(ref.shape) - 2) + (slice(None), sl)].astype(BF16)

        q = rd(q_ref)
        sa = lax.dot_general(q, rd(ka_ref), nt, preferred_element_type=F32) * scale + ba_ref[h]
        if first_a_invalid:
            sa = jnp.where(pl.program_id(2) > 0, sa, NEG)
        sb = lax.dot_general(q, rd(kb_ref), nt, preferred_element_type=F32) * scale + bb_ref[h]
        m = jnp.maximum(jnp.max(sa, axis=-1, keepdims=True), jnp.max(sb, axis=-1, keepdims=True))
        pa = jnp.exp(sa - m)
        pb = jnp.exp(sb - m)
        l = jnp.sum(pa, axis=-1, keepdims=True) + jnp.sum(pb, axis=-1, keepdims=True)
        o = (jnp.dot(pa.astype(BF16), rd(va_ref), preferred_element_type=F32)
             + jnp.dot(pb.astype(BF16), rd(vb_ref), preferred_element_type=F32))
        o_ref[:, sl] = o / l
        lse_ref[:, sl] = jnp.broadcast_to(m + jnp.log(l), (rows, HD))


def _attn_prompt(q, k, v, bias_a, bias_b, dil, nb, t):
    n = nb * t
    nblk = t // dil // CL
    cur = pl.BlockSpec((1, 1, CL, GW), lambda b, p, i: (b, p, i, 0))
    prev = pl.BlockSpec((1, 1, CL, GW), lambda b, p, i: (b, p, jnp.maximum(i - 1, 0), 0))
    out_spec = pl.BlockSpec((CL, GW), lambda b, p, i: (b * nblk + i, p))
    o, lse = pl.pallas_call(
        functools.partial(_attn_kernel, first_a_invalid=True),
        grid=(nb, dil, nblk),
        in_specs=[cur, prev, prev, cur, cur, _full((HPG, CL, CL)), _full((HPG, CL, CL))],
        out_specs=[out_spec, out_spec],
        out_shape=[jax.ShapeDtypeStruct((n // dil, dil * GW), F32)] * 2,
        compiler_params=_cparams(("parallel", "parallel", "arbitrary")),
        name=f"attn_prompt_d{dil}",
    )(q, k, v, k, v, bias_a, bias_b)
    return o.reshape(n, GW), lse.reshape(n, GW)


def _attn_sample(Ps, cache2d, bias_a, bias_b, g, layer, nb, t, window):
    tq, tk, tv = C_Q // GW + g, C_K // GW + g, C_V // GW + g
    out_spec = pl.BlockSpec((t, GW), lambda b: (b, 0))
    return pl.pallas_call(
        functools.partial(_attn_kernel, first_a_invalid=False),
        grid=(nb,),
        in_specs=[pl.BlockSpec((t, GW), lambda b: (b, tq)),
                  pl.BlockSpec((window, GW), lambda b: (layer * nb + b, 0)),
                  pl.BlockSpec((window, GW), lambda b: (layer * nb + b, 1)),
                  pl.BlockSpec((t, GW), lambda b: (b, tk)),
                  pl.BlockSpec((t, GW), lambda b: (b, tv)),
                  _full((HPG, t, window)), _full((HPG, t, t))],
        out_specs=[out_spec, out_spec],
        out_shape=[jax.ShapeDtypeStruct((nb * t, GW), F32)] * 2,
        compiler_params=_cparams(("parallel",)),
        name=f"attn_sample_w{window}",
    )(Ps, cache2d, cache2d, Ps, Ps, bias_a, bias_b)


def _mix_kernel(gs_ref, gl_ref, ga_ref, ys_ref, yl_ref, o0_ref, o1_ref, o2_ref, l0_ref, l1_ref, l2_ref,
                wbs_ref, wbl_ref, wba_ref, out_ref):
    l0, l1, l2 = l0_ref[...], l1_ref[...], l2_ref[...]
    m = jnp.maximum(jnp.maximum(l0, l1), l2)
    e0, e1, e2 = jnp.exp(l0 - m), jnp.exp(l1 - m), jnp.exp(l2 - m)
    den = e0 + e1 + e2
    ya = o0_ref[...] * (e0 / den) + o1_ref[...] * (e1 / den) + o2_ref[...] * (e2 / den)
    mixed = (gs_ref[...] * jnp.dot(ys_ref[...].astype(BF16), wbs_ref[...], preferred_element_type=F32)
             + gl_ref[...] * jnp.dot(yl_ref[...].astype(BF16), wbl_ref[...], preferred_element_type=F32)
             + ga_ref[...] * jnp.dot(ya.astype(BF16), wba_ref[...], preferred_element_type=F32))
    out_ref[...] = mixed.astype(BF16)


def _mix(P, y_ssd, y_lru, attn, lw, tm):
    n = P.shape[0]
    row = lambda w: pl.BlockSpec((tm, w), lambda i: (i, 0))
    (o0, s0), (o1, s1), (o2, s2) = attn
    return pl.pallas_call(
        _mix_kernel,
        grid=(n // tm,),
        in_specs=[pl.BlockSpec((tm, D), lambda i: (i, 0)), pl.BlockSpec((tm, D), lambda i: (i, 1)),
                  pl.BlockSpec((tm, D), lambda i: (i, 2)),
                  row(SSD_INNER), row(LRU_W), row(GW), row(GW), row(GW), row(GW), row(GW), row(GW),
                  _full((SSD_INNER, D)), _full((LRU_W, D)), _full((GW, D))],
        out_specs=row(D),
        out_shape=jax.ShapeDtypeStruct((n, D), BF16),
        compiler_params=_cparams(("parallel",)),
        name="mix",
    )(P, P, P, y_ssd, y_lru, o0, o1, o2, s0, s1, s2, lw['w_br_ssd'], lw['w_br_lru'], lw['w_br_attn'])


def _res_kernel(x_ref, mixed_ref, wo_ref, n2_ref, wr_ref, br_ref, x1_ref, h2_ref, comb_ref):
    x1 = x_ref[...] + jnp.dot(mixed_ref[...], wo_ref[...], preferred_element_type=F32)
    x1_ref[...] = x1
    ms = jnp.mean(x1 * x1, axis=-1, keepdims=True)
    h2 = x1 * lax.rsqrt(ms + EPS) * n2_ref[...]
    h2_ref[...] = h2.astype(BF16)
    logits = jnp.dot(h2, wr_ref[...], preferred_element_type=F32, precision=lax.Precision.HIGHEST) + br_ref[...]
    lane = lax.broadcasted_iota(jnp.int32, logits.shape, 1).astype(F32)
    big = float(ROUTE_LANES)

    def first_max(vals, ok):
        v = jnp.where(ok, vals, NEG)
        top = jnp.max(v, axis=-1, keepdims=True)
        idx = jnp.min(jnp.where(ok & (v == top), lane, big), axis=-1, keepdims=True)
        return top, idx

    is_g = lane < N_GROUPS
    gmax, gsel = first_max(logits, is_g)
    gp = 1.0 / jnp.sum(jnp.where(is_g, jnp.exp(logits - gmax), 0.0), axis=-1, keepdims=True)
    lo = N_GROUPS + PER_GROUP * gsel
    is_e = (lane >= lo) & (lane < lo + PER_GROUP)
    t1, i1 = first_max(logits, is_e)
    t2, i2 = first_max(logits, is_e & (lane != i1))
    e2 = jnp.exp(t2 - t1)
    w1 = gp / (1.0 + e2)
    w2 = gp * e2 / (1.0 + e2)
    comb_ref[...] = jnp.where(lane == i1, w1, 0.0) + jnp.where(lane == i2, w2, 0.0)


def _res(x, mixed, lw, tm):
    n = x.shape[0]
    return pl.pallas_call(
        _res_kernel,
        grid=(n // tm,),
        in_specs=[pl.BlockSpec((tm, D), lambda i: (i, 0)), pl.BlockSpec((tm, D), lambda i: (i, 0)),
                  _full((D, D)), _full((1, D)), _full((D, ROUTE_LANES)), _full((1, ROUTE_LANES))],
        out_specs=[pl.BlockSpec((tm, D), lambda i: (i, 0)), pl.BlockSpec((tm, D), lambda i: (i, 0)),
                   pl.BlockSpec((tm, ROUTE_LANES), lambda i: (i, 0))],
        out_shape=[jax.ShapeDtypeStruct((n, D), F32), jax.ShapeDtypeStruct((n, D), BF16),
                   jax.ShapeDtypeStruct((n, ROUTE_LANES), F32)],
        compiler_params=_cparams(("parallel",)),
        name="res_router",
    )(x, mixed, lw['w_o'], lw['norm2'], lw['w_router'], lw['b_router'])


def _moe_kernel(h2_ref, comb_ref, x1_ref, w1_ref, w3_ref, w2_ref, fn_ref, o_ref, *, final):
    e = pl.program_id(1)

    @pl.when(e == 0)
    def _():
        o_ref[...] = x1_ref[...]

    h = h2_ref[...]
    a = jnp.dot(h, w1_ref[0], preferred_element_type=F32)
    b = jnp.dot(h, w3_ref[0], preferred_element_type=F32)
    comb = comb_ref[...]
    lane = lax.broadcasted_iota(jnp.int32, comb.shape, 1)
    w = jnp.sum(jnp.where(lane == e + N_GROUPS, comb, 0.0), axis=-1, keepdims=True)
    act = (a * jax.nn.sigmoid(a)) * b * w
    o_ref[...] += jnp.dot(act.astype(BF16), w2_ref[0], preferred_element_type=F32)

    if final:
        @pl.when(e == N_EXPERTS - 1)
        def _():
            x = o_ref[...]
            ms = jnp.mean(x * x, axis=-1, keepdims=True)
            o_ref[...] = x * lax.rsqrt(ms + EPS) * fn_ref[...]


def _moe(h2, comb, x1, lw, final_norm, tm, final):
    n = x1.shape[0]
    return pl.pallas_call(
        functools.partial(_moe_kernel, final=final),
        grid=(n // tm, N_EXPERTS),
        in_specs=[pl.BlockSpec((tm, D), lambda i, e: (i, 0)),
                  pl.BlockSpec((tm, ROUTE_LANES), lambda i, e: (i, 0)),
                  pl.BlockSpec((tm, D), lambda i, e: (i, 0)),
                  pl.BlockSpec((1, D, D_EXPERT), lambda i, e: (e, 0, 0)),
                  pl.BlockSpec((1, D, D_EXPERT), lambda i, e: (e, 0, 0)),
                  pl.BlockSpec((1, D_EXPERT, D), lambda i, e: (e, 0, 0)),
                  pl.BlockSpec((1, D), lambda i, e: (0, 0))],
        out_specs=pl.BlockSpec((tm, D), lambda i, e: (i, 0)),
        out_shape=jax.ShapeDtypeStruct((n, D), F32),
        compiler_params=_cparams(("parallel", "arbitrary")),
        name="moe",
    )(h2, comb, x1, lw['w1'], lw['w3'], lw['w2'], final_norm)


def _t5_buckets(dist):
    max_exact = T5_BUCKETS // 2
    large = max_exact + (np.log(np.maximum(dist, 1) / max_exact) / np.log(T5_MAX_DIST / max_exact)
                         * (T5_BUCKETS - max_exact)).astype(np.int32)
    large = np.minimum(large, T5_BUCKETS - 1)
    return np.where(dist < max_exact, dist, large).astype(np.int32)


def _bias_tables(t5, g, dil, window, t_sample):
    nk = window // dil + 1
    hs = slice(g * HPG, (g + 1) * HPG)
    bias = t5[_t5_buckets(np.arange(nk) * dil)][:, hs].T
    rev = bias[:, ::-1]
    neg = lambda *shape: jnp.full(shape, NEG, F32)

    vec = jnp.concatenate([rev, neg(HPG, CL)], axis=1)
    both = jnp.tile(vec, (1, CL + 1))[:, :CL * 2 * CL].reshape(HPG, CL, 2 * CL)
    prev_t, cur_t = both[:, :, :CL], both[:, :, CL:]

    rows = []
    for r in range(t_sample):
        shift = r // dil
        per_u = jnp.concatenate([neg(HPG, shift), rev[:, :nk - 1 - shift]], axis=1)
        on_phase = (np.arange(dil) == r % dil)[None, None, :]
        rows.append(jnp.where(on_phase, per_u[:, :, None], NEG).reshape(HPG, window))
    cache_t = jnp.stack(rows, axis=1)
    r = np.arange(t_sample)[:, None]
    c = np.arange(t_sample)[None, :]
    ok = ((r - c) % dil == 0) & (r >= c)
    new_t = jnp.where(jnp.asarray(ok)[None], bias[:, np.clip((r - c) // dil, 0, nk - 1)], NEG)
    return prev_t, cur_t, cache_t, new_t


def _layer_weights(l, norm1, w_in, conv_ssd_w, conv_ssd_b, ssd_dt_bias, ssd_a_log, ssd_d, ssd_norm_w,
                   conv_lru_w, conv_lru_b, lru_wr, lru_br, lru_wi, lru_bi, lru_lambda,
                   w_br_ssd, w_br_lru, w_br_attn, w_gate, b_gate, w_o, norm2,
                   w_router_group, b_router_group, w_router_expert, b_router_expert, w1, w3, w2):
    wi = w_in[l]
    w_all = jnp.concatenate(
        [w_gate[l], wi[:, 0:1024], wi[:, 2576:3600], wi[:, 3600:4624], wi[:, 1024:2560], wi[:, 4624:6160],
         wi[:, 6160:7696], wi[:, 7696:9232], wi[:, 2560:2576], jnp.zeros((D, TILE - SSD_HEADS), F32)],
        axis=1).astype(BF16)
    b_all = jnp.concatenate([b_gate[l], jnp.zeros((PW - 3 * D,), F32)])[None]

    def pad128(v):
        return jnp.concatenate([v, jnp.zeros((128 - v.shape[0],), F32)])[None]

    return {
        'norm1': norm1[l][None], 'w_all': w_all, 'b_all': b_all,
        'conv_ssd_w': conv_ssd_w[l], 'conv_ssd_b': conv_ssd_b[l][None],
        'dt_bias': pad128(ssd_dt_bias[l]), 'a_log': pad128(ssd_a_log[l]), 'd_skip': pad128(ssd_d[l]),
        'ssd_norm_w': ssd_norm_w[l][None],
        'conv_lru_w': conv_lru_w[l], 'conv_lru_b': conv_lru_b[l][None],
        'lru_wr': lru_wr[l].astype(BF16), 'lru_br': lru_br[l][None],
        'lru_wi': lru_wi[l].astype(BF16), 'lru_bi': lru_bi[l][None], 'lru_lambda': lru_lambda[l][None],
        'w_br_ssd': w_br_ssd[l].astype(BF16), 'w_br_lru': w_br_lru[l].astype(BF16),
        'w_br_attn': w_br_attn[l].astype(BF16), 'w_o': w_o[l].astype(BF16), 'norm2': norm2[l][None],
        'w_router': jnp.concatenate([w_router_group[l], w_router_expert[l],
                                     jnp.zeros((D, ROUTE_LANES - N_GROUPS - N_EXPERTS), F32)], axis=1),
        'b_router': pad128(jnp.concatenate([b_router_group[l], b_router_expert[l]])),
        'w1': w1[l].astype(BF16), 'w3': w3[l].astype(BF16), 'w2': w2[l].astype(BF16),
    }


def _front_pad(buf):
    return jnp.pad(buf, ((0, 0), (8 - (CONV_W - 1), 0), (0, 0)))


def _cols(P, nb, t, start, width):
    return P.reshape(nb, t, PW)[:, :, start:start + width]


def _kv_rows(P, nb, t, g, n_rows):
    k = _cols(P, nb, t, C_K + g * GW, GW)[:, t - n_rows:].reshape(nb, n_rows, HPG, HD)
    v = _cols(P, nb, t, C_V + g * GW, GW)[:, t - n_rows:].reshape(nb, n_rows, HPG, HD)
    return jnp.stack([k, v], axis=2)


def _layer(x, lw, tables, layer, nb, t, rows, tm, tm_mix, tm_moe, conv_ssd, st_ssd, conv_lru, st_lru, caches,
           final_norm, final):
    if caches is None:
        P, *qkv = _proj(x, lw['norm1'], lw['w_all'], lw['b_all'], tm, nb, t)
    else:
        (P,) = _proj(x, lw['norm1'], lw['w_all'], lw['b_all'], tm)
    y_ssd, h_ssd = _ssd(P, _front_pad(conv_ssd), st_ssd.reshape(nb, SSD_INNER, SSD_STATE), lw, nb, t, rows)
    y_lru, h_lru = _lru(P, _front_pad(conv_lru), st_lru.reshape(nb, 1, LRU_W), lw, nb, t, rows)
    attn = []
    for g, (window, dil) in enumerate(ATTN_GROUPS):
        prev_t, cur_t, cache_t, new_t = tables[g]
        if caches is None:
            attn.append(_attn_prompt(qkv[g], qkv[3 + g], qkv[6 + g], prev_t, cur_t, dil, nb, t))
        else:
            attn.append(_attn_sample(P, caches[g], cache_t, new_t, g, layer, nb, t, window))
    mixed = _mix(P, y_ssd, y_lru, attn, lw, tm_mix)
    x1, h2, comb = _res(x, mixed, lw, tm_mix)
    x2 = _moe(h2, comb, x1, lw, final_norm, tm_moe, final)
    states = (_cols(P, nb, t, C_XBC, SSD_CONV_DIM)[:, t - 3:],
              h_ssd.reshape(nb, SSD_HEADS, SSD_HEAD_DIM, SSD_STATE),
              _cols(P, nb, t, C_XR, LRU_W)[:, t - 3:],
              h_lru.reshape(nb, LRU_W)) + tuple(
                  _kv_rows(P, nb, t, g, min(w, t)) for g, (w, _) in enumerate(ATTN_GROUPS))
    return x2, states


def kernel(x_prompt, x_sample, cache_conv_ssd, state_ssd, cache_conv_lru, state_lru, cache_kv_w128, cache_kv_w512, cache_kv_w2048, norm1, w_in, conv_ssd_w, conv_ssd_b, ssd_dt_bias, ssd_a_log, ssd_d, ssd_norm_w, conv_lru_w, conv_lru_b, lru_wr, lru_br, lru_wi, lru_bi, lru_lambda, t5_bias, w_br_ssd, w_br_lru, w_br_attn, w_gate, b_gate, w_o, norm2, w_router_group, b_router_group, w_router_expert, b_router_expert, w1, w3, w2, final_norm):
    bp, tp, _ = x_prompt.shape
    bs, ts, _ = x_sample.shape
    xp = x_prompt.reshape(bp * tp, D)
    xs = x_sample.reshape(bs * ts, D)
    fn = final_norm[None]
    tables = [_bias_tables(t5_bias, g, dil, window, ts) for g, (window, dil) in enumerate(ATTN_GROUPS)]
    caches = [c.reshape(-1, 2 * GW) for c in (cache_kv_w128, cache_kv_w512, cache_kv_w2048)]
    outs_p, outs_s = [], []
    for l in range(DEPTH):
        lw = _layer_weights(l, norm1, w_in, conv_ssd_w, conv_ssd_b, ssd_dt_bias, ssd_a_log, ssd_d, ssd_norm_w,
                            conv_lru_w, conv_lru_b, lru_wr, lru_br, lru_wi, lru_bi, lru_lambda,
                            w_br_ssd, w_br_lru, w_br_attn, w_gate, b_gate, w_o, norm2,
                            w_router_group, b_router_group, w_router_expert, b_router_expert, w1, w3, w2)
        final = l == DEPTH - 1
        xp, sp = _layer(xp, lw, tables, l, bp, tp, CL, 1024, 256, 512,
                        jnp.zeros((bp, CONV_W - 1, SSD_CONV_DIM), F32),
                        jnp.zeros((bp, SSD_HEADS, SSD_HEAD_DIM, SSD_STATE), F32),
                        jnp.zeros((bp, CONV_W - 1, LRU_W), F32), jnp.zeros((bp, LRU_W), F32),
                        None, fn, final)
        xs, ss = _layer(xs, lw, tables, l, bs, ts, ts, bs * ts, bs * ts, bs * ts,
                        cache_conv_ssd[l], state_ssd[l], cache_conv_lru[l], state_lru[l],
                        caches, fn, final)
        outs_p.append(sp)
        outs_s.append(ss)

    def stk(outs, i):
        return jnp.stack([o[i] for o in outs], axis=0)

    return ((xp.reshape(bp, tp, D), xs.reshape(bs, ts, D))
            + tuple(stk(outs_p, i) for i in range(7)) + tuple(stk(outs_s, i) for i in range(7)))
```

```python
import functools

import numpy as np
import jax
import jax.numpy as jnp
from jax import lax
from jax.experimental import pallas as pl
from jax.experimental.pallas import tpu as pltpu

F32 = jnp.float32
BF16 = jnp.bfloat16
EPS = 1e-6
NEG = -1e30

D = 2048
DEPTH = 2
PAST_LEN = 16384
CL = 128
CONV_W = 4
SSD_HEADS = 16
SSD_HEAD_DIM = 64
SSD_INNER = 1024
SSD_STATE = 128
SSD_CONV_DIM = 1536
LRU_W = 1024
LRU_BLOCKS = 8
LRU_C = 8.0
ATTN_GROUPS = ((128, 1), (512, 4), (2048, 16))
HPG = 4
HD = 128
GW = HPG * HD
T5_BUCKETS = 32
T5_MAX_DIST = 2048
N_GROUPS = 4
PER_GROUP = 4
N_EXPERTS = 16
D_EXPERT = 512

TILE = 512
C_GATE = 0
C_Z = 6144
C_XR = 7168
C_GR = 8192
C_XBC = 9216
C_Q = 10752
C_K = 12288
C_V = 13824
C_DT = 15360
PW = 15872
N_GATE_TILES = (3 * D) // TILE
ROUTE_LANES = 128
VMEM_LIMIT = 56 * 1024 * 1024


def _cparams(sem):
    return pltpu.CompilerParams(dimension_semantics=sem, vmem_limit_bytes=VMEM_LIMIT)


def _full(shape):
    nd = len(shape)
    return pl.BlockSpec(shape, lambda *_: (0,) * nd)


def _proj_kernel(x_ref, nw_ref, w_ref, b_ref, o_ref, *rest, tm, phase_major):
    if phase_major:
        qkv_refs, (h_ref, acc_ref, ph_ref) = rest[:9], rest[9:]
    else:
        h_ref, acc_ref = rest
    j = pl.program_id(1)

    @pl.when(j == 0)
    def _():
        x = x_ref[...]
        ms = jnp.mean(x * x, axis=-1, keepdims=True)
        h_ref[...] = (x * lax.rsqrt(ms + EPS) * nw_ref[...]).astype(BF16)
        acc_ref[...] = jnp.zeros(acc_ref.shape, F32)

    prev = acc_ref[...]
    o_ref[...] = jnp.where(j <= N_GATE_TILES, jax.nn.sigmoid(prev), prev)
    acc_ref[...] = jnp.dot(h_ref[...], w_ref[...], preferred_element_type=F32) + b_ref[...]

    if phase_major:
        for part in range(3):
            for g, (_, dil) in enumerate(ATTN_GROUPS):
                ref = qkv_refs[part * 3 + g]

                @pl.when(j - 1 == C_Q // TILE + part * 3 + g)
                def _(ref=ref, dil=dil):
                    if dil == 1:
                        ref[0, 0] = o_ref[...].astype(BF16)
                    else:
                        for c in range(TILE // 128):
                            ph_ref[c] = o_ref[:, c * 128:(c + 1) * 128]
                        for p in range(dil):
                            for c in range(TILE // 128):
                                ref[0, p, :, c * 128:(c + 1) * 128] = (
                                    ph_ref[c, pl.ds(p, tm // dil, stride=dil), :].astype(BF16))


def _proj(x, nw, w_all, b_all, tm, nb=None, t=None):
    n = x.shape[0]
    phase_major = nb is not None
    nt = PW // TILE
    out_specs = [pl.BlockSpec((tm, TILE), lambda i, j: (i, jnp.maximum(j - 1, 0)))]
    out_shape = [jax.ShapeDtypeStruct((n, PW), F32)]
    scratch = [pltpu.VMEM((tm, D), BF16), pltpu.VMEM((tm, TILE), F32)]
    if phase_major:
        tpb = t // tm
        for _ in range(3):
            for _, dil in ATTN_GROUPS:
                out_specs.append(pl.BlockSpec((1, dil, tm // dil, GW), lambda i, j: (i // tpb, 0, i % tpb, 0)))
                out_shape.append(jax.ShapeDtypeStruct((nb, dil, t // dil, GW), BF16))
        scratch.append(pltpu.VMEM((TILE // 128, tm, 128), F32))
    return pl.pallas_call(
        functools.partial(_proj_kernel, tm=tm, phase_major=phase_major),
        grid=(n // tm, nt + 1),
        in_specs=[pl.BlockSpec((tm, D), lambda i, j: (i, 0)),
                  pl.BlockSpec((1, D), lambda i, j: (0, 0)),
                  pl.BlockSpec((D, TILE), lambda i, j: (0, jnp.minimum(j, nt - 1))),
                  pl.BlockSpec((1, TILE), lambda i, j: (0, jnp.minimum(j, nt - 1)))],
        out_specs=out_specs,
        out_shape=out_shape,
        scratch_shapes=scratch,
        compiler_params=_cparams(("parallel", "arbitrary")),
        name="proj",
    )(x, nw, w_all, b_all)


def _conv_step(x_ref, xp_ref, cw_ref, cb_ref, rows, out_rows):
    xp_ref[8:8 + rows, :] = x_ref[...]
    acc = cb_ref[...] + cw_ref[0:1, :] * xp_ref[5:5 + out_rows, :]
    for j in range(1, CONV_W):
        acc = acc + cw_ref[j:j + 1, :] * xp_ref[5 + j:5 + j + out_rows, :]
    tail = xp_ref[rows:rows + 8, :]
    xp_ref[0:8, :] = tail
    return acc


def _softplus(x):
    return jnp.maximum(x, 0.0) + jnp.log1p(jnp.exp(-jnp.abs(x)))


def _ssd_kernel(z_ref, xbc_ref, dt_ref, tail_ref, h0_ref, cw_ref, cb_ref, dtb_ref, alog_ref, dsk_ref, nw_ref,
                y_ref, hf_ref, xp_ref, act_ref, st_ref, ysc_ref, *, rows, n_chunks):
    c = pl.program_id(1)

    @pl.when(c == 0)
    def _():
        xp_ref[0:8, :] = tail_ref[0]
        st_ref[...] = h0_ref[0]

    if rows < CL:
        xp_ref[8 + rows:, :] = jnp.zeros((CL - rows, SSD_CONV_DIM), F32)
    conv = _conv_step(xbc_ref, xp_ref, cw_ref, cb_ref, rows, CL)
    act_ref[...] = conv * jax.nn.sigmoid(conv)

    row = lax.broadcasted_iota(jnp.int32, (CL, 128), 0)
    lane = lax.broadcasted_iota(jnp.int32, (CL, 128), 1)
    raw = dt_ref[...]
    if rows < CL:
        raw = jnp.concatenate([raw, jnp.zeros((CL - rows, 128), F32)], axis=0)
    dt = _softplus(raw + dtb_ref[...])
    dt = jnp.where((lane < SSD_HEADS) & (row < rows), dt, 0.0)
    da = dt * (-jnp.exp(alog_ref[...]))
    acs = da
    d = 1
    while d < CL:
        acs = acs + jnp.where(row >= d, pltpu.roll(acs, d, 0), 0.0)
        d *= 2
    acs_t = acs.T
    last = acs[CL - 1:CL, :]
    e_acs = jnp.exp(acs)
    to_end = jnp.exp(last - acs)
    cdec = jnp.exp(last)
    causal = row >= lane
    lo_lane = lane < SSD_HEAD_DIM
    lo_row = row < SSD_HEAD_DIM
    dsk = dsk_ref[...]

    def pair_cols(arr, h):
        return jnp.where(lo_lane, arr[:, h:h + 1], arr[:, h + 1:h + 2])

    nt = (((1,), (1,)), ((), ()))
    for g in range(2):
        bm = act_ref[:, SSD_INNER + g * SSD_STATE:SSD_INNER + (g + 1) * SSD_STATE].astype(BF16)
        cm = act_ref[:, SSD_INNER + 256 + g * SSD_STATE:SSD_INNER + 256 + (g + 1) * SSD_STATE].astype(BF16)
        cb = lax.dot_general(cm, bm, nt, preferred_element_type=F32)
        for pp in range(4):
            h = g * 8 + 2 * pp
            sl = slice(h * SSD_HEAD_DIM, h * SSD_HEAD_DIM + 128)
            xs = act_ref[:, sl]
            xdt = xs * pair_cols(dt, h)
            xdt_b = xdt.astype(BF16)
            ys = []
            for hh in (h, h + 1):
                seg = acs[:, hh:hh + 1] - acs_t[hh:hh + 1, :]
                decay = jnp.exp(jnp.where(causal, seg, -jnp.inf))
                ys.append(jnp.dot((cb * decay).astype(BF16), xdt_b, preferred_element_type=F32))
            y_diag = jnp.where(lo_lane, ys[0], ys[1])
            st = st_ref[sl, :]
            y_off = lax.dot_general(cm, st.astype(BF16), nt, preferred_element_type=F32) * pair_cols(e_acs, h)
            d_pair = jnp.where(lo_lane, dsk[:, h:h + 1], dsk[:, h + 1:h + 2])
            ysc_ref[:, sl] = y_diag + y_off + d_pair * xs
            xdte_t = (xdt * pair_cols(to_end, h)).T.astype(BF16)
            s_new = jnp.dot(xdte_t, bm, preferred_element_type=F32)
            dec = jnp.where(lo_row, cdec[:, h:h + 1], cdec[:, h + 1:h + 2])
            st_ref[sl, :] = dec * st + s_new

    zz = z_ref[...]
    yg = ysc_ref[0:rows, :] * (zz * jax.nn.sigmoid(zz))
    gw = SSD_INNER // 2
    for g in range(2):
        part = yg[:, g * gw:(g + 1) * gw]
        ms = jnp.mean(part * part, axis=-1, keepdims=True)
        y_ref[:, g * gw:(g + 1) * gw] = part * lax.rsqrt(ms + EPS) * nw_ref[:, g * gw:(g + 1) * gw]

    @pl.when(c == n_chunks - 1)
    def _():
        hf_ref[0] = st_ref[...]


def _ssd(P, tail, h0, lw, nb, t, rows):
    nc = t // rows
    kern = functools.partial(_ssd_kernel, rows=rows, n_chunks=nc)
    return pl.pallas_call(
        kern,
        grid=(nb, nc),
        in_specs=[pl.BlockSpec((rows, SSD_INNER), lambda b, c: (b * nc + c, C_Z // SSD_INNER)),
                  pl.BlockSpec((rows, SSD_CONV_DIM), lambda b, c: (b * nc + c, C_XBC // SSD_CONV_DIM)),
                  pl.BlockSpec((rows, 128), lambda b, c: (b * nc + c, C_DT // 128)),
                  pl.BlockSpec((1, 8, SSD_CONV_DIM), lambda b, c: (b, 0, 0)),
                  pl.BlockSpec((1, SSD_INNER, SSD_STATE), lambda b, c: (b, 0, 0)),
                  _full((CONV_W, SSD_CONV_DIM)), _full((1, SSD_CONV_DIM)),
                  _full((1, 128)), _full((1, 128)), _full((1, 128)), _full((1, SSD_INNER))],
        out_specs=[pl.BlockSpec((rows, SSD_INNER), lambda b, c: (b * nc + c, 0)),
                   pl.BlockSpec((1, SSD_INNER, SSD_STATE), lambda b, c: (b, 0, 0))],
        out_shape=[jax.ShapeDtypeStruct((nb * t, SSD_INNER), F32),
                   jax.ShapeDtypeStruct((nb, SSD_INNER, SSD_STATE), F32)],
        scratch_shapes=[pltpu.VMEM((8 + CL, SSD_CONV_DIM), F32),
                        pltpu.VMEM((CL, SSD_CONV_DIM), F32),
                        pltpu.VMEM((SSD_INNER, SSD_STATE), F32),
                        pltpu.VMEM((CL, SSD_INNER), F32)],
        compiler_params=_cparams(("parallel", "arbitrary")),
        name="ssd",
    )(P, P, P, tail, h0, lw['conv_ssd_w'], lw['conv_ssd_b'], lw['dt_bias'], lw['a_log'], lw['d_skip'], lw['ssd_norm_w'])


def _lru_kernel(xr_ref, gr_ref, tail_ref, h0_ref, cw_ref, cb_ref, wr_ref, br_ref, wi_ref, bi_ref, lam_ref,
                y_ref, hl_ref, xp_ref, h_ref, *, rows, n_chunks):
    c = pl.program_id(1)

    @pl.when(c == 0)
    def _():
        xp_ref[0:8, :] = tail_ref[0]
        h_ref[...] = h0_ref[0]

    x = _conv_step(xr_ref, xp_ref, cw_ref, cb_ref, rows, rows)
    xb = x.astype(BF16)
    rs, is_ = [], []
    for n in range(LRU_BLOCKS):
        blk = xb[:, n * 128:(n + 1) * 128]
        rs.append(jnp.dot(blk, wr_ref[n], preferred_element_type=F32))
        is_.append(jnp.dot(blk, wi_ref[n], preferred_element_type=F32))
    r_gate = jax.nn.sigmoid(jnp.concatenate(rs, axis=1) + br_ref[...])
    i_gate = jax.nn.sigmoid(jnp.concatenate(is_, axis=1) + bi_ref[...])
    log_a = -LRU_C * r_gate * _softplus(-lam_ref[...])
    a = jnp.exp(log_a)
    b = jnp.sqrt(1.0 - jnp.exp(2.0 * log_a)) * (i_gate * x)
    row = lax.broadcasted_iota(jnp.int32, (rows, LRU_W), 0)
    d = 1
    while d < rows:
        a_s = jnp.where(row >= d, pltpu.roll(a, d, 0), 1.0)
        b_s = jnp.where(row >= d, pltpu.roll(b, d, 0), 0.0)
        b = a * b_s + b
        a = a * a_s
        d *= 2
    h = b + a * h_ref[...]
    last = h[rows - 1:rows, :]
    h_ref[...] = last
    y_ref[...] = h * jax.nn.gelu(gr_ref[...])

    @pl.when(c == n_chunks - 1)
    def _():
        hl_ref[0] = last


def _lru(P, tail, h0, lw, nb, t, rows):
    nc = t // rows
    kern = functools.partial(_lru_kernel, rows=rows, n_chunks=nc)
    return pl.pallas_call(
        kern,
        grid=(nb, nc),
        in_specs=[pl.BlockSpec((rows, LRU_W), lambda b, c: (b * nc + c, C_XR // LRU_W)),
                  pl.BlockSpec((rows, LRU_W), lambda b, c: (b * nc + c, C_GR // LRU_W)),
                  pl.BlockSpec((1, 8, LRU_W), lambda b, c: (b, 0, 0)),
                  pl.BlockSpec((1, 1, LRU_W), lambda b, c: (b, 0, 0)),
                  _full((CONV_W, LRU_W)), _full((1, LRU_W)),
                  _full((LRU_BLOCKS, 128, 128)), _full((1, LRU_W)),
                  _full((LRU_BLOCKS, 128, 128)), _full((1, LRU_W)), _full((1, LRU_W))],
        out_specs=[pl.BlockSpec((rows, LRU_W), lambda b, c: (b * nc + c, 0)),
                   pl.BlockSpec((1, 1, LRU_W), lambda b, c: (b, 0, 0))],
        out_shape=[jax.ShapeDtypeStruct((nb * t, LRU_W), F32),
                   jax.ShapeDtypeStruct((nb, 1, LRU_W), F32)],
        scratch_shapes=[pltpu.VMEM((8 + rows, LRU_W), F32), pltpu.VMEM((1, LRU_W), F32)],
        compiler_params=_cparams(("parallel", "arbitrary")),
        name="lru",
    )(P, P, tail, h0, lw['conv_lru_w'], lw['conv_lru_b'], lw['lru_wr'], lw['lru_br'], lw['lru_wi'], lw['lru_bi'],
      lw['lru_lambda'])


def _attn_kernel(q_ref, ka_ref, va_ref, kb_ref, vb_ref, ba_ref, bb_ref, o_ref, lse_ref, *, first_a_invalid):
    scale = HD ** -0.5
    nt = (((1,), (1,)), ((), ()))
    rows = o_ref.shape[0]
    for h in range(HPG):
        sl = slice(h * HD, (h + 1) * HD)

        def rd(ref):
            return ref[(0,) * (len(ref.shape) - 2) + (slice(None), sl)].astype(BF16)

        q = rd(q_ref)
        sa = lax.dot_general(q, rd(ka_ref), nt, preferred_element_type=F32) * scale + ba_ref[h]
        if first_a_invalid:
            sa = jnp.where(pl.program_id(2) > 0, sa, NEG)
        sb = lax.dot_general(q, rd(kb_ref), nt, preferred_element_type=F32) * scale + bb_ref[h]
        m = jnp.maximum(jnp.max(sa, axis=-1, keepdims=True), jnp.max(sb, axis=-1, keepdims=True))
        pa = jnp.exp(sa - m)
        pb = jnp.exp(sb - m)
        l = jnp.sum(pa, axis=-1, keepdims=True) + jnp.sum(pb, axis=-1, keepdims=True)
        o = (jnp.dot(pa.astype(BF16), rd(va_ref), preferred_element_type=F32)
             + jnp.dot(pb.astype(BF16), rd(vb_ref), preferred_element_type=F32))
        o_ref[:, sl] = o / l
        lse_ref[:, sl] = jnp.broadcast_to(m + jnp.log(l), (rows, HD))


def _attn_prompt(q, k, v, bias_a, bias_b, dil, nb, t):
    n = nb * t
    nblk = t // dil // CL
    cur = pl.BlockSpec((1, 1, CL, GW), lambda b, p, i: (b, p, i, 0))
    prev = pl.BlockSpec((1, 1, CL, GW), lambda b, p, i: (b, p, jnp.maximum(i - 1, 0), 0))
    out_spec = pl.BlockSpec((CL, GW), lambda b, p, i: (b * nblk + i, p))
    o, lse = pl.pallas_call(
        functools.partial(_attn_kernel, first_a_invalid=True),
        grid=(nb, dil, nblk),
        in_specs=[cur, prev, prev, cur, cur, _full((HPG, CL, CL)), _full((HPG, CL, CL))],
        out_specs=[out_spec, out_spec],
        out_shape=[jax.ShapeDtypeStruct((n // dil, dil * GW), F32)] * 2,
        compiler_params=_cparams(("parallel", "parallel", "arbitrary")),
        name=f"attn_prompt_d{dil}",
    )(q, k, v, k, v, bias_a, bias_b)
    return o.reshape(n, GW), lse.reshape(n, GW)


def _attn_sample_kernel(q_ref, kv_ref, kb_ref, vb_ref, ba_ref, bb_ref, o_ref, lse_ref):
    scale = HD ** -0.5
    nt = (((1,), (1,)), ((), ()))
    rows = o_ref.shape[0]
    kv = kv_ref[...].astype(BF16)
    for h in range(HPG):
        sl = slice(h * HD, (h + 1) * HD)
        q = q_ref[:, sl].astype(BF16)
        sa = lax.dot_general(q, kv, nt, preferred_element_type=F32) * scale + ba_ref[h]
        sb = lax.dot_general(q, kb_ref[:, sl].astype(BF16), nt, preferred_element_type=F32) * scale + bb_ref[h]
        m = jnp.maximum(jnp.max(sa, axis=-1, keepdims=True), jnp.max(sb, axis=-1, keepdims=True))
        pa = jnp.exp(sa - m)
        pb = jnp.exp(sb - m)
        l = jnp.sum(pa, axis=-1, keepdims=True) + jnp.sum(pb, axis=-1, keepdims=True)
        pa_v = pltpu.roll(pa, HPG, 1)
        o = (jnp.dot(pa_v.astype(BF16), kv, preferred_element_type=F32)
             + jnp.dot(pb.astype(BF16), vb_ref[:, sl].astype(BF16), preferred_element_type=F32))
        o_ref[:, sl] = o / l
        lse_ref[:, sl] = jnp.broadcast_to(m + jnp.log(l), (rows, HD))


def _attn_sample(Ps, cache_rows, bias_a, bias_b, g, layer, nb, t, window):
    tq, tk, tv = C_Q // GW + g, C_K // GW + g, C_V // GW + g
    out_spec = pl.BlockSpec((t, GW), lambda b: (b, 0))
    return pl.pallas_call(
        _attn_sample_kernel,
        grid=(nb,),
        in_specs=[pl.BlockSpec((t, GW), lambda b: (b, tq)),
                  pl.BlockSpec((window * 2 * HPG, HD), lambda b: (layer * nb + b, 0)),
                  pl.BlockSpec((t, GW), lambda b: (b, tk)),
                  pl.BlockSpec((t, GW), lambda b: (b, tv)),
                  _full((HPG, t, window * 2 * HPG)), _full((HPG, t, t))],
        out_specs=[out_spec, out_spec],
        out_shape=[jax.ShapeDtypeStruct((nb * t, GW), F32)] * 2,
        compiler_params=_cparams(("parallel",)),
        name=f"attn_sample_w{window}",
    )(Ps, cache_rows, Ps, Ps, bias_a, bias_b)


def _mix_kernel(gs_ref, gl_ref, ga_ref, ys_ref, yl_ref, o0_ref, o1_ref, o2_ref, l0_ref, l1_ref, l2_ref,
                wbs_ref, wbl_ref, wba_ref, out_ref):
    l0, l1, l2 = l0_ref[...], l1_ref[...], l2_ref[...]
    m = jnp.maximum(jnp.maximum(l0, l1), l2)
    e0, e1, e2 = jnp.exp(l0 - m), jnp.exp(l1 - m), jnp.exp(l2 - m)
    den = e0 + e1 + e2
    ya = o0_ref[...] * (e0 / den) + o1_ref[...] * (e1 / den) + o2_ref[...] * (e2 / den)
    mixed = (gs_ref[...] * jnp.dot(ys_ref[...].astype(BF16), wbs_ref[...], preferred_element_type=F32)
             + gl_ref[...] * jnp.dot(yl_ref[...].astype(BF16), wbl_ref[...], preferred_element_type=F32)
             + ga_ref[...] * jnp.dot(ya.astype(BF16), wba_ref[...], preferred_element_type=F32))
    out_ref[...] = mixed.astype(BF16)


def _mix(P, y_ssd, y_lru, attn, lw, tm):
    n = P.shape[0]
    row = lambda w: pl.BlockSpec((tm, w), lambda i: (i, 0))
    (o0, s0), (o1, s1), (o2, s2) = attn
    return pl.pallas_call(
        _mix_kernel,
        grid=(n // tm,),
        in_specs=[pl.BlockSpec((tm, D), lambda i: (i, 0)), pl.BlockSpec((tm, D), lambda i: (i, 1)),
                  pl.BlockSpec((tm, D), lambda i: (i, 2)),
                  row(SSD_INNER), row(LRU_W), row(GW), row(GW), row(GW), row(GW), row(GW), row(GW),
                  _full((SSD_INNER, D)), _full((LRU_W, D)), _full((GW, D))],
        out_specs=row(D),
        out_shape=jax.ShapeDtypeStruct((n, D), BF16),
        compiler_params=_cparams(("parallel",)),
        name="mix",
    )(P, P, P, y_ssd, y_lru, o0, o1, o2, s0, s1, s2, lw['w_br_ssd'], lw['w_br_lru'], lw['w_br_attn'])


def _res_kernel(x_ref, mixed_ref, wo_ref, n2_ref, wr_ref, br_ref, x1_ref, h2_ref, comb_ref):
    x1 = x_ref[...] + jnp.dot(mixed_ref[...], wo_ref[...], preferred_element_type=F32)
    x1_ref[...] = x1
    ms = jnp.mean(x1 * x1, axis=-1, keepdims=True)
    h2 = x1 * lax.rsqrt(ms + EPS) * n2_ref[...]
    h2b = h2.astype(BF16)
    h2_ref[...] = h2b
    logits = jnp.dot(h2b, wr_ref[...], preferred_element_type=F32) + br_ref[...]
    lane = lax.broadcasted_iota(jnp.int32, logits.shape, 1).astype(F32)
    big = float(ROUTE_LANES)

    def first_max(vals, ok):
        v = jnp.where(ok, vals, NEG)
        top = jnp.max(v, axis=-1, keepdims=True)
        idx = jnp.min(jnp.where(ok & (v == top), lane, big), axis=-1, keepdims=True)
        return top, idx

    is_g = lane < N_GROUPS
    gmax, gsel = first_max(logits, is_g)
    gp = 1.0 / jnp.sum(jnp.where(is_g, jnp.exp(logits - gmax), 0.0), axis=-1, keepdims=True)
    lo = N_GROUPS + PER_GROUP * gsel
    is_e = (lane >= lo) & (lane < lo + PER_GROUP)
    t1, i1 = first_max(logits, is_e)
    t2, i2 = first_max(logits, is_e & (lane != i1))
    e2 = jnp.exp(t2 - t1)
    w1 = gp / (1.0 + e2)
    w2 = gp * e2 / (1.0 + e2)
    comb_ref[...] = jnp.where(lane == i1, w1, 0.0) + jnp.where(lane == i2, w2, 0.0)


def _res(x, mixed, lw, tm):
    n = x.shape[0]
    return pl.pallas_call(
        _res_kernel,
        grid=(n // tm,),
        in_specs=[pl.BlockSpec((tm, D), lambda i: (i, 0)), pl.BlockSpec((tm, D), lambda i: (i, 0)),
                  _full((D, D)), _full((1, D)), _full((D, ROUTE_LANES)), _full((1, ROUTE_LANES))],
        out_specs=[pl.BlockSpec((tm, D), lambda i: (i, 0)), pl.BlockSpec((tm, D), lambda i: (i, 0)),
                   pl.BlockSpec((tm, ROUTE_LANES), lambda i: (i, 0))],
        out_shape=[jax.ShapeDtypeStruct((n, D), F32), jax.ShapeDtypeStruct((n, D), BF16),
                   jax.ShapeDtypeStruct((n, ROUTE_LANES), F32)],
        compiler_params=_cparams(("parallel",)),
        name="res_router",
    )(x, mixed, lw['w_o'], lw['norm2'], lw['w_router'], lw['b_router'])


def _moe_kernel(h2_ref, comb_ref, x1_ref, w1_ref, w3_ref, w2_ref, fn_ref, o_ref, *, final):
    e = pl.program_id(1)

    @pl.when(e == 0)
    def _():
        o_ref[...] = x1_ref[...]

    h = h2_ref[...]
    a = jnp.dot(h, w1_ref[0], preferred_element_type=F32)
    b = jnp.dot(h, w3_ref[0], preferred_element_type=F32)
    comb = comb_ref[...]
    lane = lax.broadcasted_iota(jnp.int32, comb.shape, 1)
    w = jnp.sum(jnp.where(lane == e + N_GROUPS, comb, 0.0), axis=-1, keepdims=True)
    act = (a * jax.nn.sigmoid(a)) * b * w
    o_ref[...] += jnp.dot(act.astype(BF16), w2_ref[0], preferred_element_type=F32)

    if final:
        @pl.when(e == N_EXPERTS - 1)
        def _():
            x = o_ref[...]
            ms = jnp.mean(x * x, axis=-1, keepdims=True)
            o_ref[...] = x * lax.rsqrt(ms + EPS) * fn_ref[...]


def _moe(h2, comb, x1, lw, final_norm, tm, final):
    n = x1.shape[0]
    return pl.pallas_call(
        functools.partial(_moe_kernel, final=final),
        grid=(n // tm, N_EXPERTS),
        in_specs=[pl.BlockSpec((tm, D), lambda i, e: (i, 0)),
                  pl.BlockSpec((tm, ROUTE_LANES), lambda i, e: (i, 0)),
                  pl.BlockSpec((tm, D), lambda i, e: (i, 0)),
                  pl.BlockSpec((1, D, D_EXPERT), lambda i, e: (e, 0, 0)),
                  pl.BlockSpec((1, D, D_EXPERT), lambda i, e: (e, 0, 0)),
                  pl.BlockSpec((1, D_EXPERT, D), lambda i, e: (e, 0, 0)),
                  pl.BlockSpec((1, D), lambda i, e: (0, 0))],
        out_specs=pl.BlockSpec((tm, D), lambda i, e: (i, 0)),
        out_shape=jax.ShapeDtypeStruct((n, D), F32),
        compiler_params=_cparams(("parallel", "arbitrary")),
        name="moe",
    )(h2, comb, x1, lw['w1'], lw['w3'], lw['w2'], final_norm)


def _t5_buckets(dist):
    max_exact = T5_BUCKETS // 2
    large = max_exact + (np.log(np.maximum(dist, 1) / max_exact) / np.log(T5_MAX_DIST / max_exact)
                         * (T5_BUCKETS - max_exact)).astype(np.int32)
    large = np.minimum(large, T5_BUCKETS - 1)
    return np.where(dist < max_exact, dist, large).astype(np.int32)


def _bias_tables(t5, g, dil, window, t_sample):
    nk = window // dil + 1
    hs = slice(g * HPG, (g + 1) * HPG)
    bias = t5[_t5_buckets(np.arange(nk) * dil)][:, hs].T
    rev = bias[:, ::-1]
    neg = lambda *shape: jnp.full(shape, NEG, F32)

    vec = jnp.concatenate([rev, neg(HPG, CL)], axis=1)
    both = jnp.tile(vec, (1, CL + 1))[:, :CL * 2 * CL].reshape(HPG, CL, 2 * CL)
    prev_t, cur_t = both[:, :, :CL], both[:, :, CL:]

    rows = []
    for r in range(t_sample):
        shift = r // dil
        per_u = jnp.concatenate([neg(HPG, shift), rev[:, :nk - 1 - shift]], axis=1)
        on_phase = (np.arange(dil) == r % dil)[None, None, :]
        rows.append(jnp.where(on_phase, per_u[:, :, None], NEG).reshape(HPG, window))
    cache_t = jnp.stack(rows, axis=1)
    own_k = np.eye(HPG, 2 * HPG, dtype=bool)[:, None, None, :]
    cache_t = jnp.where(own_k, cache_t[..., None], NEG).reshape(HPG, t_sample, window * 2 * HPG)
    r = np.arange(t_sample)[:, None]
    c = np.arange(t_sample)[None, :]
    ok = ((r - c) % dil == 0) & (r >= c)
    new_t = jnp.where(jnp.asarray(ok)[None], bias[:, np.clip((r - c) // dil, 0, nk - 1)], NEG)
    return prev_t, cur_t, cache_t, new_t


def _layer_weights(l, norm1, w_in, conv_ssd_w, conv_ssd_b, ssd_dt_bias, ssd_a_log, ssd_d, ssd_norm_w,
                   conv_lru_w, conv_lru_b, lru_wr, lru_br, lru_wi, lru_bi, lru_lambda,
                   w_br_ssd, w_br_lru, w_br_attn, w_gate, b_gate, w_o, norm2,
                   w_router_group, b_router_group, w_router_expert, b_router_expert, w1, w3, w2):
    wi = w_in[l]
    w_all = jnp.concatenate(
        [w_gate[l], wi[:, 0:1024], wi[:, 2576:3600], wi[:, 3600:4624], wi[:, 1024:2560], wi[:, 4624:6160],
         wi[:, 6160:7696], wi[:, 7696:9232], wi[:, 2560:2576], jnp.zeros((D, TILE - SSD_HEADS), F32)],
        axis=1).astype(BF16)
    b_all = jnp.concatenate([b_gate[l], jnp.zeros((PW - 3 * D,), F32)])[None]

    def pad128(v):
        return jnp.concatenate([v, jnp.zeros((128 - v.shape[0],), F32)])[None]

    return {
        'norm1': norm1[l][None], 'w_all': w_all, 'b_all': b_all,
        'conv_ssd_w': conv_ssd_w[l], 'conv_ssd_b': conv_ssd_b[l][None],
        'dt_bias': pad128(ssd_dt_bias[l]), 'a_log': pad128(ssd_a_log[l]), 'd_skip': pad128(ssd_d[l]),
        'ssd_norm_w': ssd_norm_w[l][None],
        'conv_lru_w': conv_lru_w[l], 'conv_lru_b': conv_lru_b[l][None],
        'lru_wr': lru_wr[l].astype(BF16), 'lru_br': lru_br[l][None],
        'lru_wi': lru_wi[l].astype(BF16), 'lru_bi': lru_bi[l][None], 'lru_lambda': lru_lambda[l][None],
        'w_br_ssd': w_br_ssd[l].astype(BF16), 'w_br_lru': w_br_lru[l].astype(BF16),
        'w_br_attn': w_br_attn[l].astype(BF16), 'w_o': w_o[l].astype(BF16), 'norm2': norm2[l][None],
        'w_router': jnp.concatenate([w_router_group[l], w_router_expert[l],
                                     jnp.zeros((D, ROUTE_LANES - N_GROUPS - N_EXPERTS), F32)],
                                    axis=1).astype(BF16),
        'b_router': pad128(jnp.concatenate([b_router_group[l], b_router_expert[l]])),
        'w1': w1[l].astype(BF16), 'w3': w3[l].astype(BF16), 'w2': w2[l].astype(BF16),
    }


def _front_pad(buf):
    return jnp.pad(buf, ((0, 0), (8 - (CONV_W - 1), 0), (0, 0)))


def _cols(P, nb, t, start, width):
    return P.reshape(nb, t, PW)[:, :, start:start + width]


def _kv_rows(P, nb, t, g, n_rows):
    k = _cols(P, nb, t, C_K + g * GW, GW)[:, t - n_rows:].reshape(nb, n_rows, HPG, HD)
    v = _cols(P, nb, t, C_V + g * GW, GW)[:, t - n_rows:].reshape(nb, n_rows, HPG, HD)
    return jnp.stack([k, v], axis=2)


def _layer(x, lw, tables, layer, nb, t, rows, tm, tm_mix, tm_res, tm_moe, conv_ssd, st_ssd, conv_lru, st_lru, caches,
           final_norm, final):
    if caches is None:
        P, *qkv = _proj(x, lw['norm1'], lw['w_all'], lw['b_all'], tm, nb, t)
    else:
        (P,) = _proj(x, lw['norm1'], lw['w_all'], lw['b_all'], tm)
    y_ssd, h_ssd = _ssd(P, _front_pad(conv_ssd), st_ssd.reshape(nb, SSD_INNER, SSD_STATE), lw, nb, t, rows)
    y_lru, h_lru = _lru(P, _front_pad(conv_lru), st_lru.reshape(nb, 1, LRU_W), lw, nb, t, rows)
    attn = []
    for g, (window, dil) in enumerate(ATTN_GROUPS):
        prev_t, cur_t, cache_t, new_t = tables[g]
        if caches is None:
            attn.append(_attn_prompt(qkv[g], qkv[3 + g], qkv[6 + g], prev_t, cur_t, dil, nb, t))
        else:
            attn.append(_attn_sample(P, caches[g], cache_t, new_t, g, layer, nb, t, window))
    mixed = _mix(P, y_ssd, y_lru, attn, lw, tm_mix)
    x1, h2, comb = _res(x, mixed, lw, tm_res)
    x2 = _moe(h2, comb, x1, lw, final_norm, tm_moe, final)
    states = (_cols(P, nb, t, C_XBC, SSD_CONV_DIM)[:, t - 3:],
              h_ssd.reshape(nb, SSD_HEADS, SSD_HEAD_DIM, SSD_STATE),
              _cols(P, nb, t, C_XR, LRU_W)[:, t - 3:],
              h_lru.reshape(nb, LRU_W)) + tuple(
                  _kv_rows(P, nb, t, g, min(w, t)) for g, (w, _) in enumerate(ATTN_GROUPS))
    return x2, states


def kernel(x_prompt, x_sample, cache_conv_ssd, state_ssd, cache_conv_lru, state_lru, cache_kv_w128, cache_kv_w512, cache_kv_w2048, norm1, w_in, conv_ssd_w, conv_ssd_b, ssd_dt_bias, ssd_a_log, ssd_d, ssd_norm_w, conv_lru_w, conv_lru_b, lru_wr, lru_br, lru_wi, lru_bi, lru_lambda, t5_bias, w_br_ssd, w_br_lru, w_br_attn, w_gate, b_gate, w_o, norm2, w_router_group, b_router_group, w_router_expert, b_router_expert, w1, w3, w2, final_norm):
    bp, tp, _ = x_prompt.shape
    bs, ts, _ = x_sample.shape
    xp = x_prompt.reshape(bp * tp, D)
    xs = x_sample.reshape(bs * ts, D)
    fn = final_norm[None]
    tables = [_bias_tables(t5_bias, g, dil, window, ts) for g, (window, dil) in enumerate(ATTN_GROUPS)]
    caches = [c.reshape(-1, HD) for c in (cache_kv_w128, cache_kv_w512, cache_kv_w2048)]
    outs_p, outs_s = [], []
    for l in range(DEPTH):
        lw = _layer_weights(l, norm1, w_in, conv_ssd_w, conv_ssd_b, ssd_dt_bias, ssd_a_log, ssd_d, ssd_norm_w,
                            conv_lru_w, conv_lru_b, lru_wr, lru_br, lru_wi, lru_bi, lru_lambda,
                            w_br_ssd, w_br_lru, w_br_attn, w_gate, b_gate, w_o, norm2,
                            w_router_group, b_router_group, w_router_expert, b_router_expert, w1, w3, w2)
        final = l == DEPTH - 1
        xp, sp = _layer(xp, lw, tables, l, bp, tp, CL, 1024, 256, 512, 512,
                        jnp.zeros((bp, CONV_W - 1, SSD_CONV_DIM), F32),
                        jnp.zeros((bp, SSD_HEADS, SSD_HEAD_DIM, SSD_STATE), F32),
                        jnp.zeros((bp, CONV_W - 1, LRU_W), F32), jnp.zeros((bp, LRU_W), F32),
                        None, fn, final)
        xs, ss = _layer(xs, lw, tables, l, bs, ts, ts, bs * ts, bs * ts, bs * ts, bs * ts,
                        cache_conv_ssd[l], state_ssd[l], cache_conv_lru[l], state_lru[l],
                        caches, fn, final)
        outs_p.append(sp)
        outs_s.append(ss)

    def stk(outs, i):
        return jnp.stack([o[i] for o in outs], axis=0)

    return ((xp.reshape(bp, tp, D), xs.reshape(bs, ts, D))
            + tuple(stk(outs_p, i) for i in range(7)) + tuple(stk(outs_s, i) for i in range(7)))
```

```python
import functools

import numpy as np
import jax
import jax.numpy as jnp
from jax import lax
from jax.experimental import pallas as pl
from jax.experimental.pallas import tpu as pltpu

F32 = jnp.float32
BF16 = jnp.bfloat16
EPS = 1e-6
NEG = -1e30

D = 2048
DEPTH = 2
PAST_LEN = 16384
CL = 128
CONV_W = 4
SSD_HEADS = 16
SSD_HEAD_DIM = 64
SSD_INNER = 1024
SSD_STATE = 128
SSD_CONV_DIM = 1536
LRU_W = 1024
LRU_BLOCKS = 8
LRU_C = 8.0
ATTN_GROUPS = ((128, 1), (512, 4), (2048, 16))
HPG = 4
HD = 128
GW = HPG * HD
T5_BUCKETS = 32
T5_MAX_DIST = 2048
N_GROUPS = 4
PER_GROUP = 4
N_EXPERTS = 16
D_EXPERT = 512

TILE = 512
C_GATE = 0
C_Z = 6144
C_XR = 7168
C_GR = 8192
C_XBC = 9216
C_Q = 10752
C_K = 12288
C_V = 13824
C_DT = 15360
PW = 15872
N_GATE_TILES = (3 * D) // TILE
ROUTE_LANES = 128
VMEM_LIMIT = 56 * 1024 * 1024


def _cparams(sem):
    return pltpu.CompilerParams(dimension_semantics=sem, vmem_limit_bytes=VMEM_LIMIT)


def _full(shape):
    nd = len(shape)
    return pl.BlockSpec(shape, lambda *_: (0,) * nd)


def _of_layer(shape, layer):
    nd = len(shape)
    return pl.BlockSpec((1,) + shape, lambda *_: (layer,) + (0,) * nd)


def _proj_kernel(x_ref, nw_ref, w_ref, b_ref, o_ref, *rest, tm, phase_major):
    if phase_major:
        qkv_refs, (h_ref, acc_ref, ph_ref) = rest[:9], rest[9:]
    else:
        h_ref, acc_ref = rest
    j = pl.program_id(1)

    @pl.when(j == 0)
    def _():
        x = x_ref[...]
        ms = jnp.mean(x * x, axis=-1, keepdims=True)
        h_ref[...] = (x * lax.rsqrt(ms + EPS) * nw_ref[...]).astype(BF16)
        acc_ref[...] = jnp.zeros(acc_ref.shape, F32)

    prev = acc_ref[...]
    o_ref[...] = jnp.where(j <= N_GATE_TILES, jax.nn.sigmoid(prev), prev)
    acc_ref[...] = jnp.dot(h_ref[...], w_ref[...], preferred_element_type=F32) + b_ref[...]

    if phase_major:
        for part in range(3):
            for g, (_, dil) in enumerate(ATTN_GROUPS):
                ref = qkv_refs[part * 3 + g]

                @pl.when(j - 1 == C_Q // TILE + part * 3 + g)
                def _(ref=ref, dil=dil):
                    if dil == 1:
                        ref[0, 0] = o_ref[...].astype(BF16)
                    else:
                        for c in range(TILE // 128):
                            ph_ref[c] = o_ref[:, c * 128:(c + 1) * 128]
                        for p in range(dil):
                            for c in range(TILE // 128):
                                ref[0, p, :, c * 128:(c + 1) * 128] = (
                                    ph_ref[c, pl.ds(p, tm // dil, stride=dil), :].astype(BF16))


def _proj(x, nw, w_all, b_all, tm, nb=None, t=None):
    n = x.shape[0]
    phase_major = nb is not None
    nt = PW // TILE
    out_specs = [pl.BlockSpec((tm, TILE), lambda i, j: (i, jnp.maximum(j - 1, 0)))]
    out_shape = [jax.ShapeDtypeStruct((n, PW), F32)]
    scratch = [pltpu.VMEM((tm, D), BF16), pltpu.VMEM((tm, TILE), F32)]
    if phase_major:
        tpb = t // tm
        for _ in range(3):
            for _, dil in ATTN_GROUPS:
                out_specs.append(pl.BlockSpec((1, dil, tm // dil, GW), lambda i, j: (i // tpb, 0, i % tpb, 0)))
                out_shape.append(jax.ShapeDtypeStruct((nb, dil, t // dil, GW), BF16))
        scratch.append(pltpu.VMEM((TILE // 128, tm, 128), F32))
    return pl.pallas_call(
        functools.partial(_proj_kernel, tm=tm, phase_major=phase_major),
        grid=(n // tm, nt + 1),
        in_specs=[pl.BlockSpec((tm, D), lambda i, j: (i, 0)),
                  pl.BlockSpec((1, D), lambda i, j: (0, 0)),
                  pl.BlockSpec((D, TILE), lambda i, j: (0, jnp.minimum(j, nt - 1))),
                  pl.BlockSpec((1, TILE), lambda i, j: (0, jnp.minimum(j, nt - 1)))],
        out_specs=out_specs,
        out_shape=out_shape,
        scratch_shapes=scratch,
        compiler_params=_cparams(("parallel", "arbitrary")),
        name="proj",
    )(x, nw, w_all, b_all)


def _conv_step(x_ref, xp_ref, cw_ref, cb_ref, rows, out_rows):
    xp_ref[8:8 + rows, :] = x_ref[...]
    acc = cb_ref[...] + cw_ref[0:1, :] * xp_ref[5:5 + out_rows, :]
    for j in range(1, CONV_W):
        acc = acc + cw_ref[j:j + 1, :] * xp_ref[5 + j:5 + j + out_rows, :]
    tail = xp_ref[rows:rows + 8, :]
    xp_ref[0:8, :] = tail
    return acc


def _softplus(x):
    return jnp.maximum(x, 0.0) + jnp.log1p(jnp.exp(-jnp.abs(x)))


def _ssd_kernel(z_ref, xbc_ref, dt_ref, tail_ref, h0_ref, cw_ref, cb_ref, dtb_ref, alog_ref, dsk_ref, nw_ref,
                y_ref, hf_ref, xp_ref, act_ref, st_ref, ysc_ref, *, rows, n_chunks):
    c = pl.program_id(1)

    @pl.when(c == 0)
    def _():
        xp_ref[0:8, :] = tail_ref[0]
        st_ref[...] = h0_ref[0]

    if rows < CL:
        xp_ref[8 + rows:, :] = jnp.zeros((CL - rows, SSD_CONV_DIM), F32)
    conv = _conv_step(xbc_ref, xp_ref, cw_ref, cb_ref, rows, CL)
    act_ref[...] = conv * jax.nn.sigmoid(conv)

    row = lax.broadcasted_iota(jnp.int32, (CL, 128), 0)
    lane = lax.broadcasted_iota(jnp.int32, (CL, 128), 1)
    raw = dt_ref[...]
    if rows < CL:
        raw = jnp.concatenate([raw, jnp.zeros((CL - rows, 128), F32)], axis=0)
    dt = _softplus(raw + dtb_ref[...])
    dt = jnp.where((lane < SSD_HEADS) & (row < rows), dt, 0.0)
    da = dt * (-jnp.exp(alog_ref[...]))
    acs = da
    d = 1
    while d < CL:
        acs = acs + jnp.where(row >= d, pltpu.roll(acs, d, 0), 0.0)
        d *= 2
    acs_t = acs.T
    last = acs[CL - 1:CL, :]
    e_acs = jnp.exp(acs)
    to_end = jnp.exp(last - acs)
    cdec = jnp.exp(last)
    causal = row >= lane
    lo_lane = lane < SSD_HEAD_DIM
    lo_row = row < SSD_HEAD_DIM
    dsk = dsk_ref[...]

    def pair_cols(arr, h):
        return jnp.where(lo_lane, arr[:, h:h + 1], arr[:, h + 1:h + 2])

    nt = (((1,), (1,)), ((), ()))
    for g in range(2):
        bm = act_ref[:, SSD_INNER + g * SSD_STATE:SSD_INNER + (g + 1) * SSD_STATE].astype(BF16)
        cm = act_ref[:, SSD_INNER + 256 + g * SSD_STATE:SSD_INNER + 256 + (g + 1) * SSD_STATE].astype(BF16)
        cb = lax.dot_general(cm, bm, nt, preferred_element_type=F32)
        for pp in range(4):
            h = g * 8 + 2 * pp
            sl = slice(h * SSD_HEAD_DIM, h * SSD_HEAD_DIM + 128)
            xs = act_ref[:, sl]
            xdt = xs * pair_cols(dt, h)
            xdt_b = xdt.astype(BF16)
            ys = []
            for hh in (h, h + 1):
                seg = acs[:, hh:hh + 1] - acs_t[hh:hh + 1, :]
                decay = jnp.exp(jnp.where(causal, seg, -jnp.inf))
                ys.append(jnp.dot((cb * decay).astype(BF16), xdt_b, preferred_element_type=F32))
            y_diag = jnp.where(lo_lane, ys[0], ys[1])
            st = st_ref[sl, :]
            y_off = lax.dot_general(cm, st.astype(BF16), nt, preferred_element_type=F32) * pair_cols(e_acs, h)
            d_pair = jnp.where(lo_lane, dsk[:, h:h + 1], dsk[:, h + 1:h + 2])
            ysc_ref[:, sl] = y_diag + y_off + d_pair * xs
            xdte_t = (xdt * pair_cols(to_end, h)).T.astype(BF16)
            s_new = jnp.dot(xdte_t, bm, preferred_element_type=F32)
            dec = jnp.where(lo_row, cdec[:, h:h + 1], cdec[:, h + 1:h + 2])
            st_ref[sl, :] = dec * st + s_new

    zz = z_ref[...]
    yg = ysc_ref[0:rows, :] * (zz * jax.nn.sigmoid(zz))
    gw = SSD_INNER // 2
    for g in range(2):
        part = yg[:, g * gw:(g + 1) * gw]
        ms = jnp.mean(part * part, axis=-1, keepdims=True)
        y_ref[:, g * gw:(g + 1) * gw] = (part * lax.rsqrt(ms + EPS)
                                         * nw_ref[:, g * gw:(g + 1) * gw]).astype(y_ref.dtype)

    @pl.when(c == n_chunks - 1)
    def _():
        hf_ref[0] = st_ref[...]


def _ssd(P, tail, h0, lw, nb, t, rows):
    nc = t // rows
    kern = functools.partial(_ssd_kernel, rows=rows, n_chunks=nc)
    return pl.pallas_call(
        kern,
        grid=(nb, nc),
        in_specs=[pl.BlockSpec((rows, SSD_INNER), lambda b, c: (b * nc + c, C_Z // SSD_INNER)),
                  pl.BlockSpec((rows, SSD_CONV_DIM), lambda b, c: (b * nc + c, C_XBC // SSD_CONV_DIM)),
                  pl.BlockSpec((rows, 128), lambda b, c: (b * nc + c, C_DT // 128)),
                  pl.BlockSpec((1, 8, SSD_CONV_DIM), lambda b, c: (b, 0, 0)),
                  pl.BlockSpec((1, SSD_INNER, SSD_STATE), lambda b, c: (b, 0, 0)),
                  _full((CONV_W, SSD_CONV_DIM)), _full((1, SSD_CONV_DIM)),
                  _full((1, 128)), _full((1, 128)), _full((1, 128)), _full((1, SSD_INNER))],
        out_specs=[pl.BlockSpec((rows, SSD_INNER), lambda b, c: (b * nc + c, 0)),
                   pl.BlockSpec((1, SSD_INNER, SSD_STATE), lambda b, c: (b, 0, 0))],
        out_shape=[jax.ShapeDtypeStruct((nb * t, SSD_INNER), BF16 if rows % 16 == 0 else F32),
                   jax.ShapeDtypeStruct((nb, SSD_INNER, SSD_STATE), F32)],
        scratch_shapes=[pltpu.VMEM((8 + CL, SSD_CONV_DIM), F32),
                        pltpu.VMEM((CL, SSD_CONV_DIM), F32),
                        pltpu.VMEM((SSD_INNER, SSD_STATE), F32),
                        pltpu.VMEM((CL, SSD_INNER), F32)],
        compiler_params=_cparams(("parallel", "arbitrary")),
        name="ssd",
    )(P, P, P, tail, h0, lw['conv_ssd_w'], lw['conv_ssd_b'], lw['dt_bias'], lw['a_log'], lw['d_skip'], lw['ssd_norm_w'])


def _lru_kernel(xr_ref, gr_ref, tail_ref, h0_ref, cw_ref, cb_ref, wr_ref, br_ref, wi_ref, bi_ref, lam_ref,
                y_ref, hl_ref, xp_ref, h_ref, *, rows, n_chunks):
    c = pl.program_id(1)

    @pl.when(c == 0)
    def _():
        xp_ref[0:8, :] = tail_ref[0]
        h_ref[...] = h0_ref[0]

    x = _conv_step(xr_ref, xp_ref, cw_ref, cb_ref, rows, rows)
    xb = x.astype(BF16)
    rs, is_ = [], []
    for n in range(LRU_BLOCKS):
        blk = xb[:, n * 128:(n + 1) * 128]
        rs.append(jnp.dot(blk, wr_ref[0, n], preferred_element_type=F32))
        is_.append(jnp.dot(blk, wi_ref[0, n], preferred_element_type=F32))
    r_gate = jax.nn.sigmoid(jnp.concatenate(rs, axis=1) + br_ref[...])
    i_gate = jax.nn.sigmoid(jnp.concatenate(is_, axis=1) + bi_ref[...])
    log_a = -LRU_C * r_gate * _softplus(-lam_ref[...])
    a = jnp.exp(log_a)
    b = jnp.sqrt(1.0 - jnp.exp(2.0 * log_a)) * (i_gate * x)
    row = lax.broadcasted_iota(jnp.int32, (rows, LRU_W), 0)
    d = 1
    while d < rows:
        a_s = jnp.where(row >= d, pltpu.roll(a, d, 0), 1.0)
        b_s = jnp.where(row >= d, pltpu.roll(b, d, 0), 0.0)
        b = a * b_s + b
        a = a * a_s
        d *= 2
    h = b + a * h_ref[...]
    last = h[rows - 1:rows, :]
    h_ref[...] = last
    y_ref[...] = (h * jax.nn.gelu(gr_ref[...])).astype(y_ref.dtype)

    @pl.when(c == n_chunks - 1)
    def _():
        hl_ref[0] = last


def _lru(P, tail, h0, lw, sw, layer, nb, t, rows):
    nc = t // rows
    kern = functools.partial(_lru_kernel, rows=rows, n_chunks=nc)
    y_dtype = BF16 if rows % 16 == 0 else F32
    return pl.pallas_call(
        kern,
        grid=(nb, nc),
        in_specs=[pl.BlockSpec((rows, LRU_W), lambda b, c: (b * nc + c, C_XR // LRU_W)),
                  pl.BlockSpec((rows, LRU_W), lambda b, c: (b * nc + c, C_GR // LRU_W)),
                  pl.BlockSpec((1, 8, LRU_W), lambda b, c: (b, 0, 0)),
                  pl.BlockSpec((1, 1, LRU_W), lambda b, c: (b, 0, 0)),
                  _full((CONV_W, LRU_W)), _full((1, LRU_W)),
                  _of_layer((LRU_BLOCKS, 128, 128), layer), _full((1, LRU_W)),
                  _of_layer((LRU_BLOCKS, 128, 128), layer), _full((1, LRU_W)), _full((1, LRU_W))],
        out_specs=[pl.BlockSpec((rows, LRU_W), lambda b, c: (b * nc + c, 0)),
                   pl.BlockSpec((1, 1, LRU_W), lambda b, c: (b, 0, 0))],
        out_shape=[jax.ShapeDtypeStruct((nb * t, LRU_W), y_dtype),
                   jax.ShapeDtypeStruct((nb, 1, LRU_W), F32)],
        scratch_shapes=[pltpu.VMEM((8 + rows, LRU_W), F32), pltpu.VMEM((1, LRU_W), F32)],
        compiler_params=_cparams(("parallel", "arbitrary")),
        name="lru",
    )(P, P, tail, h0, lw['conv_lru_w'], lw['conv_lru_b'], sw['lru_wr'], lw['lru_br'], sw['lru_wi'], lw['lru_bi'],
      lw['lru_lambda'])


def _attn_kernel(q_ref, kp_ref, vp_ref, kc_ref, vc_ref, ba_ref, bb_ref, o_ref, lse_ref, *, sub):
    scale = HD ** -0.5
    nt = (((1,), (1,)), ((), ()))
    lane = lax.broadcasted_iota(jnp.int32, (CL, 128), 1)
    for s in range(sub):
        rows = slice(s * CL, (s + 1) * CL)
        before = slice((s - 1) * CL, s * CL)
        lse_all = jnp.zeros((CL, 128), F32)
        for h in range(HPG):
            sl = slice(h * HD, (h + 1) * HD)
            q = q_ref[0, 0, rows, sl]
            k_before = kp_ref[0, 0, :, sl] if s == 0 else kc_ref[0, 0, before, sl]
            v_before = vp_ref[0, 0, :, sl] if s == 0 else vc_ref[0, 0, before, sl]
            sa = lax.dot_general(q, k_before, nt, preferred_element_type=F32) * scale + ba_ref[h]
            if s == 0:
                sa = jnp.where(pl.program_id(2) > 0, sa, NEG)
            sb = lax.dot_general(q, kc_ref[0, 0, rows, sl], nt, preferred_element_type=F32) * scale + bb_ref[h]
            m = jnp.maximum(jnp.max(sa, axis=-1, keepdims=True), jnp.max(sb, axis=-1, keepdims=True))
            pa = jnp.exp(sa - m)
            pb = jnp.exp(sb - m)
            l = jnp.sum(pa, axis=-1, keepdims=True) + jnp.sum(pb, axis=-1, keepdims=True)
            o = (jnp.dot(pa.astype(BF16), v_before, preferred_element_type=F32)
                 + jnp.dot(pb.astype(BF16), vc_ref[0, 0, rows, sl], preferred_element_type=F32))
            o_ref[rows, sl] = o / l
            lse_all = jnp.where(lane == h, m + jnp.log(l), lse_all)
        lse_ref[rows, :] = lse_all


def _attn_prompt(q, k, v, bias_a, bias_b, dil, nb, t):
    n = nb * t
    sub = min(4, t // dil // CL)
    nstep = t // dil // (sub * CL)
    cur = pl.BlockSpec((1, 1, sub * CL, GW), lambda b, p, i: (b, p, i, 0))
    prev = pl.BlockSpec((1, 1, CL, GW), lambda b, p, i: (b, p, jnp.maximum(i * sub - 1, 0), 0))
    o, lse = pl.pallas_call(
        functools.partial(_attn_kernel, sub=sub),
        grid=(nb, dil, nstep),
        in_specs=[cur, prev, prev, cur, cur, _full((HPG, CL, CL)), _full((HPG, CL, CL))],
        out_specs=[pl.BlockSpec((sub * CL, GW), lambda b, p, i: (b * nstep + i, p)),
                   pl.BlockSpec((sub * CL, 128), lambda b, p, i: (b * nstep + i, p))],
        out_shape=[jax.ShapeDtypeStruct((n // dil, dil * GW), F32),
                   jax.ShapeDtypeStruct((n // dil, dil * 128), F32)],
        compiler_params=_cparams(("parallel", "parallel", "arbitrary")),
        name=f"attn_prompt_d{dil}",
    )(q, k, v, k, v, bias_a, bias_b)
    return o.reshape(n, GW), lse.reshape(n, 128)


def _attn_sample_kernel(q_ref, kv_ref, kb_ref, vb_ref, ba_ref, bb_ref, o_ref, lse_ref):
    scale = HD ** -0.5
    nt = (((1,), (1,)), ((), ()))
    rows = o_ref.shape[0]
    kv = kv_ref[...].astype(BF16)
    lane = lax.broadcasted_iota(jnp.int32, (rows, 128), 1)
    lse_all = jnp.zeros((rows, 128), F32)
    for h in range(HPG):
        sl = slice(h * HD, (h + 1) * HD)
        q = q_ref[:, sl].astype(BF16)
        sa = lax.dot_general(q, kv, nt, preferred_element_type=F32) * scale + ba_ref[h]
        sb = lax.dot_general(q, kb_ref[:, sl].astype(BF16), nt, preferred_element_type=F32) * scale + bb_ref[h]
        m = jnp.maximum(jnp.max(sa, axis=-1, keepdims=True), jnp.max(sb, axis=-1, keepdims=True))
        pa = jnp.exp(sa - m)
        pb = jnp.exp(sb - m)
        l = jnp.sum(pa, axis=-1, keepdims=True) + jnp.sum(pb, axis=-1, keepdims=True)
        pa_v = pltpu.roll(pa, HPG, 1)
        o = (jnp.dot(pa_v.astype(BF16), kv, preferred_element_type=F32)
             + jnp.dot(pb.astype(BF16), vb_ref[:, sl].astype(BF16), preferred_element_type=F32))
        o_ref[:, sl] = o / l
        lse_all = jnp.where(lane == h, m + jnp.log(l), lse_all)
    lse_ref[...] = lse_all


def _attn_sample(Ps, cache_rows, bias_a, bias_b, g, layer, nb, t, window):
    tq, tk, tv = C_Q // GW + g, C_K // GW + g, C_V // GW + g
    out_spec = pl.BlockSpec((t, GW), lambda b: (b, 0))
    return pl.pallas_call(
        _attn_sample_kernel,
        grid=(nb,),
        in_specs=[pl.BlockSpec((t, GW), lambda b: (b, tq)),
                  pl.BlockSpec((window * 2 * HPG, HD), lambda b: (layer * nb + b, 0)),
                  pl.BlockSpec((t, GW), lambda b: (b, tk)),
                  pl.BlockSpec((t, GW), lambda b: (b, tv)),
                  _full((HPG, t, window * 2 * HPG)), _full((HPG, t, t))],
        out_specs=[out_spec, pl.BlockSpec((t, 128), lambda b: (b, 0))],
        out_shape=[jax.ShapeDtypeStruct((nb * t, GW), F32), jax.ShapeDtypeStruct((nb * t, 128), F32)],
        compiler_params=_cparams(("parallel",)),
        name=f"attn_sample_w{window}",
    )(Ps, cache_rows, Ps, Ps, bias_a, bias_b)


def _mix_kernel(gs_ref, gl_ref, ga_ref, ys_ref, yl_ref, o0_ref, o1_ref, o2_ref, l0_ref, l1_ref, l2_ref,
                wbs_ref, wbl_ref, wba_ref, out_ref):
    l0, l1, l2 = l0_ref[...], l1_ref[...], l2_ref[...]
    m = jnp.maximum(jnp.maximum(l0, l1), l2)
    e0, e1, e2 = jnp.exp(l0 - m), jnp.exp(l1 - m), jnp.exp(l2 - m)
    den = e0 + e1 + e2
    w0, w1, w2 = e0 / den, e1 / den, e2 / den
    tm = out_ref.shape[0]
    heads = []
    for h in range(HPG):
        sl = slice(h * HD, (h + 1) * HD)
        per_head = lambda w: jnp.broadcast_to(w[:, h:h + 1], (tm, HD))
        heads.append(o0_ref[:, sl] * per_head(w0) + o1_ref[:, sl] * per_head(w1) + o2_ref[:, sl] * per_head(w2))
    ya = jnp.concatenate(heads, axis=1)
    mixed = (gs_ref[...] * jnp.dot(ys_ref[...].astype(BF16), wbs_ref[0], preferred_element_type=F32)
             + gl_ref[...] * jnp.dot(yl_ref[...].astype(BF16), wbl_ref[0], preferred_element_type=F32)
             + ga_ref[...] * jnp.dot(ya.astype(BF16), wba_ref[0], preferred_element_type=F32))
    out_ref[...] = mixed.astype(BF16)


def _mix(P, y_ssd, y_lru, attn, sw, layer, tm):
    n = P.shape[0]
    row = lambda w: pl.BlockSpec((tm, w), lambda i: (i, 0))
    (o0, s0), (o1, s1), (o2, s2) = attn
    return pl.pallas_call(
        _mix_kernel,
        grid=(n // tm,),
        in_specs=[pl.BlockSpec((tm, D), lambda i: (i, 0)), pl.BlockSpec((tm, D), lambda i: (i, 1)),
                  pl.BlockSpec((tm, D), lambda i: (i, 2)),
                  row(SSD_INNER), row(LRU_W), row(GW), row(GW), row(GW), row(128), row(128), row(128),
                  _of_layer((SSD_INNER, D), layer), _of_layer((LRU_W, D), layer), _of_layer((GW, D), layer)],
        out_specs=row(D),
        out_shape=jax.ShapeDtypeStruct((n, D), BF16),
        compiler_params=_cparams(("parallel",)),
        name="mix",
    )(P, P, P, y_ssd, y_lru, o0, o1, o2, s0, s1, s2, sw['w_br_ssd'], sw['w_br_lru'], sw['w_br_attn'])


def _res_kernel(x_ref, mixed_ref, wo_ref, n2_ref, wr_ref, br_ref, x1_ref, h2_ref, comb_ref):
    x1 = x_ref[...] + jnp.dot(mixed_ref[...], wo_ref[0], preferred_element_type=F32)
    x1_ref[...] = x1
    ms = jnp.mean(x1 * x1, axis=-1, keepdims=True)
    h2 = x1 * lax.rsqrt(ms + EPS) * n2_ref[...]
    h2b = h2.astype(BF16)
    h2_ref[...] = h2b
    logits = jnp.dot(h2b, wr_ref[...], preferred_element_type=F32) + br_ref[...]
    lane = lax.broadcasted_iota(jnp.int32, logits.shape, 1).astype(F32)
    big = float(ROUTE_LANES)

    def first_max(vals, ok):
        v = jnp.where(ok, vals, NEG)
        top = jnp.max(v, axis=-1, keepdims=True)
        idx = jnp.min(jnp.where(ok & (v == top), lane, big), axis=-1, keepdims=True)
        return top, idx

    is_g = lane < N_GROUPS
    gmax, gsel = first_max(logits, is_g)
    gp = 1.0 / jnp.sum(jnp.where(is_g, jnp.exp(logits - gmax), 0.0), axis=-1, keepdims=True)
    lo = N_GROUPS + PER_GROUP * gsel
    is_e = (lane >= lo) & (lane < lo + PER_GROUP)
    t1, i1 = first_max(logits, is_e)
    t2, i2 = first_max(logits, is_e & (lane != i1))
    e2 = jnp.exp(t2 - t1)
    w1 = gp / (1.0 + e2)
    w2 = gp * e2 / (1.0 + e2)
    comb_ref[...] = jnp.where(lane == i1, w1, 0.0) + jnp.where(lane == i2, w2, 0.0)


def _res(x, mixed, lw, sw, layer, tm):
    n = x.shape[0]
    return pl.pallas_call(
        _res_kernel,
        grid=(n // tm,),
        in_specs=[pl.BlockSpec((tm, D), lambda i: (i, 0)), pl.BlockSpec((tm, D), lambda i: (i, 0)),
                  _of_layer((D, D), layer), _full((1, D)), _full((D, ROUTE_LANES)), _full((1, ROUTE_LANES))],
        out_specs=[pl.BlockSpec((tm, D), lambda i: (i, 0)), pl.BlockSpec((tm, D), lambda i: (i, 0)),
                   pl.BlockSpec((tm, ROUTE_LANES), lambda i: (i, 0))],
        out_shape=[jax.ShapeDtypeStruct((n, D), F32), jax.ShapeDtypeStruct((n, D), BF16),
                   jax.ShapeDtypeStruct((n, ROUTE_LANES), F32)],
        compiler_params=_cparams(("parallel",)),
        name="res_router",
    )(x, mixed, sw['w_o'], lw['norm2'], lw['w_router'], lw['b_router'])


def _moe_kernel(h2_ref, comb_ref, x1_ref, w1_ref, w3_ref, w2_ref, fn_ref, o_ref, *, final):
    e = pl.program_id(1)

    @pl.when(e == 0)
    def _():
        o_ref[...] = x1_ref[...]

    h = h2_ref[...]
    a = jnp.dot(h, w1_ref[0, 0], preferred_element_type=F32)
    b = jnp.dot(h, w3_ref[0, 0], preferred_element_type=F32)
    comb = comb_ref[...]
    lane = lax.broadcasted_iota(jnp.int32, comb.shape, 1)
    w = jnp.sum(jnp.where(lane == e + N_GROUPS, comb, 0.0), axis=-1, keepdims=True)
    act = (a * jax.nn.sigmoid(a)) * b * w
    o_ref[...] += jnp.dot(act.astype(BF16), w2_ref[0, 0], preferred_element_type=F32)

    if final:
        @pl.when(e == N_EXPERTS - 1)
        def _():
            x = o_ref[...]
            ms = jnp.mean(x * x, axis=-1, keepdims=True)
            o_ref[...] = x * lax.rsqrt(ms + EPS) * fn_ref[...]


def _moe(h2, comb, x1, sw, layer, final_norm, tm, final):
    n = x1.shape[0]
    return pl.pallas_call(
        functools.partial(_moe_kernel, final=final),
        grid=(n // tm, N_EXPERTS),
        in_specs=[pl.BlockSpec((tm, D), lambda i, e: (i, 0)),
                  pl.BlockSpec((tm, ROUTE_LANES), lambda i, e: (i, 0)),
                  pl.BlockSpec((tm, D), lambda i, e: (i, 0)),
                  pl.BlockSpec((1, 1, D, D_EXPERT), lambda i, e: (layer, e, 0, 0)),
                  pl.BlockSpec((1, 1, D, D_EXPERT), lambda i, e: (layer, e, 0, 0)),
                  pl.BlockSpec((1, 1, D_EXPERT, D), lambda i, e: (layer, e, 0, 0)),
                  pl.BlockSpec((1, D), lambda i, e: (0, 0))],
        out_specs=pl.BlockSpec((tm, D), lambda i, e: (i, 0)),
        out_shape=jax.ShapeDtypeStruct((n, D), F32),
        compiler_params=_cparams(("parallel", "arbitrary")),
        name="moe",
    )(h2, comb, x1, sw['w1'], sw['w3'], sw['w2'], final_norm)


def _t5_buckets(dist):
    max_exact = T5_BUCKETS // 2
    large = max_exact + (np.log(np.maximum(dist, 1) / max_exact) / np.log(T5_MAX_DIST / max_exact)
                         * (T5_BUCKETS - max_exact)).astype(np.int32)
    large = np.minimum(large, T5_BUCKETS - 1)
    return np.where(dist < max_exact, dist, large).astype(np.int32)


def _bias_tables(t5, g, dil, window, t_sample):
    nk = window // dil + 1
    hs = slice(g * HPG, (g + 1) * HPG)
    bias = t5[_t5_buckets(np.arange(nk) * dil)][:, hs].T
    rev = bias[:, ::-1]
    neg = lambda *shape: jnp.full(shape, NEG, F32)

    vec = jnp.concatenate([rev, neg(HPG, CL)], axis=1)
    both = jnp.tile(vec, (1, CL + 1))[:, :CL * 2 * CL].reshape(HPG, CL, 2 * CL)
    prev_t, cur_t = both[:, :, :CL], both[:, :, CL:]

    rows = []
    for r in range(t_sample):
        shift = r // dil
        per_u = jnp.concatenate([neg(HPG, shift), rev[:, :nk - 1 - shift]], axis=1)
        on_phase = (np.arange(dil) == r % dil)[None, None, :]
        rows.append(jnp.where(on_phase, per_u[:, :, None], NEG).reshape(HPG, window))
    cache_t = jnp.stack(rows, axis=1)
    own_k = np.eye(HPG, 2 * HPG, dtype=bool)[:, None, None, :]
    cache_t = jnp.where(own_k, cache_t[..., None], NEG).reshape(HPG, t_sample, window * 2 * HPG)
    r = np.arange(t_sample)[:, None]
    c = np.arange(t_sample)[None, :]
    ok = ((r - c) % dil == 0) & (r >= c)
    new_t = jnp.where(jnp.asarray(ok)[None], bias[:, np.clip((r - c) // dil, 0, nk - 1)], NEG)
    return prev_t, cur_t, cache_t, new_t


def _layer_weights(l, norm1, w_in, conv_ssd_w, conv_ssd_b, ssd_dt_bias, ssd_a_log, ssd_d, ssd_norm_w,
                   conv_lru_w, conv_lru_b, lru_br, lru_bi, lru_lambda, w_gate, b_gate, norm2,
                   w_router_group, b_router_group, w_router_expert, b_router_expert):
    wi = w_in[l]
    w_all = jnp.concatenate(
        [w_gate[l], wi[:, 0:1024], wi[:, 2576:3600], wi[:, 3600:4624], wi[:, 1024:2560], wi[:, 4624:6160],
         wi[:, 6160:7696], wi[:, 7696:9232], wi[:, 2560:2576], jnp.zeros((D, TILE - SSD_HEADS), F32)],
        axis=1).astype(BF16)
    b_all = jnp.concatenate([b_gate[l], jnp.zeros((PW - 3 * D,), F32)])[None]

    def pad128(v):
        return jnp.concatenate([v, jnp.zeros((128 - v.shape[0],), F32)])[None]

    return {
        'norm1': norm1[l][None], 'w_all': w_all, 'b_all': b_all,
        'conv_ssd_w': conv_ssd_w[l], 'conv_ssd_b': conv_ssd_b[l][None],
        'dt_bias': pad128(ssd_dt_bias[l]), 'a_log': pad128(ssd_a_log[l]), 'd_skip': pad128(ssd_d[l]),
        'ssd_norm_w': ssd_norm_w[l][None],
        'conv_lru_w': conv_lru_w[l], 'conv_lru_b': conv_lru_b[l][None],
        'lru_br': lru_br[l][None], 'lru_bi': lru_bi[l][None], 'lru_lambda': lru_lambda[l][None],
        'norm2': norm2[l][None],
        'w_router': jnp.concatenate([w_router_group[l], w_router_expert[l],
                                     jnp.zeros((D, ROUTE_LANES - N_GROUPS - N_EXPERTS), F32)],
                                    axis=1).astype(BF16),
        'b_router': pad128(jnp.concatenate([b_router_group[l], b_router_expert[l]])),
    }


def _front_pad(buf):
    return jnp.pad(buf, ((0, 0), (8 - (CONV_W - 1), 0), (0, 0)))


def _cols(P, nb, t, start, width):
    return P.reshape(nb, t, PW)[:, :, start:start + width]


def _kv_rows(P, nb, t, g, n_rows):
    k = _cols(P, nb, t, C_K + g * GW, GW)[:, t - n_rows:].reshape(nb, n_rows, HPG, HD)
    v = _cols(P, nb, t, C_V + g * GW, GW)[:, t - n_rows:].reshape(nb, n_rows, HPG, HD)
    return jnp.stack([k, v], axis=2)


def _layer(x, lw, sw, tables, layer, nb, t, rows, tm, tm_mix, tm_res, tm_moe, conv_ssd, st_ssd, conv_lru, st_lru,
           caches, final_norm, final):
    if caches is None:
        P, *qkv = _proj(x, lw['norm1'], lw['w_all'], lw['b_all'], tm, nb, t)
    else:
        (P,) = _proj(x, lw['norm1'], lw['w_all'], lw['b_all'], tm)
    y_ssd, h_ssd = _ssd(P, _front_pad(conv_ssd), st_ssd.reshape(nb, SSD_INNER, SSD_STATE), lw, nb, t, rows)
    y_lru, h_lru = _lru(P, _front_pad(conv_lru), st_lru.reshape(nb, 1, LRU_W), lw, sw, layer, nb, t, rows)
    attn = []
    for g, (window, dil) in enumerate(ATTN_GROUPS):
        prev_t, cur_t, cache_t, new_t = tables[g]
        if caches is None:
            attn.append(_attn_prompt(qkv[g], qkv[3 + g], qkv[6 + g], prev_t, cur_t, dil, nb, t))
        else:
            attn.append(_attn_sample(P, caches[g], cache_t, new_t, g, layer, nb, t, window))
    mixed = _mix(P, y_ssd, y_lru, attn, sw, layer, tm_mix)
    x1, h2, comb = _res(x, mixed, lw, sw, layer, tm_res)
    x2 = _moe(h2, comb, x1, sw, layer, final_norm, tm_moe, final)
    states = (_cols(P, nb, t, C_XBC, SSD_CONV_DIM)[:, t - 3:],
              h_ssd.reshape(nb, SSD_HEADS, SSD_HEAD_DIM, SSD_STATE),
              _cols(P, nb, t, C_XR, LRU_W)[:, t - 3:],
              h_lru.reshape(nb, LRU_W)) + tuple(
                  _kv_rows(P, nb, t, g, min(w, t)) for g, (w, _) in enumerate(ATTN_GROUPS))
    return x2, states


def kernel(x_prompt, x_sample, cache_conv_ssd, state_ssd, cache_conv_lru, state_lru, cache_kv_w128, cache_kv_w512, cache_kv_w2048, norm1, w_in, conv_ssd_w, conv_ssd_b, ssd_dt_bias, ssd_a_log, ssd_d, ssd_norm_w, conv_lru_w, conv_lru_b, lru_wr, lru_br, lru_wi, lru_bi, lru_lambda, t5_bias, w_br_ssd, w_br_lru, w_br_attn, w_gate, b_gate, w_o, norm2, w_router_group, b_router_group, w_router_expert, b_router_expert, w1, w3, w2, final_norm):
    bp, tp, _ = x_prompt.shape
    bs, ts, _ = x_sample.shape
    xp = x_prompt.reshape(bp * tp, D)
    xs = x_sample.reshape(bs * ts, D)
    fn = final_norm[None]
    tables = [_bias_tables(t5_bias, g, dil, window, ts) for g, (window, dil) in enumerate(ATTN_GROUPS)]
    caches = [c.reshape(-1, HD) for c in (cache_kv_w128, cache_kv_w512, cache_kv_w2048)]
    sw = {name: w.astype(BF16) for name, w in dict(
        lru_wr=lru_wr, lru_wi=lru_wi, w_br_ssd=w_br_ssd, w_br_lru=w_br_lru, w_br_attn=w_br_attn, w_o=w_o,
        w1=w1, w3=w3, w2=w2).items()}
    outs_p, outs_s = [], []
    for l in range(DEPTH):
        lw = _layer_weights(l, norm1, w_in, conv_ssd_w, conv_ssd_b, ssd_dt_bias, ssd_a_log, ssd_d, ssd_norm_w,
                            conv_lru_w, conv_lru_b, lru_br, lru_bi, lru_lambda, w_gate, b_gate, norm2,
                            w_router_group, b_router_group, w_router_expert, b_router_expert)
        final = l == DEPTH - 1
        xp, sp = _layer(xp, lw, sw, tables, l, bp, tp, CL, 1024, 256, 512, 512,
                        jnp.zeros((bp, CONV_W - 1, SSD_CONV_DIM), F32),
                        jnp.zeros((bp, SSD_HEADS, SSD_HEAD_DIM, SSD_STATE), F32),
                        jnp.zeros((bp, CONV_W - 1, LRU_W), F32), jnp.zeros((bp, LRU_W), F32),
                        None, fn, final)
        xs, ss = _layer(xs, lw, sw, tables, l, bs, ts, ts, bs * ts, bs * ts, bs * ts, bs * ts,
                        cache_conv_ssd[l], state_ssd[l], cache_conv_lru[l], state_lru[l],
                        caches, fn, final)
        outs_p.append(sp)
        outs_s.append(ss)

    def stk(outs, i):
        return jnp.stack([o[i] for o in outs], axis=0)

    return ((xp.reshape(bp, tp, D), xs.reshape(bs, ts, D))
            + tuple(stk(outs_p, i) for i in range(7)) + tuple(stk(outs_s, i) for i in range(7)))
```

```python
import functools

import numpy as np
import jax
import jax.numpy as jnp
from jax import lax
from jax.experimental import pallas as pl
from jax.experimental.pallas import tpu as pltpu
from jax.experimental.pallas import tpu_sc as plsc

F32 = jnp.float32
BF16 = jnp.bfloat16
EPS = 1e-6
NEG = -1e30

D = 2048
DEPTH = 2
PAST_LEN = 16384
CL = 128
CONV_W = 4
SSD_HEADS = 16
SSD_HEAD_DIM = 64
SSD_INNER = 1024
SSD_STATE = 128
SSD_CONV_DIM = 1536
LRU_W = 1024
LRU_BLOCKS = 8
LRU_C = 8.0
ATTN_GROUPS = ((128, 1), (512, 4), (2048, 16))
HPG = 4
HD = 128
GW = HPG * HD
T5_BUCKETS = 32
T5_MAX_DIST = 2048
N_GROUPS = 4
PER_GROUP = 4
N_EXPERTS = 16
D_EXPERT = 512

TILE = 512
C_GATE = 0
C_Z = 6144
C_XR = 7168
C_GR = 8192
C_XBC = 9216
C_Q = 10752
C_K = 12288
C_V = 13824
C_DT = 15360
PW = 15872
N_GATE_TILES = (3 * D) // TILE
ROUTE_LANES = 128
VMEM_LIMIT = 56 * 1024 * 1024


def _cparams(sem):
    return pltpu.CompilerParams(dimension_semantics=sem, vmem_limit_bytes=VMEM_LIMIT)


def _full(shape):
    nd = len(shape)
    return pl.BlockSpec(shape, lambda *_: (0,) * nd)


def _of_layer(shape, layer):
    nd = len(shape)
    return pl.BlockSpec((1,) + shape, lambda *_: (layer,) + (0,) * nd)


def _proj_kernel(x_ref, nw_ref, w_ref, b_ref, o_ref, *rest, tm, phase_major):
    if phase_major:
        qkv_refs, (h_ref, acc_ref, ph_ref) = rest[:9], rest[9:]
    else:
        h_ref, acc_ref = rest
    j = pl.program_id(1)

    @pl.when(j == 0)
    def _():
        x = x_ref[...]
        ms = jnp.mean(x * x, axis=-1, keepdims=True)
        h_ref[...] = (x * lax.rsqrt(ms + EPS) * nw_ref[...]).astype(BF16)
        acc_ref[...] = jnp.zeros(acc_ref.shape, F32)

    prev = acc_ref[...]
    o_ref[...] = jnp.where(j <= N_GATE_TILES, jax.nn.sigmoid(prev), prev)
    acc_ref[...] = jnp.dot(h_ref[...], w_ref[...], preferred_element_type=F32) + b_ref[...]

    if phase_major:
        for part in range(3):
            for g, (_, dil) in enumerate(ATTN_GROUPS):
                ref = qkv_refs[part * 3 + g]

                @pl.when(j - 1 == C_Q // TILE + part * 3 + g)
                def _(ref=ref, dil=dil):
                    if dil == 1:
                        ref[0, 0] = o_ref[...].astype(BF16)
                    else:
                        for c in range(TILE // 128):
                            ph_ref[c] = o_ref[:, c * 128:(c + 1) * 128]
                        for p in range(dil):
                            for c in range(TILE // 128):
                                ref[0, p, :, c * 128:(c + 1) * 128] = (
                                    ph_ref[c, pl.ds(p, tm // dil, stride=dil), :].astype(BF16))


def _proj(x, nw, w_all, b_all, tm, nb=None, t=None):
    n = x.shape[0]
    phase_major = nb is not None
    nt = PW // TILE
    out_specs = [pl.BlockSpec((tm, TILE), lambda i, j: (i, jnp.maximum(j - 1, 0)))]
    out_shape = [jax.ShapeDtypeStruct((n, PW), F32)]
    scratch = [pltpu.VMEM((tm, D), BF16), pltpu.VMEM((tm, TILE), F32)]
    if phase_major:
        tpb = t // tm
        for _ in range(3):
            for _, dil in ATTN_GROUPS:
                out_specs.append(pl.BlockSpec((1, dil, tm // dil, GW), lambda i, j: (i // tpb, 0, i % tpb, 0)))
                out_shape.append(jax.ShapeDtypeStruct((nb, dil, t // dil, GW), BF16))
        scratch.append(pltpu.VMEM((TILE // 128, tm, 128), F32))
    return pl.pallas_call(
        functools.partial(_proj_kernel, tm=tm, phase_major=phase_major),
        grid=(n // tm, nt + 1),
        in_specs=[pl.BlockSpec((tm, D), lambda i, j: (i, 0)),
                  pl.BlockSpec((1, D), lambda i, j: (0, 0)),
                  pl.BlockSpec((D, TILE), lambda i, j: (0, jnp.minimum(j, nt - 1))),
                  pl.BlockSpec((1, TILE), lambda i, j: (0, jnp.minimum(j, nt - 1)))],
        out_specs=out_specs,
        out_shape=out_shape,
        scratch_shapes=scratch,
        compiler_params=_cparams(("parallel", "arbitrary")),
        name="proj",
    )(x, nw, w_all, b_all)


def _conv_step(x_ref, xp_ref, cw_ref, cb_ref, rows, out_rows):
    xp_ref[8:8 + rows, :] = x_ref[...]
    acc = cb_ref[...] + cw_ref[0:1, :] * xp_ref[5:5 + out_rows, :]
    for j in range(1, CONV_W):
        acc = acc + cw_ref[j:j + 1, :] * xp_ref[5 + j:5 + j + out_rows, :]
    tail = xp_ref[rows:rows + 8, :]
    xp_ref[0:8, :] = tail
    return acc


def _softplus(x):
    return jnp.maximum(x, 0.0) + jnp.log1p(jnp.exp(-jnp.abs(x)))


def _ssd_kernel(z_ref, xbc_ref, dt_ref, tail_ref, h0_ref, cw_ref, cb_ref, dtb_ref, alog_ref, dsk_ref, nw_ref,
                y_ref, hf_ref, xp_ref, act_ref, st_ref, ysc_ref, *, rows, n_chunks):
    c = pl.program_id(1)

    @pl.when(c == 0)
    def _():
        xp_ref[0:8, :] = tail_ref[0]
        st_ref[...] = h0_ref[0]

    if rows < CL:
        xp_ref[8 + rows:, :] = jnp.zeros((CL - rows, SSD_CONV_DIM), F32)
    conv = _conv_step(xbc_ref, xp_ref, cw_ref, cb_ref, rows, CL)
    act_ref[...] = conv * jax.nn.sigmoid(conv)

    row = lax.broadcasted_iota(jnp.int32, (CL, 128), 0)
    lane = lax.broadcasted_iota(jnp.int32, (CL, 128), 1)
    raw = dt_ref[...]
    if rows < CL:
        raw = jnp.concatenate([raw, jnp.zeros((CL - rows, 128), F32)], axis=0)
    dt = _softplus(raw + dtb_ref[...])
    dt = jnp.where((lane < SSD_HEADS) & (row < rows), dt, 0.0)
    da = dt * (-jnp.exp(alog_ref[...]))
    acs = da
    d = 1
    while d < CL:
        acs = acs + jnp.where(row >= d, pltpu.roll(acs, d, 0), 0.0)
        d *= 2
    acs_t = acs.T
    last = acs[CL - 1:CL, :]
    e_acs = jnp.exp(acs)
    to_end = jnp.exp(last - acs)
    cdec = jnp.exp(last)
    causal = row >= lane
    lo_lane = lane < SSD_HEAD_DIM
    lo_row = row < SSD_HEAD_DIM
    dsk = dsk_ref[...]

    def pair_cols(arr, h):
        return jnp.where(lo_lane, arr[:, h:h + 1], arr[:, h + 1:h + 2])

    nt = (((1,), (1,)), ((), ()))
    for g in range(2):
        bm = act_ref[:, SSD_INNER + g * SSD_STATE:SSD_INNER + (g + 1) * SSD_STATE].astype(BF16)
        cm = act_ref[:, SSD_INNER + 256 + g * SSD_STATE:SSD_INNER + 256 + (g + 1) * SSD_STATE].astype(BF16)
        cb = lax.dot_general(cm, bm, nt, preferred_element_type=F32)
        for pp in range(4):
            h = g * 8 + 2 * pp
            sl = slice(h * SSD_HEAD_DIM, h * SSD_HEAD_DIM + 128)
            xs = act_ref[:, sl]
            xdt = xs * pair_cols(dt, h)
            xdt_b = xdt.astype(BF16)
            ys = []
            for hh in (h, h + 1):
                seg = acs[:, hh:hh + 1] - acs_t[hh:hh + 1, :]
                decay = jnp.exp(jnp.where(causal, seg, -jnp.inf))
                ys.append(jnp.dot((cb * decay).astype(BF16), xdt_b, preferred_element_type=F32))
            y_diag = jnp.where(lo_lane, ys[0], ys[1])
            st = st_ref[sl, :]
            y_off = lax.dot_general(cm, st.astype(BF16), nt, preferred_element_type=F32) * pair_cols(e_acs, h)
            d_pair = jnp.where(lo_lane, dsk[:, h:h + 1], dsk[:, h + 1:h + 2])
            ysc_ref[:, sl] = y_diag + y_off + d_pair * xs
            xdte_t = (xdt * pair_cols(to_end, h)).T.astype(BF16)
            s_new = jnp.dot(xdte_t, bm, preferred_element_type=F32)
            dec = jnp.where(lo_row, cdec[:, h:h + 1], cdec[:, h + 1:h + 2])
            st_ref[sl, :] = dec * st + s_new

    zz = z_ref[...]
    yg = ysc_ref[0:rows, :] * (zz * jax.nn.sigmoid(zz))
    gw = SSD_INNER // 2
    for g in range(2):
        part = yg[:, g * gw:(g + 1) * gw]
        ms = jnp.mean(part * part, axis=-1, keepdims=True)
        y_ref[:, g * gw:(g + 1) * gw] = (part * lax.rsqrt(ms + EPS)
                                         * nw_ref[:, g * gw:(g + 1) * gw]).astype(y_ref.dtype)

    @pl.when(c == n_chunks - 1)
    def _():
        hf_ref[0] = st_ref[...]


def _ssd(P, tail, h0, lw, nb, t, rows):
    nc = t // rows
    kern = functools.partial(_ssd_kernel, rows=rows, n_chunks=nc)
    return pl.pallas_call(
        kern,
        grid=(nb, nc),
        in_specs=[pl.BlockSpec((rows, SSD_INNER), lambda b, c: (b * nc + c, C_Z // SSD_INNER)),
                  pl.BlockSpec((rows, SSD_CONV_DIM), lambda b, c: (b * nc + c, C_XBC // SSD_CONV_DIM)),
                  pl.BlockSpec((rows, 128), lambda b, c: (b * nc + c, C_DT // 128)),
                  pl.BlockSpec((1, 8, SSD_CONV_DIM), lambda b, c: (b, 0, 0)),
                  pl.BlockSpec((1, SSD_INNER, SSD_STATE), lambda b, c: (b, 0, 0)),
                  _full((CONV_W, SSD_CONV_DIM)), _full((1, SSD_CONV_DIM)),
                  _full((1, 128)), _full((1, 128)), _full((1, 128)), _full((1, SSD_INNER))],
        out_specs=[pl.BlockSpec((rows, SSD_INNER), lambda b, c: (b * nc + c, 0)),
                   pl.BlockSpec((1, SSD_INNER, SSD_STATE), lambda b, c: (b, 0, 0))],
        out_shape=[jax.ShapeDtypeStruct((nb * t, SSD_INNER), BF16 if rows % 16 == 0 else F32),
                   jax.ShapeDtypeStruct((nb, SSD_INNER, SSD_STATE), F32)],
        scratch_shapes=[pltpu.VMEM((8 + CL, SSD_CONV_DIM), F32),
                        pltpu.VMEM((CL, SSD_CONV_DIM), F32),
                        pltpu.VMEM((SSD_INNER, SSD_STATE), F32),
                        pltpu.VMEM((CL, SSD_INNER), F32)],
        compiler_params=_cparams(("parallel", "arbitrary")),
        name="ssd",
    )(P, P, P, tail, h0, lw['conv_ssd_w'], lw['conv_ssd_b'], lw['dt_bias'], lw['a_log'], lw['d_skip'], lw['ssd_norm_w'])


def _lru_kernel(xr_ref, gr_ref, tail_ref, h0_ref, cw_ref, cb_ref, wr_ref, br_ref, wi_ref, bi_ref, lam_ref,
                y_ref, hl_ref, xp_ref, h_ref, *, rows, n_chunks):
    c = pl.program_id(1)

    @pl.when(c == 0)
    def _():
        xp_ref[0:8, :] = tail_ref[0]
        h_ref[...] = h0_ref[0]

    x = _conv_step(xr_ref, xp_ref, cw_ref, cb_ref, rows, rows)
    xb = x.astype(BF16)
    rs, is_ = [], []
    for n in range(LRU_BLOCKS):
        blk = xb[:, n * 128:(n + 1) * 128]
        rs.append(jnp.dot(blk, wr_ref[0, n], preferred_element_type=F32))
        is_.append(jnp.dot(blk, wi_ref[0, n], preferred_element_type=F32))
    r_gate = jax.nn.sigmoid(jnp.concatenate(rs, axis=1) + br_ref[...])
    i_gate = jax.nn.sigmoid(jnp.concatenate(is_, axis=1) + bi_ref[...])
    log_a = -LRU_C * r_gate * _softplus(-lam_ref[...])
    a = jnp.exp(log_a)
    b = jnp.sqrt(1.0 - jnp.exp(2.0 * log_a)) * (i_gate * x)
    row = lax.broadcasted_iota(jnp.int32, (rows, LRU_W), 0)
    d = 1
    while d < rows:
        a_s = jnp.where(row >= d, pltpu.roll(a, d, 0), 1.0)
        b_s = jnp.where(row >= d, pltpu.roll(b, d, 0), 0.0)
        b = a * b_s + b
        a = a * a_s
        d *= 2
    h = b + a * h_ref[...]
    last = h[rows - 1:rows, :]
    h_ref[...] = last
    y_ref[...] = (h * jax.nn.gelu(gr_ref[...])).astype(y_ref.dtype)

    @pl.when(c == n_chunks - 1)
    def _():
        hl_ref[0] = last


def _lru(P, tail, h0, lw, sw, layer, nb, t, rows):
    nc = t // rows
    kern = functools.partial(_lru_kernel, rows=rows, n_chunks=nc)
    y_dtype = BF16 if rows % 16 == 0 else F32
    return pl.pallas_call(
        kern,
        grid=(nb, nc),
        in_specs=[pl.BlockSpec((rows, LRU_W), lambda b, c: (b * nc + c, C_XR // LRU_W)),
                  pl.BlockSpec((rows, LRU_W), lambda b, c: (b * nc + c, C_GR // LRU_W)),
                  pl.BlockSpec((1, 8, LRU_W), lambda b, c: (b, 0, 0)),
                  pl.BlockSpec((1, 1, LRU_W), lambda b, c: (b, 0, 0)),
                  _full((CONV_W, LRU_W)), _full((1, LRU_W)),
                  _of_layer((LRU_BLOCKS, 128, 128), layer), _full((1, LRU_W)),
                  _of_layer((LRU_BLOCKS, 128, 128), layer), _full((1, LRU_W)), _full((1, LRU_W))],
        out_specs=[pl.BlockSpec((rows, LRU_W), lambda b, c: (b * nc + c, 0)),
                   pl.BlockSpec((1, 1, LRU_W), lambda b, c: (b, 0, 0))],
        out_shape=[jax.ShapeDtypeStruct((nb * t, LRU_W), y_dtype),
                   jax.ShapeDtypeStruct((nb, 1, LRU_W), F32)],
        scratch_shapes=[pltpu.VMEM((8 + rows, LRU_W), F32), pltpu.VMEM((1, LRU_W), F32)],
        compiler_params=_cparams(("parallel", "arbitrary")),
        name="lru",
    )(P, P, tail, h0, lw['conv_lru_w'], lw['conv_lru_b'], sw['lru_wr'], lw['lru_br'], sw['lru_wi'], lw['lru_bi'],
      lw['lru_lambda'])


def _attn_kernel(q_ref, kp_ref, vp_ref, kc_ref, vc_ref, ba_ref, bb_ref, o_ref, lse_ref, *, sub):
    scale = HD ** -0.5
    nt = (((1,), (1,)), ((), ()))
    lane = lax.broadcasted_iota(jnp.int32, (CL, 128), 1)
    for s in range(sub):
        rows = slice(s * CL, (s + 1) * CL)
        before = slice((s - 1) * CL, s * CL)
        lse_all = jnp.zeros((CL, 128), F32)
        for h in range(HPG):
            sl = slice(h * HD, (h + 1) * HD)
            q = q_ref[0, 0, rows, sl]
            k_before = kp_ref[0, 0, :, sl] if s == 0 else kc_ref[0, 0, before, sl]
            v_before = vp_ref[0, 0, :, sl] if s == 0 else vc_ref[0, 0, before, sl]
            sa = lax.dot_general(q, k_before, nt, preferred_element_type=F32) * scale + ba_ref[h]
            if s == 0:
                sa = jnp.where(pl.program_id(2) > 0, sa, NEG)
            sb = lax.dot_general(q, kc_ref[0, 0, rows, sl], nt, preferred_element_type=F32) * scale + bb_ref[h]
            m = jnp.maximum(jnp.max(sa, axis=-1, keepdims=True), jnp.max(sb, axis=-1, keepdims=True))
            pa = jnp.exp(sa - m)
            pb = jnp.exp(sb - m)
            l = jnp.sum(pa, axis=-1, keepdims=True) + jnp.sum(pb, axis=-1, keepdims=True)
            o = (jnp.dot(pa.astype(BF16), v_before, preferred_element_type=F32)
                 + jnp.dot(pb.astype(BF16), vc_ref[0, 0, rows, sl], preferred_element_type=F32))
            o_ref[rows, sl] = o / l
            lse_all = jnp.where(lane == h, m + jnp.log(l), lse_all)
        lse_ref[rows, :] = lse_all


def _attn_prompt(q, k, v, bias_a, bias_b, dil, nb, t):
    n = nb * t
    sub = min(4, t // dil // CL)
    nstep = t // dil // (sub * CL)
    cur = pl.BlockSpec((1, 1, sub * CL, GW), lambda b, p, i: (b, p, i, 0))
    prev = pl.BlockSpec((1, 1, CL, GW), lambda b, p, i: (b, p, jnp.maximum(i * sub - 1, 0), 0))
    o, lse = pl.pallas_call(
        functools.partial(_attn_kernel, sub=sub),
        grid=(nb, dil, nstep),
        in_specs=[cur, prev, prev, cur, cur, _full((HPG, CL, CL)), _full((HPG, CL, CL))],
        out_specs=[pl.BlockSpec((sub * CL, GW), lambda b, p, i: (b * nstep + i, p)),
                   pl.BlockSpec((sub * CL, 128), lambda b, p, i: (b * nstep + i, p))],
        out_shape=[jax.ShapeDtypeStruct((n // dil, dil * GW), F32),
                   jax.ShapeDtypeStruct((n // dil, dil * 128), F32)],
        compiler_params=_cparams(("parallel", "parallel", "arbitrary")),
        name=f"attn_prompt_d{dil}",
    )(q, k, v, k, v, bias_a, bias_b)
    return o.reshape(n, GW), lse.reshape(n, 128)


def _attn_sample_kernel(q_ref, kv_ref, kb_ref, vb_ref, ba_ref, bb_ref, o_ref, lse_ref):
    scale = HD ** -0.5
    nt = (((1,), (1,)), ((), ()))
    rows = o_ref.shape[0]
    kv = kv_ref[...].astype(BF16)
    lane = lax.broadcasted_iota(jnp.int32, (rows, 128), 1)
    lse_all = jnp.zeros((rows, 128), F32)
    for h in range(HPG):
        sl = slice(h * HD, (h + 1) * HD)
        q = q_ref[:, sl].astype(BF16)
        sa = lax.dot_general(q, kv, nt, preferred_element_type=F32) * scale + ba_ref[h]
        sb = lax.dot_general(q, kb_ref[:, sl].astype(BF16), nt, preferred_element_type=F32) * scale + bb_ref[h]
        m = jnp.maximum(jnp.max(sa, axis=-1, keepdims=True), jnp.max(sb, axis=-1, keepdims=True))
        pa = jnp.exp(sa - m)
        pb = jnp.exp(sb - m)
        l = jnp.sum(pa, axis=-1, keepdims=True) + jnp.sum(pb, axis=-1, keepdims=True)
        pa_v = pltpu.roll(pa, HPG, 1)
        o = (jnp.dot(pa_v.astype(BF16), kv, preferred_element_type=F32)
             + jnp.dot(pb.astype(BF16), vb_ref[:, sl].astype(BF16), preferred_element_type=F32))
        o_ref[:, sl] = o / l
        lse_all = jnp.where(lane == h, m + jnp.log(l), lse_all)
    lse_ref[...] = lse_all


def _attn_sample(Ps, cache_rows, bias_a, bias_b, g, layer, nb, t, window):
    tq, tk, tv = C_Q // GW + g, C_K // GW + g, C_V // GW + g
    out_spec = pl.BlockSpec((t, GW), lambda b: (b, 0))
    return pl.pallas_call(
        _attn_sample_kernel,
        grid=(nb,),
        in_specs=[pl.BlockSpec((t, GW), lambda b: (b, tq)),
                  pl.BlockSpec((window * 2 * HPG, HD), lambda b: (layer * nb + b, 0)),
                  pl.BlockSpec((t, GW), lambda b: (b, tk)),
                  pl.BlockSpec((t, GW), lambda b: (b, tv)),
                  _full((HPG, t, window * 2 * HPG)), _full((HPG, t, t))],
        out_specs=[out_spec, pl.BlockSpec((t, 128), lambda b: (b, 0))],
        out_shape=[jax.ShapeDtypeStruct((nb * t, GW), F32), jax.ShapeDtypeStruct((nb * t, 128), F32)],
        compiler_params=_cparams(("parallel",)),
        name=f"attn_sample_w{window}",
    )(Ps, cache_rows, Ps, Ps, bias_a, bias_b)


def _mix_kernel(gs_ref, gl_ref, ga_ref, ys_ref, yl_ref, o0_ref, o1_ref, o2_ref, l0_ref, l1_ref, l2_ref,
                wbs_ref, wbl_ref, wba_ref, out_ref):
    l0, l1, l2 = l0_ref[...], l1_ref[...], l2_ref[...]
    m = jnp.maximum(jnp.maximum(l0, l1), l2)
    e0, e1, e2 = jnp.exp(l0 - m), jnp.exp(l1 - m), jnp.exp(l2 - m)
    den = e0 + e1 + e2
    w0, w1, w2 = e0 / den, e1 / den, e2 / den
    tm = out_ref.shape[0]
    heads = []
    for h in range(HPG):
        sl = slice(h * HD, (h + 1) * HD)
        per_head = lambda w: jnp.broadcast_to(w[:, h:h + 1], (tm, HD))
        heads.append(o0_ref[:, sl] * per_head(w0) + o1_ref[:, sl] * per_head(w1) + o2_ref[:, sl] * per_head(w2))
    ya = jnp.concatenate(heads, axis=1)
    mixed = (gs_ref[...] * jnp.dot(ys_ref[...].astype(BF16), wbs_ref[0], preferred_element_type=F32)
             + gl_ref[...] * jnp.dot(yl_ref[...].astype(BF16), wbl_ref[0], preferred_element_type=F32)
             + ga_ref[...] * jnp.dot(ya.astype(BF16), wba_ref[0], preferred_element_type=F32))
    out_ref[...] = mixed.astype(BF16)


def _mix(P, y_ssd, y_lru, attn, sw, layer, tm):
    n = P.shape[0]
    row = lambda w: pl.BlockSpec((tm, w), lambda i: (i, 0))
    (o0, s0), (o1, s1), (o2, s2) = attn
    return pl.pallas_call(
        _mix_kernel,
        grid=(n // tm,),
        in_specs=[pl.BlockSpec((tm, D), lambda i: (i, 0)), pl.BlockSpec((tm, D), lambda i: (i, 1)),
                  pl.BlockSpec((tm, D), lambda i: (i, 2)),
                  row(SSD_INNER), row(LRU_W), row(GW), row(GW), row(GW), row(128), row(128), row(128),
                  _of_layer((SSD_INNER, D), layer), _of_layer((LRU_W, D), layer), _of_layer((GW, D), layer)],
        out_specs=row(D),
        out_shape=jax.ShapeDtypeStruct((n, D), BF16),
        compiler_params=_cparams(("parallel",)),
        name="mix",
    )(P, P, P, y_ssd, y_lru, o0, o1, o2, s0, s1, s2, sw['w_br_ssd'], sw['w_br_lru'], sw['w_br_attn'])


def _res_kernel(x_ref, mixed_ref, wo_ref, n2_ref, wr_ref, br_ref, x1_ref, h2_ref, comb_ref, *rest, dispatch):
    x1 = x_ref[...] + jnp.dot(mixed_ref[...], wo_ref[0], preferred_element_type=F32)
    x1_ref[...] = x1
    ms = jnp.mean(x1 * x1, axis=-1, keepdims=True)
    h2 = x1 * lax.rsqrt(ms + EPS) * n2_ref[...]
    h2b = h2.astype(BF16)
    h2_ref[...] = h2b.astype(h2_ref.dtype)
    logits = jnp.dot(h2b, wr_ref[...], preferred_element_type=F32) + br_ref[...]
    lane = lax.broadcasted_iota(jnp.int32, logits.shape, 1).astype(F32)
    big = float(ROUTE_LANES)

    def first_max(vals, ok):
        v = jnp.where(ok, vals, NEG)
        top = jnp.max(v, axis=-1, keepdims=True)
        idx = jnp.min(jnp.where(ok & (v == top), lane, big), axis=-1, keepdims=True)
        return top, idx

    is_g = lane < N_GROUPS
    gmax, gsel = first_max(logits, is_g)
    gp = 1.0 / jnp.sum(jnp.where(is_g, jnp.exp(logits - gmax), 0.0), axis=-1, keepdims=True)
    lo = N_GROUPS + PER_GROUP * gsel
    is_e = (lane >= lo) & (lane < lo + PER_GROUP)
    t1, i1 = first_max(logits, is_e)
    t2, i2 = first_max(logits, is_e & (lane != i1))
    e2 = jnp.exp(t2 - t1)
    w1 = gp / (1.0 + e2)
    w2 = gp * e2 / (1.0 + e2)
    comb_ref[...] = jnp.where(lane == i1, w1, 0.0) + jnp.where(lane == i2, w2, 0.0)

    if dispatch:
        route_ref, cnt_ref, carry_ref = rest

        @pl.when(pl.program_id(0) == 0)
        def _():
            carry_ref[...] = jnp.zeros(carry_ref.shape, F32)

        tm = x1.shape[0]
        onehot = jnp.where((lane == i1) | (lane == i2), 1.0, 0.0)
        r = lax.broadcasted_iota(jnp.int32, (tm, tm), 0)
        c = lax.broadcasted_iota(jnp.int32, (tm, tm), 1)
        earlier = jnp.where(r > c, 1.0, 0.0).astype(BF16)
        before = jnp.dot(earlier, onehot.astype(BF16), preferred_element_type=F32) + carry_ref[...]
        rank1 = jnp.sum(jnp.where(lane == i1, before, 0.0), axis=-1, keepdims=True)
        rank2 = jnp.sum(jnp.where(lane == i2, before, 0.0), axis=-1, keepdims=True)
        carry_ref[...] += jnp.sum(onehot, axis=0, keepdims=True)
        cnt_ref[...] = carry_ref[...]
        fields = (i1 - N_GROUPS, i2 - N_GROUPS, w1, w2, rank1, rank2)
        route = jnp.zeros(logits.shape, F32)
        for k, val in enumerate(fields):
            route = jnp.where(lane == k, val, route)
        route_ref[...] = route


def _res(x, mixed, lw, sw, layer, tm, dispatch):
    n = x.shape[0]
    out_specs = [pl.BlockSpec((tm, D), lambda i: (i, 0)), pl.BlockSpec((tm, D), lambda i: (i, 0)),
                 pl.BlockSpec((tm, ROUTE_LANES), lambda i: (i, 0))]
    out_shape = [jax.ShapeDtypeStruct((n, D), F32), jax.ShapeDtypeStruct((n, D), F32 if dispatch else BF16),
                 jax.ShapeDtypeStruct((n, ROUTE_LANES), F32)]
    scratch = []
    if dispatch:
        out_specs += [pl.BlockSpec((tm, ROUTE_LANES), lambda i: (i, 0)), _full((1, ROUTE_LANES))]
        out_shape += [jax.ShapeDtypeStruct((n, ROUTE_LANES), F32), jax.ShapeDtypeStruct((1, ROUTE_LANES), F32)]
        scratch = [pltpu.VMEM((1, ROUTE_LANES), F32)]
    return pl.pallas_call(
        functools.partial(_res_kernel, dispatch=dispatch),
        grid=(n // tm,),
        in_specs=[pl.BlockSpec((tm, D), lambda i: (i, 0)), pl.BlockSpec((tm, D), lambda i: (i, 0)),
                  _of_layer((D, D), layer), _full((1, D)), _full((D, ROUTE_LANES)), _full((1, ROUTE_LANES))],
        out_specs=out_specs,
        out_shape=out_shape,
        scratch_shapes=scratch,
        compiler_params=_cparams(("arbitrary",)),
        name="res_router",
    )(x, mixed, sw['w_o'], lw['norm2'], lw['w_router'], lw['b_router'])


def _moe_kernel(h2_ref, comb_ref, x1_ref, w1_ref, w3_ref, w2_ref, fn_ref, o_ref, *, final):
    e = pl.program_id(1)

    @pl.when(e == 0)
    def _():
        o_ref[...] = x1_ref[...]

    h = h2_ref[...]
    a = jnp.dot(h, w1_ref[0, 0], preferred_element_type=F32)
    b = jnp.dot(h, w3_ref[0, 0], preferred_element_type=F32)
    comb = comb_ref[...]
    lane = lax.broadcasted_iota(jnp.int32, comb.shape, 1)
    w = jnp.sum(jnp.where(lane == e + N_GROUPS, comb, 0.0), axis=-1, keepdims=True)
    act = (a * jax.nn.sigmoid(a)) * b * w
    o_ref[...] += jnp.dot(act.astype(BF16), w2_ref[0, 0], preferred_element_type=F32)

    if final:
        @pl.when(e == N_EXPERTS - 1)
        def _():
            x = o_ref[...]
            ms = jnp.mean(x * x, axis=-1, keepdims=True)
            o_ref[...] = x * lax.rsqrt(ms + EPS) * fn_ref[...]


def _moe(h2, comb, x1, sw, layer, final_norm, tm, final):
    n = x1.shape[0]
    return pl.pallas_call(
        functools.partial(_moe_kernel, final=final),
        grid=(n // tm, N_EXPERTS),
        in_specs=[pl.BlockSpec((tm, D), lambda i, e: (i, 0)),
                  pl.BlockSpec((tm, ROUTE_LANES), lambda i, e: (i, 0)),
                  pl.BlockSpec((tm, D), lambda i, e: (i, 0)),
                  pl.BlockSpec((1, 1, D, D_EXPERT), lambda i, e: (layer, e, 0, 0)),
                  pl.BlockSpec((1, 1, D, D_EXPERT), lambda i, e: (layer, e, 0, 0)),
                  pl.BlockSpec((1, 1, D_EXPERT, D), lambda i, e: (layer, e, 0, 0)),
                  pl.BlockSpec((1, D), lambda i, e: (0, 0))],
        out_specs=pl.BlockSpec((tm, D), lambda i, e: (i, 0)),
        out_shape=jax.ShapeDtypeStruct((n, D), F32),
        compiler_params=_cparams(("parallel", "arbitrary")),
        name="moe",
    )(h2, comb, x1, sw['w1'], sw['w3'], sw['w2'], final_norm)


FFN_TM = 256
SC_CORES = 2
SC_SUBCORES = 16
SC_WORKERS = SC_CORES * SC_SUBCORES
SC_CHUNK = 16


def _sc_mesh():
    return plsc.VectorSubcoreMesh(core_axis_name="c", subcore_axis_name="s", num_cores=SC_CORES,
                                  num_subcores=SC_SUBCORES)


def _sc_dispatch(x, dest1, dest2, n_sorted):
    n = x.shape[0]
    per_w = n // SC_WORKERS

    @functools.partial(
        pl.kernel, mesh=_sc_mesh(),
        out_type=jax.ShapeDtypeStruct((n_sorted, D), F32),
        scratch_types=[pltpu.VMEM((SC_CHUNK,), jnp.int32), pltpu.VMEM((SC_CHUNK,), jnp.int32),
                       pltpu.VMEM((SC_CHUNK, D), F32)],
    )
    def k(x_hbm, d1_hbm, d2_hbm, out_hbm, i1_v, i2_v, rows_v):
        base = (lax.axis_index("s") * 2 + lax.axis_index("c")) * per_w

        @pl.loop(0, per_w // SC_CHUNK)
        def _(j):
            off = base + j * SC_CHUNK
            pltpu.sync_copy(d1_hbm.at[pl.ds(off, SC_CHUNK)], i1_v)
            pltpu.sync_copy(d2_hbm.at[pl.ds(off, SC_CHUNK)], i2_v)
            pltpu.sync_copy(x_hbm.at[pl.ds(off, SC_CHUNK)], rows_v)
            pltpu.sync_copy(rows_v, out_hbm.at[i1_v])
            pltpu.sync_copy(rows_v, out_hbm.at[i2_v])

    return k(x, dest1, dest2)


def _sc_collect(y, dest1, dest2, n):
    per_w = n // SC_WORKERS

    @functools.partial(
        pl.kernel, mesh=_sc_mesh(),
        out_type=(jax.ShapeDtypeStruct((n, D), F32), jax.ShapeDtypeStruct((n, D), F32)),
        scratch_types=[pltpu.VMEM((SC_CHUNK,), jnp.int32), pltpu.VMEM((SC_CHUNK, D), F32)],
    )
    def k(y_hbm, d1_hbm, d2_hbm, g1_hbm, g2_hbm, idx_v, rows_v):
        base = (lax.axis_index("s") * 2 + lax.axis_index("c")) * per_w

        @pl.loop(0, per_w // SC_CHUNK)
        def _(j):
            off = base + j * SC_CHUNK
            for d_hbm, g_hbm in ((d1_hbm, g1_hbm), (d2_hbm, g2_hbm)):
                pltpu.sync_copy(d_hbm.at[pl.ds(off, SC_CHUNK)], idx_v)
                pltpu.sync_copy(y_hbm.at[idx_v], rows_v)
                pltpu.sync_copy(rows_v, g_hbm.at[pl.ds(off, SC_CHUNK)])

    return k(y, dest1, dest2)


def _ffn_kernel(te_ref, nt_ref, xs_ref, w1_ref, w3_ref, w2_ref, y_ref):
    del te_ref

    @pl.when(pl.program_id(0) < nt_ref[0])
    def _():
        xs = xs_ref[...].astype(BF16)
        a = jnp.dot(xs, w1_ref[0, 0], preferred_element_type=F32)
        b = jnp.dot(xs, w3_ref[0, 0], preferred_element_type=F32)
        act = (a * jax.nn.sigmoid(a)) * b
        y_ref[...] = jnp.dot(act.astype(BF16), w2_ref[0, 0], preferred_element_type=F32)


def _ffn(xs, tile_expert, n_tiles_used, sw, layer):
    n_sorted = xs.shape[0]
    w_in_spec = pl.BlockSpec((1, 1, D, D_EXPERT), lambda k, te, nt: (layer, te[k], 0, 0))
    return pl.pallas_call(
        _ffn_kernel,
        grid_spec=pltpu.PrefetchScalarGridSpec(
            num_scalar_prefetch=2,
            grid=(n_sorted // FFN_TM,),
            in_specs=[pl.BlockSpec((FFN_TM, D), lambda k, te, nt: (k, 0)), w_in_spec, w_in_spec,
                      pl.BlockSpec((1, 1, D_EXPERT, D), lambda k, te, nt: (layer, te[k], 0, 0))],
            out_specs=pl.BlockSpec((FFN_TM, D), lambda k, te, nt: (k, 0)),
        ),
        out_shape=jax.ShapeDtypeStruct((n_sorted, D), F32),
        compiler_params=_cparams(("arbitrary",)),
        name="ffn",
    )(tile_expert, n_tiles_used, xs, sw['w1'], sw['w3'], sw['w2'])


def _combine_kernel(x1_ref, g1_ref, g2_ref, route_ref, fn_ref, o_ref, *, final):
    route = route_ref[...]
    x = x1_ref[...] + route[:, 2:3] * g1_ref[...] + route[:, 3:4] * g2_ref[...]
    if final:
        ms = jnp.mean(x * x, axis=-1, keepdims=True)
        x = x * lax.rsqrt(ms + EPS) * fn_ref[...]
    o_ref[...] = x


def _combine(x1, g1, g2, route, final_norm, tm, final):
    n = x1.shape[0]
    row = pl.BlockSpec((tm, D), lambda i: (i, 0))
    return pl.pallas_call(
        functools.partial(_combine_kernel, final=final),
        grid=(n // tm,),
        in_specs=[row, row, row, pl.BlockSpec((tm, ROUTE_LANES), lambda i: (i, 0)), _full((1, D))],
        out_specs=row,
        out_shape=jax.ShapeDtypeStruct((n, D), F32),
        compiler_params=_cparams(("parallel",)),
        name="combine",
    )(x1, g1, g2, route, final_norm)


def _moe_sparse(h2, route, counts, x1, sw, layer, final_norm, final):
    n = x1.shape[0]
    n_sorted = 2 * n + N_EXPERTS * FFN_TM
    e1, e2 = route[:, 0].astype(jnp.int32), route[:, 1].astype(jnp.int32)
    rank1, rank2 = route[:, 4].astype(jnp.int32), route[:, 5].astype(jnp.int32)
    cnt = counts[0, N_GROUPS:N_GROUPS + N_EXPERTS].astype(jnp.int32)
    tiles = (cnt + FFN_TM - 1) // FFN_TM
    tile_end = jnp.cumsum(tiles)
    seg_start = (tile_end - tiles) * FFN_TM
    dest1 = jnp.take(seg_start, e1) + rank1
    dest2 = jnp.take(seg_start, e2) + rank2
    tile_ids = jnp.arange(n_sorted // FFN_TM, dtype=jnp.int32)
    tile_expert = jnp.minimum(jnp.sum(tile_end[None, :] <= tile_ids[:, None], axis=1), N_EXPERTS - 1)
    xs = _sc_dispatch(h2, dest1, dest2, n_sorted)
    y = _ffn(xs, tile_expert.astype(jnp.int32), tile_end[-1:].astype(jnp.int32), sw, layer)
    g1, g2 = _sc_collect(y, dest1, dest2, n)
    return _combine(x1, g1, g2, route, final_norm, 512, final)


def _t5_buckets(dist):
    max_exact = T5_BUCKETS // 2
    large = max_exact + (np.log(np.maximum(dist, 1) / max_exact) / np.log(T5_MAX_DIST / max_exact)
                         * (T5_BUCKETS - max_exact)).astype(np.int32)
    large = np.minimum(large, T5_BUCKETS - 1)
    return np.where(dist < max_exact, dist, large).astype(np.int32)


def _bias_tables(t5, g, dil, window, t_sample):
    nk = window // dil + 1
    hs = slice(g * HPG, (g + 1) * HPG)
    bias = t5[_t5_buckets(np.arange(nk) * dil)][:, hs].T
    rev = bias[:, ::-1]
    neg = lambda *shape: jnp.full(shape, NEG, F32)

    vec = jnp.concatenate([rev, neg(HPG, CL)], axis=1)
    both = jnp.tile(vec, (1, CL + 1))[:, :CL * 2 * CL].reshape(HPG, CL, 2 * CL)
    prev_t, cur_t = both[:, :, :CL], both[:, :, CL:]

    rows = []
    for r in range(t_sample):
        shift = r // dil
        per_u = jnp.concatenate([neg(HPG, shift), rev[:, :nk - 1 - shift]], axis=1)
        on_phase = (np.arange(dil) == r % dil)[None, None, :]
        rows.append(jnp.where(on_phase, per_u[:, :, None], NEG).reshape(HPG, window))
    cache_t = jnp.stack(rows, axis=1)
    own_k = np.eye(HPG, 2 * HPG, dtype=bool)[:, None, None, :]
    cache_t = jnp.where(own_k, cache_t[..., None], NEG).reshape(HPG, t_sample, window * 2 * HPG)
    r = np.arange(t_sample)[:, None]
    c = np.arange(t_sample)[None, :]
    ok = ((r - c) % dil == 0) & (r >= c)
    new_t = jnp.where(jnp.asarray(ok)[None], bias[:, np.clip((r - c) // dil, 0, nk - 1)], NEG)
    return prev_t, cur_t, cache_t, new_t


def _layer_weights(l, norm1, w_in, conv_ssd_w, conv_ssd_b, ssd_dt_bias, ssd_a_log, ssd_d, ssd_norm_w,
                   conv_lru_w, conv_lru_b, lru_br, lru_bi, lru_lambda, w_gate, b_gate, norm2,
                   w_router_group, b_router_group, w_router_expert, b_router_expert):
    wi = w_in[l]
    w_all = jnp.concatenate(
        [w_gate[l], wi[:, 0:1024], wi[:, 2576:3600], wi[:, 3600:4624], wi[:, 1024:2560], wi[:, 4624:6160],
         wi[:, 6160:7696], wi[:, 7696:9232], wi[:, 2560:2576], jnp.zeros((D, TILE - SSD_HEADS), F32)],
        axis=1).astype(BF16)
    b_all = jnp.concatenate([b_gate[l], jnp.zeros((PW - 3 * D,), F32)])[None]

    def pad128(v):
        return jnp.concatenate([v, jnp.zeros((128 - v.shape[0],), F32)])[None]

    return {
        'norm1': norm1[l][None], 'w_all': w_all, 'b_all': b_all,
        'conv_ssd_w': conv_ssd_w[l], 'conv_ssd_b': conv_ssd_b[l][None],
        'dt_bias': pad128(ssd_dt_bias[l]), 'a_log': pad128(ssd_a_log[l]), 'd_skip': pad128(ssd_d[l]),
        'ssd_norm_w': ssd_norm_w[l][None],
        'conv_lru_w': conv_lru_w[l], 'conv_lru_b': conv_lru_b[l][None],
        'lru_br': lru_br[l][None], 'lru_bi': lru_bi[l][None], 'lru_lambda': lru_lambda[l][None],
        'norm2': norm2[l][None],
        'w_router': jnp.concatenate([w_router_group[l], w_router_expert[l],
                                     jnp.zeros((D, ROUTE_LANES - N_GROUPS - N_EXPERTS), F32)],
                                    axis=1).astype(BF16),
        'b_router': pad128(jnp.concatenate([b_router_group[l], b_router_expert[l]])),
    }


def _front_pad(buf):
    return jnp.pad(buf, ((0, 0), (8 - (CONV_W - 1), 0), (0, 0)))


def _cols(P, nb, t, start, width):
    return P.reshape(nb, t, PW)[:, :, start:start + width]


def _kv_rows(P, nb, t, g, n_rows):
    k = _cols(P, nb, t, C_K + g * GW, GW)[:, t - n_rows:].reshape(nb, n_rows, HPG, HD)
    v = _cols(P, nb, t, C_V + g * GW, GW)[:, t - n_rows:].reshape(nb, n_rows, HPG, HD)
    return jnp.stack([k, v], axis=2)


def _layer(x, lw, sw, tables, layer, nb, t, rows, tm, tm_mix, tm_res, tm_moe, conv_ssd, st_ssd, conv_lru, st_lru,
           caches, final_norm, final):
    if caches is None:
        P, *qkv = _proj(x, lw['norm1'], lw['w_all'], lw['b_all'], tm, nb, t)
    else:
        (P,) = _proj(x, lw['norm1'], lw['w_all'], lw['b_all'], tm)
    y_ssd, h_ssd = _ssd(P, _front_pad(conv_ssd), st_ssd.reshape(nb, SSD_INNER, SSD_STATE), lw, nb, t, rows)
    y_lru, h_lru = _lru(P, _front_pad(conv_lru), st_lru.reshape(nb, 1, LRU_W), lw, sw, layer, nb, t, rows)
    attn = []
    for g, (window, dil) in enumerate(ATTN_GROUPS):
        prev_t, cur_t, cache_t, new_t = tables[g]
        if caches is None:
            attn.append(_attn_prompt(qkv[g], qkv[3 + g], qkv[6 + g], prev_t, cur_t, dil, nb, t))
        else:
            attn.append(_attn_sample(P, caches[g], cache_t, new_t, g, layer, nb, t, window))
    mixed = _mix(P, y_ssd, y_lru, attn, sw, layer, tm_mix)
    if caches is None:
        x1, h2, _, route, counts = _res(x, mixed, lw, sw, layer, tm_res, True)
        x2 = _moe_sparse(h2, route, counts, x1, sw, layer, final_norm, final)
    else:
        x1, h2, comb = _res(x, mixed, lw, sw, layer, tm_res, False)
        x2 = _moe(h2, comb, x1, sw, layer, final_norm, tm_moe, final)
    states = (_cols(P, nb, t, C_XBC, SSD_CONV_DIM)[:, t - 3:],
              h_ssd.reshape(nb, SSD_HEADS, SSD_HEAD_DIM, SSD_STATE),
              _cols(P, nb, t, C_XR, LRU_W)[:, t - 3:],
              h_lru.reshape(nb, LRU_W)) + tuple(
                  _kv_rows(P, nb, t, g, min(w, t)) for g, (w, _) in enumerate(ATTN_GROUPS))
    return x2, states


def kernel(x_prompt, x_sample, cache_conv_ssd, state_ssd, cache_conv_lru, state_lru, cache_kv_w128, cache_kv_w512, cache_kv_w2048, norm1, w_in, conv_ssd_w, conv_ssd_b, ssd_dt_bias, ssd_a_log, ssd_d, ssd_norm_w, conv_lru_w, conv_lru_b, lru_wr, lru_br, lru_wi, lru_bi, lru_lambda, t5_bias, w_br_ssd, w_br_lru, w_br_attn, w_gate, b_gate, w_o, norm2, w_router_group, b_router_group, w_router_expert, b_router_expert, w1, w3, w2, final_norm):
    bp, tp, _ = x_prompt.shape
    bs, ts, _ = x_sample.shape
    xp = x_prompt.reshape(bp * tp, D)
    xs = x_sample.reshape(bs * ts, D)
    fn = final_norm[None]
    tables = [_bias_tables(t5_bias, g, dil, window, ts) for g, (window, dil) in enumerate(ATTN_GROUPS)]
    caches = [c.reshape(-1, HD) for c in (cache_kv_w128, cache_kv_w512, cache_kv_w2048)]
    sw = {name: w.astype(BF16) for name, w in dict(
        lru_wr=lru_wr, lru_wi=lru_wi, w_br_ssd=w_br_ssd, w_br_lru=w_br_lru, w_br_attn=w_br_attn, w_o=w_o,
        w1=w1, w3=w3, w2=w2).items()}
    outs_p, outs_s = [], []
    for l in range(DEPTH):
        lw = _layer_weights(l, norm1, w_in, conv_ssd_w, conv_ssd_b, ssd_dt_bias, ssd_a_log, ssd_d, ssd_norm_w,
                            conv_lru_w, conv_lru_b, lru_br, lru_bi, lru_lambda, w_gate, b_gate, norm2,
                            w_router_group, b_router_group, w_router_expert, b_router_expert)
        final = l == DEPTH - 1
        xp, sp = _layer(xp, lw, sw, tables, l, bp, tp, CL, 1024, 256, 512, 512,
                        jnp.zeros((bp, CONV_W - 1, SSD_CONV_DIM), F32),
                        jnp.zeros((bp, SSD_HEADS, SSD_HEAD_DIM, SSD_STATE), F32),
                        jnp.zeros((bp, CONV_W - 1, LRU_W), F32), jnp.zeros((bp, LRU_W), F32),
                        None, fn, final)
        xs, ss = _layer(xs, lw, sw, tables, l, bs, ts, ts, bs * ts, bs * ts, bs * ts, bs * ts,
                        cache_conv_ssd[l], state_ssd[l], cache_conv_lru[l], state_lru[l],
                        caches, fn, final)
        outs_p.append(sp)
        outs_s.append(ss)

    def stk(outs, i):
        return jnp.stack([o[i] for o in outs], axis=0)

    return ((xp.reshape(bp, tp, D), xs.reshape(bs, ts, D))
            + tuple(stk(outs_p, i) for i in range(7)) + tuple(stk(outs_s, i) for i in range(7)))
```

```python
import functools

import numpy as np
import jax
import jax.numpy as jnp
from jax import lax
from jax.experimental import pallas as pl
from jax.experimental.pallas import tpu as pltpu
from jax.experimental.pallas import tpu_sc as plsc

F32 = jnp.float32
BF16 = jnp.bfloat16
EPS = 1e-6
NEG = -1e30

D = 2048
DEPTH = 2
PAST_LEN = 16384
CL = 128
CONV_W = 4
SSD_HEADS = 16
SSD_HEAD_DIM = 64
SSD_INNER = 1024
SSD_STATE = 128
SSD_CONV_DIM = 1536
LRU_W = 1024
LRU_BLOCKS = 8
LRU_C = 8.0
ATTN_GROUPS = ((128, 1), (512, 4), (2048, 16))
HPG = 4
HD = 128
GW = HPG * HD
T5_BUCKETS = 32
T5_MAX_DIST = 2048
N_GROUPS = 4
PER_GROUP = 4
N_EXPERTS = 16
D_EXPERT = 512

TILE = 512
C_GATE = 0
C_Z = 6144
C_XR = 7168
C_GR = 8192
C_XBC = 9216
C_Q = 10752
C_K = 12288
C_V = 13824
C_DT = 15360
PW = 15872
N_GATE_TILES = (3 * D) // TILE
ROUTE_LANES = 128
VMEM_LIMIT = 56 * 1024 * 1024


def _cparams(sem):
    return pltpu.CompilerParams(dimension_semantics=sem, vmem_limit_bytes=VMEM_LIMIT)


def _full(shape):
    nd = len(shape)
    return pl.BlockSpec(shape, lambda *_: (0,) * nd)


def _of_layer(shape, layer):
    nd = len(shape)
    return pl.BlockSpec((1,) + shape, lambda *_: (layer,) + (0,) * nd)


_W_IN_STARTS = (0, 512, 2576, 3088, 3600, 4112, 1024, 1536, 2048, 4624, 5136, 5648, 6160, 6672, 7184,
                7696, 8208, 8720, 2560)
DT_SHIFT = SSD_HEADS


def _prep_kernel(blk_ref, shift_ref, width_ref, wg_ref, wa_ref, wb_ref, o_ref):
    j = pl.program_id(1)

    @pl.when(j < N_GATE_TILES)
    def _():
        o_ref[0] = wg_ref[0].astype(BF16)

    @pl.when(j >= N_GATE_TILES)
    def _():
        a = wa_ref[0]
        shifted = jnp.concatenate([a[:, DT_SHIFT:], wb_ref[0][:, :DT_SHIFT]], axis=1)
        val = jnp.where(shift_ref[j] == 0, a, shifted)
        col = lax.broadcasted_iota(jnp.int32, val.shape, 1)
        o_ref[0] = jnp.where(col < width_ref[j], val, 0.0).astype(BF16)


def _prep_w(w_gate, w_in):
    starts = (0,) * N_GATE_TILES + _W_IN_STARTS
    blk = jnp.asarray([s // TILE for s in starts], jnp.int32)
    shift = jnp.asarray([s % TILE for s in starts], jnp.int32)
    assert all(s % TILE in (0, DT_SHIFT) for s in starts)
    width = jnp.asarray([TILE] * (len(starts) - 1) + [SSD_HEADS], jnp.int32)
    last_blk = (w_in.shape[2] - 1) // TILE
    return pl.pallas_call(
        _prep_kernel,
        grid_spec=pltpu.PrefetchScalarGridSpec(
            num_scalar_prefetch=3,
            grid=(DEPTH, PW // TILE),
            in_specs=[pl.BlockSpec((1, D, TILE), lambda l, j, b, s, w: (l, 0, jnp.minimum(j, N_GATE_TILES - 1))),
                      pl.BlockSpec((1, D, TILE), lambda l, j, b, s, w: (l, 0, b[j])),
                      pl.BlockSpec((1, D, TILE), lambda l, j, b, s, w: (l, 0, jnp.minimum(b[j] + 1, last_blk)))],
            out_specs=pl.BlockSpec((1, D, TILE), lambda l, j, b, s, w: (l, 0, j)),
        ),
        out_shape=jax.ShapeDtypeStruct((DEPTH, D, PW), BF16),
        compiler_params=_cparams(("parallel", "arbitrary")),
        name="prep_w",
    )(blk, shift, width, w_gate, w_in, w_in)


def _proj_kernel(x_ref, nw_ref, w_ref, b_ref, o_ref, *rest, tm, phase_major):
    if phase_major:
        qkv_refs, (h_ref, acc_ref, ph_ref) = rest[:9], rest[9:]
    else:
        h_ref, acc_ref = rest
    j = pl.program_id(1)

    @pl.when(j == 0)
    def _():
        x = x_ref[...]
        ms = jnp.mean(x * x, axis=-1, keepdims=True)
        h_ref[...] = (x * lax.rsqrt(ms + EPS) * nw_ref[...]).astype(BF16)
        acc_ref[...] = jnp.zeros(acc_ref.shape, F32)

    prev = acc_ref[...]
    o_ref[...] = jnp.where(j <= N_GATE_TILES, jax.nn.sigmoid(prev), prev)
    acc_ref[...] = jnp.dot(h_ref[...], w_ref[0], preferred_element_type=F32) + b_ref[...]

    if phase_major:
        for part in range(3):
            for g, (_, dil) in enumerate(ATTN_GROUPS):
                ref = qkv_refs[part * 3 + g]

                @pl.when(j - 1 == C_Q // TILE + part * 3 + g)
                def _(ref=ref, dil=dil):
                    if dil == 1:
                        ref[0, 0] = o_ref[...].astype(BF16)
                    else:
                        for c in range(TILE // 128):
                            ph_ref[c] = o_ref[:, c * 128:(c + 1) * 128]
                        for p in range(dil):
                            for c in range(TILE // 128):
                                ref[0, p, :, c * 128:(c + 1) * 128] = (
                                    ph_ref[c, pl.ds(p, tm // dil, stride=dil), :].astype(BF16))


def _proj(x, nw, w_all, b_all, layer, tm, nb=None, t=None):
    n = x.shape[0]
    phase_major = nb is not None
    nt = PW // TILE
    out_specs = [pl.BlockSpec((tm, TILE), lambda i, j: (i, jnp.maximum(j - 1, 0)))]
    out_shape = [jax.ShapeDtypeStruct((n, PW), F32)]
    scratch = [pltpu.VMEM((tm, D), BF16), pltpu.VMEM((tm, TILE), F32)]
    if phase_major:
        tpb = t // tm
        for _ in range(3):
            for _, dil in ATTN_GROUPS:
                out_specs.append(pl.BlockSpec((1, dil, tm // dil, GW), lambda i, j: (i // tpb, 0, i % tpb, 0)))
                out_shape.append(jax.ShapeDtypeStruct((nb, dil, t // dil, GW), BF16))
        scratch.append(pltpu.VMEM((TILE // 128, tm, 128), F32))
    return pl.pallas_call(
        functools.partial(_proj_kernel, tm=tm, phase_major=phase_major),
        grid=(n // tm, nt + 1),
        in_specs=[pl.BlockSpec((tm, D), lambda i, j: (i, 0)),
                  pl.BlockSpec((1, D), lambda i, j: (0, 0)),
                  pl.BlockSpec((1, D, TILE), lambda i, j: (layer, 0, jnp.minimum(j, nt - 1))),
                  pl.BlockSpec((1, TILE), lambda i, j: (0, jnp.minimum(j, nt - 1)))],
        out_specs=out_specs,
        out_shape=out_shape,
        scratch_shapes=scratch,
        compiler_params=_cparams(("parallel", "arbitrary")),
        name="proj",
    )(x, nw, w_all, b_all)


def _conv_step(x_ref, xp_ref, cw_ref, cb_ref, rows, out_rows):
    xp_ref[8:8 + rows, :] = x_ref[...]
    acc = cb_ref[...] + cw_ref[0:1, :] * xp_ref[5:5 + out_rows, :]
    for j in range(1, CONV_W):
        acc = acc + cw_ref[j:j + 1, :] * xp_ref[5 + j:5 + j + out_rows, :]
    tail = xp_ref[rows:rows + 8, :]
    xp_ref[0:8, :] = tail
    return acc


def _softplus(x):
    return jnp.maximum(x, 0.0) + jnp.log1p(jnp.exp(-jnp.abs(x)))


def _ssd_kernel(z_ref, xbc_ref, dt_ref, tail_ref, h0_ref, cw_ref, cb_ref, dtb_ref, alog_ref, dsk_ref, nw_ref,
                y_ref, hf_ref, xp_ref, act_ref, st_ref, ysc_ref, *, rows, n_chunks):
    c = pl.program_id(1)

    @pl.when(c == 0)
    def _():
        xp_ref[0:8, :] = tail_ref[0]
        st_ref[...] = h0_ref[0]

    if rows < CL:
        xp_ref[8 + rows:, :] = jnp.zeros((CL - rows, SSD_CONV_DIM), F32)
    conv = _conv_step(xbc_ref, xp_ref, cw_ref, cb_ref, rows, CL)
    act_ref[...] = conv * jax.nn.sigmoid(conv)

    row = lax.broadcasted_iota(jnp.int32, (CL, 128), 0)
    lane = lax.broadcasted_iota(jnp.int32, (CL, 128), 1)
    raw = dt_ref[...]
    if rows < CL:
        raw = jnp.concatenate([raw, jnp.zeros((CL - rows, 128), F32)], axis=0)
    dt = _softplus(raw + dtb_ref[...])
    dt = jnp.where((lane < SSD_HEADS) & (row < rows), dt, 0.0)
    da = dt * (-jnp.exp(alog_ref[...]))
    acs = da
    d = 1
    while d < CL:
        acs = acs + jnp.where(row >= d, pltpu.roll(acs, d, 0), 0.0)
        d *= 2
    acs_t = acs.T
    last = acs[CL - 1:CL, :]
    e_acs = jnp.exp(acs)
    to_end = jnp.exp(last - acs)
    cdec = jnp.exp(last)
    causal = row >= lane
    lo_lane = lane < SSD_HEAD_DIM
    lo_row = row < SSD_HEAD_DIM
    dsk = dsk_ref[...]

    def pair_cols(arr, h):
        return jnp.where(lo_lane, arr[:, h:h + 1], arr[:, h + 1:h + 2])

    nt = (((1,), (1,)), ((), ()))
    for g in range(2):
        bm = act_ref[:, SSD_INNER + g * SSD_STATE:SSD_INNER + (g + 1) * SSD_STATE].astype(BF16)
        cm = act_ref[:, SSD_INNER + 256 + g * SSD_STATE:SSD_INNER + 256 + (g + 1) * SSD_STATE].astype(BF16)
        cb = lax.dot_general(cm, bm, nt, preferred_element_type=F32)
        for pp in range(4):
            h = g * 8 + 2 * pp
            sl = slice(h * SSD_HEAD_DIM, h * SSD_HEAD_DIM + 128)
            xs = act_ref[:, sl]
            xdt = xs * pair_cols(dt, h)
            xdt_b = xdt.astype(BF16)
            ys = []
            for hh in (h, h + 1):
                seg = acs[:, hh:hh + 1] - acs_t[hh:hh + 1, :]
                decay = jnp.exp(jnp.where(causal, seg, -jnp.inf))
                ys.append(jnp.dot((cb * decay).astype(BF16), xdt_b, preferred_element_type=F32))
            y_diag = jnp.where(lo_lane, ys[0], ys[1])
            st = st_ref[sl, :]
            y_off = lax.dot_general(cm, st.astype(BF16), nt, preferred_element_type=F32) * pair_cols(e_acs, h)
            d_pair = jnp.where(lo_lane, dsk[:, h:h + 1], dsk[:, h + 1:h + 2])
            ysc_ref[:, sl] = y_diag + y_off + d_pair * xs
            xdte_t = (xdt * pair_cols(to_end, h)).T.astype(BF16)
            s_new = jnp.dot(xdte_t, bm, preferred_element_type=F32)
            dec = jnp.where(lo_row, cdec[:, h:h + 1], cdec[:, h + 1:h + 2])
            st_ref[sl, :] = dec * st + s_new

    zz = z_ref[...]
    yg = ysc_ref[0:rows, :] * (zz * jax.nn.sigmoid(zz))
    gw = SSD_INNER // 2
    for g in range(2):
        part = yg[:, g * gw:(g + 1) * gw]
        ms = jnp.mean(part * part, axis=-1, keepdims=True)
        y_ref[:, g * gw:(g + 1) * gw] = (part * lax.rsqrt(ms + EPS)
                                         * nw_ref[:, g * gw:(g + 1) * gw]).astype(y_ref.dtype)

    @pl.when(c == n_chunks - 1)
    def _():
        hf_ref[0] = st_ref[...]


def _ssd(P, tail, h0, lw, nb, t, rows):
    nc = t // rows
    kern = functools.partial(_ssd_kernel, rows=rows, n_chunks=nc)
    return pl.pallas_call(
        kern,
        grid=(nb, nc),
        in_specs=[pl.BlockSpec((rows, SSD_INNER), lambda b, c: (b * nc + c, C_Z // SSD_INNER)),
                  pl.BlockSpec((rows, SSD_CONV_DIM), lambda b, c: (b * nc + c, C_XBC // SSD_CONV_DIM)),
                  pl.BlockSpec((rows, 128), lambda b, c: (b * nc + c, C_DT // 128)),
                  pl.BlockSpec((1, 8, SSD_CONV_DIM), lambda b, c: (b, 0, 0)),
                  pl.BlockSpec((1, SSD_INNER, SSD_STATE), lambda b, c: (b, 0, 0)),
                  _full((CONV_W, SSD_CONV_DIM)), _full((1, SSD_CONV_DIM)),
                  _full((1, 128)), _full((1, 128)), _full((1, 128)), _full((1, SSD_INNER))],
        out_specs=[pl.BlockSpec((rows, SSD_INNER), lambda b, c: (b * nc + c, 0)),
                   pl.BlockSpec((1, SSD_INNER, SSD_STATE), lambda b, c: (b, 0, 0))],
        out_shape=[jax.ShapeDtypeStruct((nb * t, SSD_INNER), BF16 if rows % 16 == 0 else F32),
                   jax.ShapeDtypeStruct((nb, SSD_INNER, SSD_STATE), F32)],
        scratch_shapes=[pltpu.VMEM((8 + CL, SSD_CONV_DIM), F32),
                        pltpu.VMEM((CL, SSD_CONV_DIM), F32),
                        pltpu.VMEM((SSD_INNER, SSD_STATE), F32),
                        pltpu.VMEM((CL, SSD_INNER), F32)],
        compiler_params=_cparams(("parallel", "arbitrary")),
        name="ssd",
    )(P, P, P, tail, h0, lw['conv_ssd_w'], lw['conv_ssd_b'], lw['dt_bias'], lw['a_log'], lw['d_skip'], lw['ssd_norm_w'])


def _lru_kernel(xr_ref, gr_ref, tail_ref, h0_ref, cw_ref, cb_ref, wr_ref, br_ref, wi_ref, bi_ref, lam_ref,
                y_ref, hl_ref, xp_ref, h_ref, *, rows, n_chunks):
    c = pl.program_id(1)

    @pl.when(c == 0)
    def _():
        xp_ref[0:8, :] = tail_ref[0]
        h_ref[...] = h0_ref[0]

    x = _conv_step(xr_ref, xp_ref, cw_ref, cb_ref, rows, rows)
    xb = x.astype(BF16)
    rs, is_ = [], []
    for n in range(LRU_BLOCKS):
        blk = xb[:, n * 128:(n + 1) * 128]
        rs.append(jnp.dot(blk, wr_ref[0, n], preferred_element_type=F32))
        is_.append(jnp.dot(blk, wi_ref[0, n], preferred_element_type=F32))
    r_gate = jax.nn.sigmoid(jnp.concatenate(rs, axis=1) + br_ref[...])
    i_gate = jax.nn.sigmoid(jnp.concatenate(is_, axis=1) + bi_ref[...])
    log_a = -LRU_C * r_gate * _softplus(-lam_ref[...])
    a = jnp.exp(log_a)
    b = jnp.sqrt(1.0 - jnp.exp(2.0 * log_a)) * (i_gate * x)
    row = lax.broadcasted_iota(jnp.int32, (rows, LRU_W), 0)
    d = 1
    while d < rows:
        a_s = jnp.where(row >= d, pltpu.roll(a, d, 0), 1.0)
        b_s = jnp.where(row >= d, pltpu.roll(b, d, 0), 0.0)
        b = a * b_s + b
        a = a * a_s
        d *= 2
    h = b + a * h_ref[...]
    last = h[rows - 1:rows, :]
    h_ref[...] = last
    y_ref[...] = (h * jax.nn.gelu(gr_ref[...])).astype(y_ref.dtype)

    @pl.when(c == n_chunks - 1)
    def _():
        hl_ref[0] = last


def _lru(P, tail, h0, lw, sw, layer, nb, t, rows):
    nc = t // rows
    kern = functools.partial(_lru_kernel, rows=rows, n_chunks=nc)
    y_dtype = BF16 if rows % 16 == 0 else F32
    return pl.pallas_call(
        kern,
        grid=(nb, nc),
        in_specs=[pl.BlockSpec((rows, LRU_W), lambda b, c: (b * nc + c, C_XR // LRU_W)),
                  pl.BlockSpec((rows, LRU_W), lambda b, c: (b * nc + c, C_GR // LRU_W)),
                  pl.BlockSpec((1, 8, LRU_W), lambda b, c: (b, 0, 0)),
                  pl.BlockSpec((1, 1, LRU_W), lambda b, c: (b, 0, 0)),
                  _full((CONV_W, LRU_W)), _full((1, LRU_W)),
                  _of_layer((LRU_BLOCKS, 128, 128), layer), _full((1, LRU_W)),
                  _of_layer((LRU_BLOCKS, 128, 128), layer), _full((1, LRU_W)), _full((1, LRU_W))],
        out_specs=[pl.BlockSpec((rows, LRU_W), lambda b, c: (b * nc + c, 0)),
                   pl.BlockSpec((1, 1, LRU_W), lambda b, c: (b, 0, 0))],
        out_shape=[jax.ShapeDtypeStruct((nb * t, LRU_W), y_dtype),
                   jax.ShapeDtypeStruct((nb, 1, LRU_W), F32)],
        scratch_shapes=[pltpu.VMEM((8 + rows, LRU_W), F32), pltpu.VMEM((1, LRU_W), F32)],
        compiler_params=_cparams(("parallel", "arbitrary")),
        name="lru",
    )(P, P, tail, h0, lw['conv_lru_w'], lw['conv_lru_b'], sw['lru_wr'], lw['lru_br'], sw['lru_wi'], lw['lru_bi'],
      lw['lru_lambda'])


def _attn_kernel(q_ref, kp_ref, vp_ref, kc_ref, vc_ref, ba_ref, bb_ref, o_ref, lse_ref, *, sub):
    scale = HD ** -0.5
    nt = (((1,), (1,)), ((), ()))
    lane = lax.broadcasted_iota(jnp.int32, (CL, 128), 1)
    for s in range(sub):
        rows = slice(s * CL, (s + 1) * CL)
        before = slice((s - 1) * CL, s * CL)
        lse_all = jnp.zeros((CL, 128), F32)
        for h in range(HPG):
            sl = slice(h * HD, (h + 1) * HD)
            q = q_ref[0, 0, rows, sl]
            k_before = kp_ref[0, 0, :, sl] if s == 0 else kc_ref[0, 0, before, sl]
            v_before = vp_ref[0, 0, :, sl] if s == 0 else vc_ref[0, 0, before, sl]
            sa = lax.dot_general(q, k_before, nt, preferred_element_type=F32) * scale + ba_ref[h]
            if s == 0:
                sa = jnp.where(pl.program_id(2) > 0, sa, NEG)
            sb = lax.dot_general(q, kc_ref[0, 0, rows, sl], nt, preferred_element_type=F32) * scale + bb_ref[h]
            m = jnp.maximum(jnp.max(sa, axis=-1, keepdims=True), jnp.max(sb, axis=-1, keepdims=True))
            pa = jnp.exp(sa - m)
            pb = jnp.exp(sb - m)
            l = jnp.sum(pa, axis=-1, keepdims=True) + jnp.sum(pb, axis=-1, keepdims=True)
            o_ref[rows, sl] = (jnp.dot((pa / l).astype(BF16), v_before, preferred_element_type=F32)
                               + jnp.dot((pb / l).astype(BF16), vc_ref[0, 0, rows, sl],
                                         preferred_element_type=F32))
            lse_all = jnp.where(lane == h, m + jnp.log(l), lse_all)
        lse_ref[rows, :] = lse_all


def _attn_prompt(q, k, v, bias_a, bias_b, dil, nb, t):
    n = nb * t
    sub = min(4, t // dil // CL)
    nstep = t // dil // (sub * CL)
    cur = pl.BlockSpec((1, 1, sub * CL, GW), lambda b, p, i: (b, p, i, 0))
    prev = pl.BlockSpec((1, 1, CL, GW), lambda b, p, i: (b, p, jnp.maximum(i * sub - 1, 0), 0))
    o, lse = pl.pallas_call(
        functools.partial(_attn_kernel, sub=sub),
        grid=(nb, dil, nstep),
        in_specs=[cur, prev, prev, cur, cur, _full((HPG, CL, CL)), _full((HPG, CL, CL))],
        out_specs=[pl.BlockSpec((sub * CL, GW), lambda b, p, i: (b * nstep + i, p)),
                   pl.BlockSpec((sub * CL, 128), lambda b, p, i: (b * nstep + i, p))],
        out_shape=[jax.ShapeDtypeStruct((n // dil, dil * GW), F32),
                   jax.ShapeDtypeStruct((n // dil, dil * 128), F32)],
        compiler_params=_cparams(("parallel", "parallel", "arbitrary")),
        name=f"attn_prompt_d{dil}",
    )(q, k, v, k, v, bias_a, bias_b)
    return o.reshape(n, GW), lse.reshape(n, 128)


def _attn_sample_kernel(q_ref, kv_ref, kb_ref, vb_ref, ba_ref, bb_ref, o_ref, lse_ref):
    scale = HD ** -0.5
    nt = (((1,), (1,)), ((), ()))
    rows = o_ref.shape[0]
    window = kv_ref.shape[0] // (2 * HPG)
    lane = lax.broadcasted_iota(jnp.int32, (rows, 128), 1)
    lse_all = jnp.zeros((rows, 128), F32)
    for h in range(HPG):
        sl = slice(h * HD, (h + 1) * HD)
        q = q_ref[:, sl].astype(BF16)
        k_cache = kv_ref[pl.ds(h, window, stride=2 * HPG), :].astype(BF16)
        v_cache = kv_ref[pl.ds(HPG + h, window, stride=2 * HPG), :].astype(BF16)
        sa = lax.dot_general(q, k_cache, nt, preferred_element_type=F32) * scale + ba_ref[h]
        sb = lax.dot_general(q, kb_ref[:, sl].astype(BF16), nt, preferred_element_type=F32) * scale + bb_ref[h]
        m = jnp.maximum(jnp.max(sa, axis=-1, keepdims=True), jnp.max(sb, axis=-1, keepdims=True))
        pa = jnp.exp(sa - m)
        pb = jnp.exp(sb - m)
        l = jnp.sum(pa, axis=-1, keepdims=True) + jnp.sum(pb, axis=-1, keepdims=True)
        o_ref[:, sl] = (jnp.dot((pa / l).astype(BF16), v_cache, preferred_element_type=F32)
                        + jnp.dot((pb / l).astype(BF16), vb_ref[:, sl].astype(BF16), preferred_element_type=F32))
        lse_all = jnp.where(lane == h, m + jnp.log(l), lse_all)
    lse_ref[...] = lse_all


def _attn_sample(Ps, cache_rows, bias_a, bias_b, g, layer, nb, t, window):
    tq, tk, tv = C_Q // GW + g, C_K // GW + g, C_V // GW + g
    out_spec = pl.BlockSpec((t, GW), lambda b: (b, 0))
    return pl.pallas_call(
        _attn_sample_kernel,
        grid=(nb,),
        in_specs=[pl.BlockSpec((t, GW), lambda b: (b, tq)),
                  pl.BlockSpec((window * 2 * HPG, HD), lambda b: (layer * nb + b, 0)),
                  pl.BlockSpec((t, GW), lambda b: (b, tk)),
                  pl.BlockSpec((t, GW), lambda b: (b, tv)),
                  _full((HPG, t, window)), _full((HPG, t, t))],
        out_specs=[out_spec, pl.BlockSpec((t, 128), lambda b: (b, 0))],
        out_shape=[jax.ShapeDtypeStruct((nb * t, GW), F32), jax.ShapeDtypeStruct((nb * t, 128), F32)],
        compiler_params=_cparams(("parallel",)),
        name=f"attn_sample_w{window}",
    )(Ps, cache_rows, Ps, Ps, bias_a, bias_b)


def _mix_kernel(gs_ref, gl_ref, ga_ref, ys_ref, yl_ref, o0_ref, o1_ref, o2_ref, l0_ref, l1_ref, l2_ref,
                wbs_ref, wbl_ref, wba_ref, out_ref):
    l0, l1, l2 = l0_ref[...], l1_ref[...], l2_ref[...]
    m = jnp.maximum(jnp.maximum(l0, l1), l2)
    e0, e1, e2 = jnp.exp(l0 - m), jnp.exp(l1 - m), jnp.exp(l2 - m)
    den = e0 + e1 + e2
    w0, w1, w2 = e0 / den, e1 / den, e2 / den
    tm = out_ref.shape[0]
    heads = []
    for h in range(HPG):
        sl = slice(h * HD, (h + 1) * HD)
        per_head = lambda w: jnp.broadcast_to(w[:, h:h + 1], (tm, HD))
        heads.append(o0_ref[:, sl] * per_head(w0) + o1_ref[:, sl] * per_head(w1) + o2_ref[:, sl] * per_head(w2))
    ya = jnp.concatenate(heads, axis=1)
    mixed = (gs_ref[...] * jnp.dot(ys_ref[...].astype(BF16), wbs_ref[0], preferred_element_type=F32)
             + gl_ref[...] * jnp.dot(yl_ref[...].astype(BF16), wbl_ref[0], preferred_element_type=F32)
             + ga_ref[...] * jnp.dot(ya.astype(BF16), wba_ref[0], preferred_element_type=F32))
    out_ref[...] = mixed.astype(BF16)


def _mix(P, y_ssd, y_lru, attn, sw, layer, tm):
    n = P.shape[0]
    row = lambda w: pl.BlockSpec((tm, w), lambda i: (i, 0))
    (o0, s0), (o1, s1), (o2, s2) = attn
    return pl.pallas_call(
        _mix_kernel,
        grid=(n // tm,),
        in_specs=[pl.BlockSpec((tm, D), lambda i: (i, 0)), pl.BlockSpec((tm, D), lambda i: (i, 1)),
                  pl.BlockSpec((tm, D), lambda i: (i, 2)),
                  row(SSD_INNER), row(LRU_W), row(GW), row(GW), row(GW), row(128), row(128), row(128),
                  _of_layer((SSD_INNER, D), layer), _of_layer((LRU_W, D), layer), _of_layer((GW, D), layer)],
        out_specs=row(D),
        out_shape=jax.ShapeDtypeStruct((n, D), BF16),
        compiler_params=_cparams(("parallel",)),
        name="mix",
    )(P, P, P, y_ssd, y_lru, o0, o1, o2, s0, s1, s2, sw['w_br_ssd'], sw['w_br_lru'], sw['w_br_attn'])


def _res_kernel(x_ref, mixed_ref, wo_ref, n2_ref, wr_ref, br_ref, x1_ref, h2_ref, comb_ref, *rest, dispatch):
    x1 = x_ref[...] + jnp.dot(mixed_ref[...], wo_ref[0], preferred_element_type=F32)
    x1_ref[...] = x1
    ms = jnp.mean(x1 * x1, axis=-1, keepdims=True)
    h2 = x1 * lax.rsqrt(ms + EPS) * n2_ref[...]
    h2b = h2.astype(BF16)
    h2_ref[...] = h2b.astype(h2_ref.dtype)
    logits = jnp.dot(h2b, wr_ref[...], preferred_element_type=F32) + br_ref[...]
    lane = lax.broadcasted_iota(jnp.int32, logits.shape, 1).astype(F32)
    big = float(ROUTE_LANES)

    def first_max(vals, ok):
        v = jnp.where(ok, vals, NEG)
        top = jnp.max(v, axis=-1, keepdims=True)
        idx = jnp.min(jnp.where(ok & (v == top), lane, big), axis=-1, keepdims=True)
        return top, idx

    is_g = lane < N_GROUPS
    gmax, gsel = first_max(logits, is_g)
    gp = 1.0 / jnp.sum(jnp.where(is_g, jnp.exp(logits - gmax), 0.0), axis=-1, keepdims=True)
    lo = N_GROUPS + PER_GROUP * gsel
    is_e = (lane >= lo) & (lane < lo + PER_GROUP)
    t1, i1 = first_max(logits, is_e)
    t2, i2 = first_max(logits, is_e & (lane != i1))
    e2 = jnp.exp(t2 - t1)
    w1 = gp / (1.0 + e2)
    w2 = gp * e2 / (1.0 + e2)
    comb_ref[...] = jnp.where(lane == i1, w1, 0.0) + jnp.where(lane == i2, w2, 0.0)

    if dispatch:
        route_ref, cnt_ref, carry_ref = rest

        @pl.when(pl.program_id(0) == 0)
        def _():
            carry_ref[...] = jnp.zeros(carry_ref.shape, F32)

        tm = x1.shape[0]
        onehot = jnp.where((lane == i1) | (lane == i2), 1.0, 0.0)
        r = lax.broadcasted_iota(jnp.int32, (tm, tm), 0)
        c = lax.broadcasted_iota(jnp.int32, (tm, tm), 1)
        earlier = jnp.where(r > c, 1.0, 0.0).astype(BF16)
        before = jnp.dot(earlier, onehot.astype(BF16), preferred_element_type=F32) + carry_ref[...]
        rank1 = jnp.sum(jnp.where(lane == i1, before, 0.0), axis=-1, keepdims=True)
        rank2 = jnp.sum(jnp.where(lane == i2, before, 0.0), axis=-1, keepdims=True)
        carry_ref[...] += jnp.sum(onehot, axis=0, keepdims=True)
        cnt_ref[...] = carry_ref[...]
        fields = (i1 - N_GROUPS, i2 - N_GROUPS, w1, w2, rank1, rank2)
        route = jnp.zeros(logits.shape, F32)
        for k, val in enumerate(fields):
            route = jnp.where(lane == k, val, route)
        route_ref[...] = route


def _res(x, mixed, lw, sw, layer, tm, dispatch):
    n = x.shape[0]
    out_specs = [pl.BlockSpec((tm, D), lambda i: (i, 0)), pl.BlockSpec((tm, D), lambda i: (i, 0)),
                 pl.BlockSpec((tm, ROUTE_LANES), lambda i: (i, 0))]
    out_shape = [jax.ShapeDtypeStruct((n, D), F32), jax.ShapeDtypeStruct((n, D), F32 if dispatch else BF16),
                 jax.ShapeDtypeStruct((n, ROUTE_LANES), F32)]
    scratch = []
    if dispatch:
        out_specs += [pl.BlockSpec((tm, ROUTE_LANES), lambda i: (i, 0)), _full((1, ROUTE_LANES))]
        out_shape += [jax.ShapeDtypeStruct((n, ROUTE_LANES), F32), jax.ShapeDtypeStruct((1, ROUTE_LANES), F32)]
        scratch = [pltpu.VMEM((1, ROUTE_LANES), F32)]
    return pl.pallas_call(
        functools.partial(_res_kernel, dispatch=dispatch),
        grid=(n // tm,),
        in_specs=[pl.BlockSpec((tm, D), lambda i: (i, 0)), pl.BlockSpec((tm, D), lambda i: (i, 0)),
                  _of_layer((D, D), layer), _full((1, D)), _full((D, ROUTE_LANES)), _full((1, ROUTE_LANES))],
        out_specs=out_specs,
        out_shape=out_shape,
        scratch_shapes=scratch,
        compiler_params=_cparams(("arbitrary",)),
        name="res_router",
    )(x, mixed, sw['w_o'], lw['norm2'], lw['w_router'], lw['b_router'])


def _moe_kernel(h2_ref, comb_ref, x1_ref, w1_ref, w3_ref, w2_ref, fn_ref, o_ref, *, final):
    e = pl.program_id(1)

    @pl.when(e == 0)
    def _():
        o_ref[...] = x1_ref[...]

    h = h2_ref[...]
    a = jnp.dot(h, w1_ref[0, 0].astype(BF16), preferred_element_type=F32)
    b = jnp.dot(h, w3_ref[0, 0].astype(BF16), preferred_element_type=F32)
    comb = comb_ref[...]
    lane = lax.broadcasted_iota(jnp.int32, comb.shape, 1)
    w = jnp.sum(jnp.where(lane == e + N_GROUPS, comb, 0.0), axis=-1, keepdims=True)
    act = (a * jax.nn.sigmoid(a)) * b * w
    o_ref[...] += jnp.dot(act.astype(BF16), w2_ref[0, 0].astype(BF16), preferred_element_type=F32)

    if final:
        @pl.when(e == N_EXPERTS - 1)
        def _():
            x = o_ref[...]
            ms = jnp.mean(x * x, axis=-1, keepdims=True)
            o_ref[...] = x * lax.rsqrt(ms + EPS) * fn_ref[...]


def _moe(h2, comb, x1, sw, layer, final_norm, tm, final):
    n = x1.shape[0]
    return pl.pallas_call(
        functools.partial(_moe_kernel, final=final),
        grid=(n // tm, N_EXPERTS),
        in_specs=[pl.BlockSpec((tm, D), lambda i, e: (i, 0)),
                  pl.BlockSpec((tm, ROUTE_LANES), lambda i, e: (i, 0)),
                  pl.BlockSpec((tm, D), lambda i, e: (i, 0)),
                  pl.BlockSpec((1, 1, D, D_EXPERT), lambda i, e: (layer, e, 0, 0)),
                  pl.BlockSpec((1, 1, D, D_EXPERT), lambda i, e: (layer, e, 0, 0)),
                  pl.BlockSpec((1, 1, D_EXPERT, D), lambda i, e: (layer, e, 0, 0)),
                  pl.BlockSpec((1, D), lambda i, e: (0, 0))],
        out_specs=pl.BlockSpec((tm, D), lambda i, e: (i, 0)),
        out_shape=jax.ShapeDtypeStruct((n, D), F32),
        compiler_params=_cparams(("parallel", "arbitrary")),
        name="moe",
    )(h2, comb, x1, sw['w1'], sw['w3'], sw['w2'], final_norm)


FFN_TM = 256
SC_CORES = 2
SC_SUBCORES = 16
SC_WORKERS = SC_CORES * SC_SUBCORES
SC_CHUNK = 16


def _sc_mesh():
    return plsc.VectorSubcoreMesh(core_axis_name="c", subcore_axis_name="s", num_cores=SC_CORES,
                                  num_subcores=SC_SUBCORES)


def _sc_dispatch(x, wrow1, wrow2, dest1, dest2, n_sorted):
    n = x.shape[0]
    per_w = n // SC_WORKERS

    @functools.partial(
        pl.kernel, mesh=_sc_mesh(),
        out_type=(jax.ShapeDtypeStruct((n_sorted, D), F32), jax.ShapeDtypeStruct((n_sorted, 128), F32)),
        scratch_types=[pltpu.VMEM((SC_CHUNK,), jnp.int32), pltpu.VMEM((SC_CHUNK,), jnp.int32),
                       pltpu.VMEM((SC_CHUNK, D), F32), pltpu.VMEM((SC_CHUNK, 128), F32)],
    )
    def k(x_hbm, w1_hbm, w2_hbm, d1_hbm, d2_hbm, out_hbm, wout_hbm, i1_v, i2_v, rows_v, wrows_v):
        base = (lax.axis_index("s") * SC_CORES + lax.axis_index("c")) * per_w

        @pl.loop(0, per_w // SC_CHUNK)
        def _(j):
            off = base + j * SC_CHUNK
            pltpu.sync_copy(d1_hbm.at[pl.ds(off, SC_CHUNK)], i1_v)
            pltpu.sync_copy(d2_hbm.at[pl.ds(off, SC_CHUNK)], i2_v)
            pltpu.sync_copy(x_hbm.at[pl.ds(off, SC_CHUNK)], rows_v)
            pltpu.sync_copy(rows_v, out_hbm.at[i1_v])
            pltpu.sync_copy(rows_v, out_hbm.at[i2_v])
            pltpu.sync_copy(w1_hbm.at[pl.ds(off, SC_CHUNK)], wrows_v)
            pltpu.sync_copy(wrows_v, wout_hbm.at[i1_v])
            pltpu.sync_copy(w2_hbm.at[pl.ds(off, SC_CHUNK)], wrows_v)
            pltpu.sync_copy(wrows_v, wout_hbm.at[i2_v])

    return k(x, wrow1, wrow2, dest1, dest2)


def _sc_collect(y, dest1, dest2, n):
    per_w = n // SC_WORKERS

    @functools.partial(
        pl.kernel, mesh=_sc_mesh(),
        out_type=(jax.ShapeDtypeStruct((n, D), F32), jax.ShapeDtypeStruct((n, D), F32)),
        scratch_types=[pltpu.VMEM((SC_CHUNK,), jnp.int32), pltpu.VMEM((SC_CHUNK, D), F32)],
    )
    def k(y_hbm, d1_hbm, d2_hbm, g1_hbm, g2_hbm, idx_v, rows_v):
        base = (lax.axis_index("s") * SC_CORES + lax.axis_index("c")) * per_w

        @pl.loop(0, per_w // SC_CHUNK)
        def _(j):
            off = base + j * SC_CHUNK
            for d_hbm, g_hbm in ((d1_hbm, g1_hbm), (d2_hbm, g2_hbm)):
                pltpu.sync_copy(d_hbm.at[pl.ds(off, SC_CHUNK)], idx_v)
                pltpu.sync_copy(y_hbm.at[idx_v], rows_v)
                pltpu.sync_copy(rows_v, g_hbm.at[pl.ds(off, SC_CHUNK)])

    return k(y, dest1, dest2)


def _ffn_kernel(te_ref, nt_ref, xs_ref, ws_ref, w1_ref, w3_ref, w2_ref, y_ref, w1b_ref, w3b_ref, w2b_ref):
    k = pl.program_id(0)

    @pl.when(k < nt_ref[0])
    def _():
        @pl.when((k == 0) | (te_ref[k] != te_ref[jnp.maximum(k - 1, 0)]))
        def _():
            w1b_ref[...] = w1_ref[0, 0].astype(BF16)
            w3b_ref[...] = w3_ref[0, 0].astype(BF16)
            w2b_ref[...] = w2_ref[0, 0].astype(BF16)

        xs = xs_ref[...].astype(BF16)
        a = jnp.dot(xs, w1b_ref[...], preferred_element_type=F32)
        b = jnp.dot(xs, w3b_ref[...], preferred_element_type=F32)
        act = (a * jax.nn.sigmoid(a)) * b * ws_ref[:, 0:1]
        y_ref[...] = jnp.dot(act.astype(BF16), w2b_ref[...], preferred_element_type=F32)


def _ffn(xs, ws, tile_expert, n_tiles_used, sw, layer):
    n_sorted = xs.shape[0]
    w_in_spec = pl.BlockSpec((1, 1, D, D_EXPERT), lambda k, te, nt: (layer, te[k], 0, 0))
    return pl.pallas_call(
        _ffn_kernel,
        grid_spec=pltpu.PrefetchScalarGridSpec(
            num_scalar_prefetch=2,
            grid=(n_sorted // FFN_TM,),
            in_specs=[pl.BlockSpec((FFN_TM, D), lambda k, te, nt: (k, 0)),
                      pl.BlockSpec((FFN_TM, 128), lambda k, te, nt: (k, 0)), w_in_spec, w_in_spec,
                      pl.BlockSpec((1, 1, D_EXPERT, D), lambda k, te, nt: (layer, te[k], 0, 0))],
            out_specs=pl.BlockSpec((FFN_TM, D), lambda k, te, nt: (k, 0)),
            scratch_shapes=[pltpu.VMEM((D, D_EXPERT), BF16), pltpu.VMEM((D, D_EXPERT), BF16),
                            pltpu.VMEM((D_EXPERT, D), BF16)],
        ),
        out_shape=jax.ShapeDtypeStruct((n_sorted, D), F32),
        compiler_params=_cparams(("arbitrary",)),
        name="ffn",
    )(tile_expert, n_tiles_used, xs, ws, sw['w1'], sw['w3'], sw['w2'])


def _combine_kernel(x1_ref, g1_ref, g2_ref, fn_ref, o_ref, *, final):
    x = x1_ref[...] + (g1_ref[...] + g2_ref[...])
    if final:
        ms = jnp.mean(x * x, axis=-1, keepdims=True)
        x = x * lax.rsqrt(ms + EPS) * fn_ref[...]
    o_ref[...] = x


def _combine(x1, g1, g2, final_norm, tm, final):
    n = x1.shape[0]
    row = pl.BlockSpec((tm, D), lambda i: (i, 0))
    return pl.pallas_call(
        functools.partial(_combine_kernel, final=final),
        grid=(n // tm,),
        in_specs=[row, row, row, _full((1, D))],
        out_specs=row,
        out_shape=jax.ShapeDtypeStruct((n, D), F32),
        compiler_params=_cparams(("parallel",)),
        name="combine",
    )(x1, g1, g2, final_norm)


def _moe_sparse(h2, route, counts, x1, sw, layer, final_norm, final):
    n = x1.shape[0]
    n_sorted = 2 * n + N_EXPERTS * FFN_TM
    e1, e2 = route[:, 0].astype(jnp.int32), route[:, 1].astype(jnp.int32)
    rank1, rank2 = route[:, 4].astype(jnp.int32), route[:, 5].astype(jnp.int32)
    cnt = counts[0, N_GROUPS:N_GROUPS + N_EXPERTS].astype(jnp.int32)
    tiles = (cnt + FFN_TM - 1) // FFN_TM
    tile_end = jnp.cumsum(tiles)
    seg_start = (tile_end - tiles) * FFN_TM
    dest1 = jnp.take(seg_start, e1) + rank1
    dest2 = jnp.take(seg_start, e2) + rank2
    tile_ids = jnp.arange(n_sorted // FFN_TM, dtype=jnp.int32)
    tile_expert = jnp.minimum(jnp.sum(tile_end[None, :] <= tile_ids[:, None], axis=1), N_EXPERTS - 1)
    wrow1 = jnp.broadcast_to(route[:, 2:3], (n, 128))
    wrow2 = jnp.broadcast_to(route[:, 3:4], (n, 128))
    xs, ws = _sc_dispatch(h2, wrow1, wrow2, dest1, dest2, n_sorted)
    y = _ffn(xs, ws, tile_expert.astype(jnp.int32), tile_end[-1:].astype(jnp.int32), sw, layer)
    g1, g2 = _sc_collect(y, dest1, dest2, n)
    return _combine(x1, g1, g2, final_norm, 512, final)


def _t5_buckets(dist):
    max_exact = T5_BUCKETS // 2
    large = max_exact + (np.log(np.maximum(dist, 1) / max_exact) / np.log(T5_MAX_DIST / max_exact)
                         * (T5_BUCKETS - max_exact)).astype(np.int32)
    large = np.minimum(large, T5_BUCKETS - 1)
    return np.where(dist < max_exact, dist, large).astype(np.int32)


def _bias_tables(t5, g, dil, window, t_sample):
    nk = window // dil + 1
    hs = slice(g * HPG, (g + 1) * HPG)
    bias = t5[_t5_buckets(np.arange(nk) * dil)][:, hs].T
    rev = bias[:, ::-1]
    neg = lambda *shape: jnp.full(shape, NEG, F32)

    vec = jnp.concatenate([rev, neg(HPG, CL)], axis=1)
    both = jnp.tile(vec, (1, CL + 1))[:, :CL * 2 * CL].reshape(HPG, CL, 2 * CL)
    prev_t, cur_t = both[:, :, :CL], both[:, :, CL:]

    rows = []
    for r in range(t_sample):
        shift = r // dil
        per_u = jnp.concatenate([neg(HPG, shift), rev[:, :nk - 1 - shift]], axis=1)
        on_phase = (np.arange(dil) == r % dil)[None, None, :]
        rows.append(jnp.where(on_phase, per_u[:, :, None], NEG).reshape(HPG, window))
    cache_t = jnp.stack(rows, axis=1)
    r = np.arange(t_sample)[:, None]
    c = np.arange(t_sample)[None, :]
    ok = ((r - c) % dil == 0) & (r >= c)
    new_t = jnp.where(jnp.asarray(ok)[None], bias[:, np.clip((r - c) // dil, 0, nk - 1)], NEG)
    return prev_t, cur_t, cache_t, new_t


def _layer_weights(l, norm1, conv_ssd_w, conv_ssd_b, ssd_dt_bias, ssd_a_log, ssd_d, ssd_norm_w,
                   conv_lru_w, conv_lru_b, lru_br, lru_bi, lru_lambda, b_gate, norm2,
                   w_router_group, b_router_group, w_router_expert, b_router_expert):
    b_all = jnp.concatenate([b_gate[l], jnp.zeros((PW - 3 * D,), F32)])[None]

    def pad128(v):
        return jnp.concatenate([v, jnp.zeros((128 - v.shape[0],), F32)])[None]

    return {
        'norm1': norm1[l][None], 'b_all': b_all,
        'conv_ssd_w': conv_ssd_w[l], 'conv_ssd_b': conv_ssd_b[l][None],
        'dt_bias': pad128(ssd_dt_bias[l]), 'a_log': pad128(ssd_a_log[l]), 'd_skip': pad128(ssd_d[l]),
        'ssd_norm_w': ssd_norm_w[l][None],
        'conv_lru_w': conv_lru_w[l], 'conv_lru_b': conv_lru_b[l][None],
        'lru_br': lru_br[l][None], 'lru_bi': lru_bi[l][None], 'lru_lambda': lru_lambda[l][None],
        'norm2': norm2[l][None],
        'w_router': jnp.concatenate([w_router_group[l], w_router_expert[l],
                                     jnp.zeros((D, ROUTE_LANES - N_GROUPS - N_EXPERTS), F32)],
                                    axis=1).astype(BF16),
        'b_router': pad128(jnp.concatenate([b_router_group[l], b_router_expert[l]])),
    }


def _front_pad(buf):
    return jnp.pad(buf, ((0, 0), (8 - (CONV_W - 1), 0), (0, 0)))


def _cols(P, nb, t, start, width):
    return P.reshape(nb, t, PW)[:, :, start:start + width]


def _kv_rows(P, nb, t, g, n_rows):
    k = _cols(P, nb, t, C_K + g * GW, GW)[:, t - n_rows:].reshape(nb, n_rows, HPG, HD)
    v = _cols(P, nb, t, C_V + g * GW, GW)[:, t - n_rows:].reshape(nb, n_rows, HPG, HD)
    return jnp.stack([k, v], axis=2)


def _layer(x, lw, sw, tables, layer, nb, t, rows, tm, tm_mix, tm_res, tm_moe, conv_ssd, st_ssd, conv_lru, st_lru,
           caches, final_norm, final):
    if caches is None:
        P, *qkv = _proj(x, lw['norm1'], sw['w_all'], lw['b_all'], layer, tm, nb, t)
    else:
        (P,) = _proj(x, lw['norm1'], sw['w_all'], lw['b_all'], layer, tm)
    y_ssd, h_ssd = _ssd(P, _front_pad(conv_ssd), st_ssd.reshape(nb, SSD_INNER, SSD_STATE), lw, nb, t, rows)
    y_lru, h_lru = _lru(P, _front_pad(conv_lru), st_lru.reshape(nb, 1, LRU_W), lw, sw, layer, nb, t, rows)
    attn = []
    for g, (window, dil) in enumerate(ATTN_GROUPS):
        prev_t, cur_t, cache_t, new_t = tables[g]
        if caches is None:
            attn.append(_attn_prompt(qkv[g], qkv[3 + g], qkv[6 + g], prev_t, cur_t, dil, nb, t))
        else:
            attn.append(_attn_sample(P, caches[g], cache_t, new_t, g, layer, nb, t, window))
    mixed = _mix(P, y_ssd, y_lru, attn, sw, layer, tm_mix)
    if caches is None:
        x1, h2, _, route, counts = _res(x, mixed, lw, sw, layer, tm_res, True)
        x2 = _moe_sparse(h2, route, counts, x1, sw, layer, final_norm, final)
    else:
        x1, h2, comb = _res(x, mixed, lw, sw, layer, tm_res, False)
        x2 = _moe(h2, comb, x1, sw, layer, final_norm, tm_moe, final)
    states = (_cols(P, nb, t, C_XBC, SSD_CONV_DIM)[:, t - 3:],
              h_ssd.reshape(nb, SSD_HEADS, SSD_HEAD_DIM, SSD_STATE),
              _cols(P, nb, t, C_XR, LRU_W)[:, t - 3:],
              h_lru.reshape(nb, LRU_W)) + tuple(
                  _kv_rows(P, nb, t, g, min(w, t)) for g, (w, _) in enumerate(ATTN_GROUPS))
    return x2, states


def kernel(x_prompt, x_sample, cache_conv_ssd, state_ssd, cache_conv_lru, state_lru, cache_kv_w128, cache_kv_w512, cache_kv_w2048, norm1, w_in, conv_ssd_w, conv_ssd_b, ssd_dt_bias, ssd_a_log, ssd_d, ssd_norm_w, conv_lru_w, conv_lru_b, lru_wr, lru_br, lru_wi, lru_bi, lru_lambda, t5_bias, w_br_ssd, w_br_lru, w_br_attn, w_gate, b_gate, w_o, norm2, w_router_group, b_router_group, w_router_expert, b_router_expert, w1, w3, w2, final_norm):
    bp, tp, _ = x_prompt.shape
    bs, ts, _ = x_sample.shape
    xp = x_prompt.reshape(bp * tp, D)
    xs = x_sample.reshape(bs * ts, D)
    fn = final_norm[None]
    tables = [_bias_tables(t5_bias, g, dil, window, ts) for g, (window, dil) in enumerate(ATTN_GROUPS)]
    caches = [c.reshape(-1, HD) for c in (cache_kv_w128, cache_kv_w512, cache_kv_w2048)]
    sw = {name: w.astype(BF16) for name, w in dict(
        lru_wr=lru_wr, lru_wi=lru_wi, w_br_ssd=w_br_ssd, w_br_lru=w_br_lru, w_br_attn=w_br_attn, w_o=w_o).items()}
    sw.update(w1=w1, w3=w3, w2=w2)
    sw['w_all'] = _prep_w(w_gate, w_in)
    outs_p, outs_s = [], []
    for l in range(DEPTH):
        lw = _layer_weights(l, norm1, conv_ssd_w, conv_ssd_b, ssd_dt_bias, ssd_a_log, ssd_d, ssd_norm_w,
                            conv_lru_w, conv_lru_b, lru_br, lru_bi, lru_lambda, b_gate, norm2,
                            w_router_group, b_router_group, w_router_expert, b_router_expert)
        final = l == DEPTH - 1
        xp, sp = _layer(xp, lw, sw, tables, l, bp, tp, CL, 1024, 256, 512, 512,
                        jnp.zeros((bp, CONV_W - 1, SSD_CONV_DIM), F32),
                        jnp.zeros((bp, SSD_HEADS, SSD_HEAD_DIM, SSD_STATE), F32),
                        jnp.zeros((bp, CONV_W - 1, LRU_W), F32), jnp.zeros((bp, LRU_W), F32),
                        None, fn, final)
        xs, ss = _layer(xs, lw, sw, tables, l, bs, ts, ts, bs * ts, bs * ts, bs * ts, bs * ts,
                        cache_conv_ssd[l], state_ssd[l], cache_conv_lru[l], state_lru[l],
                        caches, fn, final)
        outs_p.append(sp)
        outs_s.append(ss)

    def stk(outs, i):
        return jnp.stack([o[i] for o in outs], axis=0)

    return ((xp.reshape(bp, tp, D), xs.reshape(bs, ts, D))
            + tuple(stk(outs_p, i) for i in range(7)) + tuple(stk(outs_s, i) for i in range(7)))
```

```python
import functools

import numpy as np
import jax
import jax.numpy as jnp
from jax import lax
from jax.experimental import pallas as pl
from jax.experimental.pallas import tpu as pltpu
from jax.experimental.pallas import tpu_sc as plsc

F32 = jnp.float32
BF16 = jnp.bfloat16
EPS = 1e-6
NEG = -1e30

D = 2048
DEPTH = 2
PAST_LEN = 16384
CL = 128
CONV_W = 4
SSD_HEADS = 16
SSD_HEAD_DIM = 64
SSD_INNER = 1024
SSD_STATE = 128
SSD_CONV_DIM = 1536
LRU_W = 1024
LRU_BLOCKS = 8
LRU_C = 8.0
ATTN_GROUPS = ((128, 1), (512, 4), (2048, 16))
HPG = 4
HD = 128
GW = HPG * HD
T5_BUCKETS = 32
T5_MAX_DIST = 2048
N_GROUPS = 4
PER_GROUP = 4
N_EXPERTS = 16
D_EXPERT = 512

TILE = 512
C_GATE = 0
C_Z = 6144
C_XR = 7168
C_GR = 8192
C_XBC = 9216
C_Q = 10752
C_K = 12288
C_V = 13824
C_DT = 15360
PW = 15872
N_GATE_TILES = (3 * D) // TILE
ROUTE_LANES = 128
VMEM_LIMIT = 56 * 1024 * 1024


def _cparams(sem):
    return pltpu.CompilerParams(dimension_semantics=sem, vmem_limit_bytes=VMEM_LIMIT)


def _full(shape):
    nd = len(shape)
    return pl.BlockSpec(shape, lambda *_: (0,) * nd)


def _of_layer(shape, layer):
    nd = len(shape)
    return pl.BlockSpec((1,) + shape, lambda *_: (layer,) + (0,) * nd)


_W_IN_STARTS = (0, 512, 2576, 3088, 3600, 4112, 1024, 1536, 2048, 4624, 5136, 5648, 6160, 6672, 7184,
                7696, 8208, 8720, 2560)
DT_SHIFT = SSD_HEADS


def _prep_kernel(blk_ref, shift_ref, width_ref, wg_ref, wa_ref, wb_ref, o_ref):
    j = pl.program_id(1)

    @pl.when(j < N_GATE_TILES)
    def _():
        o_ref[0] = wg_ref[0].astype(BF16)

    @pl.when(j >= N_GATE_TILES)
    def _():
        a = wa_ref[0]
        shifted = jnp.concatenate([a[:, DT_SHIFT:], wb_ref[0, :, :DT_SHIFT]], axis=1)
        val = jnp.where(shift_ref[j] == 0, a, shifted)
        col = lax.broadcasted_iota(jnp.int32, val.shape, 1)
        o_ref[0] = jnp.where(col < width_ref[j], val, 0.0).astype(BF16)


def _prep_w(w_gate, w_in):
    starts = (0,) * N_GATE_TILES + _W_IN_STARTS
    blk = jnp.asarray([s // TILE for s in starts], jnp.int32)
    shift = jnp.asarray([s % TILE for s in starts], jnp.int32)
    assert all(s % TILE in (0, DT_SHIFT) for s in starts)
    width = jnp.asarray([TILE] * (len(starts) - 1) + [SSD_HEADS], jnp.int32)
    per_tile = TILE // 128
    last_blk = (w_in.shape[2] - 1) // 128
    return pl.pallas_call(
        _prep_kernel,
        grid_spec=pltpu.PrefetchScalarGridSpec(
            num_scalar_prefetch=3,
            grid=(DEPTH, PW // TILE),
            in_specs=[pl.BlockSpec((1, D, TILE), lambda l, j, b, s, w: (l, 0, jnp.minimum(j, N_GATE_TILES - 1))),
                      pl.BlockSpec((1, D, TILE), lambda l, j, b, s, w: (l, 0, b[j])),
                      pl.BlockSpec((1, D, 128),
                                   lambda l, j, b, s, w: (l, 0, jnp.minimum((b[j] + 1) * per_tile, last_blk)))],
            out_specs=pl.BlockSpec((1, D, TILE), lambda l, j, b, s, w: (l, 0, j)),
        ),
        out_shape=jax.ShapeDtypeStruct((DEPTH, D, PW), BF16),
        compiler_params=_cparams(("parallel", "arbitrary")),
        name="prep_w",
    )(blk, shift, width, w_gate, w_in, w_in)


def _proj_kernel(x_ref, nw_ref, w_ref, b_ref, o_ref, *rest, tm, phase_major):
    if phase_major:
        qkv_refs, (h_ref, acc_ref, ph_ref) = rest[:9], rest[9:]
    else:
        h_ref, acc_ref = rest
    j = pl.program_id(1)

    @pl.when(j == 0)
    def _():
        x = x_ref[...]
        ms = jnp.mean(x * x, axis=-1, keepdims=True)
        h_ref[...] = (x * lax.rsqrt(ms + EPS) * nw_ref[...]).astype(BF16)
        acc_ref[...] = jnp.zeros(acc_ref.shape, F32)

    prev = acc_ref[...]
    o_ref[...] = jnp.where(j <= N_GATE_TILES, jax.nn.sigmoid(prev), prev)
    acc_ref[...] = jnp.dot(h_ref[...], w_ref[0], preferred_element_type=F32) + b_ref[...]

    if phase_major:
        for part in range(3):
            for g, (_, dil) in enumerate(ATTN_GROUPS):
                ref = qkv_refs[part * 3 + g]

                @pl.when(j - 1 == C_Q // TILE + part * 3 + g)
                def _(ref=ref, dil=dil):
                    if dil == 1:
                        ref[0, 0] = o_ref[...].astype(BF16)
                    else:
                        for c in range(TILE // 128):
                            ph_ref[c] = o_ref[:, c * 128:(c + 1) * 128]
                        for p in range(dil):
                            for c in range(TILE // 128):
                                ref[0, p, :, c * 128:(c + 1) * 128] = (
                                    ph_ref[c, pl.ds(p, tm // dil, stride=dil), :].astype(BF16))


def _proj(x, nw, w_all, b_all, layer, tm, nb=None, t=None):
    n = x.shape[0]
    phase_major = nb is not None
    nt = PW // TILE
    out_specs = [pl.BlockSpec((tm, TILE), lambda i, j: (i, jnp.maximum(j - 1, 0)))]
    out_shape = [jax.ShapeDtypeStruct((n, PW), F32)]
    scratch = [pltpu.VMEM((tm, D), BF16), pltpu.VMEM((tm, TILE), F32)]
    if phase_major:
        tpb = t // tm
        for _ in range(3):
            for _, dil in ATTN_GROUPS:
                out_specs.append(pl.BlockSpec((1, dil, tm // dil, GW), lambda i, j: (i // tpb, 0, i % tpb, 0)))
                out_shape.append(jax.ShapeDtypeStruct((nb, dil, t // dil, GW), BF16))
        scratch.append(pltpu.VMEM((TILE // 128, tm, 128), F32))
    return pl.pallas_call(
        functools.partial(_proj_kernel, tm=tm, phase_major=phase_major),
        grid=(n // tm, nt + 1),
        in_specs=[pl.BlockSpec((tm, D), lambda i, j: (i, 0)),
                  pl.BlockSpec((1, D), lambda i, j: (0, 0)),
                  pl.BlockSpec((1, D, TILE), lambda i, j: (layer, 0, jnp.minimum(j, nt - 1))),
                  pl.BlockSpec((1, TILE), lambda i, j: (0, jnp.minimum(j, nt - 1)))],
        out_specs=out_specs,
        out_shape=out_shape,
        scratch_shapes=scratch,
        compiler_params=_cparams(("parallel", "arbitrary")),
        name="proj",
    )(x, nw, w_all, b_all)


def _conv_step(x_ref, xp_ref, cw_ref, cb_ref, rows, out_rows):
    xp_ref[8:8 + rows, :] = _bf16_round(x_ref[...])
    cw = _bf16_round(cw_ref[...])
    acc = cw[0:1, :] * xp_ref[5:5 + out_rows, :]
    for j in range(1, CONV_W):
        acc = acc + cw[j:j + 1, :] * xp_ref[5 + j:5 + j + out_rows, :]
    tail = xp_ref[rows:rows + 8, :]
    xp_ref[0:8, :] = tail
    return acc + cb_ref[...]


def _bf16_round(x):
    return x.astype(BF16).astype(F32)


def _softplus(x):
    return jnp.maximum(x, 0.0) + jnp.log1p(jnp.exp(-jnp.abs(x)))


def _ssd_kernel(z_ref, xbc_ref, dt_ref, tail_ref, h0_ref, cw_ref, cb_ref, dtb_ref, alog_ref, dsk_ref, nw_ref,
                y_ref, hf_ref, xp_ref, act_ref, st_ref, ysc_ref, *, rows, n_chunks):
    c = pl.program_id(1)

    @pl.when(c == 0)
    def _():
        xp_ref[0:8, :] = _bf16_round(tail_ref[0])
        st_ref[...] = h0_ref[0]

    if rows < CL:
        xp_ref[8 + rows:, :] = jnp.zeros((CL - rows, SSD_CONV_DIM), F32)
    conv = _conv_step(xbc_ref, xp_ref, cw_ref, cb_ref, rows, CL)
    act_ref[...] = conv * jax.nn.sigmoid(conv)

    row = lax.broadcasted_iota(jnp.int32, (CL, 128), 0)
    lane = lax.broadcasted_iota(jnp.int32, (CL, 128), 1)
    raw = dt_ref[...]
    if rows < CL:
        raw = jnp.concatenate([raw, jnp.zeros((CL - rows, 128), F32)], axis=0)
    dt = _softplus(raw + dtb_ref[...])
    dt = jnp.where((lane < SSD_HEADS) & (row < rows), dt, 0.0)
    da = dt * (-jnp.exp(alog_ref[...]))
    acs = da
    d = 1
    while d < CL:
        acs = acs + jnp.where(row >= d, pltpu.roll(acs, d, 0), 0.0)
        d *= 2
    acs_t = acs.T
    last = acs[CL - 1:CL, :]
    e_acs = jnp.exp(acs)
    to_end = jnp.exp(last - acs)
    cdec = jnp.exp(last)
    causal = row >= lane
    lo_lane = lane < SSD_HEAD_DIM
    lo_row = row < SSD_HEAD_DIM
    dsk = dsk_ref[...]

    def pair_cols(arr, h):
        return jnp.where(lo_lane, arr[:, h:h + 1], arr[:, h + 1:h + 2])

    nt = (((1,), (1,)), ((), ()))
    for g in range(2):
        bm = act_ref[:, SSD_INNER + g * SSD_STATE:SSD_INNER + (g + 1) * SSD_STATE].astype(BF16)
        cm = act_ref[:, SSD_INNER + 256 + g * SSD_STATE:SSD_INNER + 256 + (g + 1) * SSD_STATE].astype(BF16)
        cb = lax.dot_general(cm, bm, nt, preferred_element_type=F32)
        for pp in range(4):
            h = g * 8 + 2 * pp
            sl = slice(h * SSD_HEAD_DIM, h * SSD_HEAD_DIM + 128)
            xs = act_ref[:, sl]
            xdt = xs * pair_cols(dt, h)
            xdt_b = xdt.astype(BF16)
            ys = []
            for hh in (h, h + 1):
                seg = acs[:, hh:hh + 1] - acs_t[hh:hh + 1, :]
                decay = jnp.exp(jnp.where(causal, seg, -jnp.inf))
                ys.append(jnp.dot((cb * decay).astype(BF16), xdt_b, preferred_element_type=F32))
            y_diag = jnp.where(lo_lane, ys[0], ys[1])
            st = st_ref[sl, :]
            y_off = lax.dot_general(cm, st.astype(BF16), nt, preferred_element_type=F32) * pair_cols(e_acs, h)
            d_pair = jnp.where(lo_lane, dsk[:, h:h + 1], dsk[:, h + 1:h + 2])
            ysc_ref[:, sl] = y_diag + y_off + d_pair * xs
            xdte_t = (xdt * pair_cols(to_end, h)).T.astype(BF16)
            s_new = jnp.dot(xdte_t, bm, preferred_element_type=F32)
            dec = jnp.where(lo_row, cdec[:, h:h + 1], cdec[:, h + 1:h + 2])
            st_ref[sl, :] = dec * st + s_new

    zz = z_ref[...]
    yg = ysc_ref[0:rows, :] * (zz * jax.nn.sigmoid(zz))
    gw = SSD_INNER // 2
    for g in range(2):
        part = yg[:, g * gw:(g + 1) * gw]
        ms = jnp.mean(part * part, axis=-1, keepdims=True)
        y_ref[:, g * gw:(g + 1) * gw] = (part * lax.rsqrt(ms + EPS)
                                         * nw_ref[:, g * gw:(g + 1) * gw]).astype(y_ref.dtype)

    @pl.when(c == n_chunks - 1)
    def _():
        hf_ref[0] = st_ref[...]


def _ssd(P, tail, h0, lw, nb, t, rows):
    nc = t // rows
    kern = functools.partial(_ssd_kernel, rows=rows, n_chunks=nc)
    return pl.pallas_call(
        kern,
        grid=(nb, nc),
        in_specs=[pl.BlockSpec((rows, SSD_INNER), lambda b, c: (b * nc + c, C_Z // SSD_INNER)),
                  pl.BlockSpec((rows, SSD_CONV_DIM), lambda b, c: (b * nc + c, C_XBC // SSD_CONV_DIM)),
                  pl.BlockSpec((rows, 128), lambda b, c: (b * nc + c, C_DT // 128)),
                  pl.BlockSpec((1, 8, SSD_CONV_DIM), lambda b, c: (b, 0, 0)),
                  pl.BlockSpec((1, SSD_INNER, SSD_STATE), lambda b, c: (b, 0, 0)),
                  _full((CONV_W, SSD_CONV_DIM)), _full((1, SSD_CONV_DIM)),
                  _full((1, 128)), _full((1, 128)), _full((1, 128)), _full((1, SSD_INNER))],
        out_specs=[pl.BlockSpec((rows, SSD_INNER), lambda b, c: (b * nc + c, 0)),
                   pl.BlockSpec((1, SSD_INNER, SSD_STATE), lambda b, c: (b, 0, 0))],
        out_shape=[jax.ShapeDtypeStruct((nb * t, SSD_INNER), BF16 if rows % 16 == 0 else F32),
                   jax.ShapeDtypeStruct((nb, SSD_INNER, SSD_STATE), F32)],
        scratch_shapes=[pltpu.VMEM((8 + CL, SSD_CONV_DIM), F32),
                        pltpu.VMEM((CL, SSD_CONV_DIM), F32),
                        pltpu.VMEM((SSD_INNER, SSD_STATE), F32),
                        pltpu.VMEM((CL, SSD_INNER), F32)],
        compiler_params=_cparams(("parallel", "arbitrary")),
        name="ssd",
    )(P, P, P, tail, h0, lw['conv_ssd_w'], lw['conv_ssd_b'], lw['dt_bias'], lw['a_log'], lw['d_skip'], lw['ssd_norm_w'])


def _lru_kernel(xr_ref, gr_ref, tail_ref, h0_ref, cw_ref, cb_ref, wr_ref, br_ref, wi_ref, bi_ref, lam_ref,
                y_ref, hl_ref, xp_ref, h_ref, *, rows, n_chunks):
    c = pl.program_id(1)

    @pl.when(c == 0)
    def _():
        xp_ref[0:8, :] = _bf16_round(tail_ref[0])
        h_ref[...] = h0_ref[0]

    x = _conv_step(xr_ref, xp_ref, cw_ref, cb_ref, rows, rows)
    xb = x.astype(BF16)
    rs, is_ = [], []
    for n in range(LRU_BLOCKS):
        blk = xb[:, n * 128:(n + 1) * 128]
        rs.append(jnp.dot(blk, wr_ref[0, n], preferred_element_type=F32))
        is_.append(jnp.dot(blk, wi_ref[0, n], preferred_element_type=F32))
    r_gate = jax.nn.sigmoid(jnp.concatenate(rs, axis=1) + br_ref[...])
    i_gate = jax.nn.sigmoid(jnp.concatenate(is_, axis=1) + bi_ref[...])
    log_a = -LRU_C * r_gate * _softplus(-lam_ref[...])
    a = jnp.exp(log_a)
    th = jnp.tanh(log_a)
    b = jnp.sqrt(-2.0 * th / (1.0 - th)) * (i_gate * x)
    row = lax.broadcasted_iota(jnp.int32, (rows, LRU_W), 0)
    d = 1
    while d < rows:
        a_s = jnp.where(row >= d, pltpu.roll(a, d, 0), 1.0)
        b_s = jnp.where(row >= d, pltpu.roll(b, d, 0), 0.0)
        b = a * b_s + b
        a = a * a_s
        d *= 2
    h = b + a * h_ref[...]
    last = h[rows - 1:rows, :]
    h_ref[...] = last
    y_ref[...] = (h * jax.nn.gelu(gr_ref[...])).astype(y_ref.dtype)

    @pl.when(c == n_chunks - 1)
    def _():
        hl_ref[0] = last


def _lru(P, tail, h0, lw, sw, layer, nb, t, rows):
    nc = t // rows
    kern = functools.partial(_lru_kernel, rows=rows, n_chunks=nc)
    y_dtype = BF16 if rows % 16 == 0 else F32
    return pl.pallas_call(
        kern,
        grid=(nb, nc),
        in_specs=[pl.BlockSpec((rows, LRU_W), lambda b, c: (b * nc + c, C_XR // LRU_W)),
                  pl.BlockSpec((rows, LRU_W), lambda b, c: (b * nc + c, C_GR // LRU_W)),
                  pl.BlockSpec((1, 8, LRU_W), lambda b, c: (b, 0, 0)),
                  pl.BlockSpec((1, 1, LRU_W), lambda b, c: (b, 0, 0)),
                  _full((CONV_W, LRU_W)), _full((1, LRU_W)),
                  _of_layer((LRU_BLOCKS, 128, 128), layer), _full((1, LRU_W)),
                  _of_layer((LRU_BLOCKS, 128, 128), layer), _full((1, LRU_W)), _full((1, LRU_W))],
        out_specs=[pl.BlockSpec((rows, LRU_W), lambda b, c: (b * nc + c, 0)),
                   pl.BlockSpec((1, 1, LRU_W), lambda b, c: (b, 0, 0))],
        out_shape=[jax.ShapeDtypeStruct((nb * t, LRU_W), y_dtype),
                   jax.ShapeDtypeStruct((nb, 1, LRU_W), F32)],
        scratch_shapes=[pltpu.VMEM((8 + rows, LRU_W), F32), pltpu.VMEM((1, LRU_W), F32)],
        compiler_params=_cparams(("parallel", "arbitrary")),
        name="lru",
    )(P, P, tail, h0, lw['conv_lru_w'], lw['conv_lru_b'], sw['lru_wr'], lw['lru_br'], sw['lru_wi'], lw['lru_bi'],
      lw['lru_lambda'])


def _attn_kernel(q_ref, kp_ref, vp_ref, kc_ref, vc_ref, ba_ref, bb_ref, o_ref, lse_ref, *, sub):
    scale = HD ** -0.5
    nt = (((1,), (1,)), ((), ()))
    lane = lax.broadcasted_iota(jnp.int32, (CL, 128), 1)
    for s in range(sub):
        rows = slice(s * CL, (s + 1) * CL)
        before = slice((s - 1) * CL, s * CL)
        lse_all = jnp.zeros((CL, 128), F32)
        for h in range(HPG):
            sl = slice(h * HD, (h + 1) * HD)
            q = q_ref[0, 0, rows, sl]
            k_before = kp_ref[0, 0, :, sl] if s == 0 else kc_ref[0, 0, before, sl]
            v_before = vp_ref[0, 0, :, sl] if s == 0 else vc_ref[0, 0, before, sl]
            sa = lax.dot_general(q, k_before, nt, preferred_element_type=F32) * scale + ba_ref[h]
            if s == 0:
                sa = jnp.where(pl.program_id(2) > 0, sa, NEG)
            sb = lax.dot_general(q, kc_ref[0, 0, rows, sl], nt, preferred_element_type=F32) * scale + bb_ref[h]
            m = jnp.maximum(jnp.max(sa, axis=-1, keepdims=True), jnp.max(sb, axis=-1, keepdims=True))
            pa = jnp.exp(sa - m)
            pb = jnp.exp(sb - m)
            l = jnp.sum(pa, axis=-1, keepdims=True) + jnp.sum(pb, axis=-1, keepdims=True)
            inv = 1.0 / l
            o_ref[rows, sl] = (jnp.dot((pa * inv).astype(BF16), v_before, preferred_element_type=F32)
                               + jnp.dot((pb * inv).astype(BF16), vc_ref[0, 0, rows, sl],
                                         preferred_element_type=F32))
            lse_all = jnp.where(lane == h, m + jnp.log(l), lse_all)
        lse_ref[rows, :] = lse_all


def _attn_prompt(q, k, v, bias_a, bias_b, dil, nb, t):
    n = nb * t
    sub = min(4, t // dil // CL)
    nstep = t // dil // (sub * CL)
    cur = pl.BlockSpec((1, 1, sub * CL, GW), lambda b, p, i: (b, p, i, 0))
    prev = pl.BlockSpec((1, 1, CL, GW), lambda b, p, i: (b, p, jnp.maximum(i * sub - 1, 0), 0))
    o, lse = pl.pallas_call(
        functools.partial(_attn_kernel, sub=sub),
        grid=(nb, dil, nstep),
        in_specs=[cur, prev, prev, cur, cur, _full((HPG, CL, CL)), _full((HPG, CL, CL))],
        out_specs=[pl.BlockSpec((sub * CL, GW), lambda b, p, i: (b * nstep + i, p)),
                   pl.BlockSpec((sub * CL, 128), lambda b, p, i: (b * nstep + i, p))],
        out_shape=[jax.ShapeDtypeStruct((n // dil, dil * GW), F32),
                   jax.ShapeDtypeStruct((n // dil, dil * 128), F32)],
        compiler_params=_cparams(("parallel", "parallel", "arbitrary")),
        name=f"attn_prompt_d{dil}",
    )(q, k, v, k, v, bias_a, bias_b)
    return o.reshape(n, GW), lse.reshape(n, 128)


def _attn_sample_kernel(q_ref, kv_ref, kb_ref, vb_ref, ba_ref, bb_ref, o_ref, lse_ref):
    scale = HD ** -0.5
    nt = (((1,), (1,)), ((), ()))
    rows = o_ref.shape[0]
    window = kv_ref.shape[0] // (2 * HPG)
    lane = lax.broadcasted_iota(jnp.int32, (rows, 128), 1)
    lse_all = jnp.zeros((rows, 128), F32)
    for h in range(HPG):
        sl = slice(h * HD, (h + 1) * HD)
        q = q_ref[:, sl].astype(BF16)
        k_cache = kv_ref[pl.ds(h, window, stride=2 * HPG), :].astype(BF16)
        v_cache = kv_ref[pl.ds(HPG + h, window, stride=2 * HPG), :].astype(BF16)
        sa = lax.dot_general(q, k_cache, nt, preferred_element_type=F32) * scale + ba_ref[h]
        sb = lax.dot_general(q, kb_ref[:, sl].astype(BF16), nt, preferred_element_type=F32) * scale + bb_ref[h]
        m = jnp.maximum(jnp.max(sa, axis=-1, keepdims=True), jnp.max(sb, axis=-1, keepdims=True))
        pa = jnp.exp(sa - m)
        pb = jnp.exp(sb - m)
        l = jnp.sum(pa, axis=-1, keepdims=True) + jnp.sum(pb, axis=-1, keepdims=True)
        inv = 1.0 / l
        o_ref[:, sl] = (jnp.dot((pa * inv).astype(BF16), v_cache, preferred_element_type=F32)
                        + jnp.dot((pb * inv).astype(BF16), vb_ref[:, sl].astype(BF16), preferred_element_type=F32))
        lse_all = jnp.where(lane == h, m + jnp.log(l), lse_all)
    lse_ref[...] = lse_all


def _attn_sample(Ps, cache_rows, bias_a, bias_b, g, layer, nb, t, window):
    tq, tk, tv = C_Q // GW + g, C_K // GW + g, C_V // GW + g
    out_spec = pl.BlockSpec((t, GW), lambda b: (b, 0))
    return pl.pallas_call(
        _attn_sample_kernel,
        grid=(nb,),
        in_specs=[pl.BlockSpec((t, GW), lambda b: (b, tq)),
                  pl.BlockSpec((window * 2 * HPG, HD), lambda b: (layer * nb + b, 0)),
                  pl.BlockSpec((t, GW), lambda b: (b, tk)),
                  pl.BlockSpec((t, GW), lambda b: (b, tv)),
                  _full((HPG, t, window)), _full((HPG, t, t))],
        out_specs=[out_spec, pl.BlockSpec((t, 128), lambda b: (b, 0))],
        out_shape=[jax.ShapeDtypeStruct((nb * t, GW), F32), jax.ShapeDtypeStruct((nb * t, 128), F32)],
        compiler_params=_cparams(("parallel",)),
        name=f"attn_sample_w{window}",
    )(Ps, cache_rows, Ps, Ps, bias_a, bias_b)


def _mix_kernel(gs_ref, gl_ref, ga_ref, ys_ref, yl_ref, o0_ref, o1_ref, o2_ref, l0_ref, l1_ref, l2_ref,
                wbs_ref, wbl_ref, wba_ref, out_ref):
    l0, l1, l2 = l0_ref[...], l1_ref[...], l2_ref[...]
    m = jnp.maximum(jnp.maximum(l0, l1), l2)
    e0, e1, e2 = jnp.exp(l0 - m), jnp.exp(l1 - m), jnp.exp(l2 - m)
    den = e0 + e1 + e2
    w0, w1, w2 = e0 / den, e1 / den, e2 / den
    tm = out_ref.shape[0]
    heads = []
    for h in range(HPG):
        sl = slice(h * HD, (h + 1) * HD)
        per_head = lambda w: jnp.broadcast_to(w[:, h:h + 1], (tm, HD))
        heads.append(o0_ref[:, sl] * per_head(w0) + o1_ref[:, sl] * per_head(w1) + o2_ref[:, sl] * per_head(w2))
    ya = jnp.concatenate(heads, axis=1)
    mixed = (gs_ref[...] * jnp.dot(ys_ref[...].astype(BF16), wbs_ref[0], preferred_element_type=F32)
             + gl_ref[...] * jnp.dot(yl_ref[...].astype(BF16), wbl_ref[0], preferred_element_type=F32)
             + ga_ref[...] * jnp.dot(ya.astype(BF16), wba_ref[0], preferred_element_type=F32))
    out_ref[...] = mixed.astype(BF16)


def _mix(P, y_ssd, y_lru, attn, sw, layer, tm):
    n = P.shape[0]
    row = lambda w: pl.BlockSpec((tm, w), lambda i: (i, 0))
    (o0, s0), (o1, s1), (o2, s2) = attn
    return pl.pallas_call(
        _mix_kernel,
        grid=(n // tm,),
        in_specs=[pl.BlockSpec((tm, D), lambda i: (i, 0)), pl.BlockSpec((tm, D), lambda i: (i, 1)),
                  pl.BlockSpec((tm, D), lambda i: (i, 2)),
                  row(SSD_INNER), row(LRU_W), row(GW), row(GW), row(GW), row(128), row(128), row(128),
                  _of_layer((SSD_INNER, D), layer), _of_layer((LRU_W, D), layer), _of_layer((GW, D), layer)],
        out_specs=row(D),
        out_shape=jax.ShapeDtypeStruct((n, D), BF16),
        compiler_params=_cparams(("parallel",)),
        name="mix",
    )(P, P, P, y_ssd, y_lru, o0, o1, o2, s0, s1, s2, sw['w_br_ssd'], sw['w_br_lru'], sw['w_br_attn'])


def _res_kernel(x_ref, mixed_ref, wo_ref, n2_ref, wr_ref, br_ref, x1_ref, h2_ref, comb_ref, *rest, dispatch):
    x1 = x_ref[...] + jnp.dot(mixed_ref[...], wo_ref[0], preferred_element_type=F32)
    x1_ref[...] = x1
    ms = jnp.mean(x1 * x1, axis=-1, keepdims=True)
    h2 = x1 * lax.rsqrt(ms + EPS) * n2_ref[...]
    h2b = h2.astype(BF16)
    if dispatch:
        lo = pltpu.bitcast(h2b[:, :HALF].astype(F32), jnp.uint32)
        hi = pltpu.bitcast(h2b[:, HALF:].astype(F32), jnp.uint32)
        h2_ref[...] = pltpu.bitcast(lax.shift_right_logical(lo, jnp.uint32(16)) | hi, jnp.int32)
    else:
        h2_ref[...] = h2b
    logits = jnp.dot(h2b, wr_ref[...], preferred_element_type=F32) + br_ref[...]
    lane = lax.broadcasted_iota(jnp.int32, logits.shape, 1).astype(F32)
    big = float(ROUTE_LANES)

    def first_max(vals, ok):
        v = jnp.where(ok, vals, NEG)
        top = jnp.max(v, axis=-1, keepdims=True)
        idx = jnp.min(jnp.where(ok & (v == top), lane, big), axis=-1, keepdims=True)
        return top, idx

    is_g = lane < N_GROUPS
    gmax, gsel = first_max(logits, is_g)
    gp = 1.0 / jnp.sum(jnp.where(is_g, jnp.exp(logits - gmax), 0.0), axis=-1, keepdims=True)
    lo = N_GROUPS + PER_GROUP * gsel
    is_e = (lane >= lo) & (lane < lo + PER_GROUP)
    t1, i1 = first_max(logits, is_e)
    t2, i2 = first_max(logits, is_e & (lane != i1))
    e2 = jnp.exp(t2 - t1)
    w1 = gp / (1.0 + e2)
    w2 = gp * e2 / (1.0 + e2)
    comb_ref[...] = jnp.where(lane == i1, w1, 0.0) + jnp.where(lane == i2, w2, 0.0)

    if dispatch:
        route_ref, cnt_ref, carry_ref = rest

        @pl.when(pl.program_id(0) == 0)
        def _():
            carry_ref[...] = jnp.zeros(carry_ref.shape, F32)

        tm = x1.shape[0]
        onehot = jnp.where((lane == i1) | (lane == i2), 1.0, 0.0)
        r = lax.broadcasted_iota(jnp.int32, (tm, tm), 0)
        c = lax.broadcasted_iota(jnp.int32, (tm, tm), 1)
        earlier = jnp.where(r > c, 1.0, 0.0).astype(BF16)
        before = jnp.dot(earlier, onehot.astype(BF16), preferred_element_type=F32) + carry_ref[...]
        rank1 = jnp.sum(jnp.where(lane == i1, before, 0.0), axis=-1, keepdims=True)
        rank2 = jnp.sum(jnp.where(lane == i2, before, 0.0), axis=-1, keepdims=True)
        carry_ref[...] += jnp.sum(onehot, axis=0, keepdims=True)
        cnt_ref[...] = carry_ref[...]
        fields = (i1 - N_GROUPS, i2 - N_GROUPS, w1, w2, rank1, rank2)
        route = jnp.zeros(logits.shape, F32)
        for k, val in enumerate(fields):
            route = jnp.where(lane == k, val, route)
        route_ref[...] = route


def _res(x, mixed, lw, sw, layer, tm, dispatch):
    n = x.shape[0]
    h2_cols, h2_dtype = (HALF, jnp.int32) if dispatch else (D, BF16)
    out_specs = [pl.BlockSpec((tm, D), lambda i: (i, 0)), pl.BlockSpec((tm, h2_cols), lambda i: (i, 0)),
                 pl.BlockSpec((tm, ROUTE_LANES), lambda i: (i, 0))]
    out_shape = [jax.ShapeDtypeStruct((n, D), F32), jax.ShapeDtypeStruct((n, h2_cols), h2_dtype),
                 jax.ShapeDtypeStruct((n, ROUTE_LANES), F32)]
    scratch = []
    if dispatch:
        out_specs += [pl.BlockSpec((tm, ROUTE_LANES), lambda i: (i, 0)), _full((1, ROUTE_LANES))]
        out_shape += [jax.ShapeDtypeStruct((n, ROUTE_LANES), F32), jax.ShapeDtypeStruct((1, ROUTE_LANES), F32)]
        scratch = [pltpu.VMEM((1, ROUTE_LANES), F32)]
    return pl.pallas_call(
        functools.partial(_res_kernel, dispatch=dispatch),
        grid=(n // tm,),
        in_specs=[pl.BlockSpec((tm, D), lambda i: (i, 0)), pl.BlockSpec((tm, D), lambda i: (i, 0)),
                  _of_layer((D, D), layer), _full((1, D)), _full((D, ROUTE_LANES)), _full((1, ROUTE_LANES))],
        out_specs=out_specs,
        out_shape=out_shape,
        scratch_shapes=scratch,
        compiler_params=_cparams(("arbitrary",)),
        name="res_router",
    )(x, mixed, sw['w_o'], lw['norm2'], lw['w_router'], lw['b_router'])


def _moe_kernel(h2_ref, comb_ref, x1_ref, w1_ref, w3_ref, w2_ref, fn_ref, o_ref, *, final):
    e = pl.program_id(1)

    @pl.when(e == 0)
    def _():
        o_ref[...] = x1_ref[...]

    h = h2_ref[...]
    a = jnp.dot(h, w1_ref[0, 0].astype(BF16), preferred_element_type=F32)
    b = jnp.dot(h, w3_ref[0, 0].astype(BF16), preferred_element_type=F32)
    comb = comb_ref[...]
    lane = lax.broadcasted_iota(jnp.int32, comb.shape, 1)
    w = jnp.sum(jnp.where(lane == e + N_GROUPS, comb, 0.0), axis=-1, keepdims=True)
    act = (a * jax.nn.sigmoid(a)) * b * w
    o_ref[...] += jnp.dot(act.astype(BF16), w2_ref[0, 0].astype(BF16), preferred_element_type=F32)

    if final:
        @pl.when(e == N_EXPERTS - 1)
        def _():
            x = o_ref[...]
            ms = jnp.mean(x * x, axis=-1, keepdims=True)
            o_ref[...] = x * lax.rsqrt(ms + EPS) * fn_ref[...]


def _moe(h2, comb, x1, sw, layer, final_norm, tm, final):
    n = x1.shape[0]
    return pl.pallas_call(
        functools.partial(_moe_kernel, final=final),
        grid=(n // tm, N_EXPERTS),
        in_specs=[pl.BlockSpec((tm, D), lambda i, e: (i, 0)),
                  pl.BlockSpec((tm, ROUTE_LANES), lambda i, e: (i, 0)),
                  pl.BlockSpec((tm, D), lambda i, e: (i, 0)),
                  pl.BlockSpec((1, 1, D, D_EXPERT), lambda i, e: (layer, e, 0, 0)),
                  pl.BlockSpec((1, 1, D, D_EXPERT), lambda i, e: (layer, e, 0, 0)),
                  pl.BlockSpec((1, 1, D_EXPERT, D), lambda i, e: (layer, e, 0, 0)),
                  pl.BlockSpec((1, D), lambda i, e: (0, 0))],
        out_specs=pl.BlockSpec((tm, D), lambda i, e: (i, 0)),
        out_shape=jax.ShapeDtypeStruct((n, D), F32),
        compiler_params=_cparams(("parallel", "arbitrary")),
        name="moe",
    )(h2, comb, x1, sw['w1'], sw['w3'], sw['w2'], final_norm)


FFN_TM = 512
SC_CORES = 2
SC_SUBCORES = 16
SC_WORKERS = SC_CORES * SC_SUBCORES
SC_CHUNK = 32
HALF = D // 2


def _sc_mesh():
    return plsc.VectorSubcoreMesh(core_axis_name="c", subcore_axis_name="s", num_cores=SC_CORES,
                                  num_subcores=SC_SUBCORES)


def _sc_dispatch(x, wrow1, wrow2, dest1, dest2, n_sorted):
    n = x.shape[0]
    per_w = n // SC_WORKERS

    @functools.partial(
        pl.kernel, mesh=_sc_mesh(),
        out_type=(jax.ShapeDtypeStruct((n_sorted, HALF), jnp.int32), jax.ShapeDtypeStruct((n_sorted, 128), F32)),
        scratch_types=[pltpu.VMEM((SC_CHUNK,), jnp.int32), pltpu.VMEM((SC_CHUNK,), jnp.int32),
                       pltpu.VMEM((SC_CHUNK, HALF), jnp.int32), pltpu.VMEM((SC_CHUNK, 128), F32)],
    )
    def k(x_hbm, w1_hbm, w2_hbm, d1_hbm, d2_hbm, out_hbm, wout_hbm, i1_v, i2_v, rows_v, wrows_v):
        base = (lax.axis_index("s") * SC_CORES + lax.axis_index("c")) * per_w

        @pl.loop(0, per_w // SC_CHUNK)
        def _(j):
            off = base + j * SC_CHUNK
            pltpu.sync_copy(d1_hbm.at[pl.ds(off, SC_CHUNK)], i1_v)
            pltpu.sync_copy(d2_hbm.at[pl.ds(off, SC_CHUNK)], i2_v)
            pltpu.sync_copy(x_hbm.at[pl.ds(off, SC_CHUNK)], rows_v)
            pltpu.sync_copy(rows_v, out_hbm.at[i1_v])
            pltpu.sync_copy(rows_v, out_hbm.at[i2_v])
            pltpu.sync_copy(w1_hbm.at[pl.ds(off, SC_CHUNK)], wrows_v)
            pltpu.sync_copy(wrows_v, wout_hbm.at[i1_v])
            pltpu.sync_copy(w2_hbm.at[pl.ds(off, SC_CHUNK)], wrows_v)
            pltpu.sync_copy(wrows_v, wout_hbm.at[i2_v])

    return k(x, wrow1, wrow2, dest1, dest2)


def _sc_collect(y, dest1, dest2, n):
    per_w = n // SC_WORKERS

    @functools.partial(
        pl.kernel, mesh=_sc_mesh(),
        out_type=(jax.ShapeDtypeStruct((n, D), F32), jax.ShapeDtypeStruct((n, D), F32)),
        scratch_types=[pltpu.VMEM((SC_CHUNK,), jnp.int32), pltpu.VMEM((SC_CHUNK, D), F32)],
    )
    def k(y_hbm, d1_hbm, d2_hbm, g1_hbm, g2_hbm, idx_v, rows_v):
        base = (lax.axis_index("s") * SC_CORES + lax.axis_index("c")) * per_w

        @pl.loop(0, per_w // SC_CHUNK)
        def _(j):
            off = base + j * SC_CHUNK
            for d_hbm, g_hbm in ((d1_hbm, g1_hbm), (d2_hbm, g2_hbm)):
                pltpu.sync_copy(d_hbm.at[pl.ds(off, SC_CHUNK)], idx_v)
                pltpu.sync_copy(y_hbm.at[idx_v], rows_v)
                pltpu.sync_copy(rows_v, g_hbm.at[pl.ds(off, SC_CHUNK)])

    return k(y, dest1, dest2)


def _ffn_kernel(te_ref, nt_ref, xs_ref, ws_ref, w1_ref, w3_ref, w2_ref, y_ref, w1b_ref, w3b_ref, w2b_ref):
    k = pl.program_id(0)

    @pl.when(k < nt_ref[0])
    def _():
        @pl.when((k == 0) | (te_ref[k] != te_ref[jnp.maximum(k - 1, 0)]))
        def _():
            w1b_ref[...] = w1_ref[0, 0].astype(BF16)
            w3b_ref[...] = w3_ref[0, 0].astype(BF16)
            w2b_ref[...] = w2_ref[0, 0].astype(BF16)

        words = pltpu.bitcast(xs_ref[...], jnp.uint32)
        lo = pltpu.bitcast(lax.shift_left(words, jnp.uint32(16)), F32)
        hi = pltpu.bitcast(words & jnp.uint32(0xFFFF0000), F32)
        xs = jnp.concatenate([lo, hi], axis=1).astype(BF16)
        a = jnp.dot(xs, w1b_ref[...], preferred_element_type=F32)
        b = jnp.dot(xs, w3b_ref[...], preferred_element_type=F32)
        act = (a * jax.nn.sigmoid(a)) * b * ws_ref[:, 0:1]
        y_ref[...] = jnp.dot(act.astype(BF16), w2b_ref[...], preferred_element_type=F32)


def _ffn(xs, ws, tile_expert, n_tiles_used, sw, layer):
    n_sorted = xs.shape[0]
    w_in_spec = pl.BlockSpec((1, 1, D, D_EXPERT), lambda k, te, nt: (layer, te[k], 0, 0))
    return pl.pallas_call(
        _ffn_kernel,
        grid_spec=pltpu.PrefetchScalarGridSpec(
            num_scalar_prefetch=2,
            grid=(n_sorted // FFN_TM,),
            in_specs=[pl.BlockSpec((FFN_TM, HALF), lambda k, te, nt: (k, 0)),
                      pl.BlockSpec((FFN_TM, 128), lambda k, te, nt: (k, 0)), w_in_spec, w_in_spec,
                      pl.BlockSpec((1, 1, D_EXPERT, D), lambda k, te, nt: (layer, te[k], 0, 0))],
            out_specs=pl.BlockSpec((FFN_TM, D), lambda k, te, nt: (k, 0)),
            scratch_shapes=[pltpu.VMEM((D, D_EXPERT), BF16), pltpu.VMEM((D, D_EXPERT), BF16),
                            pltpu.VMEM((D_EXPERT, D), BF16)],
        ),
        out_shape=jax.ShapeDtypeStruct((n_sorted, D), F32),
        compiler_params=_cparams(("arbitrary",)),
        name="ffn",
    )(tile_expert, n_tiles_used, xs, ws, sw['w1'], sw['w3'], sw['w2'])


def _combine_kernel(x1_ref, g1_ref, g2_ref, fn_ref, o_ref, *, final):
    x = x1_ref[...] + (g1_ref[...] + g2_ref[...])
    if final:
        ms = jnp.mean(x * x, axis=-1, keepdims=True)
        x = x * lax.rsqrt(ms + EPS) * fn_ref[...]
    o_ref[...] = x


def _combine(x1, g1, g2, final_norm, tm, final):
    n = x1.shape[0]
    row = pl.BlockSpec((tm, D), lambda i: (i, 0))
    return pl.pallas_call(
        functools.partial(_combine_kernel, final=final),
        grid=(n // tm,),
        in_specs=[row, row, row, _full((1, D))],
        out_specs=row,
        out_shape=jax.ShapeDtypeStruct((n, D), F32),
        compiler_params=_cparams(("parallel",)),
        name="combine",
    )(x1, g1, g2, final_norm)


def _moe_sparse(h2, route, counts, x1, sw, layer, final_norm, final):
    n = x1.shape[0]
    n_sorted = 2 * n + N_EXPERTS * FFN_TM
    e1, e2 = route[:, 0].astype(jnp.int32), route[:, 1].astype(jnp.int32)
    rank1, rank2 = route[:, 4].astype(jnp.int32), route[:, 5].astype(jnp.int32)
    cnt = counts[0, N_GROUPS:N_GROUPS + N_EXPERTS].astype(jnp.int32)
    tiles = (cnt + FFN_TM - 1) // FFN_TM
    tile_end = jnp.cumsum(tiles)
    seg_start = (tile_end - tiles) * FFN_TM
    dest1 = jnp.take(seg_start, e1) + rank1
    dest2 = jnp.take(seg_start, e2) + rank2
    tile_ids = jnp.arange(n_sorted // FFN_TM, dtype=jnp.int32)
    tile_expert = jnp.minimum(jnp.sum(tile_end[None, :] <= tile_ids[:, None], axis=1), N_EXPERTS - 1)
    wrow1 = jnp.broadcast_to(route[:, 2:3], (n, 128))
    wrow2 = jnp.broadcast_to(route[:, 3:4], (n, 128))
    xs, ws = _sc_dispatch(h2, wrow1, wrow2, dest1, dest2, n_sorted)
    y = _ffn(xs, ws, tile_expert.astype(jnp.int32), tile_end[-1:].astype(jnp.int32), sw, layer)
    g1, g2 = _sc_collect(y, dest1, dest2, n)
    return _combine(x1, g1, g2, final_norm, 512, final)


def _t5_buckets(dist):
    max_exact = T5_BUCKETS // 2
    large = max_exact + (np.log(np.maximum(dist, 1) / max_exact) / np.log(T5_MAX_DIST / max_exact)
                         * (T5_BUCKETS - max_exact)).astype(np.int32)
    large = np.minimum(large, T5_BUCKETS - 1)
    return np.where(dist < max_exact, dist, large).astype(np.int32)


def _bias_tables(t5, g, dil, window, t_sample):
    nk = window // dil + 1
    hs = slice(g * HPG, (g + 1) * HPG)
    bias = t5[_t5_buckets(np.arange(nk) * dil)][:, hs].T
    rev = bias[:, ::-1]
    neg = lambda *shape: jnp.full(shape, NEG, F32)

    vec = jnp.concatenate([rev, neg(HPG, CL)], axis=1)
    both = jnp.tile(vec, (1, CL + 1))[:, :CL * 2 * CL].reshape(HPG, CL, 2 * CL)
    prev_t, cur_t = both[:, :, :CL], both[:, :, CL:]

    rows = []
    for r in range(t_sample):
        shift = r // dil
        per_u = jnp.concatenate([neg(HPG, shift), rev[:, :nk - 1 - shift]], axis=1)
        on_phase = (np.arange(dil) == r % dil)[None, None, :]
        rows.append(jnp.where(on_phase, per_u[:, :, None], NEG).reshape(HPG, window))
    cache_t = jnp.stack(rows, axis=1)
    r = np.arange(t_sample)[:, None]
    c = np.arange(t_sample)[None, :]
    ok = ((r - c) % dil == 0) & (r >= c)
    new_t = jnp.where(jnp.asarray(ok)[None], bias[:, np.clip((r - c) // dil, 0, nk - 1)], NEG)
    return prev_t, cur_t, cache_t, new_t


def _layer_weights(l, norm1, conv_ssd_w, conv_ssd_b, ssd_dt_bias, ssd_a_log, ssd_d, ssd_norm_w,
                   conv_lru_w, conv_lru_b, lru_br, lru_bi, lru_lambda, b_gate, norm2,
                   w_router_group, b_router_group, w_router_expert, b_router_expert):
    b_all = jnp.concatenate([b_gate[l], jnp.zeros((PW - 3 * D,), F32)])[None]

    def pad128(v):
        return jnp.concatenate([v, jnp.zeros((128 - v.shape[0],), F32)])[None]

    return {
        'norm1': norm1[l][None], 'b_all': b_all,
        'conv_ssd_w': conv_ssd_w[l], 'conv_ssd_b': conv_ssd_b[l][None],
        'dt_bias': pad128(ssd_dt_bias[l]), 'a_log': pad128(ssd_a_log[l]), 'd_skip': pad128(ssd_d[l]),
        'ssd_norm_w': ssd_norm_w[l][None],
        'conv_lru_w': conv_lru_w[l], 'conv_lru_b': conv_lru_b[l][None],
        'lru_br': lru_br[l][None], 'lru_bi': lru_bi[l][None], 'lru_lambda': lru_lambda[l][None],
        'norm2': norm2[l][None],
        'w_router': jnp.concatenate([w_router_group[l], w_router_expert[l],
                                     jnp.zeros((D, ROUTE_LANES - N_GROUPS - N_EXPERTS), F32)],
                                    axis=1).astype(BF16),
        'b_router': pad128(jnp.concatenate([b_router_group[l], b_router_expert[l]])),
    }


def _front_pad(buf):
    return jnp.pad(buf, ((0, 0), (8 - (CONV_W - 1), 0), (0, 0)))


def _cols(P, nb, t, start, width):
    return P.reshape(nb, t, PW)[:, :, start:start + width]


def _kv_rows(P, nb, t, g, n_rows):
    k = _cols(P, nb, t, C_K + g * GW, GW)[:, t - n_rows:].reshape(nb, n_rows, HPG, HD)
    v = _cols(P, nb, t, C_V + g * GW, GW)[:, t - n_rows:].reshape(nb, n_rows, HPG, HD)
    return jnp.stack([k, v], axis=2)


def _layer(x, lw, sw, tables, layer, nb, t, rows, tm, tm_mix, tm_res, tm_moe, conv_ssd, st_ssd, conv_lru, st_lru,
           caches, final_norm, final):
    if caches is None:
        P, *qkv = _proj(x, lw['norm1'], sw['w_all'], lw['b_all'], layer, tm, nb, t)
    else:
        (P,) = _proj(x, lw['norm1'], sw['w_all'], lw['b_all'], layer, tm)
    y_ssd, h_ssd = _ssd(P, _front_pad(conv_ssd), st_ssd.reshape(nb, SSD_INNER, SSD_STATE), lw, nb, t, rows)
    y_lru, h_lru = _lru(P, _front_pad(conv_lru), st_lru.reshape(nb, 1, LRU_W), lw, sw, layer, nb, t, rows)
    attn = []
    for g, (window, dil) in enumerate(ATTN_GROUPS):
        prev_t, cur_t, cache_t, new_t = tables[g]
        if caches is None:
            attn.append(_attn_prompt(qkv[g], qkv[3 + g], qkv[6 + g], prev_t, cur_t, dil, nb, t))
        else:
            attn.append(_attn_sample(P, caches[g], cache_t, new_t, g, layer, nb, t, window))
    mixed = _mix(P, y_ssd, y_lru, attn, sw, layer, tm_mix)
    if caches is None:
        x1, h2, _, route, counts = _res(x, mixed, lw, sw, layer, tm_res, True)
        x2 = _moe_sparse(h2, route, counts, x1, sw, layer, final_norm, final)
    else:
        x1, h2, comb = _res(x, mixed, lw, sw, layer, tm_res, False)
        x2 = _moe(h2, comb, x1, sw, layer, final_norm, tm_moe, final)
    states = (_cols(P, nb, t, C_XBC, SSD_CONV_DIM)[:, t - 3:],
              h_ssd.reshape(nb, SSD_HEADS, SSD_HEAD_DIM, SSD_STATE),
              _cols(P, nb, t, C_XR, LRU_W)[:, t - 3:],
              h_lru.reshape(nb, LRU_W)) + tuple(
                  _kv_rows(P, nb, t, g, min(w, t)) for g, (w, _) in enumerate(ATTN_GROUPS))
    return x2, states


def kernel(x_prompt, x_sample, cache_conv_ssd, state_ssd, cache_conv_lru, state_lru, cache_kv_w128, cache_kv_w512, cache_kv_w2048, norm1, w_in, conv_ssd_w, conv_ssd_b, ssd_dt_bias, ssd_a_log, ssd_d, ssd_norm_w, conv_lru_w, conv_lru_b, lru_wr, lru_br, lru_wi, lru_bi, lru_lambda, t5_bias, w_br_ssd, w_br_lru, w_br_attn, w_gate, b_gate, w_o, norm2, w_router_group, b_router_group, w_router_expert, b_router_expert, w1, w3, w2, final_norm):
    bp, tp, _ = x_prompt.shape
    bs, ts, _ = x_sample.shape
    xp = x_prompt.reshape(bp * tp, D)
    xs = x_sample.reshape(bs * ts, D)
    fn = final_norm[None]
    tables = [_bias_tables(t5_bias, g, dil, window, ts) for g, (window, dil) in enumerate(ATTN_GROUPS)]
    caches = [c.reshape(-1, HD) for c in (cache_kv_w128, cache_kv_w512, cache_kv_w2048)]
    sw = {name: w.astype(BF16) for name, w in dict(
        lru_wr=lru_wr, lru_wi=lru_wi, w_br_ssd=w_br_ssd, w_br_lru=w_br_lru, w_br_attn=w_br_attn, w_o=w_o).items()}
    sw.update(w1=w1, w3=w3, w2=w2)
    sw['w_all'] = _prep_w(w_gate, w_in)
    outs_p, outs_s = [], []
    for l in range(DEPTH):
        lw = _layer_weights(l, norm1, conv_ssd_w, conv_ssd_b, ssd_dt_bias, ssd_a_log, ssd_d, ssd_norm_w,
                            conv_lru_w, conv_lru_b, lru_br, lru_bi, lru_lambda, b_gate, norm2,
                            w_router_group, b_router_group, w_router_expert, b_router_expert)
        final = l == DEPTH - 1
        xp, sp = _layer(xp, lw, sw, tables, l, bp, tp, CL, 1024, 256, 512, 512,
                        jnp.zeros((bp, CONV_W - 1, SSD_CONV_DIM), F32),
                        jnp.zeros((bp, SSD_HEADS, SSD_HEAD_DIM, SSD_STATE), F32),
                        jnp.zeros((bp, CONV_W - 1, LRU_W), F32), jnp.zeros((bp, LRU_W), F32),
                        None, fn, final)
        xs, ss = _layer(xs, lw, sw, tables, l, bs, ts, ts, bs * ts, bs * ts, bs * ts, bs * ts,
                        cache_conv_ssd[l], state_ssd[l], cache_conv_lru[l], state_lru[l],
                        caches, fn, final)
        outs_p.append(sp)
        outs_s.append(ss)

    def stk(outs, i):
        return jnp.stack([o[i] for o in outs], axis=0)

    return ((xp.reshape(bp, tp, D), xs.reshape(bs, ts, D))
            + tuple(stk(outs_p, i) for i in range(7)) + tuple(stk(outs_s, i) for i in range(7)))
```

```python
import functools

import numpy as np
import jax
import jax.numpy as jnp
from jax import lax
from jax.experimental import pallas as pl
from jax.experimental.pallas import tpu as pltpu
from jax.experimental.pallas import tpu_sc as plsc

F32 = jnp.float32
BF16 = jnp.bfloat16
EPS = 1e-6
NEG = -1e30

D = 2048
DEPTH = 2
PAST_LEN = 16384
CL = 128
CONV_W = 4
SSD_HEADS = 16
SSD_HEAD_DIM = 64
SSD_INNER = 1024
SSD_STATE = 128
SSD_CONV_DIM = 1536
LRU_W = 1024
LRU_BLOCKS = 8
LRU_C = 8.0
ATTN_GROUPS = ((128, 1), (512, 4), (2048, 16))
HPG = 4
HD = 128
GW = HPG * HD
T5_BUCKETS = 32
T5_MAX_DIST = 2048
N_GROUPS = 4
PER_GROUP = 4
N_EXPERTS = 16
D_EXPERT = 512

TILE = 512
C_GATE = 0
C_Z = 6144
C_XR = 7168
C_GR = 8192
C_XBC = 9216
C_Q = 10752
C_K = 12288
C_V = 13824
C_DT = 15360
PW = 15872
N_GATE_TILES = (3 * D) // TILE
ROUTE_LANES = 128
VMEM_LIMIT = 56 * 1024 * 1024


def _cparams(sem):
    return pltpu.CompilerParams(dimension_semantics=sem, vmem_limit_bytes=VMEM_LIMIT)


def _full(shape):
    nd = len(shape)
    return pl.BlockSpec(shape, lambda *_: (0,) * nd)


def _of_layer(shape, layer):
    nd = len(shape)
    return pl.BlockSpec((1,) + shape, lambda *_: (layer,) + (0,) * nd)


_W_IN_STARTS = (0, 512, 2576, 3088, 3600, 4112, 1024, 1536, 2048, 4624, 5136, 5648, 6160, 6672, 7184,
                7696, 8208, 8720, 2560)
DT_SHIFT = SSD_HEADS


def _prep_kernel(blk_ref, shift_ref, width_ref, wg_ref, wa_ref, wb_ref, o_ref):
    j = pl.program_id(1)

    @pl.when(j < N_GATE_TILES)
    def _():
        o_ref[0] = wg_ref[0].astype(BF16)

    @pl.when(j >= N_GATE_TILES)
    def _():
        a = wa_ref[0]
        shifted = jnp.concatenate([a[DT_SHIFT:, :], wb_ref[0]], axis=0)
        val = jnp.where(shift_ref[j] == 0, a, shifted)
        row = lax.broadcasted_iota(jnp.int32, val.shape, 0)
        o_ref[0] = jnp.where(row < width_ref[j], val, 0.0).T.astype(BF16)


def _prep_w(w_gate, w_in):
    starts = (0,) * N_GATE_TILES + _W_IN_STARTS
    blk = jnp.asarray([s // TILE for s in starts], jnp.int32)
    shift = jnp.asarray([s % TILE for s in starts], jnp.int32)
    assert all(s % TILE in (0, DT_SHIFT) for s in starts) and w_in.shape[2] % DT_SHIFT == 0
    width = jnp.asarray([TILE] * (len(starts) - 1) + [SSD_HEADS], jnp.int32)
    per_tile = TILE // DT_SHIFT
    w_in_t = jnp.swapaxes(w_in, 1, 2)
    return pl.pallas_call(
        _prep_kernel,
        grid_spec=pltpu.PrefetchScalarGridSpec(
            num_scalar_prefetch=3,
            grid=(DEPTH, PW // TILE),
            in_specs=[pl.BlockSpec((1, D, TILE), lambda l, j, b, s, w: (l, 0, jnp.minimum(j, N_GATE_TILES - 1))),
                      pl.BlockSpec((1, TILE, D), lambda l, j, b, s, w: (l, b[j], 0)),
                      pl.BlockSpec((1, DT_SHIFT, D), lambda l, j, b, s, w: (l, (b[j] + 1) * per_tile, 0))],
            out_specs=pl.BlockSpec((1, D, TILE), lambda l, j, b, s, w: (l, 0, j)),
        ),
        out_shape=jax.ShapeDtypeStruct((DEPTH, D, PW), BF16),
        compiler_params=_cparams(("parallel", "arbitrary")),
        name="prep_w",
    )(blk, shift, width, w_gate, w_in_t, w_in_t)


def _proj_kernel(x_ref, nw_ref, w_ref, b_ref, o_ref, *rest, tm, phase_major):
    if phase_major:
        qkv_refs, (h_ref, acc_ref, ph_ref) = rest[:9], rest[9:]
    else:
        h_ref, acc_ref = rest
    j = pl.program_id(1)

    @pl.when(j == 0)
    def _():
        x = x_ref[...]
        ms = jnp.mean(x * x, axis=-1, keepdims=True)
        h_ref[...] = (x * lax.rsqrt(ms + EPS) * nw_ref[...]).astype(BF16)
        acc_ref[...] = jnp.zeros(acc_ref.shape, F32)

    prev = acc_ref[...]
    o_ref[...] = jnp.where(j <= N_GATE_TILES, jax.nn.sigmoid(prev), prev)
    acc_ref[...] = jnp.dot(h_ref[...], w_ref[0], preferred_element_type=F32) + b_ref[...]

    if phase_major:
        for part in range(3):
            for g, (_, dil) in enumerate(ATTN_GROUPS):
                ref = qkv_refs[part * 3 + g]

                @pl.when(j - 1 == C_Q // TILE + part * 3 + g)
                def _(ref=ref, dil=dil):
                    if dil == 1:
                        ref[0, 0] = o_ref[...].astype(BF16)
                    else:
                        for c in range(TILE // 128):
                            ph_ref[c] = o_ref[:, c * 128:(c + 1) * 128]
                        for p in range(dil):
                            for c in range(TILE // 128):
                                ref[0, p, :, c * 128:(c + 1) * 128] = (
                                    ph_ref[c, pl.ds(p, tm // dil, stride=dil), :].astype(BF16))


def _proj(x, nw, w_all, b_all, layer, tm, nb=None, t=None):
    n = x.shape[0]
    phase_major = nb is not None
    nt = PW // TILE
    out_specs = [pl.BlockSpec((tm, TILE), lambda i, j: (i, jnp.maximum(j - 1, 0)))]
    out_shape = [jax.ShapeDtypeStruct((n, PW), F32)]
    scratch = [pltpu.VMEM((tm, D), BF16), pltpu.VMEM((tm, TILE), F32)]
    if phase_major:
        tpb = t // tm
        for _ in range(3):
            for _, dil in ATTN_GROUPS:
                out_specs.append(pl.BlockSpec((1, dil, tm // dil, GW), lambda i, j: (i // tpb, 0, i % tpb, 0)))
                out_shape.append(jax.ShapeDtypeStruct((nb, dil, t // dil, GW), BF16))
        scratch.append(pltpu.VMEM((TILE // 128, tm, 128), F32))
    return pl.pallas_call(
        functools.partial(_proj_kernel, tm=tm, phase_major=phase_major),
        grid=(n // tm, nt + 1),
        in_specs=[pl.BlockSpec((tm, D), lambda i, j: (i, 0)),
                  pl.BlockSpec((1, D), lambda i, j: (0, 0)),
                  pl.BlockSpec((1, D, TILE), lambda i, j: (layer, 0, jnp.minimum(j, nt - 1))),
                  pl.BlockSpec((1, TILE), lambda i, j: (0, jnp.minimum(j, nt - 1)))],
        out_specs=out_specs,
        out_shape=out_shape,
        scratch_shapes=scratch,
        compiler_params=_cparams(("parallel", "arbitrary")),
        name="proj",
    )(x, nw, w_all, b_all)


def _conv_step(x_ref, xp_ref, cw_ref, cb_ref, rows, out_rows):
    xp_ref[8:8 + rows, :] = _bf16_round(x_ref[...])
    cw = _bf16_round(cw_ref[...])
    acc = cw[0:1, :] * xp_ref[5:5 + out_rows, :]
    for j in range(1, CONV_W):
        acc = acc + cw[j:j + 1, :] * xp_ref[5 + j:5 + j + out_rows, :]
    tail = xp_ref[rows:rows + 8, :]
    xp_ref[0:8, :] = tail
    return acc + cb_ref[...]


def _bf16_round(x):
    return x.astype(BF16).astype(F32)


def _softplus(x):
    return jnp.maximum(x, 0.0) + jnp.log1p(jnp.exp(-jnp.abs(x)))


def _ssd_kernel(z_ref, xbc_ref, dt_ref, tail_ref, h0_ref, cw_ref, cb_ref, dtb_ref, alog_ref, dsk_ref, nw_ref,
                y_ref, hf_ref, xp_ref, act_ref, st_ref, ysc_ref, *, rows, n_chunks):
    c = pl.program_id(1)

    @pl.when(c == 0)
    def _():
        xp_ref[0:8, :] = _bf16_round(tail_ref[0])
        st_ref[...] = h0_ref[0]

    if rows < CL:
        xp_ref[8 + rows:, :] = jnp.zeros((CL - rows, SSD_CONV_DIM), F32)
    conv = _conv_step(xbc_ref, xp_ref, cw_ref, cb_ref, rows, CL)
    act_ref[...] = conv * jax.nn.sigmoid(conv)

    row = lax.broadcasted_iota(jnp.int32, (CL, 128), 0)
    lane = lax.broadcasted_iota(jnp.int32, (CL, 128), 1)
    raw = dt_ref[...]
    if rows < CL:
        raw = jnp.concatenate([raw, jnp.zeros((CL - rows, 128), F32)], axis=0)
    dt = _softplus(raw + dtb_ref[...])
    dt = jnp.where((lane < SSD_HEADS) & (row < rows), dt, 0.0)
    da = dt * (-jnp.exp(alog_ref[...]))
    acs = da
    d = 1
    while d < CL:
        acs = acs + jnp.where(row >= d, pltpu.roll(acs, d, 0), 0.0)
        d *= 2
    acs_t = acs.T
    last = acs[CL - 1:CL, :]
    e_acs = jnp.exp(acs)
    to_end = jnp.exp(last - acs)
    cdec = jnp.exp(last)
    causal = row >= lane
    lo_lane = lane < SSD_HEAD_DIM
    lo_row = row < SSD_HEAD_DIM
    dsk = dsk_ref[...]

    def pair_cols(arr, h):
        return jnp.where(lo_lane, arr[:, h:h + 1], arr[:, h + 1:h + 2])

    nt = (((1,), (1,)), ((), ()))
    for g in range(2):
        bm = act_ref[:, SSD_INNER + g * SSD_STATE:SSD_INNER + (g + 1) * SSD_STATE].astype(BF16)
        cm = act_ref[:, SSD_INNER + 256 + g * SSD_STATE:SSD_INNER + 256 + (g + 1) * SSD_STATE].astype(BF16)
        cb = lax.dot_general(cm, bm, nt, preferred_element_type=F32)
        for pp in range(4):
            h = g * 8 + 2 * pp
            sl = slice(h * SSD_HEAD_DIM, h * SSD_HEAD_DIM + 128)
            xs = act_ref[:, sl]
            xdt = xs * pair_cols(dt, h)
            xdt_b = xdt.astype(BF16)
            ys = []
            for hh in (h, h + 1):
                seg = acs[:, hh:hh + 1] - acs_t[hh:hh + 1, :]
                decay = jnp.exp(jnp.where(causal, seg, -jnp.inf))
                ys.append(jnp.dot((cb * decay).astype(BF16), xdt_b, preferred_element_type=F32))
            y_diag = jnp.where(lo_lane, ys[0], ys[1])
            st = st_ref[sl, :]
            y_off = lax.dot_general(cm, st.astype(BF16), nt, preferred_element_type=F32) * pair_cols(e_acs, h)
            d_pair = jnp.where(lo_lane, dsk[:, h:h + 1], dsk[:, h + 1:h + 2])
            ysc_ref[:, sl] = y_diag + y_off + d_pair * xs
            xdte_t = (xdt * pair_cols(to_end, h)).T.astype(BF16)
            s_new = jnp.dot(xdte_t, bm, preferred_element_type=F32)
            dec = jnp.where(lo_row, cdec[:, h:h + 1], cdec[:, h + 1:h + 2])
            st_ref[sl, :] = dec * st + s_new

    zz = z_ref[...]
    yg = ysc_ref[0:rows, :] * (zz * jax.nn.sigmoid(zz))
    gw = SSD_INNER // 2
    for g in range(2):
        part = yg[:, g * gw:(g + 1) * gw]
        ms = jnp.mean(part * part, axis=-1, keepdims=True)
        y_ref[:, g * gw:(g + 1) * gw] = (part * lax.rsqrt(ms + EPS)
                                         * nw_ref[:, g * gw:(g + 1) * gw]).astype(y_ref.dtype)

    @pl.when(c == n_chunks - 1)
    def _():
        hf_ref[0] = st_ref[...]


def _ssd(P, tail, h0, lw, nb, t, rows):
    nc = t // rows
    kern = functools.partial(_ssd_kernel, rows=rows, n_chunks=nc)
    return pl.pallas_call(
        kern,
        grid=(nb, nc),
        in_specs=[pl.BlockSpec((rows, SSD_INNER), lambda b, c: (b * nc + c, C_Z // SSD_INNER)),
                  pl.BlockSpec((rows, SSD_CONV_DIM), lambda b, c: (b * nc + c, C_XBC // SSD_CONV_DIM)),
                  pl.BlockSpec((rows, 128), lambda b, c: (b * nc + c, C_DT // 128)),
                  pl.BlockSpec((1, 8, SSD_CONV_DIM), lambda b, c: (b, 0, 0)),
                  pl.BlockSpec((1, SSD_INNER, SSD_STATE), lambda b, c: (b, 0, 0)),
                  _full((CONV_W, SSD_CONV_DIM)), _full((1, SSD_CONV_DIM)),
                  _full((1, 128)), _full((1, 128)), _full((1, 128)), _full((1, SSD_INNER))],
        out_specs=[pl.BlockSpec((rows, SSD_INNER), lambda b, c: (b * nc + c, 0)),
                   pl.BlockSpec((1, SSD_INNER, SSD_STATE), lambda b, c: (b, 0, 0))],
        out_shape=[jax.ShapeDtypeStruct((nb * t, SSD_INNER), BF16 if rows % 16 == 0 else F32),
                   jax.ShapeDtypeStruct((nb, SSD_INNER, SSD_STATE), F32)],
        scratch_shapes=[pltpu.VMEM((8 + CL, SSD_CONV_DIM), F32),
                        pltpu.VMEM((CL, SSD_CONV_DIM), F32),
                        pltpu.VMEM((SSD_INNER, SSD_STATE), F32),
                        pltpu.VMEM((CL, SSD_INNER), F32)],
        compiler_params=_cparams(("parallel", "arbitrary")),
        name="ssd",
    )(P, P, P, tail, h0, lw['conv_ssd_w'], lw['conv_ssd_b'], lw['dt_bias'], lw['a_log'], lw['d_skip'], lw['ssd_norm_w'])


def _lru_kernel(xr_ref, gr_ref, tail_ref, h0_ref, cw_ref, cb_ref, wr_ref, br_ref, wi_ref, bi_ref, lam_ref,
                y_ref, hl_ref, xp_ref, h_ref, *, rows, n_chunks):
    c = pl.program_id(1)

    @pl.when(c == 0)
    def _():
        xp_ref[0:8, :] = _bf16_round(tail_ref[0])
        h_ref[...] = h0_ref[0]

    x = _conv_step(xr_ref, xp_ref, cw_ref, cb_ref, rows, rows)
    xb = x.astype(BF16)
    rs, is_ = [], []
    for n in range(LRU_BLOCKS):
        blk = xb[:, n * 128:(n + 1) * 128]
        rs.append(jnp.dot(blk, wr_ref[0, n], preferred_element_type=F32))
        is_.append(jnp.dot(blk, wi_ref[0, n], preferred_element_type=F32))
    r_gate = jax.nn.sigmoid(jnp.concatenate(rs, axis=1) + br_ref[...])
    i_gate = jax.nn.sigmoid(jnp.concatenate(is_, axis=1) + bi_ref[...])
    log_a = -LRU_C * r_gate * _softplus(-lam_ref[...])
    a = jnp.exp(log_a)
    th = jnp.tanh(log_a)
    b = jnp.sqrt(-2.0 * th / (1.0 - th)) * (i_gate * x)
    in_group = lax.broadcasted_iota(jnp.int32, (rows, LRU_W), 0) % 8
    for d in (1, 2, 4):
        a_s = jnp.where(in_group >= d, pltpu.roll(a, d, 0), 1.0)
        b_s = jnp.where(in_group >= d, pltpu.roll(b, d, 0), 0.0)
        b = a * b_s + b
        a = a * a_s
    carry = h_ref[...]
    groups = []
    for g in range(rows // 8):
        h_g = b[8 * g:8 * g + 8, :] + a[8 * g:8 * g + 8, :] * carry
        groups.append(h_g)
        carry = h_g[7:8, :]
    h = jnp.concatenate(groups, axis=0) if len(groups) > 1 else groups[0]
    last = carry
    h_ref[...] = last
    y_ref[...] = (h * jax.nn.gelu(gr_ref[...])).astype(y_ref.dtype)

    @pl.when(c == n_chunks - 1)
    def _():
        hl_ref[0] = last


def _lru(P, tail, h0, lw, sw, layer, nb, t, rows):
    nc = t // rows
    kern = functools.partial(_lru_kernel, rows=rows, n_chunks=nc)
    y_dtype = BF16 if rows % 16 == 0 else F32
    return pl.pallas_call(
        kern,
        grid=(nb, nc),
        in_specs=[pl.BlockSpec((rows, LRU_W), lambda b, c: (b * nc + c, C_XR // LRU_W)),
                  pl.BlockSpec((rows, LRU_W), lambda b, c: (b * nc + c, C_GR // LRU_W)),
                  pl.BlockSpec((1, 8, LRU_W), lambda b, c: (b, 0, 0)),
                  pl.BlockSpec((1, 1, LRU_W), lambda b, c: (b, 0, 0)),
                  _full((CONV_W, LRU_W)), _full((1, LRU_W)),
                  _of_layer((LRU_BLOCKS, 128, 128), layer), _full((1, LRU_W)),
                  _of_layer((LRU_BLOCKS, 128, 128), layer), _full((1, LRU_W)), _full((1, LRU_W))],
        out_specs=[pl.BlockSpec((rows, LRU_W), lambda b, c: (b * nc + c, 0)),
                   pl.BlockSpec((1, 1, LRU_W), lambda b, c: (b, 0, 0))],
        out_shape=[jax.ShapeDtypeStruct((nb * t, LRU_W), y_dtype),
                   jax.ShapeDtypeStruct((nb, 1, LRU_W), F32)],
        scratch_shapes=[pltpu.VMEM((8 + rows, LRU_W), F32), pltpu.VMEM((1, LRU_W), F32)],
        compiler_params=_cparams(("parallel", "arbitrary")),
        name="lru",
    )(P, P, tail, h0, lw['conv_lru_w'], lw['conv_lru_b'], sw['lru_wr'], lw['lru_br'], sw['lru_wi'], lw['lru_bi'],
      lw['lru_lambda'])


def _attn_kernel(q_ref, kp_ref, vp_ref, kc_ref, vc_ref, ba_ref, bb_ref, o_ref, lse_ref, *, sub):
    scale = HD ** -0.5
    nt = (((1,), (1,)), ((), ()))
    lane = lax.broadcasted_iota(jnp.int32, (CL, 128), 1)
    for s in range(sub):
        rows = slice(s * CL, (s + 1) * CL)
        before = slice((s - 1) * CL, s * CL)
        lse_all = jnp.zeros((CL, 128), F32)
        for h in range(HPG):
            sl = slice(h * HD, (h + 1) * HD)
            q = q_ref[0, 0, rows, sl]
            k_before = kp_ref[0, 0, :, sl] if s == 0 else kc_ref[0, 0, before, sl]
            v_before = vp_ref[0, 0, :, sl] if s == 0 else vc_ref[0, 0, before, sl]
            sa = lax.dot_general(q, k_before, nt, preferred_element_type=F32) * scale + ba_ref[h]
            if s == 0:
                sa = jnp.where(pl.program_id(2) > 0, sa, NEG)
            sb = lax.dot_general(q, kc_ref[0, 0, rows, sl], nt, preferred_element_type=F32) * scale + bb_ref[h]
            m = jnp.maximum(jnp.max(sa, axis=-1, keepdims=True), jnp.max(sb, axis=-1, keepdims=True))
            pa = jnp.exp(sa - m)
            pb = jnp.exp(sb - m)
            l = jnp.sum(pa, axis=-1, keepdims=True) + jnp.sum(pb, axis=-1, keepdims=True)
            inv = 1.0 / l
            o_ref[rows, sl] = (jnp.dot((pa * inv).astype(BF16), v_before, preferred_element_type=F32)
                               + jnp.dot((pb * inv).astype(BF16), vc_ref[0, 0, rows, sl],
                                         preferred_element_type=F32))
            lse_all = jnp.where(lane == h, m + jnp.log(l), lse_all)
        lse_ref[rows, :] = lse_all


def _attn_prompt(q, k, v, bias_a, bias_b, dil, nb, t):
    n = nb * t
    sub = min(4, t // dil // CL)
    nstep = t // dil // (sub * CL)
    cur = pl.BlockSpec((1, 1, sub * CL, GW), lambda b, p, i: (b, p, i, 0))
    prev = pl.BlockSpec((1, 1, CL, GW), lambda b, p, i: (b, p, jnp.maximum(i * sub - 1, 0), 0))
    o, lse = pl.pallas_call(
        functools.partial(_attn_kernel, sub=sub),
        grid=(nb, dil, nstep),
        in_specs=[cur, prev, prev, cur, cur, _full((HPG, CL, CL)), _full((HPG, CL, CL))],
        out_specs=[pl.BlockSpec((sub * CL, GW), lambda b, p, i: (b * nstep + i, p)),
                   pl.BlockSpec((sub * CL, 128), lambda b, p, i: (b * nstep + i, p))],
        out_shape=[jax.ShapeDtypeStruct((n // dil, dil * GW), F32),
                   jax.ShapeDtypeStruct((n // dil, dil * 128), F32)],
        compiler_params=_cparams(("parallel", "parallel", "arbitrary")),
        name=f"attn_prompt_d{dil}",
    )(q, k, v, k, v, bias_a, bias_b)
    return o.reshape(n, GW), lse.reshape(n, 128)


def _attn_sample_kernel(q_ref, kv_ref, kb_ref, vb_ref, ba_ref, bb_ref, o_ref, lse_ref):
    scale = HD ** -0.5
    nt = (((1,), (1,)), ((), ()))
    rows = o_ref.shape[0]
    window = kv_ref.shape[0] // (2 * HPG)
    lane = lax.broadcasted_iota(jnp.int32, (rows, 128), 1)
    lse_all = jnp.zeros((rows, 128), F32)
    for h in range(HPG):
        sl = slice(h * HD, (h + 1) * HD)
        q = q_ref[:, sl].astype(BF16)
        k_cache = kv_ref[pl.ds(h, window, stride=2 * HPG), :].astype(BF16)
        v_cache = kv_ref[pl.ds(HPG + h, window, stride=2 * HPG), :].astype(BF16)
        sa = lax.dot_general(q, k_cache, nt, preferred_element_type=F32) * scale + ba_ref[h]
        sb = lax.dot_general(q, kb_ref[:, sl].astype(BF16), nt, preferred_element_type=F32) * scale + bb_ref[h]
        m = jnp.maximum(jnp.max(sa, axis=-1, keepdims=True), jnp.max(sb, axis=-1, keepdims=True))
        pa = jnp.exp(sa - m)
        pb = jnp.exp(sb - m)
        l = jnp.sum(pa, axis=-1, keepdims=True) + jnp.sum(pb, axis=-1, keepdims=True)
        inv = 1.0 / l
        o_ref[:, sl] = (jnp.dot((pa * inv).astype(BF16), v_cache, preferred_element_type=F32)
                        + jnp.dot((pb * inv).astype(BF16), vb_ref[:, sl].astype(BF16), preferred_element_type=F32))
        lse_all = jnp.where(lane == h, m + jnp.log(l), lse_all)
    lse_ref[...] = lse_all


def _attn_sample(Ps, cache_rows, bias_a, bias_b, g, layer, nb, t, window):
    tq, tk, tv = C_Q // GW + g, C_K // GW + g, C_V // GW + g
    out_spec = pl.BlockSpec((t, GW), lambda b: (b, 0))
    return pl.pallas_call(
        _attn_sample_kernel,
        grid=(nb,),
        in_specs=[pl.BlockSpec((t, GW), lambda b: (b, tq)),
                  pl.BlockSpec((window * 2 * HPG, HD), lambda b: (layer * nb + b, 0)),
                  pl.BlockSpec((t, GW), lambda b: (b, tk)),
                  pl.BlockSpec((t, GW), lambda b: (b, tv)),
                  _full((HPG, t, window)), _full((HPG, t, t))],
        out_specs=[out_spec, pl.BlockSpec((t, 128), lambda b: (b, 0))],
        out_shape=[jax.ShapeDtypeStruct((nb * t, GW), F32), jax.ShapeDtypeStruct((nb * t, 128), F32)],
        compiler_params=_cparams(("parallel",)),
        name=f"attn_sample_w{window}",
    )(Ps, cache_rows, Ps, Ps, bias_a, bias_b)


def _mix_kernel(gs_ref, gl_ref, ga_ref, ys_ref, yl_ref, o0_ref, o1_ref, o2_ref, l0_ref, l1_ref, l2_ref,
                wbs_ref, wbl_ref, wba_ref, out_ref):
    l0, l1, l2 = l0_ref[...], l1_ref[...], l2_ref[...]
    m = jnp.maximum(jnp.maximum(l0, l1), l2)
    e0, e1, e2 = jnp.exp(l0 - m), jnp.exp(l1 - m), jnp.exp(l2 - m)
    den = e0 + e1 + e2
    w0, w1, w2 = e0 / den, e1 / den, e2 / den
    tm = out_ref.shape[0]
    heads = []
    for h in range(HPG):
        sl = slice(h * HD, (h + 1) * HD)
        per_head = lambda w: jnp.broadcast_to(w[:, h:h + 1], (tm, HD))
        heads.append(o0_ref[:, sl] * per_head(w0) + o1_ref[:, sl] * per_head(w1) + o2_ref[:, sl] * per_head(w2))
    ya = jnp.concatenate(heads, axis=1)
    mixed = (gs_ref[...] * jnp.dot(ys_ref[...].astype(BF16), wbs_ref[0], preferred_element_type=F32)
             + gl_ref[...] * jnp.dot(yl_ref[...].astype(BF16), wbl_ref[0], preferred_element_type=F32)
             + ga_ref[...] * jnp.dot(ya.astype(BF16), wba_ref[0], preferred_element_type=F32))
    out_ref[...] = mixed.astype(BF16)


def _mix(P, y_ssd, y_lru, attn, sw, layer, tm):
    n = P.shape[0]
    row = lambda w: pl.BlockSpec((tm, w), lambda i: (i, 0))
    (o0, s0), (o1, s1), (o2, s2) = attn
    return pl.pallas_call(
        _mix_kernel,
        grid=(n // tm,),
        in_specs=[pl.BlockSpec((tm, D), lambda i: (i, 0)), pl.BlockSpec((tm, D), lambda i: (i, 1)),
                  pl.BlockSpec((tm, D), lambda i: (i, 2)),
                  row(SSD_INNER), row(LRU_W), row(GW), row(GW), row(GW), row(128), row(128), row(128),
                  _of_layer((SSD_INNER, D), layer), _of_layer((LRU_W, D), layer), _of_layer((GW, D), layer)],
        out_specs=row(D),
        out_shape=jax.ShapeDtypeStruct((n, D), BF16),
        compiler_params=_cparams(("parallel",)),
        name="mix",
    )(P, P, P, y_ssd, y_lru, o0, o1, o2, s0, s1, s2, sw['w_br_ssd'], sw['w_br_lru'], sw['w_br_attn'])


def _res_kernel(x_ref, mixed_ref, wo_ref, n2_ref, wr_ref, br_ref, x1_ref, h2_ref, comb_ref, *rest, dispatch):
    x1 = x_ref[...] + jnp.dot(mixed_ref[...], wo_ref[0], preferred_element_type=F32)
    x1_ref[...] = x1
    ms = jnp.mean(x1 * x1, axis=-1, keepdims=True)
    h2 = x1 * lax.rsqrt(ms + EPS) * n2_ref[...]
    h2b = h2.astype(BF16)
    if dispatch:
        lo = pltpu.bitcast(h2b[:, :HALF].astype(F32), jnp.uint32)
        hi = pltpu.bitcast(h2b[:, HALF:].astype(F32), jnp.uint32)
        h2_ref[...] = pltpu.bitcast(lax.shift_right_logical(lo, jnp.uint32(16)) | hi, jnp.int32)
    else:
        h2_ref[...] = h2b
    logits = jnp.dot(h2b, wr_ref[...], preferred_element_type=F32) + br_ref[...]
    lane = lax.broadcasted_iota(jnp.int32, logits.shape, 1).astype(F32)
    big = float(ROUTE_LANES)

    def first_max(vals, ok):
        v = jnp.where(ok, vals, NEG)
        top = jnp.max(v, axis=-1, keepdims=True)
        idx = jnp.min(jnp.where(ok & (v == top), lane, big), axis=-1, keepdims=True)
        return top, idx

    is_g = lane < N_GROUPS
    gmax, gsel = first_max(logits, is_g)
    gp = 1.0 / jnp.sum(jnp.where(is_g, jnp.exp(logits - gmax), 0.0), axis=-1, keepdims=True)
    lo = N_GROUPS + PER_GROUP * gsel
    is_e = (lane >= lo) & (lane < lo + PER_GROUP)
    t1, i1 = first_max(logits, is_e)
    t2, i2 = first_max(logits, is_e & (lane != i1))
    e2 = jnp.exp(t2 - t1)
    w1 = gp / (1.0 + e2)
    w2 = gp * e2 / (1.0 + e2)
    comb_ref[...] = jnp.where(lane == i1, w1, 0.0) + jnp.where(lane == i2, w2, 0.0)

    if dispatch:
        route_ref, cnt_ref, carry_ref = rest

        @pl.when(pl.program_id(0) == 0)
        def _():
            carry_ref[...] = jnp.zeros(carry_ref.shape, F32)

        tm = x1.shape[0]
        onehot = jnp.where((lane == i1) | (lane == i2), 1.0, 0.0)
        r = lax.broadcasted_iota(jnp.int32, (tm, tm), 0)
        c = lax.broadcasted_iota(jnp.int32, (tm, tm), 1)
        earlier = jnp.where(r > c, 1.0, 0.0).astype(BF16)
        before = jnp.dot(earlier, onehot.astype(BF16), preferred_element_type=F32) + carry_ref[...]
        rank1 = jnp.sum(jnp.where(lane == i1, before, 0.0), axis=-1, keepdims=True)
        rank2 = jnp.sum(jnp.where(lane == i2, before, 0.0), axis=-1, keepdims=True)
        carry_ref[...] += jnp.sum(onehot, axis=0, keepdims=True)
        cnt_ref[...] = carry_ref[...]
        fields = (i1 - N_GROUPS, i2 - N_GROUPS, w1, w2, rank1, rank2)
        route = jnp.zeros(logits.shape, F32)
        for k, val in enumerate(fields):
            route = jnp.where(lane == k, val, route)
        route_ref[...] = route


def _res(x, mixed, lw, sw, layer, tm, dispatch):
    n = x.shape[0]
    h2_cols, h2_dtype = (HALF, jnp.int32) if dispatch else (D, BF16)
    out_specs = [pl.BlockSpec((tm, D), lambda i: (i, 0)), pl.BlockSpec((tm, h2_cols), lambda i: (i, 0)),
                 pl.BlockSpec((tm, ROUTE_LANES), lambda i: (i, 0))]
    out_shape = [jax.ShapeDtypeStruct((n, D), F32), jax.ShapeDtypeStruct((n, h2_cols), h2_dtype),
                 jax.ShapeDtypeStruct((n, ROUTE_LANES), F32)]
    scratch = []
    if dispatch:
        out_specs += [pl.BlockSpec((tm, ROUTE_LANES), lambda i: (i, 0)), _full((1, ROUTE_LANES))]
        out_shape += [jax.ShapeDtypeStruct((n, ROUTE_LANES), F32), jax.ShapeDtypeStruct((1, ROUTE_LANES), F32)]
        scratch = [pltpu.VMEM((1, ROUTE_LANES), F32)]
    return pl.pallas_call(
        functools.partial(_res_kernel, dispatch=dispatch),
        grid=(n // tm,),
        in_specs=[pl.BlockSpec((tm, D), lambda i: (i, 0)), pl.BlockSpec((tm, D), lambda i: (i, 0)),
                  _of_layer((D, D), layer), _full((1, D)), _full((D, ROUTE_LANES)), _full((1, ROUTE_LANES))],
        out_specs=out_specs,
        out_shape=out_shape,
        scratch_shapes=scratch,
        compiler_params=_cparams(("arbitrary",)),
        name="res_router",
    )(x, mixed, sw['w_o'], lw['norm2'], lw['w_router'], lw['b_router'])


def _moe_kernel(h2_ref, comb_ref, x1_ref, w1_ref, w3_ref, w2_ref, fn_ref, o_ref, *, final):
    e = pl.program_id(1)

    @pl.when(e == 0)
    def _():
        o_ref[...] = x1_ref[...]

    h = h2_ref[...]
    a = jnp.dot(h, w1_ref[0, 0].astype(BF16), preferred_element_type=F32)
    b = jnp.dot(h, w3_ref[0, 0].astype(BF16), preferred_element_type=F32)
    comb = comb_ref[...]
    lane = lax.broadcasted_iota(jnp.int32, comb.shape, 1)
    w = jnp.sum(jnp.where(lane == e + N_GROUPS, comb, 0.0), axis=-1, keepdims=True)
    act = (a * jax.nn.sigmoid(a)) * b * w
    o_ref[...] += jnp.dot(act.astype(BF16), w2_ref[0, 0].astype(BF16), preferred_element_type=F32)

    if final:
        @pl.when(e == N_EXPERTS - 1)
        def _():
            x = o_ref[...]
            ms = jnp.mean(x * x, axis=-1, keepdims=True)
            o_ref[...] = x * lax.rsqrt(ms + EPS) * fn_ref[...]


def _moe(h2, comb, x1, sw, layer, final_norm, tm, final):
    n = x1.shape[0]
    return pl.pallas_call(
        functools.partial(_moe_kernel, final=final),
        grid=(n // tm, N_EXPERTS),
        in_specs=[pl.BlockSpec((tm, D), lambda i, e: (i, 0)),
                  pl.BlockSpec((tm, ROUTE_LANES), lambda i, e: (i, 0)),
                  pl.BlockSpec((tm, D), lambda i, e: (i, 0)),
                  pl.BlockSpec((1, 1, D, D_EXPERT), lambda i, e: (layer, e, 0, 0)),
                  pl.BlockSpec((1, 1, D, D_EXPERT), lambda i, e: (layer, e, 0, 0)),
                  pl.BlockSpec((1, 1, D_EXPERT, D), lambda i, e: (layer, e, 0, 0)),
                  pl.BlockSpec((1, D), lambda i, e: (0, 0))],
        out_specs=pl.BlockSpec((tm, D), lambda i, e: (i, 0)),
        out_shape=jax.ShapeDtypeStruct((n, D), F32),
        compiler_params=_cparams(("parallel", "arbitrary")),
        name="moe",
    )(h2, comb, x1, sw['w1'], sw['w3'], sw['w2'], final_norm)


FFN_TM = 512
SC_CORES = 2
SC_SUBCORES = 16
SC_WORKERS = SC_CORES * SC_SUBCORES
SC_CHUNK = 32
HALF = D // 2


def _sc_mesh():
    return plsc.VectorSubcoreMesh(core_axis_name="c", subcore_axis_name="s", num_cores=SC_CORES,
                                  num_subcores=SC_SUBCORES)


def _sc_dispatch(x, wrow1, wrow2, dest1, dest2, n_sorted):
    n = x.shape[0]
    per_w = n // SC_WORKERS

    @functools.partial(
        pl.kernel, mesh=_sc_mesh(),
        out_type=(jax.ShapeDtypeStruct((n_sorted, HALF), jnp.int32), jax.ShapeDtypeStruct((n_sorted, 128), F32)),
        scratch_types=[pltpu.VMEM((SC_CHUNK,), jnp.int32), pltpu.VMEM((SC_CHUNK,), jnp.int32),
                       pltpu.VMEM((SC_CHUNK, HALF), jnp.int32), pltpu.VMEM((SC_CHUNK, 128), F32)],
    )
    def k(x_hbm, w1_hbm, w2_hbm, d1_hbm, d2_hbm, out_hbm, wout_hbm, i1_v, i2_v, rows_v, wrows_v):
        base = (lax.axis_index("s") * SC_CORES + lax.axis_index("c")) * per_w

        @pl.loop(0, per_w // SC_CHUNK)
        def _(j):
            off = base + j * SC_CHUNK
            pltpu.sync_copy(d1_hbm.at[pl.ds(off, SC_CHUNK)], i1_v)
            pltpu.sync_copy(d2_hbm.at[pl.ds(off, SC_CHUNK)], i2_v)
            pltpu.sync_copy(x_hbm.at[pl.ds(off, SC_CHUNK)], rows_v)
            pltpu.sync_copy(rows_v, out_hbm.at[i1_v])
            pltpu.sync_copy(rows_v, out_hbm.at[i2_v])
            pltpu.sync_copy(w1_hbm.at[pl.ds(off, SC_CHUNK)], wrows_v)
            pltpu.sync_copy(wrows_v, wout_hbm.at[i1_v])
            pltpu.sync_copy(w2_hbm.at[pl.ds(off, SC_CHUNK)], wrows_v)
            pltpu.sync_copy(wrows_v, wout_hbm.at[i2_v])

    return k(x, wrow1, wrow2, dest1, dest2)


def _sc_collect(y, dest1, dest2, n):
    per_w = n // SC_WORKERS

    @functools.partial(
        pl.kernel, mesh=_sc_mesh(),
        out_type=(jax.ShapeDtypeStruct((n, D), F32), jax.ShapeDtypeStruct((n, D), F32)),
        scratch_types=[pltpu.VMEM((SC_CHUNK,), jnp.int32), pltpu.VMEM((SC_CHUNK, D), F32)],
    )
    def k(y_hbm, d1_hbm, d2_hbm, g1_hbm, g2_hbm, idx_v, rows_v):
        base = (lax.axis_index("s") * SC_CORES + lax.axis_index("c")) * per_w

        @pl.loop(0, per_w // SC_CHUNK)
        def _(j):
            off = base + j * SC_CHUNK
            for d_hbm, g_hbm in ((d1_hbm, g1_hbm), (d2_hbm, g2_hbm)):
                pltpu.sync_copy(d_hbm.at[pl.ds(off, SC_CHUNK)], idx_v)
                pltpu.sync_copy(y_hbm.at[idx_v], rows_v)
                pltpu.sync_copy(rows_v, g_hbm.at[pl.ds(off, SC_CHUNK)])

    return k(y, dest1, dest2)


def _ffn_kernel(te_ref, nt_ref, xs_ref, ws_ref, w1_ref, w3_ref, w2_ref, y_ref, w1b_ref, w3b_ref, w2b_ref):
    k = pl.program_id(0)

    @pl.when(k < nt_ref[0])
    def _():
        @pl.when((k == 0) | (te_ref[k] != te_ref[jnp.maximum(k - 1, 0)]))
        def _():
            w1b_ref[...] = w1_ref[0, 0].astype(BF16)
            w3b_ref[...] = w3_ref[0, 0].astype(BF16)
            w2b_ref[...] = w2_ref[0, 0].astype(BF16)

        words = pltpu.bitcast(xs_ref[...], jnp.uint32)
        lo = pltpu.bitcast(lax.shift_left(words, jnp.uint32(16)), F32)
        hi = pltpu.bitcast(words & jnp.uint32(0xFFFF0000), F32)
        xs = jnp.concatenate([lo, hi], axis=1).astype(BF16)
        a = jnp.dot(xs, w1b_ref[...], preferred_element_type=F32)
        b = jnp.dot(xs, w3b_ref[...], preferred_element_type=F32)
        act = (a * jax.nn.sigmoid(a)) * b * ws_ref[:, 0:1]
        y_ref[...] = jnp.dot(act.astype(BF16), w2b_ref[...], preferred_element_type=F32)


def _ffn(xs, ws, tile_expert, n_tiles_used, sw, layer):
    n_sorted = xs.shape[0]
    w_in_spec = pl.BlockSpec((1, 1, D, D_EXPERT), lambda k, te, nt: (layer, te[k], 0, 0))
    return pl.pallas_call(
        _ffn_kernel,
        grid_spec=pltpu.PrefetchScalarGridSpec(
            num_scalar_prefetch=2,
            grid=(n_sorted // FFN_TM,),
            in_specs=[pl.BlockSpec((FFN_TM, HALF), lambda k, te, nt: (k, 0)),
                      pl.BlockSpec((FFN_TM, 128), lambda k, te, nt: (k, 0)), w_in_spec, w_in_spec,
                      pl.BlockSpec((1, 1, D_EXPERT, D), lambda k, te, nt: (layer, te[k], 0, 0))],
            out_specs=pl.BlockSpec((FFN_TM, D), lambda k, te, nt: (k, 0)),
            scratch_shapes=[pltpu.VMEM((D, D_EXPERT), BF16), pltpu.VMEM((D, D_EXPERT), BF16),
                            pltpu.VMEM((D_EXPERT, D), BF16)],
        ),
        out_shape=jax.ShapeDtypeStruct((n_sorted, D), F32),
        compiler_params=_cparams(("arbitrary",)),
        name="ffn",
    )(tile_expert, n_tiles_used, xs, ws, sw['w1'], sw['w3'], sw['w2'])


def _combine_kernel(x1_ref, g1_ref, g2_ref, fn_ref, o_ref, *, final):
    x = x1_ref[...] + (g1_ref[...] + g2_ref[...])
    if final:
        ms = jnp.mean(x * x, axis=-1, keepdims=True)
        x = x * lax.rsqrt(ms + EPS) * fn_ref[...]
    o_ref[...] = x


def _combine(x1, g1, g2, final_norm, tm, final):
    n = x1.shape[0]
    row = pl.BlockSpec((tm, D), lambda i: (i, 0))
    return pl.pallas_call(
        functools.partial(_combine_kernel, final=final),
        grid=(n // tm,),
        in_specs=[row, row, row, _full((1, D))],
        out_specs=row,
        out_shape=jax.ShapeDtypeStruct((n, D), F32),
        compiler_params=_cparams(("parallel",)),
        name="combine",
    )(x1, g1, g2, final_norm)


def _moe_sparse(h2, route, counts, x1, sw, layer, final_norm, final):
    n = x1.shape[0]
    n_sorted = 2 * n + N_EXPERTS * FFN_TM
    e1, e2 = route[:, 0].astype(jnp.int32), route[:, 1].astype(jnp.int32)
    rank1, rank2 = route[:, 4].astype(jnp.int32), route[:, 5].astype(jnp.int32)
    cnt = counts[0, N_GROUPS:N_GROUPS + N_EXPERTS].astype(jnp.int32)
    tiles = (cnt + FFN_TM - 1) // FFN_TM
    tile_end = jnp.cumsum(tiles)
    seg_start = (tile_end - tiles) * FFN_TM
    dest1 = jnp.take(seg_start, e1) + rank1
    dest2 = jnp.take(seg_start, e2) + rank2
    tile_ids = jnp.arange(n_sorted // FFN_TM, dtype=jnp.int32)
    tile_expert = jnp.minimum(jnp.sum(tile_end[None, :] <= tile_ids[:, None], axis=1), N_EXPERTS - 1)
    wrow1 = jnp.broadcast_to(route[:, 2:3], (n, 128))
    wrow2 = jnp.broadcast_to(route[:, 3:4], (n, 128))
    xs, ws = _sc_dispatch(h2, wrow1, wrow2, dest1, dest2, n_sorted)
    y = _ffn(xs, ws, tile_expert.astype(jnp.int32), tile_end[-1:].astype(jnp.int32), sw, layer)
    g1, g2 = _sc_collect(y, dest1, dest2, n)
    return _combine(x1, g1, g2, final_norm, 512, final)


def _t5_buckets(dist):
    max_exact = T5_BUCKETS // 2
    large = max_exact + (np.log(np.maximum(dist, 1) / max_exact) / np.log(T5_MAX_DIST / max_exact)
                         * (T5_BUCKETS - max_exact)).astype(np.int32)
    large = np.minimum(large, T5_BUCKETS - 1)
    return np.where(dist < max_exact, dist, large).astype(np.int32)


def _bias_tables(t5, g, dil, window, t_sample):
    nk = window // dil + 1
    hs = slice(g * HPG, (g + 1) * HPG)
    bias = t5[_t5_buckets(np.arange(nk) * dil)][:, hs].T
    rev = bias[:, ::-1]
    neg = lambda *shape: jnp.full(shape, NEG, F32)

    vec = jnp.concatenate([rev, neg(HPG, CL)], axis=1)
    both = jnp.tile(vec, (1, CL + 1))[:, :CL * 2 * CL].reshape(HPG, CL, 2 * CL)
    prev_t, cur_t = both[:, :, :CL], both[:, :, CL:]

    rows = []
    for r in range(t_sample):
        shift = r // dil
        per_u = jnp.concatenate([neg(HPG, shift), rev[:, :nk - 1 - shift]], axis=1)
        on_phase = (np.arange(dil) == r % dil)[None, None, :]
        rows.append(jnp.where(on_phase, per_u[:, :, None], NEG).reshape(HPG, window))
    cache_t = jnp.stack(rows, axis=1)
    r = np.arange(t_sample)[:, None]
    c = np.arange(t_sample)[None, :]
    ok = ((r - c) % dil == 0) & (r >= c)
    new_t = jnp.where(jnp.asarray(ok)[None], bias[:, np.clip((r - c) // dil, 0, nk - 1)], NEG)
    return prev_t, cur_t, cache_t, new_t


def _layer_weights(l, norm1, conv_ssd_w, conv_ssd_b, ssd_dt_bias, ssd_a_log, ssd_d, ssd_norm_w,
                   conv_lru_w, conv_lru_b, lru_br, lru_bi, lru_lambda, b_gate, norm2,
                   w_router_group, b_router_group, w_router_expert, b_router_expert):
    b_all = jnp.concatenate([b_gate[l], jnp.zeros((PW - 3 * D,), F32)])[None]

    def pad128(v):
        return jnp.concatenate([v, jnp.zeros((128 - v.shape[0],), F32)])[None]

    return {
        'norm1': norm1[l][None], 'b_all': b_all,
        'conv_ssd_w': conv_ssd_w[l], 'conv_ssd_b': conv_ssd_b[l][None],
        'dt_bias': pad128(ssd_dt_bias[l]), 'a_log': pad128(ssd_a_log[l]), 'd_skip': pad128(ssd_d[l]),
        'ssd_norm_w': ssd_norm_w[l][None],
        'conv_lru_w': conv_lru_w[l], 'conv_lru_b': conv_lru_b[l][None],
        'lru_br': lru_br[l][None], 'lru_bi': lru_bi[l][None], 'lru_lambda': lru_lambda[l][None],
        'norm2': norm2[l][None],
        'w_router': jnp.concatenate([w_router_group[l], w_router_expert[l],
                                     jnp.zeros((D, ROUTE_LANES - N_GROUPS - N_EXPERTS), F32)],
                                    axis=1).astype(BF16),
        'b_router': pad128(jnp.concatenate([b_router_group[l], b_router_expert[l]])),
    }


def _front_pad(buf):
    return jnp.pad(buf, ((0, 0), (8 - (CONV_W - 1), 0), (0, 0)))


def _cols(P, nb, t, start, width):
    return P.reshape(nb, t, PW)[:, :, start:start + width]


def _kv_rows(P, nb, t, g, n_rows):
    k = _cols(P, nb, t, C_K + g * GW, GW)[:, t - n_rows:].reshape(nb, n_rows, HPG, HD)
    v = _cols(P, nb, t, C_V + g * GW, GW)[:, t - n_rows:].reshape(nb, n_rows, HPG, HD)
    return jnp.stack([k, v], axis=2)


def _layer(x, lw, sw, tables, layer, nb, t, rows, tm, tm_mix, tm_res, tm_moe, conv_ssd, st_ssd, conv_lru, st_lru,
           caches, final_norm, final):
    if caches is None:
        P, *qkv = _proj(x, lw['norm1'], sw['w_all'], lw['b_all'], layer, tm, nb, t)
    else:
        (P,) = _proj(x, lw['norm1'], sw['w_all'], lw['b_all'], layer, tm)
    y_ssd, h_ssd = _ssd(P, _front_pad(conv_ssd), st_ssd.reshape(nb, SSD_INNER, SSD_STATE), lw, nb, t, rows)
    y_lru, h_lru = _lru(P, _front_pad(conv_lru), st_lru.reshape(nb, 1, LRU_W), lw, sw, layer, nb, t, rows)
    attn = []
    for g, (window, dil) in enumerate(ATTN_GROUPS):
        prev_t, cur_t, cache_t, new_t = tables[g]
        if caches is None:
            attn.append(_attn_prompt(qkv[g], qkv[3 + g], qkv[6 + g], prev_t, cur_t, dil, nb, t))
        else:
            attn.append(_attn_sample(P, caches[g], cache_t, new_t, g, layer, nb, t, window))
    mixed = _mix(P, y_ssd, y_lru, attn, sw, layer, tm_mix)
    if caches is None:
        x1, h2, _, route, counts = _res(x, mixed, lw, sw, layer, tm_res, True)
        x2 = _moe_sparse(h2, route, counts, x1, sw, layer, final_norm, final)
    else:
        x1, h2, comb = _res(x, mixed, lw, sw, layer, tm_res, False)
        x2 = _moe(h2, comb, x1, sw, layer, final_norm, tm_moe, final)
    states = (_cols(P, nb, t, C_XBC, SSD_CONV_DIM)[:, t - 3:],
              h_ssd.reshape(nb, SSD_HEADS, SSD_HEAD_DIM, SSD_STATE),
              _cols(P, nb, t, C_XR, LRU_W)[:, t - 3:],
              h_lru.reshape(nb, LRU_W)) + tuple(
                  _kv_rows(P, nb, t, g, min(w, t)) for g, (w, _) in enumerate(ATTN_GROUPS))
    return x2, states


def kernel(x_prompt, x_sample, cache_conv_ssd, state_ssd, cache_conv_lru, state_lru, cache_kv_w128, cache_kv_w512, cache_kv_w2048, norm1, w_in, conv_ssd_w, conv_ssd_b, ssd_dt_bias, ssd_a_log, ssd_d, ssd_norm_w, conv_lru_w, conv_lru_b, lru_wr, lru_br, lru_wi, lru_bi, lru_lambda, t5_bias, w_br_ssd, w_br_lru, w_br_attn, w_gate, b_gate, w_o, norm2, w_router_group, b_router_group, w_router_expert, b_router_expert, w1, w3, w2, final_norm):
    bp, tp, _ = x_prompt.shape
    bs, ts, _ = x_sample.shape
    xp = x_prompt.reshape(bp * tp, D)
    xs = x_sample.reshape(bs * ts, D)
    fn = final_norm[None]
    tables = [_bias_tables(t5_bias, g, dil, window, ts) for g, (window, dil) in enumerate(ATTN_GROUPS)]
    caches = [c.reshape(-1, HD) for c in (cache_kv_w128, cache_kv_w512, cache_kv_w2048)]
    sw = {name: w.astype(BF16) for name, w in dict(
        lru_wr=lru_wr, lru_wi=lru_wi, w_br_ssd=w_br_ssd, w_br_lru=w_br_lru, w_br_attn=w_br_attn, w_o=w_o).items()}
    sw.update(w1=w1, w3=w3, w2=w2)
    sw['w_all'] = _prep_w(w_gate, w_in)
    outs_p, outs_s = [], []
    for l in range(DEPTH):
        lw = _layer_weights(l, norm1, conv_ssd_w, conv_ssd_b, ssd_dt_bias, ssd_a_log, ssd_d, ssd_norm_w,
                            conv_lru_w, conv_lru_b, lru_br, lru_bi, lru_lambda, b_gate, norm2,
                            w_router_group, b_router_group, w_router_expert, b_router_expert)
        final = l == DEPTH - 1
        xp, sp = _layer(xp, lw, sw, tables, l, bp, tp, CL, 1024, 256, 512, 512,
                        jnp.zeros((bp, CONV_W - 1, SSD_CONV_DIM), F32),
                        jnp.zeros((bp, SSD_HEADS, SSD_HEAD_DIM, SSD_STATE), F32),
                        jnp.zeros((bp, CONV_W - 1, LRU_W), F32), jnp.zeros((bp, LRU_W), F32),
                        None, fn, final)
        xs, ss = _layer(xs, lw, sw, tables, l, bs, ts, ts, bs * ts, bs * ts, bs * ts, bs * ts,
                        cache_conv_ssd[l], state_ssd[l], cache_conv_lru[l], state_lru[l],
                        caches, fn, final)
        outs_p.append(sp)
        outs_s.append(ss)

    def stk(outs, i):
        return jnp.stack([o[i] for o in outs], axis=0)

    return ((xp.reshape(bp, tp, D), xs.reshape(bs, ts, D))
            + tuple(stk(outs_p, i) for i in range(7)) + tuple(stk(outs_s, i) for i in range(7)))
```

```python
import functools

import numpy as np
import jax
import jax.numpy as jnp
from jax import lax
from jax.experimental import pallas as pl
from jax.experimental.pallas import tpu as pltpu
from jax.experimental.pallas import tpu_sc as plsc

F32 = jnp.float32
BF16 = jnp.bfloat16
EPS = 1e-6
NEG = -1e30

D = 2048
DEPTH = 2
PAST_LEN = 16384
CL = 128
CONV_W = 4
SSD_HEADS = 16
SSD_HEAD_DIM = 64
SSD_INNER = 1024
SSD_STATE = 128
SSD_CONV_DIM = 1536
LRU_W = 1024
LRU_BLOCKS = 8
LRU_C = 8.0
ATTN_GROUPS = ((128, 1), (512, 4), (2048, 16))
HPG = 4
HD = 128
GW = HPG * HD
T5_BUCKETS = 32
T5_MAX_DIST = 2048
N_GROUPS = 4
PER_GROUP = 4
N_EXPERTS = 16
D_EXPERT = 512

TILE = 512
C_GATE = 0
C_Z = 6144
C_XR = 7168
C_GR = 8192
C_XBC = 9216
C_Q = 10752
C_K = 12288
C_V = 13824
C_DT = 15360
PW = 15872
N_GATE_TILES = (3 * D) // TILE
ROUTE_LANES = 128
VMEM_LIMIT = 56 * 1024 * 1024


def _cparams(sem):
    return pltpu.CompilerParams(dimension_semantics=sem, vmem_limit_bytes=VMEM_LIMIT)


def _full(shape):
    nd = len(shape)
    return pl.BlockSpec(shape, lambda *_: (0,) * nd)


def _of_layer(shape, layer):
    nd = len(shape)
    return pl.BlockSpec((1,) + shape, lambda *_: (layer,) + (0,) * nd)


_W_IN_STARTS = (0, 512, 2576, 3088, 3600, 4112, 1024, 1536, 2048, 4624, 5136, 5648, 6160, 6672, 7184,
                7696, 8208, 8720, 2560)
DT_SHIFT = SSD_HEADS


def _prep_kernel(blk_ref, shift_ref, width_ref, wg_ref, wa_ref, wb_ref, o_ref):
    j = pl.program_id(1)

    @pl.when(j < N_GATE_TILES)
    def _():
        o_ref[0] = wg_ref[0].astype(BF16)

    @pl.when(j >= N_GATE_TILES)
    def _():
        a = wa_ref[0]
        shifted = jnp.concatenate([a[DT_SHIFT:, :], wb_ref[0]], axis=0)
        val = jnp.where(shift_ref[j] == 0, a, shifted)
        row = lax.broadcasted_iota(jnp.int32, val.shape, 0)
        o_ref[0] = jnp.where(row < width_ref[j], val, 0.0).T.astype(BF16)


def _prep_w(w_gate, w_in):
    starts = (0,) * N_GATE_TILES + _W_IN_STARTS
    blk = jnp.asarray([s // TILE for s in starts], jnp.int32)
    shift = jnp.asarray([s % TILE for s in starts], jnp.int32)
    assert all(s % TILE in (0, DT_SHIFT) for s in starts) and w_in.shape[2] % DT_SHIFT == 0
    width = jnp.asarray([TILE] * (len(starts) - 1) + [SSD_HEADS], jnp.int32)
    per_tile = TILE // DT_SHIFT
    w_in_t = jnp.swapaxes(w_in, 1, 2)
    return pl.pallas_call(
        _prep_kernel,
        grid_spec=pltpu.PrefetchScalarGridSpec(
            num_scalar_prefetch=3,
            grid=(DEPTH, PW // TILE),
            in_specs=[pl.BlockSpec((1, D, TILE), lambda l, j, b, s, w: (l, 0, jnp.minimum(j, N_GATE_TILES - 1))),
                      pl.BlockSpec((1, TILE, D), lambda l, j, b, s, w: (l, b[j], 0)),
                      pl.BlockSpec((1, DT_SHIFT, D), lambda l, j, b, s, w: (l, (b[j] + 1) * per_tile, 0))],
            out_specs=pl.BlockSpec((1, D, TILE), lambda l, j, b, s, w: (l, 0, j)),
        ),
        out_shape=jax.ShapeDtypeStruct((DEPTH, D, PW), BF16),
        compiler_params=_cparams(("parallel", "arbitrary")),
        name="prep_w",
    )(blk, shift, width, w_gate, w_in_t, w_in_t)


def _proj_kernel(x_ref, nw_ref, w_ref, b_ref, o_ref, *rest, tm, phase_major):
    if phase_major:
        qkv_refs, (h_ref, acc_ref, ph_ref) = rest[:9], rest[9:]
    else:
        h_ref, acc_ref = rest
    j = pl.program_id(1)

    @pl.when(j == 0)
    def _():
        x = x_ref[...]
        ms = jnp.mean(x * x, axis=-1, keepdims=True)
        h_ref[...] = (x * lax.rsqrt(ms + EPS) * nw_ref[...]).astype(BF16)
        acc_ref[...] = jnp.zeros(acc_ref.shape, F32)

    prev = acc_ref[...]
    o_ref[...] = jnp.where(j <= N_GATE_TILES, jax.nn.sigmoid(prev), prev)
    acc_ref[...] = jnp.dot(h_ref[...], w_ref[0], preferred_element_type=F32) + b_ref[...]

    if phase_major:
        for part in range(3):
            for g, (_, dil) in enumerate(ATTN_GROUPS):
                ref = qkv_refs[part * 3 + g]

                @pl.when(j - 1 == C_Q // TILE + part * 3 + g)
                def _(ref=ref, dil=dil):
                    if dil == 1:
                        ref[0, 0] = o_ref[...].astype(BF16)
                    else:
                        for c in range(TILE // 128):
                            ph_ref[c] = o_ref[:, c * 128:(c + 1) * 128]
                        for p in range(dil):
                            for c in range(TILE // 128):
                                ref[0, p, :, c * 128:(c + 1) * 128] = (
                                    ph_ref[c, pl.ds(p, tm // dil, stride=dil), :].astype(BF16))


def _proj(x, nw, w_all, b_all, layer, tm, nb=None, t=None):
    n = x.shape[0]
    phase_major = nb is not None
    nt = PW // TILE
    out_specs = [pl.BlockSpec((tm, TILE), lambda i, j: (i, jnp.maximum(j - 1, 0)))]
    out_shape = [jax.ShapeDtypeStruct((n, PW), F32)]
    scratch = [pltpu.VMEM((tm, D), BF16), pltpu.VMEM((tm, TILE), F32)]
    if phase_major:
        tpb = t // tm
        for _ in range(3):
            for _, dil in ATTN_GROUPS:
                out_specs.append(pl.BlockSpec((1, dil, tm // dil, GW), lambda i, j: (i // tpb, 0, i % tpb, 0)))
                out_shape.append(jax.ShapeDtypeStruct((nb, dil, t // dil, GW), BF16))
        scratch.append(pltpu.VMEM((TILE // 128, tm, 128), F32))
    return pl.pallas_call(
        functools.partial(_proj_kernel, tm=tm, phase_major=phase_major),
        grid=(n // tm, nt + 1),
        in_specs=[pl.BlockSpec((tm, D), lambda i, j: (i, 0)),
                  pl.BlockSpec((1, D), lambda i, j: (0, 0)),
                  pl.BlockSpec((1, D, TILE), lambda i, j: (layer, 0, jnp.minimum(j, nt - 1))),
                  pl.BlockSpec((1, TILE), lambda i, j: (0, jnp.minimum(j, nt - 1)))],
        out_specs=out_specs,
        out_shape=out_shape,
        scratch_shapes=scratch,
        compiler_params=_cparams(("parallel", "arbitrary")),
        name="proj",
    )(x, nw, w_all, b_all)


def _conv_step(x_ref, xp_ref, cw_ref, cb_ref, rows, out_rows):
    xp_ref[8:8 + rows, :] = _bf16_round(x_ref[...])
    cw = _bf16_round(cw_ref[...])
    acc = cw[0:1, :] * xp_ref[5:5 + out_rows, :]
    for j in range(1, CONV_W):
        acc = acc + cw[j:j + 1, :] * xp_ref[5 + j:5 + j + out_rows, :]
    tail = xp_ref[rows:rows + 8, :]
    xp_ref[0:8, :] = tail
    return acc + cb_ref[...]


def _bf16_round(x):
    return x.astype(BF16).astype(F32)


def _softplus(x):
    return jnp.maximum(x, 0.0) + jnp.log1p(jnp.exp(-jnp.abs(x)))


def _ssd_kernel(z_ref, xbc_ref, dt_ref, tail_ref, h0_ref, cw_ref, cb_ref, dtb_ref, alog_ref, dsk_ref, nw_ref,
                y_ref, hf_ref, xp_ref, act_ref, st_ref, ysc_ref, *, rows, n_chunks):
    c = pl.program_id(1)

    @pl.when(c == 0)
    def _():
        xp_ref[0:8, :] = _bf16_round(tail_ref[0])
        st_ref[...] = h0_ref[0]

    if rows < CL:
        xp_ref[8 + rows:, :] = jnp.zeros((CL - rows, SSD_CONV_DIM), F32)
    conv = _conv_step(xbc_ref, xp_ref, cw_ref, cb_ref, rows, CL)
    act_ref[...] = conv * jax.nn.sigmoid(conv)

    row = lax.broadcasted_iota(jnp.int32, (CL, 128), 0)
    lane = lax.broadcasted_iota(jnp.int32, (CL, 128), 1)
    raw = dt_ref[...]
    if rows < CL:
        raw = jnp.concatenate([raw, jnp.zeros((CL - rows, 128), F32)], axis=0)
    dt = _softplus(raw + dtb_ref[...])
    dt = jnp.where((lane < SSD_HEADS) & (row < rows), dt, 0.0)
    da = dt * (-jnp.exp(alog_ref[...]))
    acs = da
    d = 1
    while d < CL:
        acs = acs + jnp.where(row >= d, pltpu.roll(acs, d, 0), 0.0)
        d *= 2
    acs_t = acs.T
    last = acs[CL - 1:CL, :]
    e_acs = jnp.exp(acs)
    to_end = jnp.exp(last - acs)
    cdec = jnp.exp(last)
    causal = row >= lane
    lo_lane = lane < SSD_HEAD_DIM
    lo_row = row < SSD_HEAD_DIM
    dsk = dsk_ref[...]

    def pair_cols(arr, h):
        return jnp.where(lo_lane, arr[:, h:h + 1], arr[:, h + 1:h + 2])

    nt = (((1,), (1,)), ((), ()))
    for g in range(2):
        bm = act_ref[:, SSD_INNER + g * SSD_STATE:SSD_INNER + (g + 1) * SSD_STATE].astype(BF16)
        cm = act_ref[:, SSD_INNER + 256 + g * SSD_STATE:SSD_INNER + 256 + (g + 1) * SSD_STATE].astype(BF16)
        cb = lax.dot_general(cm, bm, nt, preferred_element_type=F32)
        for pp in range(4):
            h = g * 8 + 2 * pp
            sl = slice(h * SSD_HEAD_DIM, h * SSD_HEAD_DIM + 128)
            xs = act_ref[:, sl]
            xdt = xs * pair_cols(dt, h)
            xdt_b = xdt.astype(BF16)
            ys = []
            for hh in (h, h + 1):
                seg = acs[:, hh:hh + 1] - acs_t[hh:hh + 1, :]
                decay = jnp.exp(jnp.where(causal, seg, -jnp.inf))
                ys.append(jnp.dot((cb * decay).astype(BF16), xdt_b, preferred_element_type=F32))
            y_diag = jnp.where(lo_lane, ys[0], ys[1])
            st = st_ref[sl, :]
            y_off = lax.dot_general(cm, st.astype(BF16), nt, preferred_element_type=F32) * pair_cols(e_acs, h)
            d_pair = jnp.where(lo_lane, dsk[:, h:h + 1], dsk[:, h + 1:h + 2])
            ysc_ref[:, sl] = y_diag + y_off + d_pair * xs
            xdte_t = (xdt * pair_cols(to_end, h)).T.astype(BF16)
            s_new = jnp.dot(xdte_t, bm, preferred_element_type=F32)
            dec = jnp.where(lo_row, cdec[:, h:h + 1], cdec[:, h + 1:h + 2])
            st_ref[sl, :] = dec * st + s_new

    zz = z_ref[...]
    yg = ysc_ref[0:rows, :] * (zz * jax.nn.sigmoid(zz))
    gw = SSD_INNER // 2
    for g in range(2):
        part = yg[:, g * gw:(g + 1) * gw]
        ms = jnp.mean(part * part, axis=-1, keepdims=True)
        y_ref[:, g * gw:(g + 1) * gw] = (part * lax.rsqrt(ms + EPS)
                                         * nw_ref[:, g * gw:(g + 1) * gw]).astype(y_ref.dtype)

    @pl.when(c == n_chunks - 1)
    def _():
        hf_ref[0] = st_ref[...]


def _ssd(P, tail, h0, lw, nb, t, rows):
    nc = t // rows
    kern = functools.partial(_ssd_kernel, rows=rows, n_chunks=nc)
    return pl.pallas_call(
        kern,
        grid=(nb, nc),
        in_specs=[pl.BlockSpec((rows, SSD_INNER), lambda b, c: (b * nc + c, C_Z // SSD_INNER)),
                  pl.BlockSpec((rows, SSD_CONV_DIM), lambda b, c: (b * nc + c, C_XBC // SSD_CONV_DIM)),
                  pl.BlockSpec((rows, 128), lambda b, c: (b * nc + c, C_DT // 128)),
                  pl.BlockSpec((1, 8, SSD_CONV_DIM), lambda b, c: (b, 0, 0)),
                  pl.BlockSpec((1, SSD_INNER, SSD_STATE), lambda b, c: (b, 0, 0)),
                  _full((CONV_W, SSD_CONV_DIM)), _full((1, SSD_CONV_DIM)),
                  _full((1, 128)), _full((1, 128)), _full((1, 128)), _full((1, SSD_INNER))],
        out_specs=[pl.BlockSpec((rows, SSD_INNER), lambda b, c: (b * nc + c, 0)),
                   pl.BlockSpec((1, SSD_INNER, SSD_STATE), lambda b, c: (b, 0, 0))],
        out_shape=[jax.ShapeDtypeStruct((nb * t, SSD_INNER), BF16 if rows % 16 == 0 else F32),
                   jax.ShapeDtypeStruct((nb, SSD_INNER, SSD_STATE), F32)],
        scratch_shapes=[pltpu.VMEM((8 + CL, SSD_CONV_DIM), F32),
                        pltpu.VMEM((CL, SSD_CONV_DIM), F32),
                        pltpu.VMEM((SSD_INNER, SSD_STATE), F32),
                        pltpu.VMEM((CL, SSD_INNER), F32)],
        compiler_params=_cparams(("parallel", "arbitrary")),
        name="ssd",
    )(P, P, P, tail, h0, lw['conv_ssd_w'], lw['conv_ssd_b'], lw['dt_bias'], lw['a_log'], lw['d_skip'], lw['ssd_norm_w'])


def _lru_kernel(xr_ref, gr_ref, tail_ref, h0_ref, cw_ref, cb_ref, wr_ref, br_ref, wi_ref, bi_ref, lam_ref,
                y_ref, hl_ref, xp_ref, h_ref, *, rows, n_chunks):
    c = pl.program_id(1)

    @pl.when(c == 0)
    def _():
        xp_ref[0:8, :] = _bf16_round(tail_ref[0])
        h_ref[...] = h0_ref[0]

    x = _conv_step(xr_ref, xp_ref, cw_ref, cb_ref, rows, rows)
    xb = x.astype(BF16)
    rs, is_ = [], []
    for n in range(LRU_BLOCKS):
        blk = xb[:, n * 128:(n + 1) * 128]
        rs.append(jnp.dot(blk, wr_ref[0, n], preferred_element_type=F32))
        is_.append(jnp.dot(blk, wi_ref[0, n], preferred_element_type=F32))
    r_gate = jax.nn.sigmoid(jnp.concatenate(rs, axis=1) + br_ref[...])
    i_gate = jax.nn.sigmoid(jnp.concatenate(is_, axis=1) + bi_ref[...])
    log_a = -LRU_C * r_gate * _softplus(-lam_ref[...])
    a = jnp.exp(log_a)
    th = jnp.tanh(log_a)
    b = jnp.sqrt(-2.0 * th / (1.0 - th)) * (i_gate * x)
    in_group = lax.broadcasted_iota(jnp.int32, (rows, LRU_W), 0) % 8
    for d in (1, 2, 4):
        a_s = jnp.where(in_group >= d, pltpu.roll(a, d, 0), 1.0)
        b_s = jnp.where(in_group >= d, pltpu.roll(b, d, 0), 0.0)
        b = a * b_s + b
        a = a * a_s
    carry = h_ref[...]
    groups = []
    for g in range(rows // 8):
        h_g = b[8 * g:8 * g + 8, :] + a[8 * g:8 * g + 8, :] * carry
        groups.append(h_g)
        carry = h_g[7:8, :]
    h = jnp.concatenate(groups, axis=0) if len(groups) > 1 else groups[0]
    last = carry
    h_ref[...] = last
    y_ref[...] = (h * jax.nn.gelu(gr_ref[...])).astype(y_ref.dtype)

    @pl.when(c == n_chunks - 1)
    def _():
        hl_ref[0] = last


def _lru(P, tail, h0, lw, sw, layer, nb, t, rows):
    nc = t // rows
    kern = functools.partial(_lru_kernel, rows=rows, n_chunks=nc)
    y_dtype = BF16 if rows % 16 == 0 else F32
    return pl.pallas_call(
        kern,
        grid=(nb, nc),
        in_specs=[pl.BlockSpec((rows, LRU_W), lambda b, c: (b * nc + c, C_XR // LRU_W)),
                  pl.BlockSpec((rows, LRU_W), lambda b, c: (b * nc + c, C_GR // LRU_W)),
                  pl.BlockSpec((1, 8, LRU_W), lambda b, c: (b, 0, 0)),
                  pl.BlockSpec((1, 1, LRU_W), lambda b, c: (b, 0, 0)),
                  _full((CONV_W, LRU_W)), _full((1, LRU_W)),
                  _of_layer((LRU_BLOCKS, 128, 128), layer), _full((1, LRU_W)),
                  _of_layer((LRU_BLOCKS, 128, 128), layer), _full((1, LRU_W)), _full((1, LRU_W))],
        out_specs=[pl.BlockSpec((rows, LRU_W), lambda b, c: (b * nc + c, 0)),
                   pl.BlockSpec((1, 1, LRU_W), lambda b, c: (b, 0, 0))],
        out_shape=[jax.ShapeDtypeStruct((nb * t, LRU_W), y_dtype),
                   jax.ShapeDtypeStruct((nb, 1, LRU_W), F32)],
        scratch_shapes=[pltpu.VMEM((8 + rows, LRU_W), F32), pltpu.VMEM((1, LRU_W), F32)],
        compiler_params=_cparams(("parallel", "arbitrary")),
        name="lru",
    )(P, P, tail, h0, lw['conv_lru_w'], lw['conv_lru_b'], sw['lru_wr'], lw['lru_br'], sw['lru_wi'], lw['lru_bi'],
      lw['lru_lambda'])


def _attn_kernel(q_ref, kp_ref, vp_ref, kc_ref, vc_ref, bias_ref, o_ref, lse_ref, *, sub):
    scale = HD ** -0.5
    nt = (((1,), (1,)), ((), ()))
    lane = lax.broadcasted_iota(jnp.int32, (CL, 128), 1)
    key = lax.broadcasted_iota(jnp.int32, (CL, 2 * CL), 1)
    first_ok = (pl.program_id(2) > 0) | (key >= CL)
    for s in range(sub):
        rows = slice(s * CL, (s + 1) * CL)
        both = slice((s - 1) * CL, (s + 1) * CL)
        lse_all = jnp.zeros((CL, 128), F32)
        for h in range(HPG):
            sl = slice(h * HD, (h + 1) * HD)
            q = q_ref[0, 0, rows, sl]
            if s == 0:
                kk = jnp.concatenate([kp_ref[0, 0, :, sl], kc_ref[0, 0, rows, sl]], axis=0)
                vv = jnp.concatenate([vp_ref[0, 0, :, sl], vc_ref[0, 0, rows, sl]], axis=0)
            else:
                kk, vv = kc_ref[0, 0, both, sl], vc_ref[0, 0, both, sl]
            sc = lax.dot_general(q, kk, nt, preferred_element_type=F32) * scale + bias_ref[h]
            if s == 0:
                sc = jnp.where(first_ok, sc, NEG)
            m = jnp.max(sc, axis=-1, keepdims=True)
            p = jnp.exp(sc - m)
            l = jnp.sum(p, axis=-1, keepdims=True)
            o_ref[rows, sl] = jnp.dot((p * (1.0 / l)).astype(BF16), vv, preferred_element_type=F32)
            lse_all = jnp.where(lane == h, m + jnp.log(l), lse_all)
        lse_ref[rows, :] = lse_all


def _attn_prompt(q, k, v, bias, dil, nb, t):
    n = nb * t
    sub = min(8, t // dil // CL)
    nstep = t // dil // (sub * CL)
    cur = pl.BlockSpec((1, 1, sub * CL, GW), lambda b, p, i: (b, p, i, 0))
    prev = pl.BlockSpec((1, 1, CL, GW), lambda b, p, i: (b, p, jnp.maximum(i * sub - 1, 0), 0))
    o, lse = pl.pallas_call(
        functools.partial(_attn_kernel, sub=sub),
        grid=(nb, dil, nstep),
        in_specs=[cur, prev, prev, cur, cur, _full((HPG, CL, 2 * CL))],
        out_specs=[pl.BlockSpec((sub * CL, GW), lambda b, p, i: (b * nstep + i, p)),
                   pl.BlockSpec((sub * CL, 128), lambda b, p, i: (b * nstep + i, p))],
        out_shape=[jax.ShapeDtypeStruct((n // dil, dil * GW), F32),
                   jax.ShapeDtypeStruct((n // dil, dil * 128), F32)],
        compiler_params=_cparams(("parallel", "parallel", "arbitrary")),
        name=f"attn_prompt_d{dil}",
    )(q, k, v, k, v, bias)
    return o.reshape(n, GW), lse.reshape(n, 128)


def _attn_sample_kernel(q_ref, kv_ref, kb_ref, vb_ref, ba_ref, bb_ref, o_ref, lse_ref):
    scale = HD ** -0.5
    nt = (((1,), (1,)), ((), ()))
    rows = o_ref.shape[0]
    window = kv_ref.shape[0] // (2 * HPG)
    lane = lax.broadcasted_iota(jnp.int32, (rows, 128), 1)
    lse_all = jnp.zeros((rows, 128), F32)
    for h in range(HPG):
        sl = slice(h * HD, (h + 1) * HD)
        q = q_ref[:, sl].astype(BF16)
        k_cache = kv_ref[pl.ds(h, window, stride=2 * HPG), :].astype(BF16)
        v_cache = kv_ref[pl.ds(HPG + h, window, stride=2 * HPG), :].astype(BF16)
        sa = lax.dot_general(q, k_cache, nt, preferred_element_type=F32) * scale + ba_ref[h]
        sb = lax.dot_general(q, kb_ref[:, sl].astype(BF16), nt, preferred_element_type=F32) * scale + bb_ref[h]
        m = jnp.maximum(jnp.max(sa, axis=-1, keepdims=True), jnp.max(sb, axis=-1, keepdims=True))
        pa = jnp.exp(sa - m)
        pb = jnp.exp(sb - m)
        l = jnp.sum(pa, axis=-1, keepdims=True) + jnp.sum(pb, axis=-1, keepdims=True)
        inv = 1.0 / l
        o_ref[:, sl] = (jnp.dot((pa * inv).astype(BF16), v_cache, preferred_element_type=F32)
                        + jnp.dot((pb * inv).astype(BF16), vb_ref[:, sl].astype(BF16), preferred_element_type=F32))
        lse_all = jnp.where(lane == h, m + jnp.log(l), lse_all)
    lse_ref[...] = lse_all


def _attn_sample(Ps, cache_rows, bias_a, bias_b, g, layer, nb, t, window):
    tq, tk, tv = C_Q // GW + g, C_K // GW + g, C_V // GW + g
    out_spec = pl.BlockSpec((t, GW), lambda b: (b, 0))
    return pl.pallas_call(
        _attn_sample_kernel,
        grid=(nb,),
        in_specs=[pl.BlockSpec((t, GW), lambda b: (b, tq)),
                  pl.BlockSpec((window * 2 * HPG, HD), lambda b: (layer * nb + b, 0)),
                  pl.BlockSpec((t, GW), lambda b: (b, tk)),
                  pl.BlockSpec((t, GW), lambda b: (b, tv)),
                  _full((HPG, t, window)), _full((HPG, t, t))],
        out_specs=[out_spec, pl.BlockSpec((t, 128), lambda b: (b, 0))],
        out_shape=[jax.ShapeDtypeStruct((nb * t, GW), F32), jax.ShapeDtypeStruct((nb * t, 128), F32)],
        compiler_params=_cparams(("parallel",)),
        name=f"attn_sample_w{window}",
    )(Ps, cache_rows, Ps, Ps, bias_a, bias_b)


def _mix_kernel(gs_ref, gl_ref, ga_ref, ys_ref, yl_ref, o0_ref, o1_ref, o2_ref, l0_ref, l1_ref, l2_ref,
                wbs_ref, wbl_ref, wba_ref, out_ref):
    l0, l1, l2 = l0_ref[...], l1_ref[...], l2_ref[...]
    m = jnp.maximum(jnp.maximum(l0, l1), l2)
    e0, e1, e2 = jnp.exp(l0 - m), jnp.exp(l1 - m), jnp.exp(l2 - m)
    den = e0 + e1 + e2
    w0, w1, w2 = e0 / den, e1 / den, e2 / den
    tm = out_ref.shape[0]
    heads = []
    for h in range(HPG):
        sl = slice(h * HD, (h + 1) * HD)
        per_head = lambda w: jnp.broadcast_to(w[:, h:h + 1], (tm, HD))
        heads.append(o0_ref[:, sl] * per_head(w0) + o1_ref[:, sl] * per_head(w1) + o2_ref[:, sl] * per_head(w2))
    ya = jnp.concatenate(heads, axis=1)
    mixed = (gs_ref[...] * jnp.dot(ys_ref[...].astype(BF16), wbs_ref[0], preferred_element_type=F32)
             + gl_ref[...] * jnp.dot(yl_ref[...].astype(BF16), wbl_ref[0], preferred_element_type=F32)
             + ga_ref[...] * jnp.dot(ya.astype(BF16), wba_ref[0], preferred_element_type=F32))
    out_ref[...] = mixed.astype(BF16)


def _mix(P, y_ssd, y_lru, attn, sw, layer, tm):
    n = P.shape[0]
    row = lambda w: pl.BlockSpec((tm, w), lambda i: (i, 0))
    (o0, s0), (o1, s1), (o2, s2) = attn
    return pl.pallas_call(
        _mix_kernel,
        grid=(n // tm,),
        in_specs=[pl.BlockSpec((tm, D), lambda i: (i, 0)), pl.BlockSpec((tm, D), lambda i: (i, 1)),
                  pl.BlockSpec((tm, D), lambda i: (i, 2)),
                  row(SSD_INNER), row(LRU_W), row(GW), row(GW), row(GW), row(128), row(128), row(128),
                  _of_layer((SSD_INNER, D), layer), _of_layer((LRU_W, D), layer), _of_layer((GW, D), layer)],
        out_specs=row(D),
        out_shape=jax.ShapeDtypeStruct((n, D), BF16),
        compiler_params=_cparams(("parallel",)),
        name="mix",
    )(P, P, P, y_ssd, y_lru, o0, o1, o2, s0, s1, s2, sw['w_br_ssd'], sw['w_br_lru'], sw['w_br_attn'])


def _res_kernel(x_ref, mixed_ref, wo_ref, n2_ref, wr_ref, br_ref, x1_ref, h2_ref, comb_ref, *rest, dispatch):
    x1 = x_ref[...] + jnp.dot(mixed_ref[...], wo_ref[0], preferred_element_type=F32)
    x1_ref[...] = x1
    ms = jnp.mean(x1 * x1, axis=-1, keepdims=True)
    h2 = x1 * lax.rsqrt(ms + EPS) * n2_ref[...]
    h2b = h2.astype(BF16)
    if dispatch:
        lo = pltpu.bitcast(h2b[:, :HALF].astype(F32), jnp.uint32)
        hi = pltpu.bitcast(h2b[:, HALF:].astype(F32), jnp.uint32)
        h2_ref[...] = pltpu.bitcast(lax.shift_right_logical(lo, jnp.uint32(16)) | hi, jnp.int32)
    else:
        h2_ref[...] = h2b
    logits = jnp.dot(h2b, wr_ref[...], preferred_element_type=F32) + br_ref[...]
    lane = lax.broadcasted_iota(jnp.int32, logits.shape, 1).astype(F32)
    big = float(ROUTE_LANES)

    def first_max(vals, ok):
        v = jnp.where(ok, vals, NEG)
        top = jnp.max(v, axis=-1, keepdims=True)
        idx = jnp.min(jnp.where(ok & (v == top), lane, big), axis=-1, keepdims=True)
        return top, idx

    is_g = lane < N_GROUPS
    gmax, gsel = first_max(logits, is_g)
    gp = 1.0 / jnp.sum(jnp.where(is_g, jnp.exp(logits - gmax), 0.0), axis=-1, keepdims=True)
    lo = N_GROUPS + PER_GROUP * gsel
    is_e = (lane >= lo) & (lane < lo + PER_GROUP)
    t1, i1 = first_max(logits, is_e)
    t2, i2 = first_max(logits, is_e & (lane != i1))
    e2 = jnp.exp(t2 - t1)
    w1 = gp / (1.0 + e2)
    w2 = gp * e2 / (1.0 + e2)
    comb_ref[...] = jnp.where(lane == i1, w1, 0.0) + jnp.where(lane == i2, w2, 0.0)

    if dispatch:
        route_ref, cnt_ref, carry_ref = rest

        @pl.when(pl.program_id(0) == 0)
        def _():
            carry_ref[...] = jnp.zeros(carry_ref.shape, F32)

        tm = x1.shape[0]
        onehot = jnp.where((lane == i1) | (lane == i2), 1.0, 0.0)
        r = lax.broadcasted_iota(jnp.int32, (tm, tm), 0)
        c = lax.broadcasted_iota(jnp.int32, (tm, tm), 1)
        earlier = jnp.where(r > c, 1.0, 0.0).astype(BF16)
        before = jnp.dot(earlier, onehot.astype(BF16), preferred_element_type=F32) + carry_ref[...]
        rank1 = jnp.sum(jnp.where(lane == i1, before, 0.0), axis=-1, keepdims=True)
        rank2 = jnp.sum(jnp.where(lane == i2, before, 0.0), axis=-1, keepdims=True)
        carry_ref[...] += jnp.sum(onehot, axis=0, keepdims=True)
        cnt_ref[...] = carry_ref[...]
        fields = (i1 - N_GROUPS, i2 - N_GROUPS, w1, w2, rank1, rank2)
        route = jnp.zeros(logits.shape, F32)
        for k, val in enumerate(fields):
            route = jnp.where(lane == k, val, route)
        route_ref[...] = route


def _res(x, mixed, lw, sw, layer, tm, dispatch):
    n = x.shape[0]
    h2_cols, h2_dtype = (HALF, jnp.int32) if dispatch else (D, BF16)
    out_specs = [pl.BlockSpec((tm, D), lambda i: (i, 0)), pl.BlockSpec((tm, h2_cols), lambda i: (i, 0)),
                 pl.BlockSpec((tm, ROUTE_LANES), lambda i: (i, 0))]
    out_shape = [jax.ShapeDtypeStruct((n, D), F32), jax.ShapeDtypeStruct((n, h2_cols), h2_dtype),
                 jax.ShapeDtypeStruct((n, ROUTE_LANES), F32)]
    scratch = []
    if dispatch:
        out_specs += [pl.BlockSpec((tm, ROUTE_LANES), lambda i: (i, 0)), _full((1, ROUTE_LANES))]
        out_shape += [jax.ShapeDtypeStruct((n, ROUTE_LANES), F32), jax.ShapeDtypeStruct((1, ROUTE_LANES), F32)]
        scratch = [pltpu.VMEM((1, ROUTE_LANES), F32)]
    return pl.pallas_call(
        functools.partial(_res_kernel, dispatch=dispatch),
        grid=(n // tm,),
        in_specs=[pl.BlockSpec((tm, D), lambda i: (i, 0)), pl.BlockSpec((tm, D), lambda i: (i, 0)),
                  _of_layer((D, D), layer), _full((1, D)), _full((D, ROUTE_LANES)), _full((1, ROUTE_LANES))],
        out_specs=out_specs,
        out_shape=out_shape,
        scratch_shapes=scratch,
        compiler_params=_cparams(("arbitrary",)),
        name="res_router",
    )(x, mixed, sw['w_o'], lw['norm2'], lw['w_router'], lw['b_router'])


def _moe_kernel(h2_ref, comb_ref, x1_ref, w1_ref, w3_ref, w2_ref, fn_ref, o_ref, *, final):
    e = pl.program_id(1)

    @pl.when(e == 0)
    def _():
        o_ref[...] = x1_ref[...]

    h = h2_ref[...]
    a = jnp.dot(h, w1_ref[0, 0].astype(BF16), preferred_element_type=F32)
    b = jnp.dot(h, w3_ref[0, 0].astype(BF16), preferred_element_type=F32)
    comb = comb_ref[...]
    lane = lax.broadcasted_iota(jnp.int32, comb.shape, 1)
    w = jnp.sum(jnp.where(lane == e + N_GROUPS, comb, 0.0), axis=-1, keepdims=True)
    act = (a * jax.nn.sigmoid(a)) * b * w
    o_ref[...] += jnp.dot(act.astype(BF16), w2_ref[0, 0].astype(BF16), preferred_element_type=F32)

    if final:
        @pl.when(e == N_EXPERTS - 1)
        def _():
            x = o_ref[...]
            ms = jnp.mean(x * x, axis=-1, keepdims=True)
            o_ref[...] = x * lax.rsqrt(ms + EPS) * fn_ref[...]


def _moe(h2, comb, x1, sw, layer, final_norm, tm, final):
    n = x1.shape[0]
    return pl.pallas_call(
        functools.partial(_moe_kernel, final=final),
        grid=(n // tm, N_EXPERTS),
        in_specs=[pl.BlockSpec((tm, D), lambda i, e: (i, 0)),
                  pl.BlockSpec((tm, ROUTE_LANES), lambda i, e: (i, 0)),
                  pl.BlockSpec((tm, D), lambda i, e: (i, 0)),
                  pl.BlockSpec((1, 1, D, D_EXPERT), lambda i, e: (layer, e, 0, 0)),
                  pl.BlockSpec((1, 1, D, D_EXPERT), lambda i, e: (layer, e, 0, 0)),
                  pl.BlockSpec((1, 1, D_EXPERT, D), lambda i, e: (layer, e, 0, 0)),
                  pl.BlockSpec((1, D), lambda i, e: (0, 0))],
        out_specs=pl.BlockSpec((tm, D), lambda i, e: (i, 0)),
        out_shape=jax.ShapeDtypeStruct((n, D), F32),
        compiler_params=_cparams(("parallel", "arbitrary")),
        name="moe",
    )(h2, comb, x1, sw['w1'], sw['w3'], sw['w2'], final_norm)


FFN_TM = 512
SC_CORES = 2
SC_SUBCORES = 16
SC_WORKERS = SC_CORES * SC_SUBCORES
SC_CHUNK = 32
HALF = D // 2


def _sc_mesh():
    return plsc.VectorSubcoreMesh(core_axis_name="c", subcore_axis_name="s", num_cores=SC_CORES,
                                  num_subcores=SC_SUBCORES)


def _sc_dispatch(x, wrow1, wrow2, dest1, dest2, n_sorted):
    n = x.shape[0]
    per_w = n // SC_WORKERS

    @functools.partial(
        pl.kernel, mesh=_sc_mesh(),
        out_type=(jax.ShapeDtypeStruct((n_sorted, HALF), jnp.int32), jax.ShapeDtypeStruct((n_sorted, 128), F32)),
        scratch_types=[pltpu.VMEM((SC_CHUNK,), jnp.int32), pltpu.VMEM((SC_CHUNK,), jnp.int32),
                       pltpu.VMEM((SC_CHUNK, HALF), jnp.int32), pltpu.VMEM((SC_CHUNK, 128), F32)],
    )
    def k(x_hbm, w1_hbm, w2_hbm, d1_hbm, d2_hbm, out_hbm, wout_hbm, i1_v, i2_v, rows_v, wrows_v):
        base = (lax.axis_index("s") * SC_CORES + lax.axis_index("c")) * per_w

        @pl.loop(0, per_w // SC_CHUNK)
        def _(j):
            off = base + j * SC_CHUNK
            pltpu.sync_copy(d1_hbm.at[pl.ds(off, SC_CHUNK)], i1_v)
            pltpu.sync_copy(d2_hbm.at[pl.ds(off, SC_CHUNK)], i2_v)
            pltpu.sync_copy(x_hbm.at[pl.ds(off, SC_CHUNK)], rows_v)
            pltpu.sync_copy(rows_v, out_hbm.at[i1_v])
            pltpu.sync_copy(rows_v, out_hbm.at[i2_v])
            pltpu.sync_copy(w1_hbm.at[pl.ds(off, SC_CHUNK)], wrows_v)
            pltpu.sync_copy(wrows_v, wout_hbm.at[i1_v])
            pltpu.sync_copy(w2_hbm.at[pl.ds(off, SC_CHUNK)], wrows_v)
            pltpu.sync_copy(wrows_v, wout_hbm.at[i2_v])

    return k(x, wrow1, wrow2, dest1, dest2)


def _sc_collect(y, dest1, dest2, n):
    per_w = n // SC_WORKERS

    @functools.partial(
        pl.kernel, mesh=_sc_mesh(),
        out_type=(jax.ShapeDtypeStruct((n, D), F32), jax.ShapeDtypeStruct((n, D), F32)),
        scratch_types=[pltpu.VMEM((SC_CHUNK,), jnp.int32), pltpu.VMEM((SC_CHUNK, D), F32)],
    )
    def k(y_hbm, d1_hbm, d2_hbm, g1_hbm, g2_hbm, idx_v, rows_v):
        base = (lax.axis_index("s") * SC_CORES + lax.axis_index("c")) * per_w

        @pl.loop(0, per_w // SC_CHUNK)
        def _(j):
            off = base + j * SC_CHUNK
            for d_hbm, g_hbm in ((d1_hbm, g1_hbm), (d2_hbm, g2_hbm)):
                pltpu.sync_copy(d_hbm.at[pl.ds(off, SC_CHUNK)], idx_v)
                pltpu.sync_copy(y_hbm.at[idx_v], rows_v)
                pltpu.sync_copy(rows_v, g_hbm.at[pl.ds(off, SC_CHUNK)])

    return k(y, dest1, dest2)


def _ffn_kernel(te_ref, nt_ref, xs_ref, ws_ref, w1_ref, w3_ref, w2_ref, y_ref, w1b_ref, w3b_ref, w2b_ref):
    k = pl.program_id(0)

    @pl.when(k < nt_ref[0])
    def _():
        @pl.when((k == 0) | (te_ref[k] != te_ref[jnp.maximum(k - 1, 0)]))
        def _():
            w1b_ref[...] = w1_ref[0, 0].astype(BF16)
            w3b_ref[...] = w3_ref[0, 0].astype(BF16)
            w2b_ref[...] = w2_ref[0, 0].astype(BF16)

        words = pltpu.bitcast(xs_ref[...], jnp.uint32)
        lo = pltpu.bitcast(lax.shift_left(words, jnp.uint32(16)), F32)
        hi = pltpu.bitcast(words & jnp.uint32(0xFFFF0000), F32)
        xs = jnp.concatenate([lo, hi], axis=1).astype(BF16)
        a = jnp.dot(xs, w1b_ref[...], preferred_element_type=F32)
        b = jnp.dot(xs, w3b_ref[...], preferred_element_type=F32)
        act = (a * jax.nn.sigmoid(a)) * b * ws_ref[:, 0:1]
        y_ref[...] = jnp.dot(act.astype(BF16), w2b_ref[...], preferred_element_type=F32)


def _ffn(xs, ws, tile_expert, n_tiles_used, sw, layer):
    n_sorted = xs.shape[0]
    w_in_spec = pl.BlockSpec((1, 1, D, D_EXPERT), lambda k, te, nt: (layer, te[k], 0, 0))
    return pl.pallas_call(
        _ffn_kernel,
        grid_spec=pltpu.PrefetchScalarGridSpec(
            num_scalar_prefetch=2,
            grid=(n_sorted // FFN_TM,),
            in_specs=[pl.BlockSpec((FFN_TM, HALF), lambda k, te, nt: (k, 0)),
                      pl.BlockSpec((FFN_TM, 128), lambda k, te, nt: (k, 0)), w_in_spec, w_in_spec,
                      pl.BlockSpec((1, 1, D_EXPERT, D), lambda k, te, nt: (layer, te[k], 0, 0))],
            out_specs=pl.BlockSpec((FFN_TM, D), lambda k, te, nt: (k, 0)),
            scratch_shapes=[pltpu.VMEM((D, D_EXPERT), BF16), pltpu.VMEM((D, D_EXPERT), BF16),
                            pltpu.VMEM((D_EXPERT, D), BF16)],
        ),
        out_shape=jax.ShapeDtypeStruct((n_sorted, D), F32),
        compiler_params=_cparams(("arbitrary",)),
        name="ffn",
    )(tile_expert, n_tiles_used, xs, ws, sw['w1'], sw['w3'], sw['w2'])


def _combine_kernel(x1_ref, g1_ref, g2_ref, fn_ref, o_ref, *, final):
    x = x1_ref[...] + (g1_ref[...] + g2_ref[...])
    if final:
        ms = jnp.mean(x * x, axis=-1, keepdims=True)
        x = x * lax.rsqrt(ms + EPS) * fn_ref[...]
    o_ref[...] = x


def _combine(x1, g1, g2, final_norm, tm, final):
    n = x1.shape[0]
    row = pl.BlockSpec((tm, D), lambda i: (i, 0))
    return pl.pallas_call(
        functools.partial(_combine_kernel, final=final),
        grid=(n // tm,),
        in_specs=[row, row, row, _full((1, D))],
        out_specs=row,
        out_shape=jax.ShapeDtypeStruct((n, D), F32),
        compiler_params=_cparams(("parallel",)),
        name="combine",
    )(x1, g1, g2, final_norm)


def _moe_sparse(h2, route, counts, x1, sw, layer, final_norm, final):
    n = x1.shape[0]
    n_sorted = 2 * n + N_EXPERTS * FFN_TM
    e1, e2 = route[:, 0].astype(jnp.int32), route[:, 1].astype(jnp.int32)
    rank1, rank2 = route[:, 4].astype(jnp.int32), route[:, 5].astype(jnp.int32)
    cnt = counts[0, N_GROUPS:N_GROUPS + N_EXPERTS].astype(jnp.int32)
    tiles = (cnt + FFN_TM - 1) // FFN_TM
    tile_end = jnp.cumsum(tiles)
    seg_start = (tile_end - tiles) * FFN_TM
    dest1 = jnp.take(seg_start, e1) + rank1
    dest2 = jnp.take(seg_start, e2) + rank2
    tile_ids = jnp.arange(n_sorted // FFN_TM, dtype=jnp.int32)
    tile_expert = jnp.minimum(jnp.sum(tile_end[None, :] <= tile_ids[:, None], axis=1), N_EXPERTS - 1)
    wrow1 = jnp.broadcast_to(route[:, 2:3], (n, 128))
    wrow2 = jnp.broadcast_to(route[:, 3:4], (n, 128))
    xs, ws = _sc_dispatch(h2, wrow1, wrow2, dest1, dest2, n_sorted)
    y = _ffn(xs, ws, tile_expert.astype(jnp.int32), tile_end[-1:].astype(jnp.int32), sw, layer)
    g1, g2 = _sc_collect(y, dest1, dest2, n)
    return _combine(x1, g1, g2, final_norm, 512, final)


def _t5_buckets(dist):
    max_exact = T5_BUCKETS // 2
    large = max_exact + (np.log(np.maximum(dist, 1) / max_exact) / np.log(T5_MAX_DIST / max_exact)
                         * (T5_BUCKETS - max_exact)).astype(np.int32)
    large = np.minimum(large, T5_BUCKETS - 1)
    return np.where(dist < max_exact, dist, large).astype(np.int32)


def _bias_tables(t5, g, dil, window, t_sample):
    nk = window // dil + 1
    hs = slice(g * HPG, (g + 1) * HPG)
    bias = t5[_t5_buckets(np.arange(nk) * dil)][:, hs].T
    rev = bias[:, ::-1]
    neg = lambda *shape: jnp.full(shape, NEG, F32)

    vec = jnp.concatenate([rev, neg(HPG, CL)], axis=1)
    both = jnp.tile(vec, (1, CL + 1))[:, :CL * 2 * CL].reshape(HPG, CL, 2 * CL)

    rows = []
    for r in range(t_sample):
        shift = r // dil
        per_u = jnp.concatenate([neg(HPG, shift), rev[:, :nk - 1 - shift]], axis=1)
        on_phase = (np.arange(dil) == r % dil)[None, None, :]
        rows.append(jnp.where(on_phase, per_u[:, :, None], NEG).reshape(HPG, window))
    cache_t = jnp.stack(rows, axis=1)
    r = np.arange(t_sample)[:, None]
    c = np.arange(t_sample)[None, :]
    ok = ((r - c) % dil == 0) & (r >= c)
    new_t = jnp.where(jnp.asarray(ok)[None], bias[:, np.clip((r - c) // dil, 0, nk - 1)], NEG)
    return both, cache_t, new_t


def _layer_weights(l, norm1, conv_ssd_w, conv_ssd_b, ssd_dt_bias, ssd_a_log, ssd_d, ssd_norm_w,
                   conv_lru_w, conv_lru_b, lru_br, lru_bi, lru_lambda, b_gate, norm2,
                   w_router_group, b_router_group, w_router_expert, b_router_expert):
    b_all = jnp.concatenate([b_gate[l], jnp.zeros((PW - 3 * D,), F32)])[None]

    def pad128(v):
        return jnp.concatenate([v, jnp.zeros((128 - v.shape[0],), F32)])[None]

    return {
        'norm1': norm1[l][None], 'b_all': b_all,
        'conv_ssd_w': conv_ssd_w[l], 'conv_ssd_b': conv_ssd_b[l][None],
        'dt_bias': pad128(ssd_dt_bias[l]), 'a_log': pad128(ssd_a_log[l]), 'd_skip': pad128(ssd_d[l]),
        'ssd_norm_w': ssd_norm_w[l][None],
        'conv_lru_w': conv_lru_w[l], 'conv_lru_b': conv_lru_b[l][None],
        'lru_br': lru_br[l][None], 'lru_bi': lru_bi[l][None], 'lru_lambda': lru_lambda[l][None],
        'norm2': norm2[l][None],
        'w_router': jnp.concatenate([w_router_group[l], w_router_expert[l],
                                     jnp.zeros((D, ROUTE_LANES - N_GROUPS - N_EXPERTS), F32)],
                                    axis=1).astype(BF16),
        'b_router': pad128(jnp.concatenate([b_router_group[l], b_router_expert[l]])),
    }


def _front_pad(buf):
    return jnp.pad(buf, ((0, 0), (8 - (CONV_W - 1), 0), (0, 0)))


def _cols(P, nb, t, start, width):
    return P.reshape(nb, t, PW)[:, :, start:start + width]


def _kv_rows(P, nb, t, g, n_rows):
    k = _cols(P, nb, t, C_K + g * GW, GW)[:, t - n_rows:].reshape(nb, n_rows, HPG, HD)
    v = _cols(P, nb, t, C_V + g * GW, GW)[:, t - n_rows:].reshape(nb, n_rows, HPG, HD)
    return jnp.stack([k, v], axis=2)


def _layer(x, lw, sw, tables, layer, nb, t, rows, tm, tm_mix, tm_res, tm_moe, conv_ssd, st_ssd, conv_lru, st_lru,
           caches, final_norm, final):
    if caches is None:
        P, *qkv = _proj(x, lw['norm1'], sw['w_all'], lw['b_all'], layer, tm, nb, t)
    else:
        (P,) = _proj(x, lw['norm1'], sw['w_all'], lw['b_all'], layer, tm)
    y_ssd, h_ssd = _ssd(P, _front_pad(conv_ssd), st_ssd.reshape(nb, SSD_INNER, SSD_STATE), lw, nb, t, rows)
    y_lru, h_lru = _lru(P, _front_pad(conv_lru), st_lru.reshape(nb, 1, LRU_W), lw, sw, layer, nb, t, rows)
    attn = []
    for g, (window, dil) in enumerate(ATTN_GROUPS):
        both_t, cache_t, new_t = tables[g]
        if caches is None:
            attn.append(_attn_prompt(qkv[g], qkv[3 + g], qkv[6 + g], both_t, dil, nb, t))
        else:
            attn.append(_attn_sample(P, caches[g], cache_t, new_t, g, layer, nb, t, window))
    mixed = _mix(P, y_ssd, y_lru, attn, sw, layer, tm_mix)
    if caches is None:
        x1, h2, _, route, counts = _res(x, mixed, lw, sw, layer, tm_res, True)
        x2 = _moe_sparse(h2, route, counts, x1, sw, layer, final_norm, final)
    else:
        x1, h2, comb = _res(x, mixed, lw, sw, layer, tm_res, False)
        x2 = _moe(h2, comb, x1, sw, layer, final_norm, tm_moe, final)
    states = (_cols(P, nb, t, C_XBC, SSD_CONV_DIM)[:, t - 3:],
              h_ssd.reshape(nb, SSD_HEADS, SSD_HEAD_DIM, SSD_STATE),
              _cols(P, nb, t, C_XR, LRU_W)[:, t - 3:],
              h_lru.reshape(nb, LRU_W)) + tuple(
                  _kv_rows(P, nb, t, g, min(w, t)) for g, (w, _) in enumerate(ATTN_GROUPS))
    return x2, states


def kernel(x_prompt, x_sample, cache_conv_ssd, state_ssd, cache_conv_lru, state_lru, cache_kv_w128, cache_kv_w512, cache_kv_w2048, norm1, w_in, conv_ssd_w, conv_ssd_b, ssd_dt_bias, ssd_a_log, ssd_d, ssd_norm_w, conv_lru_w, conv_lru_b, lru_wr, lru_br, lru_wi, lru_bi, lru_lambda, t5_bias, w_br_ssd, w_br_lru, w_br_attn, w_gate, b_gate, w_o, norm2, w_router_group, b_router_group, w_router_expert, b_router_expert, w1, w3, w2, final_norm):
    bp, tp, _ = x_prompt.shape
    bs, ts, _ = x_sample.shape
    xp = x_prompt.reshape(bp * tp, D)
    xs = x_sample.reshape(bs * ts, D)
    fn = final_norm[None]
    tables = [_bias_tables(t5_bias, g, dil, window, ts) for g, (window, dil) in enumerate(ATTN_GROUPS)]
    caches = [c.reshape(-1, HD) for c in (cache_kv_w128, cache_kv_w512, cache_kv_w2048)]
    sw = {name: w.astype(BF16) for name, w in dict(
        lru_wr=lru_wr, lru_wi=lru_wi, w_br_ssd=w_br_ssd, w_br_lru=w_br_lru, w_br_attn=w_br_attn, w_o=w_o).items()}
    sw.update(w1=w1, w3=w3, w2=w2)
    sw['w_all'] = _prep_w(w_gate, w_in)
    outs_p, outs_s = [], []
    for l in range(DEPTH):
        lw = _layer_weights(l, norm1, conv_ssd_w, conv_ssd_b, ssd_dt_bias, ssd_a_log, ssd_d, ssd_norm_w,
                            conv_lru_w, conv_lru_b, lru_br, lru_bi, lru_lambda, b_gate, norm2,
                            w_router_group, b_router_group, w_router_expert, b_router_expert)
        final = l == DEPTH - 1
        xp, sp = _layer(xp, lw, sw, tables, l, bp, tp, CL, 1024, 256, 512, 512,
                        jnp.zeros((bp, CONV_W - 1, SSD_CONV_DIM), F32),
                        jnp.zeros((bp, SSD_HEADS, SSD_HEAD_DIM, SSD_STATE), F32),
                        jnp.zeros((bp, CONV_W - 1, LRU_W), F32), jnp.zeros((bp, LRU_W), F32),
                        None, fn, final)
        xs, ss = _layer(xs, lw, sw, tables, l, bs, ts, ts, bs * ts, bs * ts, bs * ts, bs * ts,
                        cache_conv_ssd[l], state_ssd[l], cache_conv_lru[l], state_lru[l],
                        caches, fn, final)
        outs_p.append(sp)
        outs_s.append(ss)

    def stk(outs, i):
        return jnp.stack([o[i] for o in outs], axis=0)

    return ((xp.reshape(bp, tp, D), xs.reshape(bs, ts, D))
            + tuple(stk(outs_p, i) for i in range(7)) + tuple(stk(outs_s, i) for i in range(7)))
```

```python
import functools

import numpy as np
import jax
import jax.numpy as jnp
from jax import lax
from jax.experimental import pallas as pl
from jax.experimental.pallas import tpu as pltpu
from jax.experimental.pallas import tpu_sc as plsc

F32 = jnp.float32
BF16 = jnp.bfloat16
EPS = 1e-6
NEG = -1e30

D = 2048
DEPTH = 2
PAST_LEN = 16384
CL = 128
CONV_W = 4
SSD_HEADS = 16
SSD_HEAD_DIM = 64
SSD_INNER = 1024
SSD_STATE = 128
SSD_CONV_DIM = 1536
LRU_W = 1024
LRU_BLOCKS = 8
LRU_C = 8.0
ATTN_GROUPS = ((128, 1), (512, 4), (2048, 16))
HPG = 4
HD = 128
GW = HPG * HD
T5_BUCKETS = 32
T5_MAX_DIST = 2048
N_GROUPS = 4
PER_GROUP = 4
N_EXPERTS = 16
D_EXPERT = 512

TILE = 512
C_GATE = 0
C_Z = 6144
C_XR = 7168
C_GR = 8192
C_XBC = 9216
C_Q = 10752
C_K = 12288
C_V = 13824
C_DT = 15360
PW = 15872
N_GATE_TILES = (3 * D) // TILE
ROUTE_LANES = 128
VMEM_LIMIT = 56 * 1024 * 1024


def _cparams(sem):
    return pltpu.CompilerParams(dimension_semantics=sem, vmem_limit_bytes=VMEM_LIMIT)


def _full(shape):
    nd = len(shape)
    return pl.BlockSpec(shape, lambda *_: (0,) * nd)


def _of_layer(shape, layer):
    nd = len(shape)
    return pl.BlockSpec((1,) + shape, lambda *_: (layer,) + (0,) * nd)


_W_IN_STARTS = (0, 512, 2576, 3088, 3600, 4112, 1024, 1536, 2048, 4624, 5136, 5648, 6160, 6672, 7184,
                7696, 8208, 8720, 2560)
DT_SHIFT = SSD_HEADS


def _prep_kernel(blk_ref, shift_ref, width_ref, wg_ref, wa_ref, wb_ref, o_ref):
    j = pl.program_id(1)

    @pl.when(j < N_GATE_TILES)
    def _():
        o_ref[0] = wg_ref[0].astype(BF16)

    @pl.when(j >= N_GATE_TILES)
    def _():
        a = wa_ref[0]
        shifted = jnp.concatenate([a[DT_SHIFT:, :], wb_ref[0]], axis=0)
        val = jnp.where(shift_ref[j] == 0, a, shifted)
        row = lax.broadcasted_iota(jnp.int32, val.shape, 0)
        o_ref[0] = jnp.where(row < width_ref[j], val, 0.0).T.astype(BF16)


def _prep_w(w_gate, w_in):
    starts = (0,) * N_GATE_TILES + _W_IN_STARTS
    blk = jnp.asarray([s // TILE for s in starts], jnp.int32)
    shift = jnp.asarray([s % TILE for s in starts], jnp.int32)
    assert all(s % TILE in (0, DT_SHIFT) for s in starts) and w_in.shape[2] % DT_SHIFT == 0
    width = jnp.asarray([TILE] * (len(starts) - 1) + [SSD_HEADS], jnp.int32)
    per_tile = TILE // DT_SHIFT
    w_in_t = jnp.swapaxes(w_in, 1, 2)
    return pl.pallas_call(
        _prep_kernel,
        grid_spec=pltpu.PrefetchScalarGridSpec(
            num_scalar_prefetch=3,
            grid=(DEPTH, PW // TILE),
            in_specs=[pl.BlockSpec((1, D, TILE), lambda l, j, b, s, w: (l, 0, jnp.minimum(j, N_GATE_TILES - 1))),
                      pl.BlockSpec((1, TILE, D), lambda l, j, b, s, w: (l, b[j], 0)),
                      pl.BlockSpec((1, DT_SHIFT, D), lambda l, j, b, s, w: (l, (b[j] + 1) * per_tile, 0))],
            out_specs=pl.BlockSpec((1, D, TILE), lambda l, j, b, s, w: (l, 0, j)),
        ),
        out_shape=jax.ShapeDtypeStruct((DEPTH, D, PW), BF16),
        compiler_params=_cparams(("parallel", "arbitrary")),
        name="prep_w",
    )(blk, shift, width, w_gate, w_in_t, w_in_t)


def _proj_kernel(x_ref, nw_ref, w_ref, b_ref, o_ref, *rest, tm, phase_major):
    if phase_major:
        qkv_refs, (h_ref, acc_ref, ph_ref) = rest[:9], rest[9:]
    else:
        h_ref, acc_ref = rest
    j = pl.program_id(1)

    @pl.when(j == 0)
    def _():
        x = x_ref[...]
        ms = jnp.mean(x * x, axis=-1, keepdims=True)
        h_ref[...] = (x * lax.rsqrt(ms + EPS) * nw_ref[...]).astype(BF16)
        acc_ref[...] = jnp.zeros(acc_ref.shape, F32)

    prev = acc_ref[...]
    o_ref[...] = jnp.where(j <= N_GATE_TILES, jax.nn.sigmoid(prev), prev)
    acc_ref[...] = jnp.dot(h_ref[...], w_ref[0], preferred_element_type=F32) + b_ref[...]

    if phase_major:
        for part in range(3):
            for g, (_, dil) in enumerate(ATTN_GROUPS):
                ref = qkv_refs[part * 3 + g]

                @pl.when(j - 1 == C_Q // TILE + part * 3 + g)
                def _(ref=ref, dil=dil):
                    if dil == 1:
                        ref[0, 0] = o_ref[...].astype(BF16)
                    else:
                        for c in range(TILE // 128):
                            ph_ref[c] = o_ref[:, c * 128:(c + 1) * 128]
                        for p in range(dil):
                            for c in range(TILE // 128):
                                ref[0, p, :, c * 128:(c + 1) * 128] = (
                                    ph_ref[c, pl.ds(p, tm // dil, stride=dil), :].astype(BF16))


def _proj(x, nw, w_all, b_all, layer, tm, nb=None, t=None):
    n = x.shape[0]
    phase_major = nb is not None
    nt = PW // TILE
    out_specs = [pl.BlockSpec((tm, TILE), lambda i, j: (i, jnp.maximum(j - 1, 0)))]
    out_shape = [jax.ShapeDtypeStruct((n, PW), F32)]
    scratch = [pltpu.VMEM((tm, D), BF16), pltpu.VMEM((tm, TILE), F32)]
    if phase_major:
        tpb = t // tm
        for _ in range(3):
            for _, dil in ATTN_GROUPS:
                out_specs.append(pl.BlockSpec((1, dil, tm // dil, GW), lambda i, j: (i // tpb, 0, i % tpb, 0)))
                out_shape.append(jax.ShapeDtypeStruct((nb, dil, t // dil, GW), BF16))
        scratch.append(pltpu.VMEM((TILE // 128, tm, 128), F32))
    return pl.pallas_call(
        functools.partial(_proj_kernel, tm=tm, phase_major=phase_major),
        grid=(n // tm, nt + 1),
        in_specs=[pl.BlockSpec((tm, D), lambda i, j: (i, 0)),
                  pl.BlockSpec((1, D), lambda i, j: (0, 0)),
                  pl.BlockSpec((1, D, TILE), lambda i, j: (layer, 0, jnp.minimum(j, nt - 1))),
                  pl.BlockSpec((1, TILE), lambda i, j: (0, jnp.minimum(j, nt - 1)))],
        out_specs=out_specs,
        out_shape=out_shape,
        scratch_shapes=scratch,
        compiler_params=_cparams(("parallel", "arbitrary")),
        name="proj",
    )(x, nw, w_all, b_all)


def _conv_step(x_ref, xp_ref, cw_ref, cb_ref, rows, out_rows):
    xp_ref[8:8 + rows, :] = _bf16_round(x_ref[...])
    cw = _bf16_round(cw_ref[...])
    acc = cw[0:1, :] * xp_ref[5:5 + out_rows, :]
    for j in range(1, CONV_W):
        acc = acc + cw[j:j + 1, :] * xp_ref[5 + j:5 + j + out_rows, :]
    tail = xp_ref[rows:rows + 8, :]
    xp_ref[0:8, :] = tail
    return acc + cb_ref[...]


def _bf16_round(x):
    return x.astype(BF16).astype(F32)


def _softplus(x):
    return jnp.maximum(x, 0.0) + jnp.log1p(jnp.exp(-jnp.abs(x)))


def _ssd_kernel(z_ref, xbc_ref, dt_ref, tail_ref, h0_ref, cw_ref, cb_ref, dtb_ref, alog_ref, dsk_ref, nw_ref,
                y_ref, hf_ref, xp_ref, act_ref, st_ref, ysc_ref, *, rows, n_chunks):
    c = pl.program_id(1)

    @pl.when(c == 0)
    def _():
        xp_ref[0:8, :] = _bf16_round(tail_ref[0])
        st_ref[...] = h0_ref[0]

    if rows < CL:
        xp_ref[8 + rows:, :] = jnp.zeros((CL - rows, SSD_CONV_DIM), F32)
    conv = _conv_step(xbc_ref, xp_ref, cw_ref, cb_ref, rows, CL)
    act_ref[...] = conv * jax.nn.sigmoid(conv)

    row = lax.broadcasted_iota(jnp.int32, (CL, 128), 0)
    lane = lax.broadcasted_iota(jnp.int32, (CL, 128), 1)
    raw = dt_ref[...]
    if rows < CL:
        raw = jnp.concatenate([raw, jnp.zeros((CL - rows, 128), F32)], axis=0)
    dt = _softplus(raw + dtb_ref[...])
    dt = jnp.where((lane < SSD_HEADS) & (row < rows), dt, 0.0)
    da = dt * (-jnp.exp(alog_ref[...]))
    acs = da
    d = 1
    while d < CL:
        acs = acs + jnp.where(row >= d, pltpu.roll(acs, d, 0), 0.0)
        d *= 2
    acs_t = acs.T
    last = acs[CL - 1:CL, :]
    e_acs = jnp.exp(acs)
    to_end = jnp.exp(last - acs)
    cdec = jnp.exp(last)
    causal = row >= lane
    lo_lane = lane < SSD_HEAD_DIM
    lo_row = row < SSD_HEAD_DIM
    dsk = dsk_ref[...]

    def pair_cols(arr, h):
        return jnp.where(lo_lane, arr[:, h:h + 1], arr[:, h + 1:h + 2])

    nt = (((1,), (1,)), ((), ()))
    for g in range(2):
        bm = act_ref[:, SSD_INNER + g * SSD_STATE:SSD_INNER + (g + 1) * SSD_STATE].astype(BF16)
        cm = act_ref[:, SSD_INNER + 256 + g * SSD_STATE:SSD_INNER + 256 + (g + 1) * SSD_STATE].astype(BF16)
        cb = lax.dot_general(cm, bm, nt, preferred_element_type=F32)
        for pp in range(4):
            h = g * 8 + 2 * pp
            sl = slice(h * SSD_HEAD_DIM, h * SSD_HEAD_DIM + 128)
            xs = act_ref[:, sl]
            xdt = xs * pair_cols(dt, h)
            xdt_b = xdt.astype(BF16)
            ys = []
            for hh in (h, h + 1):
                seg = acs[:, hh:hh + 1] - acs_t[hh:hh + 1, :]
                decay = jnp.exp(jnp.where(causal, seg, -jnp.inf))
                ys.append(jnp.dot((cb * decay).astype(BF16), xdt_b, preferred_element_type=F32))
            y_diag = jnp.where(lo_lane, ys[0], ys[1])
            st = st_ref[sl, :]
            y_off = lax.dot_general(cm, st.astype(BF16), nt, preferred_element_type=F32) * pair_cols(e_acs, h)
            d_pair = jnp.where(lo_lane, dsk[:, h:h + 1], dsk[:, h + 1:h + 2])
            ysc_ref[:, sl] = y_diag + y_off + d_pair * xs
            xdte_t = (xdt * pair_cols(to_end, h)).T.astype(BF16)
            s_new = jnp.dot(xdte_t, bm, preferred_element_type=F32)
            dec = jnp.where(lo_row, cdec[:, h:h + 1], cdec[:, h + 1:h + 2])
            st_ref[sl, :] = dec * st + s_new

    zz = z_ref[...]
    yg = ysc_ref[0:rows, :] * (zz * jax.nn.sigmoid(zz))
    gw = SSD_INNER // 2
    for g in range(2):
        part = yg[:, g * gw:(g + 1) * gw]
        ms = jnp.mean(part * part, axis=-1, keepdims=True)
        y_ref[:, g * gw:(g + 1) * gw] = (part * lax.rsqrt(ms + EPS)
                                         * nw_ref[:, g * gw:(g + 1) * gw]).astype(y_ref.dtype)

    @pl.when(c == n_chunks - 1)
    def _():
        hf_ref[0] = st_ref[...]


def _ssd(P, tail, h0, lw, nb, t, rows):
    nc = t // rows
    kern = functools.partial(_ssd_kernel, rows=rows, n_chunks=nc)
    return pl.pallas_call(
        kern,
        grid=(nb, nc),
        in_specs=[pl.BlockSpec((rows, SSD_INNER), lambda b, c: (b * nc + c, C_Z // SSD_INNER)),
                  pl.BlockSpec((rows, SSD_CONV_DIM), lambda b, c: (b * nc + c, C_XBC // SSD_CONV_DIM)),
                  pl.BlockSpec((rows, 128), lambda b, c: (b * nc + c, C_DT // 128)),
                  pl.BlockSpec((1, 8, SSD_CONV_DIM), lambda b, c: (b, 0, 0)),
                  pl.BlockSpec((1, SSD_INNER, SSD_STATE), lambda b, c: (b, 0, 0)),
                  _full((CONV_W, SSD_CONV_DIM)), _full((1, SSD_CONV_DIM)),
                  _full((1, 128)), _full((1, 128)), _full((1, 128)), _full((1, SSD_INNER))],
        out_specs=[pl.BlockSpec((rows, SSD_INNER), lambda b, c: (b * nc + c, 0)),
                   pl.BlockSpec((1, SSD_INNER, SSD_STATE), lambda b, c: (b, 0, 0))],
        out_shape=[jax.ShapeDtypeStruct((nb * t, SSD_INNER), BF16 if rows % 16 == 0 else F32),
                   jax.ShapeDtypeStruct((nb, SSD_INNER, SSD_STATE), F32)],
        scratch_shapes=[pltpu.VMEM((8 + CL, SSD_CONV_DIM), F32),
                        pltpu.VMEM((CL, SSD_CONV_DIM), F32),
                        pltpu.VMEM((SSD_INNER, SSD_STATE), F32),
                        pltpu.VMEM((CL, SSD_INNER), F32)],
        compiler_params=_cparams(("parallel", "arbitrary")),
        name="ssd",
    )(P, P, P, tail, h0, lw['conv_ssd_w'], lw['conv_ssd_b'], lw['dt_bias'], lw['a_log'], lw['d_skip'], lw['ssd_norm_w'])


def _lru_kernel(xr_ref, gr_ref, tail_ref, h0_ref, cw_ref, cb_ref, wr_ref, br_ref, wi_ref, bi_ref, lam_ref,
                y_ref, hl_ref, xp_ref, h_ref, *, rows, n_chunks):
    c = pl.program_id(1)

    @pl.when(c == 0)
    def _():
        xp_ref[0:8, :] = _bf16_round(tail_ref[0])
        h_ref[...] = h0_ref[0]

    x = _conv_step(xr_ref, xp_ref, cw_ref, cb_ref, rows, rows)
    xb = x.astype(BF16)
    rs, is_ = [], []
    for n in range(LRU_BLOCKS):
        blk = xb[:, n * 128:(n + 1) * 128]
        rs.append(jnp.dot(blk, wr_ref[0, n], preferred_element_type=F32))
        is_.append(jnp.dot(blk, wi_ref[0, n], preferred_element_type=F32))
    r_gate = jax.nn.sigmoid(jnp.concatenate(rs, axis=1) + br_ref[...])
    i_gate = jax.nn.sigmoid(jnp.concatenate(is_, axis=1) + bi_ref[...])
    log_a = -LRU_C * r_gate * _softplus(-lam_ref[...])
    a = jnp.exp(log_a)
    th = jnp.tanh(log_a)
    b = jnp.sqrt(-2.0 * th / (1.0 - th)) * (i_gate * x)
    in_group = lax.broadcasted_iota(jnp.int32, (rows, LRU_W), 0) % 8
    for d in (1, 2, 4):
        a_s = jnp.where(in_group >= d, pltpu.roll(a, d, 0), 1.0)
        b_s = jnp.where(in_group >= d, pltpu.roll(b, d, 0), 0.0)
        b = a * b_s + b
        a = a * a_s
    carry = h_ref[...]
    groups = []
    for g in range(rows // 8):
        h_g = b[8 * g:8 * g + 8, :] + a[8 * g:8 * g + 8, :] * carry
        groups.append(h_g)
        carry = h_g[7:8, :]
    h = jnp.concatenate(groups, axis=0) if len(groups) > 1 else groups[0]
    last = carry
    h_ref[...] = last
    y_ref[...] = (h * jax.nn.gelu(gr_ref[...])).astype(y_ref.dtype)

    @pl.when(c == n_chunks - 1)
    def _():
        hl_ref[0] = last


def _lru(P, tail, h0, lw, sw, layer, nb, t, rows):
    nc = t // rows
    kern = functools.partial(_lru_kernel, rows=rows, n_chunks=nc)
    y_dtype = BF16 if rows % 16 == 0 else F32
    return pl.pallas_call(
        kern,
        grid=(nb, nc),
        in_specs=[pl.BlockSpec((rows, LRU_W), lambda b, c: (b * nc + c, C_XR // LRU_W)),
                  pl.BlockSpec((rows, LRU_W), lambda b, c: (b * nc + c, C_GR // LRU_W)),
                  pl.BlockSpec((1, 8, LRU_W), lambda b, c: (b, 0, 0)),
                  pl.BlockSpec((1, 1, LRU_W), lambda b, c: (b, 0, 0)),
                  _full((CONV_W, LRU_W)), _full((1, LRU_W)),
                  _of_layer((LRU_BLOCKS, 128, 128), layer), _full((1, LRU_W)),
                  _of_layer((LRU_BLOCKS, 128, 128), layer), _full((1, LRU_W)), _full((1, LRU_W))],
        out_specs=[pl.BlockSpec((rows, LRU_W), lambda b, c: (b * nc + c, 0)),
                   pl.BlockSpec((1, 1, LRU_W), lambda b, c: (b, 0, 0))],
        out_shape=[jax.ShapeDtypeStruct((nb * t, LRU_W), y_dtype),
                   jax.ShapeDtypeStruct((nb, 1, LRU_W), F32)],
        scratch_shapes=[pltpu.VMEM((8 + rows, LRU_W), F32), pltpu.VMEM((1, LRU_W), F32)],
        compiler_params=_cparams(("parallel", "arbitrary")),
        name="lru",
    )(P, P, tail, h0, lw['conv_lru_w'], lw['conv_lru_b'], sw['lru_wr'], lw['lru_br'], sw['lru_wi'], lw['lru_bi'],
      lw['lru_lambda'])


def _attn_kernel(q_ref, kp_ref, vp_ref, kc_ref, vc_ref, bias_ref, o_ref, lse_ref, *, sub, phases):
    scale = HD ** -0.5
    nt = (((1,), (1,)), ((), ()))
    lane = lax.broadcasted_iota(jnp.int32, (CL, 128), 1)
    key = lax.broadcasted_iota(jnp.int32, (CL, 2 * CL), 1)
    first_ok = (pl.program_id(2) > 0) | (key >= CL)
    for z in range(phases):
        for s in range(sub):
            rows = slice(s * CL, (s + 1) * CL)
            both = slice((s - 1) * CL, (s + 1) * CL)
            lse_all = jnp.zeros((CL, 128), F32)
            for h in range(HPG):
                sl = slice(h * HD, (h + 1) * HD)
                q = q_ref[0, z, rows, sl]
                if s == 0:
                    kk = jnp.concatenate([kp_ref[0, z, :, sl], kc_ref[0, z, rows, sl]], axis=0)
                    vv = jnp.concatenate([vp_ref[0, z, :, sl], vc_ref[0, z, rows, sl]], axis=0)
                else:
                    kk, vv = kc_ref[0, z, both, sl], vc_ref[0, z, both, sl]
                sc = lax.dot_general(q, kk, nt, preferred_element_type=F32) * scale + bias_ref[h]
                if s == 0:
                    sc = jnp.where(first_ok, sc, NEG)
                m = jnp.max(sc, axis=-1, keepdims=True)
                p = jnp.exp(sc - m)
                l = jnp.sum(p, axis=-1, keepdims=True)
                o_ref[rows, z * GW + h * HD:z * GW + (h + 1) * HD] = jnp.dot(
                    (p * (1.0 / l)).astype(BF16), vv, preferred_element_type=F32)
                lse_all = jnp.where(lane == h, m + jnp.log(l), lse_all)
            lse_ref[rows, z * 128:(z + 1) * 128] = lse_all


def _attn_prompt(q, k, v, bias, dil, nb, t):
    n = nb * t
    blocks = 8
    sub = min(blocks, t // dil // CL)
    phases = min(blocks // sub, dil)
    nstep = t // dil // (sub * CL)
    cur = pl.BlockSpec((1, phases, sub * CL, GW), lambda b, p, i: (b, p, i, 0))
    prev = pl.BlockSpec((1, phases, CL, GW), lambda b, p, i: (b, p, jnp.maximum(i * sub - 1, 0), 0))
    return pl.pallas_call(
        functools.partial(_attn_kernel, sub=sub, phases=phases),
        grid=(nb, dil // phases, nstep),
        in_specs=[cur, prev, prev, cur, cur, _full((HPG, CL, 2 * CL))],
        out_specs=[pl.BlockSpec((sub * CL, phases * GW), lambda b, p, i: (b * nstep + i, p)),
                   pl.BlockSpec((sub * CL, phases * 128), lambda b, p, i: (b * nstep + i, p))],
        out_shape=[jax.ShapeDtypeStruct((n // dil, dil * GW), F32),
                   jax.ShapeDtypeStruct((n // dil, dil * 128), F32)],
        compiler_params=_cparams(("parallel", "parallel", "arbitrary")),
        name=f"attn_prompt_d{dil}",
    )(q, k, v, k, v, bias)


def _attn_sample_kernel(q_ref, kv_ref, kb_ref, vb_ref, ba_ref, bb_ref, o_ref, lse_ref):
    scale = HD ** -0.5
    nt = (((1,), (1,)), ((), ()))
    rows = o_ref.shape[0]
    window = kv_ref.shape[0] // (2 * HPG)
    lane = lax.broadcasted_iota(jnp.int32, (rows, 128), 1)
    lse_all = jnp.zeros((rows, 128), F32)
    for h in range(HPG):
        sl = slice(h * HD, (h + 1) * HD)
        q = q_ref[:, sl].astype(BF16)
        k_cache = kv_ref[pl.ds(h, window, stride=2 * HPG), :].astype(BF16)
        v_cache = kv_ref[pl.ds(HPG + h, window, stride=2 * HPG), :].astype(BF16)
        sa = lax.dot_general(q, k_cache, nt, preferred_element_type=F32) * scale + ba_ref[h]
        sb = lax.dot_general(q, kb_ref[:, sl].astype(BF16), nt, preferred_element_type=F32) * scale + bb_ref[h]
        m = jnp.maximum(jnp.max(sa, axis=-1, keepdims=True), jnp.max(sb, axis=-1, keepdims=True))
        pa = jnp.exp(sa - m)
        pb = jnp.exp(sb - m)
        l = jnp.sum(pa, axis=-1, keepdims=True) + jnp.sum(pb, axis=-1, keepdims=True)
        inv = 1.0 / l
        o_ref[:, sl] = (jnp.dot((pa * inv).astype(BF16), v_cache, preferred_element_type=F32)
                        + jnp.dot((pb * inv).astype(BF16), vb_ref[:, sl].astype(BF16), preferred_element_type=F32))
        lse_all = jnp.where(lane == h, m + jnp.log(l), lse_all)
    lse_ref[...] = lse_all


def _attn_sample(Ps, cache_rows, bias_a, bias_b, g, layer, nb, t, window):
    tq, tk, tv = C_Q // GW + g, C_K // GW + g, C_V // GW + g
    out_spec = pl.BlockSpec((t, GW), lambda b: (b, 0))
    return pl.pallas_call(
        _attn_sample_kernel,
        grid=(nb,),
        in_specs=[pl.BlockSpec((t, GW), lambda b: (b, tq)),
                  pl.BlockSpec((window * 2 * HPG, HD), lambda b: (layer * nb + b, 0)),
                  pl.BlockSpec((t, GW), lambda b: (b, tk)),
                  pl.BlockSpec((t, GW), lambda b: (b, tv)),
                  _full((HPG, t, window)), _full((HPG, t, t))],
        out_specs=[out_spec, pl.BlockSpec((t, 128), lambda b: (b, 0))],
        out_shape=[jax.ShapeDtypeStruct((nb * t, GW), F32), jax.ShapeDtypeStruct((nb * t, 128), F32)],
        compiler_params=_cparams(("parallel",)),
        name=f"attn_sample_w{window}",
    )(Ps, cache_rows, Ps, Ps, bias_a, bias_b)


def _mix_kernel(gs_ref, gl_ref, ga_ref, ys_ref, yl_ref, o0_ref, o1_ref, o2_ref, l0_ref, l1_ref, l2_ref,
                wbs_ref, wbl_ref, wba_ref, out_ref, *scratch, dils):
    tm = out_ref.shape[0]
    o_heads, lses = [], []
    scratch = list(scratch)
    for o_ref, l_ref, dil in zip((o0_ref, o1_ref, o2_ref), (l0_ref, l1_ref, l2_ref), dils):
        if dil == 1:
            o_heads.append([o_ref[:, h * HD:(h + 1) * HD] for h in range(HPG)])
            lses.append(l_ref[...])
            continue
        o_scr, l_scr = scratch.pop(0), scratch.pop(0)
        for p in range(dil):
            rows = pl.ds(p, tm // dil, stride=dil)
            for h in range(HPG):
                o_scr[h, rows, :] = o_ref[:, p * GW + h * HD:p * GW + (h + 1) * HD]
            l_scr[rows, :] = l_ref[:, p * 128:(p + 1) * 128]
        o_heads.append([o_scr[h] for h in range(HPG)])
        lses.append(l_scr[...])
    l0, l1, l2 = lses
    m = jnp.maximum(jnp.maximum(l0, l1), l2)
    e0, e1, e2 = jnp.exp(l0 - m), jnp.exp(l1 - m), jnp.exp(l2 - m)
    den = e0 + e1 + e2
    w0, w1, w2 = e0 / den, e1 / den, e2 / den
    heads = []
    for h in range(HPG):
        per_head = lambda w: jnp.broadcast_to(w[:, h:h + 1], (tm, HD))
        heads.append(o_heads[0][h] * per_head(w0) + o_heads[1][h] * per_head(w1) + o_heads[2][h] * per_head(w2))
    ya = jnp.concatenate(heads, axis=1)
    mixed = (gs_ref[...] * jnp.dot(ys_ref[...].astype(BF16), wbs_ref[0], preferred_element_type=F32)
             + gl_ref[...] * jnp.dot(yl_ref[...].astype(BF16), wbl_ref[0], preferred_element_type=F32)
             + ga_ref[...] * jnp.dot(ya.astype(BF16), wba_ref[0], preferred_element_type=F32))
    out_ref[...] = mixed.astype(BF16)


def _mix(P, y_ssd, y_lru, attn, dils, sw, layer, tm):
    n = P.shape[0]
    row = lambda w: pl.BlockSpec((tm, w), lambda i: (i, 0))
    phased = lambda w, d: pl.BlockSpec((tm // d, d * w), lambda i: (i, 0))
    (o0, s0), (o1, s1), (o2, s2) = attn
    scratch = []
    for d in dils:
        if d > 1:
            scratch += [pltpu.VMEM((HPG, tm, HD), F32), pltpu.VMEM((tm, 128), F32)]
    return pl.pallas_call(
        functools.partial(_mix_kernel, dils=dils),
        grid=(n // tm,),
        in_specs=[pl.BlockSpec((tm, D), lambda i: (i, 0)), pl.BlockSpec((tm, D), lambda i: (i, 1)),
                  pl.BlockSpec((tm, D), lambda i: (i, 2)),
                  row(SSD_INNER), row(LRU_W)] + [phased(GW, d) for d in dils] + [phased(128, d) for d in dils] + [
                  _of_layer((SSD_INNER, D), layer), _of_layer((LRU_W, D), layer), _of_layer((GW, D), layer)],
        out_specs=row(D),
        out_shape=jax.ShapeDtypeStruct((n, D), BF16),
        scratch_shapes=scratch,
        compiler_params=_cparams(("parallel",)),
        name="mix",
    )(P, P, P, y_ssd, y_lru, o0, o1, o2, s0, s1, s2, sw['w_br_ssd'], sw['w_br_lru'], sw['w_br_attn'])


def _res_kernel(x_ref, mixed_ref, wo_ref, n2_ref, wr_ref, br_ref, x1_ref, h2_ref, comb_ref, *rest, dispatch):
    x1 = x_ref[...] + jnp.dot(mixed_ref[...], wo_ref[0], preferred_element_type=F32)
    x1_ref[...] = x1
    ms = jnp.mean(x1 * x1, axis=-1, keepdims=True)
    h2 = x1 * lax.rsqrt(ms + EPS) * n2_ref[...]
    h2b = h2.astype(BF16)
    if dispatch:
        lo = pltpu.bitcast(h2b[:, :HALF].astype(F32), jnp.uint32)
        hi = pltpu.bitcast(h2b[:, HALF:].astype(F32), jnp.uint32)
        h2_ref[...] = pltpu.bitcast(lax.shift_right_logical(lo, jnp.uint32(16)) | hi, jnp.int32)
    else:
        h2_ref[...] = h2b
    logits = jnp.dot(h2b, wr_ref[...], preferred_element_type=F32) + br_ref[...]
    lane = lax.broadcasted_iota(jnp.int32, logits.shape, 1).astype(F32)
    big = float(ROUTE_LANES)

    def first_max(vals, ok):
        v = jnp.where(ok, vals, NEG)
        top = jnp.max(v, axis=-1, keepdims=True)
        idx = jnp.min(jnp.where(ok & (v == top), lane, big), axis=-1, keepdims=True)
        return top, idx

    is_g = lane < N_GROUPS
    gmax, gsel = first_max(logits, is_g)
    gp = 1.0 / jnp.sum(jnp.where(is_g, jnp.exp(logits - gmax), 0.0), axis=-1, keepdims=True)
    lo = N_GROUPS + PER_GROUP * gsel
    is_e = (lane >= lo) & (lane < lo + PER_GROUP)
    t1, i1 = first_max(logits, is_e)
    t2, i2 = first_max(logits, is_e & (lane != i1))
    e2 = jnp.exp(t2 - t1)
    w1 = gp / (1.0 + e2)
    w2 = gp * e2 / (1.0 + e2)
    comb_ref[...] = jnp.where(lane == i1, w1, 0.0) + jnp.where(lane == i2, w2, 0.0)

    if dispatch:
        route_ref, cnt_ref, carry_ref = rest

        @pl.when(pl.program_id(0) == 0)
        def _():
            carry_ref[...] = jnp.zeros(carry_ref.shape, F32)

        tm = x1.shape[0]
        onehot = jnp.where((lane == i1) | (lane == i2), 1.0, 0.0)
        r = lax.broadcasted_iota(jnp.int32, (tm, tm), 0)
        c = lax.broadcasted_iota(jnp.int32, (tm, tm), 1)
        earlier = jnp.where(r > c, 1.0, 0.0).astype(BF16)
        before = jnp.dot(earlier, onehot.astype(BF16), preferred_element_type=F32) + carry_ref[...]
        rank1 = jnp.sum(jnp.where(lane == i1, before, 0.0), axis=-1, keepdims=True)
        rank2 = jnp.sum(jnp.where(lane == i2, before, 0.0), axis=-1, keepdims=True)
        carry_ref[...] += jnp.sum(onehot, axis=0, keepdims=True)
        cnt_ref[...] = carry_ref[...]
        fields = (i1 - N_GROUPS, i2 - N_GROUPS, w1, w2, rank1, rank2)
        route = jnp.zeros(logits.shape, F32)
        for k, val in enumerate(fields):
            route = jnp.where(lane == k, val, route)
        route_ref[...] = route


def _res(x, mixed, lw, sw, layer, tm, dispatch):
    n = x.shape[0]
    h2_cols, h2_dtype = (HALF, jnp.int32) if dispatch else (D, BF16)
    out_specs = [pl.BlockSpec((tm, D), lambda i: (i, 0)), pl.BlockSpec((tm, h2_cols), lambda i: (i, 0)),
                 pl.BlockSpec((tm, ROUTE_LANES), lambda i: (i, 0))]
    out_shape = [jax.ShapeDtypeStruct((n, D), F32), jax.ShapeDtypeStruct((n, h2_cols), h2_dtype),
                 jax.ShapeDtypeStruct((n, ROUTE_LANES), F32)]
    scratch = []
    if dispatch:
        out_specs += [pl.BlockSpec((tm, ROUTE_LANES), lambda i: (i, 0)), _full((1, ROUTE_LANES))]
        out_shape += [jax.ShapeDtypeStruct((n, ROUTE_LANES), F32), jax.ShapeDtypeStruct((1, ROUTE_LANES), F32)]
        scratch = [pltpu.VMEM((1, ROUTE_LANES), F32)]
    return pl.pallas_call(
        functools.partial(_res_kernel, dispatch=dispatch),
        grid=(n // tm,),
        in_specs=[pl.BlockSpec((tm, D), lambda i: (i, 0)), pl.BlockSpec((tm, D), lambda i: (i, 0)),
                  _of_layer((D, D), layer), _full((1, D)), _full((D, ROUTE_LANES)), _full((1, ROUTE_LANES))],
        out_specs=out_specs,
        out_shape=out_shape,
        scratch_shapes=scratch,
        compiler_params=_cparams(("arbitrary",)),
        name="res_router",
    )(x, mixed, sw['w_o'], lw['norm2'], lw['w_router'], lw['b_router'])


def _moe_kernel(h2_ref, comb_ref, x1_ref, w1_ref, w3_ref, w2_ref, fn_ref, o_ref, *, final):
    e = pl.program_id(1)

    @pl.when(e == 0)
    def _():
        o_ref[...] = x1_ref[...]

    h = h2_ref[...]
    a = jnp.dot(h, w1_ref[0, 0].astype(BF16), preferred_element_type=F32)
    b = jnp.dot(h, w3_ref[0, 0].astype(BF16), preferred_element_type=F32)
    comb = comb_ref[...]
    lane = lax.broadcasted_iota(jnp.int32, comb.shape, 1)
    w = jnp.sum(jnp.where(lane == e + N_GROUPS, comb, 0.0), axis=-1, keepdims=True)
    act = (a * jax.nn.sigmoid(a)) * b * w
    o_ref[...] += jnp.dot(act.astype(BF16), w2_ref[0, 0].astype(BF16), preferred_element_type=F32)

    if final:
        @pl.when(e == N_EXPERTS - 1)
        def _():
            x = o_ref[...]
            ms = jnp.mean(x * x, axis=-1, keepdims=True)
            o_ref[...] = x * lax.rsqrt(ms + EPS) * fn_ref[...]


def _moe(h2, comb, x1, sw, layer, final_norm, tm, final):
    n = x1.shape[0]
    return pl.pallas_call(
        functools.partial(_moe_kernel, final=final),
        grid=(n // tm, N_EXPERTS),
        in_specs=[pl.BlockSpec((tm, D), lambda i, e: (i, 0)),
                  pl.BlockSpec((tm, ROUTE_LANES), lambda i, e: (i, 0)),
                  pl.BlockSpec((tm, D), lambda i, e: (i, 0)),
                  pl.BlockSpec((1, 1, D, D_EXPERT), lambda i, e: (layer, e, 0, 0)),
                  pl.BlockSpec((1, 1, D, D_EXPERT), lambda i, e: (layer, e, 0, 0)),
                  pl.BlockSpec((1, 1, D_EXPERT, D), lambda i, e: (layer, e, 0, 0)),
                  pl.BlockSpec((1, D), lambda i, e: (0, 0))],
        out_specs=pl.BlockSpec((tm, D), lambda i, e: (i, 0)),
        out_shape=jax.ShapeDtypeStruct((n, D), F32),
        compiler_params=_cparams(("parallel", "arbitrary")),
        name="moe",
    )(h2, comb, x1, sw['w1'], sw['w3'], sw['w2'], final_norm)


FFN_TM = 512
SC_CORES = 2
SC_SUBCORES = 16
SC_WORKERS = SC_CORES * SC_SUBCORES
SC_CHUNK = 32
HALF = D // 2


def _sc_mesh():
    return plsc.VectorSubcoreMesh(core_axis_name="c", subcore_axis_name="s", num_cores=SC_CORES,
                                  num_subcores=SC_SUBCORES)


def _sc_dispatch(x, wrow1, wrow2, dest1, dest2, n_sorted):
    n = x.shape[0]
    per_w = n // SC_WORKERS

    @functools.partial(
        pl.kernel, mesh=_sc_mesh(),
        out_type=(jax.ShapeDtypeStruct((n_sorted, HALF), jnp.int32), jax.ShapeDtypeStruct((n_sorted, 128), F32)),
        scratch_types=[pltpu.VMEM((SC_CHUNK,), jnp.int32), pltpu.VMEM((SC_CHUNK,), jnp.int32),
                       pltpu.VMEM((SC_CHUNK, HALF), jnp.int32), pltpu.VMEM((SC_CHUNK, 128), F32)],
    )
    def k(x_hbm, w1_hbm, w2_hbm, d1_hbm, d2_hbm, out_hbm, wout_hbm, i1_v, i2_v, rows_v, wrows_v):
        base = (lax.axis_index("s") * SC_CORES + lax.axis_index("c")) * per_w

        @pl.loop(0, per_w // SC_CHUNK)
        def _(j):
            off = base + j * SC_CHUNK
            pltpu.sync_copy(d1_hbm.at[pl.ds(off, SC_CHUNK)], i1_v)
            pltpu.sync_copy(d2_hbm.at[pl.ds(off, SC_CHUNK)], i2_v)
            pltpu.sync_copy(x_hbm.at[pl.ds(off, SC_CHUNK)], rows_v)
            pltpu.sync_copy(rows_v, out_hbm.at[i1_v])
            pltpu.sync_copy(rows_v, out_hbm.at[i2_v])
            pltpu.sync_copy(w1_hbm.at[pl.ds(off, SC_CHUNK)], wrows_v)
            pltpu.sync_copy(wrows_v, wout_hbm.at[i1_v])
            pltpu.sync_copy(w2_hbm.at[pl.ds(off, SC_CHUNK)], wrows_v)
            pltpu.sync_copy(wrows_v, wout_hbm.at[i2_v])

    return k(x, wrow1, wrow2, dest1, dest2)


def _sc_collect(y, dest1, dest2, n):
    per_w = n // SC_WORKERS

    @functools.partial(
        pl.kernel, mesh=_sc_mesh(),
        out_type=(jax.ShapeDtypeStruct((n, D), F32), jax.ShapeDtypeStruct((n, D), F32)),
        scratch_types=[pltpu.VMEM((SC_CHUNK,), jnp.int32), pltpu.VMEM((SC_CHUNK, D), F32)],
    )
    def k(y_hbm, d1_hbm, d2_hbm, g1_hbm, g2_hbm, idx_v, rows_v):
        base = (lax.axis_index("s") * SC_CORES + lax.axis_index("c")) * per_w

        @pl.loop(0, per_w // SC_CHUNK)
        def _(j):
            off = base + j * SC_CHUNK
            for d_hbm, g_hbm in ((d1_hbm, g1_hbm), (d2_hbm, g2_hbm)):
                pltpu.sync_copy(d_hbm.at[pl.ds(off, SC_CHUNK)], idx_v)
                pltpu.sync_copy(y_hbm.at[idx_v], rows_v)
                pltpu.sync_copy(rows_v, g_hbm.at[pl.ds(off, SC_CHUNK)])

    return k(y, dest1, dest2)


def _ffn_kernel(te_ref, nt_ref, xs_ref, ws_ref, w1_ref, w3_ref, w2_ref, y_ref, w1b_ref, w3b_ref, w2b_ref):
    k = pl.program_id(0)

    @pl.when(k < nt_ref[0])
    def _():
        @pl.when((k == 0) | (te_ref[k] != te_ref[jnp.maximum(k - 1, 0)]))
        def _():
            w1b_ref[...] = w1_ref[0, 0].astype(BF16)
            w3b_ref[...] = w3_ref[0, 0].astype(BF16)
            w2b_ref[...] = w2_ref[0, 0].astype(BF16)

        words = pltpu.bitcast(xs_ref[...], jnp.uint32)
        lo = pltpu.bitcast(lax.shift_left(words, jnp.uint32(16)), F32)
        hi = pltpu.bitcast(words & jnp.uint32(0xFFFF0000), F32)
        xs = jnp.concatenate([lo, hi], axis=1).astype(BF16)
        a = jnp.dot(xs, w1b_ref[...], preferred_element_type=F32)
        b = jnp.dot(xs, w3b_ref[...], preferred_element_type=F32)
        act = (a * jax.nn.sigmoid(a)) * b * ws_ref[:, 0:1]
        y_ref[...] = jnp.dot(act.astype(BF16), w2b_ref[...], preferred_element_type=F32)


def _ffn(xs, ws, tile_expert, n_tiles_used, sw, layer):
    n_sorted = xs.shape[0]
    w_in_spec = pl.BlockSpec((1, 1, D, D_EXPERT), lambda k, te, nt: (layer, te[k], 0, 0))
    return pl.pallas_call(
        _ffn_kernel,
        grid_spec=pltpu.PrefetchScalarGridSpec(
            num_scalar_prefetch=2,
            grid=(n_sorted // FFN_TM,),
            in_specs=[pl.BlockSpec((FFN_TM, HALF), lambda k, te, nt: (k, 0)),
                      pl.BlockSpec((FFN_TM, 128), lambda k, te, nt: (k, 0)), w_in_spec, w_in_spec,
                      pl.BlockSpec((1, 1, D_EXPERT, D), lambda k, te, nt: (layer, te[k], 0, 0))],
            out_specs=pl.BlockSpec((FFN_TM, D), lambda k, te, nt: (k, 0)),
            scratch_shapes=[pltpu.VMEM((D, D_EXPERT), BF16), pltpu.VMEM((D, D_EXPERT), BF16),
                            pltpu.VMEM((D_EXPERT, D), BF16)],
        ),
        out_shape=jax.ShapeDtypeStruct((n_sorted, D), F32),
        compiler_params=_cparams(("arbitrary",)),
        name="ffn",
    )(tile_expert, n_tiles_used, xs, ws, sw['w1'], sw['w3'], sw['w2'])


def _combine_kernel(x1_ref, g1_ref, g2_ref, fn_ref, o_ref, *, final):
    x = x1_ref[...] + (g1_ref[...] + g2_ref[...])
    if final:
        ms = jnp.mean(x * x, axis=-1, keepdims=True)
        x = x * lax.rsqrt(ms + EPS) * fn_ref[...]
    o_ref[...] = x


def _combine(x1, g1, g2, final_norm, tm, final):
    n = x1.shape[0]
    row = pl.BlockSpec((tm, D), lambda i: (i, 0))
    return pl.pallas_call(
        functools.partial(_combine_kernel, final=final),
        grid=(n // tm,),
        in_specs=[row, row, row, _full((1, D))],
        out_specs=row,
        out_shape=jax.ShapeDtypeStruct((n, D), F32),
        compiler_params=_cparams(("parallel",)),
        name="combine",
    )(x1, g1, g2, final_norm)


def _moe_sparse(h2, route, counts, x1, sw, layer, final_norm, final):
    n = x1.shape[0]
    n_sorted = 2 * n + N_EXPERTS * FFN_TM
    e1, e2 = route[:, 0].astype(jnp.int32), route[:, 1].astype(jnp.int32)
    rank1, rank2 = route[:, 4].astype(jnp.int32), route[:, 5].astype(jnp.int32)
    cnt = counts[0, N_GROUPS:N_GROUPS + N_EXPERTS].astype(jnp.int32)
    tiles = (cnt + FFN_TM - 1) // FFN_TM
    tile_end = jnp.cumsum(tiles)
    seg_start = (tile_end - tiles) * FFN_TM
    dest1 = jnp.take(seg_start, e1) + rank1
    dest2 = jnp.take(seg_start, e2) + rank2
    tile_ids = jnp.arange(n_sorted // FFN_TM, dtype=jnp.int32)
    tile_expert = jnp.minimum(jnp.sum(tile_end[None, :] <= tile_ids[:, None], axis=1), N_EXPERTS - 1)
    wrow1 = jnp.broadcast_to(route[:, 2:3], (n, 128))
    wrow2 = jnp.broadcast_to(route[:, 3:4], (n, 128))
    xs, ws = _sc_dispatch(h2, wrow1, wrow2, dest1, dest2, n_sorted)
    y = _ffn(xs, ws, tile_expert.astype(jnp.int32), tile_end[-1:].astype(jnp.int32), sw, layer)
    g1, g2 = _sc_collect(y, dest1, dest2, n)
    return _combine(x1, g1, g2, final_norm, 512, final)


def _t5_buckets(dist):
    max_exact = T5_BUCKETS // 2
    large = max_exact + (np.log(np.maximum(dist, 1) / max_exact) / np.log(T5_MAX_DIST / max_exact)
                         * (T5_BUCKETS - max_exact)).astype(np.int32)
    large = np.minimum(large, T5_BUCKETS - 1)
    return np.where(dist < max_exact, dist, large).astype(np.int32)


def _bias_tables(t5, g, dil, window, t_sample):
    nk = window // dil + 1
    hs = slice(g * HPG, (g + 1) * HPG)
    bias = t5[_t5_buckets(np.arange(nk) * dil)][:, hs].T
    rev = bias[:, ::-1]
    neg = lambda *shape: jnp.full(shape, NEG, F32)

    vec = jnp.concatenate([rev, neg(HPG, CL)], axis=1)
    both = jnp.tile(vec, (1, CL + 1))[:, :CL * 2 * CL].reshape(HPG, CL, 2 * CL)

    rows = []
    for r in range(t_sample):
        shift = r // dil
        per_u = jnp.concatenate([neg(HPG, shift), rev[:, :nk - 1 - shift]], axis=1)
        on_phase = (np.arange(dil) == r % dil)[None, None, :]
        rows.append(jnp.where(on_phase, per_u[:, :, None], NEG).reshape(HPG, window))
    cache_t = jnp.stack(rows, axis=1)
    r = np.arange(t_sample)[:, None]
    c = np.arange(t_sample)[None, :]
    ok = ((r - c) % dil == 0) & (r >= c)
    new_t = jnp.where(jnp.asarray(ok)[None], bias[:, np.clip((r - c) // dil, 0, nk - 1)], NEG)
    return both, cache_t, new_t


def _layer_weights(l, norm1, conv_ssd_w, conv_ssd_b, ssd_dt_bias, ssd_a_log, ssd_d, ssd_norm_w,
                   conv_lru_w, conv_lru_b, lru_br, lru_bi, lru_lambda, b_gate, norm2,
                   w_router_group, b_router_group, w_router_expert, b_router_expert):
    b_all = jnp.concatenate([b_gate[l], jnp.zeros((PW - 3 * D,), F32)])[None]

    def pad128(v):
        return jnp.concatenate([v, jnp.zeros((128 - v.shape[0],), F32)])[None]

    return {
        'norm1': norm1[l][None], 'b_all': b_all,
        'conv_ssd_w': conv_ssd_w[l], 'conv_ssd_b': conv_ssd_b[l][None],
        'dt_bias': pad128(ssd_dt_bias[l]), 'a_log': pad128(ssd_a_log[l]), 'd_skip': pad128(ssd_d[l]),
        'ssd_norm_w': ssd_norm_w[l][None],
        'conv_lru_w': conv_lru_w[l], 'conv_lru_b': conv_lru_b[l][None],
        'lru_br': lru_br[l][None], 'lru_bi': lru_bi[l][None], 'lru_lambda': lru_lambda[l][None],
        'norm2': norm2[l][None],
        'w_router': jnp.concatenate([w_router_group[l], w_router_expert[l],
                                     jnp.zeros((D, ROUTE_LANES - N_GROUPS - N_EXPERTS), F32)],
                                    axis=1).astype(BF16),
        'b_router': pad128(jnp.concatenate([b_router_group[l], b_router_expert[l]])),
    }


def _front_pad(buf):
    return jnp.pad(buf, ((0, 0), (8 - (CONV_W - 1), 0), (0, 0)))


def _cols(P, nb, t, start, width):
    return P.reshape(nb, t, PW)[:, :, start:start + width]


def _kv_rows(P, nb, t, g, n_rows):
    k = _cols(P, nb, t, C_K + g * GW, GW)[:, t - n_rows:].reshape(nb, n_rows, HPG, HD)
    v = _cols(P, nb, t, C_V + g * GW, GW)[:, t - n_rows:].reshape(nb, n_rows, HPG, HD)
    return jnp.stack([k, v], axis=2)


def _layer(x, lw, sw, tables, layer, nb, t, rows, tm, tm_mix, tm_res, tm_moe, conv_ssd, st_ssd, conv_lru, st_lru,
           caches, final_norm, final):
    if caches is None:
        P, *qkv = _proj(x, lw['norm1'], sw['w_all'], lw['b_all'], layer, tm, nb, t)
    else:
        (P,) = _proj(x, lw['norm1'], sw['w_all'], lw['b_all'], layer, tm)
    y_ssd, h_ssd = _ssd(P, _front_pad(conv_ssd), st_ssd.reshape(nb, SSD_INNER, SSD_STATE), lw, nb, t, rows)
    y_lru, h_lru = _lru(P, _front_pad(conv_lru), st_lru.reshape(nb, 1, LRU_W), lw, sw, layer, nb, t, rows)
    attn = []
    for g, (window, dil) in enumerate(ATTN_GROUPS):
        both_t, cache_t, new_t = tables[g]
        if caches is None:
            attn.append(_attn_prompt(qkv[g], qkv[3 + g], qkv[6 + g], both_t, dil, nb, t))
        else:
            attn.append(_attn_sample(P, caches[g], cache_t, new_t, g, layer, nb, t, window))
    dils = tuple(dil if caches is None else 1 for _, dil in ATTN_GROUPS)
    mixed = _mix(P, y_ssd, y_lru, attn, dils, sw, layer, tm_mix)
    if caches is None:
        x1, h2, _, route, counts = _res(x, mixed, lw, sw, layer, tm_res, True)
        x2 = _moe_sparse(h2, route, counts, x1, sw, layer, final_norm, final)
    else:
        x1, h2, comb = _res(x, mixed, lw, sw, layer, tm_res, False)
        x2 = _moe(h2, comb, x1, sw, layer, final_norm, tm_moe, final)
    states = (_cols(P, nb, t, C_XBC, SSD_CONV_DIM)[:, t - 3:],
              h_ssd.reshape(nb, SSD_HEADS, SSD_HEAD_DIM, SSD_STATE),
              _cols(P, nb, t, C_XR, LRU_W)[:, t - 3:],
              h_lru.reshape(nb, LRU_W)) + tuple(
                  _kv_rows(P, nb, t, g, min(w, t)) for g, (w, _) in enumerate(ATTN_GROUPS))
    return x2, states


def kernel(x_prompt, x_sample, cache_conv_ssd, state_ssd, cache_conv_lru, state_lru, cache_kv_w128, cache_kv_w512, cache_kv_w2048, norm1, w_in, conv_ssd_w, conv_ssd_b, ssd_dt_bias, ssd_a_log, ssd_d, ssd_norm_w, conv_lru_w, conv_lru_b, lru_wr, lru_br, lru_wi, lru_bi, lru_lambda, t5_bias, w_br_ssd, w_br_lru, w_br_attn, w_gate, b_gate, w_o, norm2, w_router_group, b_router_group, w_router_expert, b_router_expert, w1, w3, w2, final_norm):
    bp, tp, _ = x_prompt.shape
    bs, ts, _ = x_sample.shape
    xp = x_prompt.reshape(bp * tp, D)
    xs = x_sample.reshape(bs * ts, D)
    fn = final_norm[None]
    tables = [_bias_tables(t5_bias, g, dil, window, ts) for g, (window, dil) in enumerate(ATTN_GROUPS)]
    caches = [c.reshape(-1, HD) for c in (cache_kv_w128, cache_kv_w512, cache_kv_w2048)]
    sw = {name: w.astype(BF16) for name, w in dict(
        lru_wr=lru_wr, lru_wi=lru_wi, w_br_ssd=w_br_ssd, w_br_lru=w_br_lru, w_br_attn=w_br_attn, w_o=w_o).items()}
    sw.update(w1=w1, w3=w3, w2=w2)
    sw['w_all'] = _prep_w(w_gate, w_in)
    outs_p, outs_s = [], []
    for l in range(DEPTH):
        lw = _layer_weights(l, norm1, conv_ssd_w, conv_ssd_b, ssd_dt_bias, ssd_a_log, ssd_d, ssd_norm_w,
                            conv_lru_w, conv_lru_b, lru_br, lru_bi, lru_lambda, b_gate, norm2,
                            w_router_group, b_router_group, w_router_expert, b_router_expert)
        final = l == DEPTH - 1
        xp, sp = _layer(xp, lw, sw, tables, l, bp, tp, CL, 1024, 256, 512, 512,
                        jnp.zeros((bp, CONV_W - 1, SSD_CONV_DIM), F32),
                        jnp.zeros((bp, SSD_HEADS, SSD_HEAD_DIM, SSD_STATE), F32),
                        jnp.zeros((bp, CONV_W - 1, LRU_W), F32), jnp.zeros((bp, LRU_W), F32),
                        None, fn, final)
        xs, ss = _layer(xs, lw, sw, tables, l, bs, ts, ts, bs * ts, bs * ts, bs * ts, bs * ts,
                        cache_conv_ssd[l], state_ssd[l], cache_conv_lru[l], state_lru[l],
                        caches, fn, final)
        outs_p.append(sp)
        outs_s.append(ss)

    def stk(outs, i):
        return jnp.stack([o[i] for o in outs], axis=0)

    return ((xp.reshape(bp, tp, D), xs.reshape(bs, ts, D))
            + tuple(stk(outs_p, i) for i in range(7)) + tuple(stk(outs_s, i) for i in range(7)))
```

```python
import functools

import numpy as np
import jax
import jax.numpy as jnp
from jax import lax
from jax.experimental import pallas as pl
from jax.experimental.pallas import tpu as pltpu
from jax.experimental.pallas import tpu_sc as plsc

F32 = jnp.float32
BF16 = jnp.bfloat16
EPS = 1e-6
NEG = -1e30

D = 2048
DEPTH = 2
PAST_LEN = 16384
CL = 128
CONV_W = 4
SSD_HEADS = 16
SSD_HEAD_DIM = 64
SSD_INNER = 1024
SSD_STATE = 128
SSD_CONV_DIM = 1536
LRU_W = 1024
LRU_BLOCKS = 8
LRU_C = 8.0
ATTN_GROUPS = ((128, 1), (512, 4), (2048, 16))
HPG = 4
HD = 128
GW = HPG * HD
T5_BUCKETS = 32
T5_MAX_DIST = 2048
N_GROUPS = 4
PER_GROUP = 4
N_EXPERTS = 16
D_EXPERT = 512

TILE = 512
C_GATE = 0
C_Z = 6144
C_XR = 7168
C_GR = 8192
C_XBC = 9216
C_Q = 10752
C_K = 12288
C_V = 13824
C_DT = 15360
PW = 15872
N_GATE_TILES = (3 * D) // TILE
ROUTE_LANES = 128
VMEM_LIMIT = 56 * 1024 * 1024


def _cparams(sem):
    return pltpu.CompilerParams(dimension_semantics=sem, vmem_limit_bytes=VMEM_LIMIT)


def _full(shape):
    nd = len(shape)
    return pl.BlockSpec(shape, lambda *_: (0,) * nd)


def _of_layer(shape, layer):
    nd = len(shape)
    return pl.BlockSpec((1,) + shape, lambda *_: (layer,) + (0,) * nd)


_W_IN_STARTS = (0, 512, 2576, 3088, 3600, 4112, 1024, 1536, 2048, 4624, 5136, 5648, 6160, 6672, 7184,
                7696, 8208, 8720, 2560)
DT_SHIFT = SSD_HEADS


def _prep_kernel(blk_ref, shift_ref, width_ref, wg_ref, wa_ref, wb_ref, o_ref):
    j = pl.program_id(1)

    @pl.when(j < N_GATE_TILES)
    def _():
        o_ref[0] = wg_ref[0].astype(BF16)

    @pl.when(j >= N_GATE_TILES)
    def _():
        a = wa_ref[0]
        shifted = jnp.concatenate([a[DT_SHIFT:, :], wb_ref[0]], axis=0)
        val = jnp.where(shift_ref[j] == 0, a, shifted)
        row = lax.broadcasted_iota(jnp.int32, val.shape, 0)
        o_ref[0] = jnp.where(row < width_ref[j], val, 0.0).T.astype(BF16)


def _prep_w(w_gate, w_in):
    starts = (0,) * N_GATE_TILES + _W_IN_STARTS
    blk = jnp.asarray([s // TILE for s in starts], jnp.int32)
    shift = jnp.asarray([s % TILE for s in starts], jnp.int32)
    assert all(s % TILE in (0, DT_SHIFT) for s in starts) and w_in.shape[2] % DT_SHIFT == 0
    width = jnp.asarray([TILE] * (len(starts) - 1) + [SSD_HEADS], jnp.int32)
    per_tile = TILE // DT_SHIFT
    w_in_t = jnp.swapaxes(w_in, 1, 2)
    return pl.pallas_call(
        _prep_kernel,
        grid_spec=pltpu.PrefetchScalarGridSpec(
            num_scalar_prefetch=3,
            grid=(DEPTH, PW // TILE),
            in_specs=[pl.BlockSpec((1, D, TILE), lambda l, j, b, s, w: (l, 0, jnp.minimum(j, N_GATE_TILES - 1))),
                      pl.BlockSpec((1, TILE, D), lambda l, j, b, s, w: (l, b[j], 0)),
                      pl.BlockSpec((1, DT_SHIFT, D), lambda l, j, b, s, w: (l, (b[j] + 1) * per_tile, 0))],
            out_specs=pl.BlockSpec((1, D, TILE), lambda l, j, b, s, w: (l, 0, j)),
        ),
        out_shape=jax.ShapeDtypeStruct((DEPTH, D, PW), BF16),
        compiler_params=_cparams(("parallel", "arbitrary")),
        name="prep_w",
    )(blk, shift, width, w_gate, w_in_t, w_in_t)


def _proj_kernel(x_ref, nw_ref, w_ref, b_ref, o_ref, *rest, tm, phase_major):
    if phase_major:
        qkv_refs, (h_ref, acc_ref, ph_ref) = rest[:9], rest[9:]
    else:
        h_ref, acc_ref = rest
    j = pl.program_id(1)

    @pl.when(j == 0)
    def _():
        x = x_ref[...]
        ms = jnp.mean(x * x, axis=-1, keepdims=True)
        h_ref[...] = (x * lax.rsqrt(ms + EPS) * nw_ref[...]).astype(BF16)
        acc_ref[...] = jnp.zeros(acc_ref.shape, F32)

    prev = acc_ref[...]
    o_ref[...] = jnp.where(j <= N_GATE_TILES, jax.nn.sigmoid(prev), prev)
    acc_ref[...] = jnp.dot(h_ref[...], w_ref[0], preferred_element_type=F32) + b_ref[...]

    if phase_major:
        for part in range(3):
            for g, (_, dil) in enumerate(ATTN_GROUPS):
                ref = qkv_refs[part * 3 + g]

                @pl.when(j - 1 == C_Q // TILE + part * 3 + g)
                def _(ref=ref, dil=dil):
                    if dil == 1:
                        ref[0, 0] = o_ref[...].astype(BF16)
                    else:
                        for c in range(TILE // 128):
                            ph_ref[c] = o_ref[:, c * 128:(c + 1) * 128]
                        for p in range(dil):
                            for c in range(TILE // 128):
                                ref[0, p, :, c * 128:(c + 1) * 128] = (
                                    ph_ref[c, pl.ds(p, tm // dil, stride=dil), :].astype(BF16))


def _proj(x, nw, w_all, b_all, layer, tm, nb=None, t=None):
    n = x.shape[0]
    phase_major = nb is not None
    nt = PW // TILE
    out_specs = [pl.BlockSpec((tm, TILE), lambda i, j: (i, jnp.maximum(j - 1, 0)))]
    out_shape = [jax.ShapeDtypeStruct((n, PW), F32)]
    scratch = [pltpu.VMEM((tm, D), BF16), pltpu.VMEM((tm, TILE), F32)]
    if phase_major:
        tpb = t // tm
        for _ in range(3):
            for _, dil in ATTN_GROUPS:
                out_specs.append(pl.BlockSpec((1, dil, tm // dil, GW), lambda i, j: (i // tpb, 0, i % tpb, 0)))
                out_shape.append(jax.ShapeDtypeStruct((nb, dil, t // dil, GW), BF16))
        scratch.append(pltpu.VMEM((TILE // 128, tm, 128), F32))
    return pl.pallas_call(
        functools.partial(_proj_kernel, tm=tm, phase_major=phase_major),
        grid=(n // tm, nt + 1),
        in_specs=[pl.BlockSpec((tm, D), lambda i, j: (i, 0)),
                  pl.BlockSpec((1, D), lambda i, j: (0, 0)),
                  pl.BlockSpec((1, D, TILE), lambda i, j: (layer, 0, jnp.minimum(j, nt - 1))),
                  pl.BlockSpec((1, TILE), lambda i, j: (0, jnp.minimum(j, nt - 1)))],
        out_specs=out_specs,
        out_shape=out_shape,
        scratch_shapes=scratch,
        compiler_params=_cparams(("parallel", "arbitrary")),
        name="proj",
    )(x, nw, w_all, b_all)


def _conv_step(x_ref, xp_ref, cw_ref, cb_ref, rows, out_rows):
    xp_ref[8:8 + rows, :] = _bf16_round(x_ref[...])
    cw = _bf16_round(cw_ref[...])
    acc = cw[0:1, :] * xp_ref[5:5 + out_rows, :]
    for j in range(1, CONV_W):
        acc = acc + cw[j:j + 1, :] * xp_ref[5 + j:5 + j + out_rows, :]
    tail = xp_ref[rows:rows + 8, :]
    xp_ref[0:8, :] = tail
    return acc + cb_ref[...]


def _bf16_round(x):
    return x.astype(BF16).astype(F32)


def _softplus(x):
    return jnp.maximum(x, 0.0) + jnp.log1p(jnp.exp(-jnp.abs(x)))


def _ssd_kernel(z_ref, xbc_ref, dt_ref, tail_ref, h0_ref, cw_ref, cb_ref, dtb_ref, alog_ref, dsk_ref, nw_ref,
                y_ref, hf_ref, xp_ref, act_ref, st_ref, ysc_ref, *, rows, n_chunks):
    c = pl.program_id(1)

    @pl.when(c == 0)
    def _():
        xp_ref[0:8, :] = _bf16_round(tail_ref[0])
        st_ref[...] = h0_ref[0]

    if rows < CL:
        xp_ref[8 + rows:, :] = jnp.zeros((CL - rows, SSD_CONV_DIM), F32)
    conv = _conv_step(xbc_ref, xp_ref, cw_ref, cb_ref, rows, CL)
    act_ref[...] = conv * jax.nn.sigmoid(conv)

    row = lax.broadcasted_iota(jnp.int32, (CL, 128), 0)
    lane = lax.broadcasted_iota(jnp.int32, (CL, 128), 1)
    raw = dt_ref[...]
    if rows < CL:
        raw = jnp.concatenate([raw, jnp.zeros((CL - rows, 128), F32)], axis=0)
    dt = _softplus(raw + dtb_ref[...])
    dt = jnp.where((lane < SSD_HEADS) & (row < rows), dt, 0.0)
    da = dt * (-jnp.exp(alog_ref[...]))
    acs = da
    d = 1
    while d < CL:
        acs = acs + jnp.where(row >= d, pltpu.roll(acs, d, 0), 0.0)
        d *= 2
    acs_t = acs.T
    last = acs[CL - 1:CL, :]
    e_acs = jnp.exp(acs)
    to_end = jnp.exp(last - acs)
    cdec = jnp.exp(last)
    causal = row >= lane
    lo_lane = lane < SSD_HEAD_DIM
    lo_row = row < SSD_HEAD_DIM
    dsk = dsk_ref[...]

    def pair_cols(arr, h):
        return jnp.where(lo_lane, arr[:, h:h + 1], arr[:, h + 1:h + 2])

    nt = (((1,), (1,)), ((), ()))
    for g in range(2):
        bm = act_ref[:, SSD_INNER + g * SSD_STATE:SSD_INNER + (g + 1) * SSD_STATE].astype(BF16)
        cm = act_ref[:, SSD_INNER + 256 + g * SSD_STATE:SSD_INNER + 256 + (g + 1) * SSD_STATE].astype(BF16)
        cb = lax.dot_general(cm, bm, nt, preferred_element_type=F32)
        for pp in range(4):
            h = g * 8 + 2 * pp
            sl = slice(h * SSD_HEAD_DIM, h * SSD_HEAD_DIM + 128)
            xs = act_ref[:, sl]
            xdt = xs * pair_cols(dt, h)
            xdt_b = xdt.astype(BF16)
            ys = []
            for hh in (h, h + 1):
                seg = acs[:, hh:hh + 1] - acs_t[hh:hh + 1, :]
                decay = jnp.exp(jnp.where(causal, seg, -jnp.inf))
                ys.append(jnp.dot((cb * decay).astype(BF16), xdt_b, preferred_element_type=F32))
            y_diag = jnp.where(lo_lane, ys[0], ys[1])
            st = st_ref[sl, :]
            y_off = lax.dot_general(cm, st.astype(BF16), nt, preferred_element_type=F32) * pair_cols(e_acs, h)
            d_pair = jnp.where(lo_lane, dsk[:, h:h + 1], dsk[:, h + 1:h + 2])
            ysc_ref[:, sl] = y_diag + y_off + d_pair * xs
            xdte_t = (xdt * pair_cols(to_end, h)).T.astype(BF16)
            s_new = jnp.dot(xdte_t, bm, preferred_element_type=F32)
            dec = jnp.where(lo_row, cdec[:, h:h + 1], cdec[:, h + 1:h + 2])
            st_ref[sl, :] = dec * st + s_new

    zz = z_ref[...]
    yg = ysc_ref[0:rows, :] * (zz * jax.nn.sigmoid(zz))
    gw = SSD_INNER // 2
    for g in range(2):
        part = yg[:, g * gw:(g + 1) * gw]
        ms = jnp.mean(part * part, axis=-1, keepdims=True)
        y_ref[:, g * gw:(g + 1) * gw] = (part * lax.rsqrt(ms + EPS)
                                         * nw_ref[:, g * gw:(g + 1) * gw]).astype(y_ref.dtype)

    @pl.when(c == n_chunks - 1)
    def _():
        hf_ref[0] = st_ref[...]


def _ssd(P, tail, h0, lw, nb, t, rows):
    nc = t // rows
    kern = functools.partial(_ssd_kernel, rows=rows, n_chunks=nc)
    return pl.pallas_call(
        kern,
        grid=(nb, nc),
        in_specs=[pl.BlockSpec((rows, SSD_INNER), lambda b, c: (b * nc + c, C_Z // SSD_INNER)),
                  pl.BlockSpec((rows, SSD_CONV_DIM), lambda b, c: (b * nc + c, C_XBC // SSD_CONV_DIM)),
                  pl.BlockSpec((rows, 128), lambda b, c: (b * nc + c, C_DT // 128)),
                  pl.BlockSpec((1, 8, SSD_CONV_DIM), lambda b, c: (b, 0, 0)),
                  pl.BlockSpec((1, SSD_INNER, SSD_STATE), lambda b, c: (b, 0, 0)),
                  _full((CONV_W, SSD_CONV_DIM)), _full((1, SSD_CONV_DIM)),
                  _full((1, 128)), _full((1, 128)), _full((1, 128)), _full((1, SSD_INNER))],
        out_specs=[pl.BlockSpec((rows, SSD_INNER), lambda b, c: (b * nc + c, 0)),
                   pl.BlockSpec((1, SSD_INNER, SSD_STATE), lambda b, c: (b, 0, 0))],
        out_shape=[jax.ShapeDtypeStruct((nb * t, SSD_INNER), BF16 if rows % 16 == 0 else F32),
                   jax.ShapeDtypeStruct((nb, SSD_INNER, SSD_STATE), F32)],
        scratch_shapes=[pltpu.VMEM((8 + CL, SSD_CONV_DIM), F32),
                        pltpu.VMEM((CL, SSD_CONV_DIM), F32),
                        pltpu.VMEM((SSD_INNER, SSD_STATE), F32),
                        pltpu.VMEM((CL, SSD_INNER), F32)],
        compiler_params=_cparams(("parallel", "arbitrary")),
        name="ssd",
    )(P, P, P, tail, h0, lw['conv_ssd_w'], lw['conv_ssd_b'], lw['dt_bias'], lw['a_log'], lw['d_skip'], lw['ssd_norm_w'])


def _lru_kernel(xr_ref, gr_ref, tail_ref, h0_ref, cw_ref, cb_ref, wr_ref, br_ref, wi_ref, bi_ref, lam_ref,
                y_ref, hl_ref, xp_ref, h_ref, *, rows, n_chunks):
    c = pl.program_id(1)

    @pl.when(c == 0)
    def _():
        xp_ref[0:8, :] = _bf16_round(tail_ref[0])
        h_ref[...] = h0_ref[0]

    x = _conv_step(xr_ref, xp_ref, cw_ref, cb_ref, rows, rows)
    xb = x.astype(BF16)
    rs, is_ = [], []
    for n in range(LRU_BLOCKS):
        blk = xb[:, n * 128:(n + 1) * 128]
        rs.append(jnp.dot(blk, wr_ref[0, n], preferred_element_type=F32))
        is_.append(jnp.dot(blk, wi_ref[0, n], preferred_element_type=F32))
    r_gate = jax.nn.sigmoid(jnp.concatenate(rs, axis=1) + br_ref[...])
    i_gate = jax.nn.sigmoid(jnp.concatenate(is_, axis=1) + bi_ref[...])
    log_a = -LRU_C * r_gate * _softplus(-lam_ref[...])
    a = jnp.exp(log_a)
    th = jnp.tanh(log_a)
    b = jnp.sqrt(-2.0 * th / (1.0 - th)) * (i_gate * x)
    in_group = lax.broadcasted_iota(jnp.int32, (rows, LRU_W), 0) % 8
    for d in (1, 2, 4):
        a_s = jnp.where(in_group >= d, pltpu.roll(a, d, 0), 1.0)
        b_s = jnp.where(in_group >= d, pltpu.roll(b, d, 0), 0.0)
        b = a * b_s + b
        a = a * a_s
    carry = h_ref[...]
    groups = []
    for g in range(rows // 8):
        h_g = b[8 * g:8 * g + 8, :] + a[8 * g:8 * g + 8, :] * carry
        groups.append(h_g)
        carry = h_g[7:8, :]
    h = jnp.concatenate(groups, axis=0) if len(groups) > 1 else groups[0]
    last = carry
    h_ref[...] = last
    y_ref[...] = (h * jax.nn.gelu(gr_ref[...])).astype(y_ref.dtype)

    @pl.when(c == n_chunks - 1)
    def _():
        hl_ref[0] = last


def _lru(P, tail, h0, lw, sw, layer, nb, t, rows):
    nc = t // rows
    kern = functools.partial(_lru_kernel, rows=rows, n_chunks=nc)
    y_dtype = BF16 if rows % 16 == 0 else F32
    return pl.pallas_call(
        kern,
        grid=(nb, nc),
        in_specs=[pl.BlockSpec((rows, LRU_W), lambda b, c: (b * nc + c, C_XR // LRU_W)),
                  pl.BlockSpec((rows, LRU_W), lambda b, c: (b * nc + c, C_GR // LRU_W)),
                  pl.BlockSpec((1, 8, LRU_W), lambda b, c: (b, 0, 0)),
                  pl.BlockSpec((1, 1, LRU_W), lambda b, c: (b, 0, 0)),
                  _full((CONV_W, LRU_W)), _full((1, LRU_W)),
                  _of_layer((LRU_BLOCKS, 128, 128), layer), _full((1, LRU_W)),
                  _of_layer((LRU_BLOCKS, 128, 128), layer), _full((1, LRU_W)), _full((1, LRU_W))],
        out_specs=[pl.BlockSpec((rows, LRU_W), lambda b, c: (b * nc + c, 0)),
                   pl.BlockSpec((1, 1, LRU_W), lambda b, c: (b, 0, 0))],
        out_shape=[jax.ShapeDtypeStruct((nb * t, LRU_W), y_dtype),
                   jax.ShapeDtypeStruct((nb, 1, LRU_W), F32)],
        scratch_shapes=[pltpu.VMEM((8 + rows, LRU_W), F32), pltpu.VMEM((1, LRU_W), F32)],
        compiler_params=_cparams(("parallel", "arbitrary")),
        name="lru",
    )(P, P, tail, h0, lw['conv_lru_w'], lw['conv_lru_b'], sw['lru_wr'], lw['lru_br'], sw['lru_wi'], lw['lru_bi'],
      lw['lru_lambda'])


def _attn_kernel(q_ref, kp_ref, vp_ref, kc_ref, vc_ref, bias_ref, o_ref, lse_ref, *, sub, phases):
    scale = HD ** -0.5
    nt = (((1,), (1,)), ((), ()))
    lane = lax.broadcasted_iota(jnp.int32, (CL, 128), 1)
    key = lax.broadcasted_iota(jnp.int32, (CL, 2 * CL), 1)
    first_ok = (pl.program_id(2) > 0) | (key >= CL)
    for z in range(phases):
        for s in range(sub):
            rows = slice(s * CL, (s + 1) * CL)
            both = slice((s - 1) * CL, (s + 1) * CL)
            lse_all = jnp.zeros((CL, 128), F32)
            for h in range(HPG):
                sl = slice(h * HD, (h + 1) * HD)
                q = q_ref[0, z, rows, sl]
                if s == 0:
                    kk = jnp.concatenate([kp_ref[0, z, :, sl], kc_ref[0, z, rows, sl]], axis=0)
                    vv = jnp.concatenate([vp_ref[0, z, :, sl], vc_ref[0, z, rows, sl]], axis=0)
                else:
                    kk, vv = kc_ref[0, z, both, sl], vc_ref[0, z, both, sl]
                sc = lax.dot_general(q, kk, nt, preferred_element_type=F32) * scale + bias_ref[h]
                if s == 0:
                    sc = jnp.where(first_ok, sc, NEG)
                m = jnp.max(sc, axis=-1, keepdims=True)
                p = jnp.exp(sc - m)
                l = jnp.sum(p, axis=-1, keepdims=True)
                o_ref[rows, z * GW + h * HD:z * GW + (h + 1) * HD] = jnp.dot(
                    (p * (1.0 / l)).astype(BF16), vv, preferred_element_type=F32)
                lse_all = jnp.where(lane == h, m + jnp.log(l), lse_all)
            lse_ref[rows, z * 128:(z + 1) * 128] = lse_all


def _attn_prompt(q, k, v, bias, dil, nb, t):
    n = nb * t
    blocks = 8
    sub = min(blocks, t // dil // CL)
    phases = min(blocks // sub, dil)
    nstep = t // dil // (sub * CL)
    cur = pl.BlockSpec((1, phases, sub * CL, GW), lambda b, p, i: (b, p, i, 0))
    prev = pl.BlockSpec((1, phases, CL, GW), lambda b, p, i: (b, p, jnp.maximum(i * sub - 1, 0), 0))
    return pl.pallas_call(
        functools.partial(_attn_kernel, sub=sub, phases=phases),
        grid=(nb, dil // phases, nstep),
        in_specs=[cur, prev, prev, cur, cur, _full((HPG, CL, 2 * CL))],
        out_specs=[pl.BlockSpec((sub * CL, phases * GW), lambda b, p, i: (b * nstep + i, p)),
                   pl.BlockSpec((sub * CL, phases * 128), lambda b, p, i: (b * nstep + i, p))],
        out_shape=[jax.ShapeDtypeStruct((n // dil, dil * GW), F32),
                   jax.ShapeDtypeStruct((n // dil, dil * 128), F32)],
        compiler_params=_cparams(("parallel", "parallel", "arbitrary")),
        name=f"attn_prompt_d{dil}",
    )(q, k, v, k, v, bias)


def _attn_sample_kernel(q_ref, kv_ref, kb_ref, vb_ref, ba_ref, bb_ref, o_ref, lse_ref):
    scale = HD ** -0.5
    nt = (((1,), (1,)), ((), ()))
    rows = o_ref.shape[0]
    window = kv_ref.shape[0] // (2 * HPG)
    lane = lax.broadcasted_iota(jnp.int32, (rows, 128), 1)
    lse_all = jnp.zeros((rows, 128), F32)
    for h in range(HPG):
        sl = slice(h * HD, (h + 1) * HD)
        q = q_ref[:, sl].astype(BF16)
        k_cache = kv_ref[pl.ds(h, window, stride=2 * HPG), :].astype(BF16)
        v_cache = kv_ref[pl.ds(HPG + h, window, stride=2 * HPG), :].astype(BF16)
        sa = lax.dot_general(q, k_cache, nt, preferred_element_type=F32) * scale + ba_ref[h]
        sb = lax.dot_general(q, kb_ref[:, sl].astype(BF16), nt, preferred_element_type=F32) * scale + bb_ref[h]
        m = jnp.maximum(jnp.max(sa, axis=-1, keepdims=True), jnp.max(sb, axis=-1, keepdims=True))
        pa = jnp.exp(sa - m)
        pb = jnp.exp(sb - m)
        l = jnp.sum(pa, axis=-1, keepdims=True) + jnp.sum(pb, axis=-1, keepdims=True)
        inv = 1.0 / l
        o_ref[:, sl] = (jnp.dot((pa * inv).astype(BF16), v_cache, preferred_element_type=F32)
                        + jnp.dot((pb * inv).astype(BF16), vb_ref[:, sl].astype(BF16), preferred_element_type=F32))
        lse_all = jnp.where(lane == h, m + jnp.log(l), lse_all)
    lse_ref[...] = lse_all


def _attn_sample(Ps, cache_rows, bias_a, bias_b, g, layer, nb, t, window):
    tq, tk, tv = C_Q // GW + g, C_K // GW + g, C_V // GW + g
    out_spec = pl.BlockSpec((t, GW), lambda b: (b, 0))
    return pl.pallas_call(
        _attn_sample_kernel,
        grid=(nb,),
        in_specs=[pl.BlockSpec((t, GW), lambda b: (b, tq)),
                  pl.BlockSpec((window * 2 * HPG, HD), lambda b: (layer * nb + b, 0)),
                  pl.BlockSpec((t, GW), lambda b: (b, tk)),
                  pl.BlockSpec((t, GW), lambda b: (b, tv)),
                  _full((HPG, t, window)), _full((HPG, t, t))],
        out_specs=[out_spec, pl.BlockSpec((t, 128), lambda b: (b, 0))],
        out_shape=[jax.ShapeDtypeStruct((nb * t, GW), F32), jax.ShapeDtypeStruct((nb * t, 128), F32)],
        compiler_params=_cparams(("parallel",)),
        name=f"attn_sample_w{window}",
    )(Ps, cache_rows, Ps, Ps, bias_a, bias_b)


def _mix_kernel(gs_ref, gl_ref, ga_ref, ys_ref, yl_ref, o0_ref, o1_ref, o2_ref, l0_ref, l1_ref, l2_ref,
                wbs_ref, wbl_ref, wba_ref, out_ref, *scratch, dils):
    tm = out_ref.shape[0]
    o_heads, lses = [], []
    scratch = list(scratch)
    for o_ref, l_ref, dil in zip((o0_ref, o1_ref, o2_ref), (l0_ref, l1_ref, l2_ref), dils):
        if dil == 1:
            o_heads.append([o_ref[:, h * HD:(h + 1) * HD] for h in range(HPG)])
            lses.append(l_ref[...])
            continue
        o_scr, l_scr = scratch.pop(0), scratch.pop(0)
        for p in range(dil):
            rows = pl.ds(p, tm // dil, stride=dil)
            for h in range(HPG):
                o_scr[h, rows, :] = o_ref[:, p * GW + h * HD:p * GW + (h + 1) * HD]
            l_scr[rows, :] = l_ref[:, p * 128:(p + 1) * 128]
        o_heads.append([o_scr[h] for h in range(HPG)])
        lses.append(l_scr[...])
    l0, l1, l2 = lses
    m = jnp.maximum(jnp.maximum(l0, l1), l2)
    e0, e1, e2 = jnp.exp(l0 - m), jnp.exp(l1 - m), jnp.exp(l2 - m)
    den = e0 + e1 + e2
    w0, w1, w2 = e0 / den, e1 / den, e2 / den
    heads = []
    for h in range(HPG):
        per_head = lambda w: jnp.broadcast_to(w[:, h:h + 1], (tm, HD))
        heads.append(o_heads[0][h] * per_head(w0) + o_heads[1][h] * per_head(w1) + o_heads[2][h] * per_head(w2))
    ya = jnp.concatenate(heads, axis=1)
    mixed = (gs_ref[...] * jnp.dot(ys_ref[...].astype(BF16), wbs_ref[0], preferred_element_type=F32)
             + gl_ref[...] * jnp.dot(yl_ref[...].astype(BF16), wbl_ref[0], preferred_element_type=F32)
             + ga_ref[...] * jnp.dot(ya.astype(BF16), wba_ref[0], preferred_element_type=F32))
    out_ref[...] = mixed.astype(BF16)


def _mix(P, y_ssd, y_lru, attn, dils, sw, layer, tm):
    n = P.shape[0]
    row = lambda w: pl.BlockSpec((tm, w), lambda i: (i, 0))
    phased = lambda w, d: pl.BlockSpec((tm // d, d * w), lambda i: (i, 0))
    (o0, s0), (o1, s1), (o2, s2) = attn
    scratch = []
    for d in dils:
        if d > 1:
            scratch += [pltpu.VMEM((HPG, tm, HD), F32), pltpu.VMEM((tm, 128), F32)]
    return pl.pallas_call(
        functools.partial(_mix_kernel, dils=dils),
        grid=(n // tm,),
        in_specs=[pl.BlockSpec((tm, D), lambda i: (i, 0)), pl.BlockSpec((tm, D), lambda i: (i, 1)),
                  pl.BlockSpec((tm, D), lambda i: (i, 2)),
                  row(SSD_INNER), row(LRU_W)] + [phased(GW, d) for d in dils] + [phased(128, d) for d in dils] + [
                  _of_layer((SSD_INNER, D), layer), _of_layer((LRU_W, D), layer), _of_layer((GW, D), layer)],
        out_specs=row(D),
        out_shape=jax.ShapeDtypeStruct((n, D), BF16),
        scratch_shapes=scratch,
        compiler_params=_cparams(("parallel",)),
        name="mix",
    )(P, P, P, y_ssd, y_lru, o0, o1, o2, s0, s1, s2, sw['w_br_ssd'], sw['w_br_lru'], sw['w_br_attn'])


def _res_kernel(x_ref, mixed_ref, wo_ref, n2_ref, wr_ref, br_ref, x1_ref, h2_ref, comb_ref, *rest, dispatch):
    x1 = x_ref[...] + jnp.dot(mixed_ref[...], wo_ref[0], preferred_element_type=F32)
    x1_ref[...] = x1
    ms = jnp.mean(x1 * x1, axis=-1, keepdims=True)
    h2 = x1 * lax.rsqrt(ms + EPS) * n2_ref[...]
    h2b = h2.astype(BF16)
    if dispatch:
        lo = pltpu.bitcast(h2b[:, :HALF].astype(F32), jnp.uint32)
        hi = pltpu.bitcast(h2b[:, HALF:].astype(F32), jnp.uint32)
        h2_ref[...] = pltpu.bitcast(lax.shift_right_logical(lo, jnp.uint32(16)) | hi, jnp.int32)
    else:
        h2_ref[...] = h2b
    logits = jnp.dot(h2b, wr_ref[...], preferred_element_type=F32) + br_ref[...]
    lane = lax.broadcasted_iota(jnp.int32, logits.shape, 1).astype(F32)
    big = float(ROUTE_LANES)

    def first_max(vals, ok):
        v = jnp.where(ok, vals, NEG)
        top = jnp.max(v, axis=-1, keepdims=True)
        idx = jnp.min(jnp.where(ok & (v == top), lane, big), axis=-1, keepdims=True)
        return top, idx

    is_g = lane < N_GROUPS
    gmax, gsel = first_max(logits, is_g)
    gp = 1.0 / jnp.sum(jnp.where(is_g, jnp.exp(logits - gmax), 0.0), axis=-1, keepdims=True)
    lo = N_GROUPS + PER_GROUP * gsel
    is_e = (lane >= lo) & (lane < lo + PER_GROUP)
    t1, i1 = first_max(logits, is_e)
    t2, i2 = first_max(logits, is_e & (lane != i1))
    e2 = jnp.exp(t2 - t1)
    w1 = gp / (1.0 + e2)
    w2 = gp * e2 / (1.0 + e2)
    comb_ref[...] = jnp.where(lane == i1, w1, 0.0) + jnp.where(lane == i2, w2, 0.0)

    if dispatch:
        route_ref, cnt_ref, carry_ref = rest

        @pl.when(pl.program_id(0) == 0)
        def _():
            carry_ref[...] = jnp.zeros(carry_ref.shape, F32)

        tm = x1.shape[0]
        onehot = jnp.where((lane == i1) | (lane == i2), 1.0, 0.0)
        r = lax.broadcasted_iota(jnp.int32, (tm, tm), 0)
        c = lax.broadcasted_iota(jnp.int32, (tm, tm), 1)
        earlier = jnp.where(r > c, 1.0, 0.0).astype(BF16)
        before = jnp.dot(earlier, onehot.astype(BF16), preferred_element_type=F32) + carry_ref[...]
        rank1 = jnp.sum(jnp.where(lane == i1, before, 0.0), axis=-1, keepdims=True)
        rank2 = jnp.sum(jnp.where(lane == i2, before, 0.0), axis=-1, keepdims=True)
        carry_ref[...] += jnp.sum(onehot, axis=0, keepdims=True)
        cnt_ref[...] = carry_ref[...]
        fields = (i1 - N_GROUPS, i2 - N_GROUPS, w1, w2, rank1, rank2)
        route = jnp.zeros(logits.shape, F32)
        for k, val in enumerate(fields):
            route = jnp.where(lane == k, val, route)
        route_ref[...] = route


def _res(x, mixed, lw, sw, layer, tm, dispatch):
    n = x.shape[0]
    h2_cols, h2_dtype = (HALF, jnp.int32) if dispatch else (D, BF16)
    out_specs = [pl.BlockSpec((tm, D), lambda i: (i, 0)), pl.BlockSpec((tm, h2_cols), lambda i: (i, 0)),
                 pl.BlockSpec((tm, ROUTE_LANES), lambda i: (i, 0))]
    out_shape = [jax.ShapeDtypeStruct((n, D), F32), jax.ShapeDtypeStruct((n, h2_cols), h2_dtype),
                 jax.ShapeDtypeStruct((n, ROUTE_LANES), F32)]
    scratch = []
    if dispatch:
        out_specs += [pl.BlockSpec((tm, ROUTE_LANES), lambda i: (i, 0)), _full((1, ROUTE_LANES))]
        out_shape += [jax.ShapeDtypeStruct((n, ROUTE_LANES), F32), jax.ShapeDtypeStruct((1, ROUTE_LANES), F32)]
        scratch = [pltpu.VMEM((1, ROUTE_LANES), F32)]
    return pl.pallas_call(
        functools.partial(_res_kernel, dispatch=dispatch),
        grid=(n // tm,),
        in_specs=[pl.BlockSpec((tm, D), lambda i: (i, 0)), pl.BlockSpec((tm, D), lambda i: (i, 0)),
                  _of_layer((D, D), layer), _full((1, D)), _full((D, ROUTE_LANES)), _full((1, ROUTE_LANES))],
        out_specs=out_specs,
        out_shape=out_shape,
        scratch_shapes=scratch,
        compiler_params=_cparams(("arbitrary",)),
        name="res_router",
    )(x, mixed, sw['w_o'], lw['norm2'], lw['w_router'], lw['b_router'])


def _moe_kernel(h2_ref, comb_ref, x1_ref, w1_ref, w3_ref, w2_ref, fn_ref, o_ref, *, final):
    e = pl.program_id(1)

    @pl.when(e == 0)
    def _():
        o_ref[...] = x1_ref[...]

    h = h2_ref[...]
    a = jnp.dot(h, w1_ref[0, 0].astype(BF16), preferred_element_type=F32)
    b = jnp.dot(h, w3_ref[0, 0].astype(BF16), preferred_element_type=F32)
    comb = comb_ref[...]
    lane = lax.broadcasted_iota(jnp.int32, comb.shape, 1)
    w = jnp.sum(jnp.where(lane == e + N_GROUPS, comb, 0.0), axis=-1, keepdims=True)
    act = (a * jax.nn.sigmoid(a)) * b * w
    o_ref[...] += jnp.dot(act.astype(BF16), w2_ref[0, 0].astype(BF16), preferred_element_type=F32)

    if final:
        @pl.when(e == N_EXPERTS - 1)
        def _():
            x = o_ref[...]
            ms = jnp.mean(x * x, axis=-1, keepdims=True)
            o_ref[...] = x * lax.rsqrt(ms + EPS) * fn_ref[...]


def _moe(h2, comb, x1, sw, layer, final_norm, tm, final):
    n = x1.shape[0]
    return pl.pallas_call(
        functools.partial(_moe_kernel, final=final),
        grid=(n // tm, N_EXPERTS),
        in_specs=[pl.BlockSpec((tm, D), lambda i, e: (i, 0)),
                  pl.BlockSpec((tm, ROUTE_LANES), lambda i, e: (i, 0)),
                  pl.BlockSpec((tm, D), lambda i, e: (i, 0)),
                  pl.BlockSpec((1, 1, D, D_EXPERT), lambda i, e: (layer, e, 0, 0)),
                  pl.BlockSpec((1, 1, D, D_EXPERT), lambda i, e: (layer, e, 0, 0)),
                  pl.BlockSpec((1, 1, D_EXPERT, D), lambda i, e: (layer, e, 0, 0)),
                  pl.BlockSpec((1, D), lambda i, e: (0, 0))],
        out_specs=pl.BlockSpec((tm, D), lambda i, e: (i, 0)),
        out_shape=jax.ShapeDtypeStruct((n, D), F32),
        compiler_params=_cparams(("parallel", "arbitrary")),
        name="moe",
    )(h2, comb, x1, sw['w1'], sw['w3'], sw['w2'], final_norm)


FFN_TM = 512
SC_CORES = 2
SC_SUBCORES = 16
SC_WORKERS = SC_CORES * SC_SUBCORES
SC_CHUNK = 32
HALF = D // 2


def _sc_mesh():
    return plsc.VectorSubcoreMesh(core_axis_name="c", subcore_axis_name="s", num_cores=SC_CORES,
                                  num_subcores=SC_SUBCORES)


def _sc_dispatch(x, wrow1, wrow2, dest1, dest2, n_sorted):
    n = x.shape[0]
    per_w = n // SC_WORKERS

    @functools.partial(
        pl.kernel, mesh=_sc_mesh(),
        out_type=(jax.ShapeDtypeStruct((n_sorted, HALF), jnp.int32), jax.ShapeDtypeStruct((n_sorted, 128), F32)),
        scratch_types=[pltpu.VMEM((SC_CHUNK,), jnp.int32), pltpu.VMEM((SC_CHUNK,), jnp.int32),
                       pltpu.VMEM((SC_CHUNK, HALF), jnp.int32), pltpu.VMEM((SC_CHUNK, 128), F32)],
    )
    def k(x_hbm, w1_hbm, w2_hbm, d1_hbm, d2_hbm, out_hbm, wout_hbm, i1_v, i2_v, rows_v, wrows_v):
        base = (lax.axis_index("s") * SC_CORES + lax.axis_index("c")) * per_w

        @pl.loop(0, per_w // SC_CHUNK)
        def _(j):
            off = base + j * SC_CHUNK
            pltpu.sync_copy(d1_hbm.at[pl.ds(off, SC_CHUNK)], i1_v)
            pltpu.sync_copy(d2_hbm.at[pl.ds(off, SC_CHUNK)], i2_v)
            pltpu.sync_copy(x_hbm.at[pl.ds(off, SC_CHUNK)], rows_v)
            pltpu.sync_copy(rows_v, out_hbm.at[i1_v])
            pltpu.sync_copy(rows_v, out_hbm.at[i2_v])
            pltpu.sync_copy(w1_hbm.at[pl.ds(off, SC_CHUNK)], wrows_v)
            pltpu.sync_copy(wrows_v, wout_hbm.at[i1_v])
            pltpu.sync_copy(w2_hbm.at[pl.ds(off, SC_CHUNK)], wrows_v)
            pltpu.sync_copy(wrows_v, wout_hbm.at[i2_v])

    return k(x, wrow1, wrow2, dest1, dest2)


def _sc_collect(y, dest1, dest2, n):
    per_w = n // SC_WORKERS

    @functools.partial(
        pl.kernel, mesh=_sc_mesh(),
        out_type=(jax.ShapeDtypeStruct((n, D), F32), jax.ShapeDtypeStruct((n, D), F32)),
        scratch_types=[pltpu.VMEM((SC_CHUNK,), jnp.int32), pltpu.VMEM((SC_CHUNK, D), F32)],
    )
    def k(y_hbm, d1_hbm, d2_hbm, g1_hbm, g2_hbm, idx_v, rows_v):
        base = (lax.axis_index("s") * SC_CORES + lax.axis_index("c")) * per_w

        @pl.loop(0, per_w // SC_CHUNK)
        def _(j):
            off = base + j * SC_CHUNK
            for d_hbm, g_hbm in ((d1_hbm, g1_hbm), (d2_hbm, g2_hbm)):
                pltpu.sync_copy(d_hbm.at[pl.ds(off, SC_CHUNK)], idx_v)
                pltpu.sync_copy(y_hbm.at[idx_v], rows_v)
                pltpu.sync_copy(rows_v, g_hbm.at[pl.ds(off, SC_CHUNK)])

    return k(y, dest1, dest2)


def _ffn_kernel(te_ref, nt_ref, xs_ref, ws_ref, w1_ref, w3_ref, w2_ref, y_ref, w1b_ref, w3b_ref, w2b_ref):
    k = pl.program_id(0)

    @pl.when(k < nt_ref[0])
    def _():
        @pl.when((k == 0) | (te_ref[k] != te_ref[jnp.maximum(k - 1, 0)]))
        def _():
            w1b_ref[...] = w1_ref[0, 0].astype(BF16)
            w3b_ref[...] = w3_ref[0, 0].astype(BF16)
            w2b_ref[...] = w2_ref[0, 0].astype(BF16)

        words = pltpu.bitcast(xs_ref[...], jnp.uint32)
        lo = pltpu.bitcast(lax.shift_left(words, jnp.uint32(16)), F32)
        hi = pltpu.bitcast(words & jnp.uint32(0xFFFF0000), F32)
        xs = jnp.concatenate([lo, hi], axis=1).astype(BF16)
        a = jnp.dot(xs, w1b_ref[...], preferred_element_type=F32)
        b = jnp.dot(xs, w3b_ref[...], preferred_element_type=F32)
        act = (a * jax.nn.sigmoid(a)) * b * ws_ref[:, 0:1]
        y_ref[...] = jnp.dot(act.astype(BF16), w2b_ref[...], preferred_element_type=F32)


def _ffn(xs, ws, tile_expert, n_tiles_used, sw, layer):
    n_sorted = xs.shape[0]
    w_in_spec = pl.BlockSpec((1, 1, D, D_EXPERT), lambda k, te, nt: (layer, te[k], 0, 0))
    return pl.pallas_call(
        _ffn_kernel,
        grid_spec=pltpu.PrefetchScalarGridSpec(
            num_scalar_prefetch=2,
            grid=(n_sorted // FFN_TM,),
            in_specs=[pl.BlockSpec((FFN_TM, HALF), lambda k, te, nt: (k, 0)),
                      pl.BlockSpec((FFN_TM, 128), lambda k, te, nt: (k, 0)), w_in_spec, w_in_spec,
                      pl.BlockSpec((1, 1, D_EXPERT, D), lambda k, te, nt: (layer, te[k], 0, 0))],
            out_specs=pl.BlockSpec((FFN_TM, D), lambda k, te, nt: (k, 0)),
            scratch_shapes=[pltpu.VMEM((D, D_EXPERT), BF16), pltpu.VMEM((D, D_EXPERT), BF16),
                            pltpu.VMEM((D_EXPERT, D), BF16)],
        ),
        out_shape=jax.ShapeDtypeStruct((n_sorted, D), F32),
        compiler_params=_cparams(("arbitrary",)),
        name="ffn",
    )(tile_expert, n_tiles_used, xs, ws, sw['w1'], sw['w3'], sw['w2'])


def _combine_kernel(x1_ref, g1_ref, g2_ref, fn_ref, o_ref, *, final):
    x = x1_ref[...] + (g1_ref[...] + g2_ref[...])
    if final:
        ms = jnp.mean(x * x, axis=-1, keepdims=True)
        x = x * lax.rsqrt(ms + EPS) * fn_ref[...]
    o_ref[...] = x


def _combine(x1, g1, g2, final_norm, tm, final):
    n = x1.shape[0]
    row = pl.BlockSpec((tm, D), lambda i: (i, 0))
    return pl.pallas_call(
        functools.partial(_combine_kernel, final=final),
        grid=(n // tm,),
        in_specs=[row, row, row, _full((1, D))],
        out_specs=row,
        out_shape=jax.ShapeDtypeStruct((n, D), F32),
        compiler_params=_cparams(("parallel",)),
        name="combine",
    )(x1, g1, g2, final_norm)


def _after(values, others):
    values, _ = lax.optimization_barrier((values, others))
    return values


def _moe_sparse(h2, route, counts, x1, sw, layer, final_norm, final, during_dispatch, during_collect):
    n = x1.shape[0]
    n_sorted = 2 * n + N_EXPERTS * FFN_TM
    e1, e2 = route[:, 0].astype(jnp.int32), route[:, 1].astype(jnp.int32)
    rank1, rank2 = route[:, 4].astype(jnp.int32), route[:, 5].astype(jnp.int32)
    cnt = counts[0, N_GROUPS:N_GROUPS + N_EXPERTS].astype(jnp.int32)
    tiles = (cnt + FFN_TM - 1) // FFN_TM
    tile_end = jnp.cumsum(tiles)
    seg_start = (tile_end - tiles) * FFN_TM
    dest1 = jnp.take(seg_start, e1) + rank1
    dest2 = jnp.take(seg_start, e2) + rank2
    tile_ids = jnp.arange(n_sorted // FFN_TM, dtype=jnp.int32)
    tile_expert = jnp.minimum(jnp.sum(tile_end[None, :] <= tile_ids[:, None], axis=1), N_EXPERTS - 1)
    wrow1 = jnp.broadcast_to(route[:, 2:3], (n, 128))
    wrow2 = jnp.broadcast_to(route[:, 3:4], (n, 128))
    xs, ws = _after(_sc_dispatch(h2, wrow1, wrow2, dest1, dest2, n_sorted), during_dispatch())
    y = _ffn(xs, ws, tile_expert.astype(jnp.int32), tile_end[-1:].astype(jnp.int32), sw, layer)
    g1, g2 = _after(_sc_collect(y, dest1, dest2, n), during_collect())
    return _combine(x1, g1, g2, final_norm, 512, final)


def _t5_buckets(dist):
    max_exact = T5_BUCKETS // 2
    large = max_exact + (np.log(np.maximum(dist, 1) / max_exact) / np.log(T5_MAX_DIST / max_exact)
                         * (T5_BUCKETS - max_exact)).astype(np.int32)
    large = np.minimum(large, T5_BUCKETS - 1)
    return np.where(dist < max_exact, dist, large).astype(np.int32)


def _bias_tables(t5, g, dil, window, t_sample):
    nk = window // dil + 1
    hs = slice(g * HPG, (g + 1) * HPG)
    bias = t5[_t5_buckets(np.arange(nk) * dil)][:, hs].T
    rev = bias[:, ::-1]
    neg = lambda *shape: jnp.full(shape, NEG, F32)

    vec = jnp.concatenate([rev, neg(HPG, CL)], axis=1)
    both = jnp.tile(vec, (1, CL + 1))[:, :CL * 2 * CL].reshape(HPG, CL, 2 * CL)

    rows = []
    for r in range(t_sample):
        shift = r // dil
        per_u = jnp.concatenate([neg(HPG, shift), rev[:, :nk - 1 - shift]], axis=1)
        on_phase = (np.arange(dil) == r % dil)[None, None, :]
        rows.append(jnp.where(on_phase, per_u[:, :, None], NEG).reshape(HPG, window))
    cache_t = jnp.stack(rows, axis=1)
    r = np.arange(t_sample)[:, None]
    c = np.arange(t_sample)[None, :]
    ok = ((r - c) % dil == 0) & (r >= c)
    new_t = jnp.where(jnp.asarray(ok)[None], bias[:, np.clip((r - c) // dil, 0, nk - 1)], NEG)
    return both, cache_t, new_t


def _layer_weights(l, norm1, conv_ssd_w, conv_ssd_b, ssd_dt_bias, ssd_a_log, ssd_d, ssd_norm_w,
                   conv_lru_w, conv_lru_b, lru_br, lru_bi, lru_lambda, b_gate, norm2,
                   w_router_group, b_router_group, w_router_expert, b_router_expert):
    b_all = jnp.concatenate([b_gate[l], jnp.zeros((PW - 3 * D,), F32)])[None]

    def pad128(v):
        return jnp.concatenate([v, jnp.zeros((128 - v.shape[0],), F32)])[None]

    return {
        'norm1': norm1[l][None], 'b_all': b_all,
        'conv_ssd_w': conv_ssd_w[l], 'conv_ssd_b': conv_ssd_b[l][None],
        'dt_bias': pad128(ssd_dt_bias[l]), 'a_log': pad128(ssd_a_log[l]), 'd_skip': pad128(ssd_d[l]),
        'ssd_norm_w': ssd_norm_w[l][None],
        'conv_lru_w': conv_lru_w[l], 'conv_lru_b': conv_lru_b[l][None],
        'lru_br': lru_br[l][None], 'lru_bi': lru_bi[l][None], 'lru_lambda': lru_lambda[l][None],
        'norm2': norm2[l][None],
        'w_router': jnp.concatenate([w_router_group[l], w_router_expert[l],
                                     jnp.zeros((D, ROUTE_LANES - N_GROUPS - N_EXPERTS), F32)],
                                    axis=1).astype(BF16),
        'b_router': pad128(jnp.concatenate([b_router_group[l], b_router_expert[l]])),
    }


def _front_pad(buf):
    return jnp.pad(buf, ((0, 0), (8 - (CONV_W - 1), 0), (0, 0)))


def _cols(P, nb, t, start, width):
    return P.reshape(nb, t, PW)[:, :, start:start + width]


def _kv_rows(P, nb, t, g, n_rows):
    k = _cols(P, nb, t, C_K + g * GW, GW)[:, t - n_rows:].reshape(nb, n_rows, HPG, HD)
    v = _cols(P, nb, t, C_V + g * GW, GW)[:, t - n_rows:].reshape(nb, n_rows, HPG, HD)
    return jnp.stack([k, v], axis=2)


def _layer_front(x, lw, sw, tables, layer, nb, t, rows, tm, tm_mix, tm_res, conv_ssd, st_ssd, conv_lru, st_lru,
                 caches):
    if caches is None:
        P, *qkv = _proj(x, lw['norm1'], sw['w_all'], lw['b_all'], layer, tm, nb, t)
    else:
        (P,) = _proj(x, lw['norm1'], sw['w_all'], lw['b_all'], layer, tm)
    y_ssd, h_ssd = _ssd(P, _front_pad(conv_ssd), st_ssd.reshape(nb, SSD_INNER, SSD_STATE), lw, nb, t, rows)
    y_lru, h_lru = _lru(P, _front_pad(conv_lru), st_lru.reshape(nb, 1, LRU_W), lw, sw, layer, nb, t, rows)
    attn = []
    for g, (window, dil) in enumerate(ATTN_GROUPS):
        both_t, cache_t, new_t = tables[g]
        if caches is None:
            attn.append(_attn_prompt(qkv[g], qkv[3 + g], qkv[6 + g], both_t, dil, nb, t))
        else:
            attn.append(_attn_sample(P, caches[g], cache_t, new_t, g, layer, nb, t, window))
    dils = tuple(dil if caches is None else 1 for _, dil in ATTN_GROUPS)
    mixed = _mix(P, y_ssd, y_lru, attn, dils, sw, layer, tm_mix)
    routed = _res(x, mixed, lw, sw, layer, tm_res, caches is None)
    states = (_cols(P, nb, t, C_XBC, SSD_CONV_DIM)[:, t - 3:],
              h_ssd.reshape(nb, SSD_HEADS, SSD_HEAD_DIM, SSD_STATE),
              _cols(P, nb, t, C_XR, LRU_W)[:, t - 3:],
              h_lru.reshape(nb, LRU_W)) + tuple(
                  _kv_rows(P, nb, t, g, min(w, t)) for g, (w, _) in enumerate(ATTN_GROUPS))
    return routed, states


def kernel(x_prompt, x_sample, cache_conv_ssd, state_ssd, cache_conv_lru, state_lru, cache_kv_w128, cache_kv_w512, cache_kv_w2048, norm1, w_in, conv_ssd_w, conv_ssd_b, ssd_dt_bias, ssd_a_log, ssd_d, ssd_norm_w, conv_lru_w, conv_lru_b, lru_wr, lru_br, lru_wi, lru_bi, lru_lambda, t5_bias, w_br_ssd, w_br_lru, w_br_attn, w_gate, b_gate, w_o, norm2, w_router_group, b_router_group, w_router_expert, b_router_expert, w1, w3, w2, final_norm):
    bp, tp, _ = x_prompt.shape
    bs, ts, _ = x_sample.shape
    xp = x_prompt.reshape(bp * tp, D)
    xs = x_sample.reshape(bs * ts, D)
    fn = final_norm[None]
    tables = [_bias_tables(t5_bias, g, dil, window, ts) for g, (window, dil) in enumerate(ATTN_GROUPS)]
    caches = [c.reshape(-1, HD) for c in (cache_kv_w128, cache_kv_w512, cache_kv_w2048)]
    sw = {name: w.astype(BF16) for name, w in dict(
        lru_wr=lru_wr, lru_wi=lru_wi, w_br_ssd=w_br_ssd, w_br_lru=w_br_lru, w_br_attn=w_br_attn, w_o=w_o).items()}
    sw.update(w1=w1, w3=w3, w2=w2)
    sw['w_all'] = _prep_w(w_gate, w_in)
    outs_p, outs_s = [], []
    for l in range(DEPTH):
        lw = _layer_weights(l, norm1, conv_ssd_w, conv_ssd_b, ssd_dt_bias, ssd_a_log, ssd_d, ssd_norm_w,
                            conv_lru_w, conv_lru_b, lru_br, lru_bi, lru_lambda, b_gate, norm2,
                            w_router_group, b_router_group, w_router_expert, b_router_expert)
        final = l == DEPTH - 1
        (x1, h2, _, route, counts), sp = _layer_front(
            xp, lw, sw, tables, l, bp, tp, CL, 1024, 256, 512,
            jnp.zeros((bp, CONV_W - 1, SSD_CONV_DIM), F32), jnp.zeros((bp, SSD_HEADS, SSD_HEAD_DIM, SSD_STATE), F32),
            jnp.zeros((bp, CONV_W - 1, LRU_W), F32), jnp.zeros((bp, LRU_W), F32), None)
        sample = {}

        def sample_front(l=l, lw=lw, xs=xs):
            sample['front'] = _layer_front(xs, lw, sw, tables, l, bs, ts, ts, bs * ts, bs * ts, bs * ts,
                                           cache_conv_ssd[l], state_ssd[l], cache_conv_lru[l], state_lru[l], caches)
            return sample['front']

        def sample_moe(l=l, final=final):
            (x1_s, h2_s, comb_s), _ = sample['front']
            sample['x'] = _moe(h2_s, comb_s, x1_s, sw, l, fn, bs * ts, final)
            return sample['x']

        xp = _moe_sparse(h2, route, counts, x1, sw, l, fn, final, sample_front, sample_moe)
        xs = sample['x']
        outs_p.append(sp)
        outs_s.append(sample['front'][1])

    def stk(outs, i):
        return jnp.stack([o[i] for o in outs], axis=0)

    return ((xp.reshape(bp, tp, D), xs.reshape(bs, ts, D))
            + tuple(stk(outs_p, i) for i in range(7)) + tuple(stk(outs_s, i) for i in range(7)))
```

```python
import functools

import numpy as np
import jax
import jax.numpy as jnp
from jax import lax
from jax.experimental import pallas as pl
from jax.experimental.pallas import tpu as pltpu
from jax.experimental.pallas import tpu_sc as plsc

F32 = jnp.float32
BF16 = jnp.bfloat16
EPS = 1e-6
NEG = -1e30

D = 2048
DEPTH = 2
PAST_LEN = 16384
CL = 128
CONV_W = 4
SSD_HEADS = 16
SSD_HEAD_DIM = 64
SSD_INNER = 1024
SSD_STATE = 128
SSD_CONV_DIM = 1536
LRU_W = 1024
LRU_BLOCKS = 8
LRU_C = 8.0
ATTN_GROUPS = ((128, 1), (512, 4), (2048, 16))
HPG = 4
HD = 128
GW = HPG * HD
T5_BUCKETS = 32
T5_MAX_DIST = 2048
N_GROUPS = 4
PER_GROUP = 4
N_EXPERTS = 16
D_EXPERT = 512

TILE = 512
C_GATE = 0
C_Z = 6144
C_XR = 7168
C_GR = 8192
C_XBC = 9216
C_Q = 10752
C_K = 12288
C_V = 13824
C_DT = 15360
PW = 15872
N_GATE_TILES = (3 * D) // TILE
ROUTE_LANES = 128
VMEM_LIMIT = 56 * 1024 * 1024


def _cparams(sem):
    return pltpu.CompilerParams(dimension_semantics=sem, vmem_limit_bytes=VMEM_LIMIT)


def _full(shape):
    nd = len(shape)
    return pl.BlockSpec(shape, lambda *_: (0,) * nd)


def _of_layer(shape, layer):
    nd = len(shape)
    return pl.BlockSpec((1,) + shape, lambda *_: (layer,) + (0,) * nd)


_W_IN_STARTS = (0, 512, 2576, 3088, 3600, 4112, 1024, 1536, 2048, 4624, 5136, 5648, 6160, 6672, 7184,
                7696, 8208, 8720, 2560)
DT_SHIFT = SSD_HEADS


def _prep_kernel(blk_ref, shift_ref, width_ref, wg_ref, wa_ref, wb_ref, o_ref):
    j = pl.program_id(1)

    @pl.when(j < N_GATE_TILES)
    def _():
        o_ref[0] = wg_ref[0].astype(BF16)

    @pl.when(j >= N_GATE_TILES)
    def _():
        a = wa_ref[0]
        shifted = jnp.concatenate([a[DT_SHIFT:, :], wb_ref[0]], axis=0)
        val = jnp.where(shift_ref[j] == 0, a, shifted)
        row = lax.broadcasted_iota(jnp.int32, val.shape, 0)
        o_ref[0] = jnp.where(row < width_ref[j], val, 0.0).T.astype(BF16)


def _prep_w(w_gate, w_in):
    starts = (0,) * N_GATE_TILES + _W_IN_STARTS
    blk = jnp.asarray([s // TILE for s in starts], jnp.int32)
    shift = jnp.asarray([s % TILE for s in starts], jnp.int32)
    assert all(s % TILE in (0, DT_SHIFT) for s in starts) and w_in.shape[2] % DT_SHIFT == 0
    width = jnp.asarray([TILE] * (len(starts) - 1) + [SSD_HEADS], jnp.int32)
    per_tile = TILE // DT_SHIFT
    w_in_t = jnp.swapaxes(w_in, 1, 2)
    return pl.pallas_call(
        _prep_kernel,
        grid_spec=pltpu.PrefetchScalarGridSpec(
            num_scalar_prefetch=3,
            grid=(DEPTH, PW // TILE),
            in_specs=[pl.BlockSpec((1, D, TILE), lambda l, j, b, s, w: (l, 0, jnp.minimum(j, N_GATE_TILES - 1))),
                      pl.BlockSpec((1, TILE, D), lambda l, j, b, s, w: (l, b[j], 0)),
                      pl.BlockSpec((1, DT_SHIFT, D), lambda l, j, b, s, w: (l, (b[j] + 1) * per_tile, 0))],
            out_specs=pl.BlockSpec((1, D, TILE), lambda l, j, b, s, w: (l, 0, j)),
        ),
        out_shape=jax.ShapeDtypeStruct((DEPTH, D, PW), BF16),
        compiler_params=_cparams(("parallel", "arbitrary")),
        name="prep_w",
    )(blk, shift, width, w_gate, w_in_t, w_in_t)


def _proj_kernel(x_ref, nw_ref, w_ref, b_ref, o_ref, *rest, tm, phase_major):
    if phase_major:
        qkv_refs, (h_ref, acc_ref, ph_ref) = rest[:9], rest[9:]
    else:
        h_ref, acc_ref = rest
    j = pl.program_id(1)

    @pl.when(j == 0)
    def _():
        x = x_ref[...]
        ms = jnp.mean(x * x, axis=-1, keepdims=True)
        h_ref[...] = (x * lax.rsqrt(ms + EPS) * nw_ref[...]).astype(BF16)
        acc_ref[...] = jnp.zeros(acc_ref.shape, F32)

    prev = acc_ref[...]
    o_ref[...] = jnp.where(j <= N_GATE_TILES, jax.nn.sigmoid(prev), prev)
    acc_ref[...] = jnp.dot(h_ref[...], w_ref[0], preferred_element_type=F32) + b_ref[...]

    if phase_major:
        for part in range(3):
            for g, (_, dil) in enumerate(ATTN_GROUPS):
                ref = qkv_refs[part * 3 + g]

                @pl.when(j - 1 == C_Q // TILE + part * 3 + g)
                def _(ref=ref, dil=dil):
                    if dil == 1:
                        ref[0, 0] = o_ref[...].astype(BF16)
                    else:
                        for c in range(TILE // 128):
                            ph_ref[c] = o_ref[:, c * 128:(c + 1) * 128]
                        for p in range(dil):
                            for c in range(TILE // 128):
                                ref[0, p, :, c * 128:(c + 1) * 128] = (
                                    ph_ref[c, pl.ds(p, tm // dil, stride=dil), :].astype(BF16))


def _proj(x, nw, w_all, b_all, layer, tm, nb=None, t=None):
    n = x.shape[0]
    phase_major = nb is not None
    nt = PW // TILE
    out_specs = [pl.BlockSpec((tm, TILE), lambda i, j: (i, jnp.maximum(j - 1, 0)))]
    out_shape = [jax.ShapeDtypeStruct((n, PW), F32)]
    scratch = [pltpu.VMEM((tm, D), BF16), pltpu.VMEM((tm, TILE), F32)]
    if phase_major:
        tpb = t // tm
        for _ in range(3):
            for _, dil in ATTN_GROUPS:
                out_specs.append(pl.BlockSpec((1, dil, tm // dil, GW), lambda i, j: (i // tpb, 0, i % tpb, 0)))
                out_shape.append(jax.ShapeDtypeStruct((nb, dil, t // dil, GW), BF16))
        scratch.append(pltpu.VMEM((TILE // 128, tm, 128), F32))
    return pl.pallas_call(
        functools.partial(_proj_kernel, tm=tm, phase_major=phase_major),
        grid=(n // tm, nt + 1),
        in_specs=[pl.BlockSpec((tm, D), lambda i, j: (i, 0)),
                  pl.BlockSpec((1, D), lambda i, j: (0, 0)),
                  pl.BlockSpec((1, D, TILE), lambda i, j: (layer, 0, jnp.minimum(j, nt - 1))),
                  pl.BlockSpec((1, TILE), lambda i, j: (0, jnp.minimum(j, nt - 1)))],
        out_specs=out_specs,
        out_shape=out_shape,
        scratch_shapes=scratch,
        compiler_params=_cparams(("parallel", "arbitrary")),
        name="proj",
    )(x, nw, w_all, b_all)


def _conv_step(x_ref, xp_ref, cw_ref, cb_ref, rows, out_rows):
    xp_ref[8:8 + rows, :] = _bf16_round(x_ref[...])
    cw = _bf16_round(cw_ref[...])
    acc = cw[0:1, :] * xp_ref[5:5 + out_rows, :]
    for j in range(1, CONV_W):
        acc = acc + cw[j:j + 1, :] * xp_ref[5 + j:5 + j + out_rows, :]
    tail = xp_ref[rows:rows + 8, :]
    xp_ref[0:8, :] = tail
    return acc + cb_ref[...]


def _bf16_round(x):
    return x.astype(BF16).astype(F32)


def _softplus(x):
    return jnp.maximum(x, 0.0) + jnp.log1p(jnp.exp(-jnp.abs(x)))


def _ssd_kernel(z_ref, xbc_ref, dt_ref, tail_ref, h0_ref, cw_ref, cb_ref, dtb_ref, alog_ref, dsk_ref, nw_ref,
                y_ref, hf_ref, xp_ref, act_ref, st_ref, ysc_ref, *, rows, n_chunks):
    c = pl.program_id(1)

    @pl.when(c == 0)
    def _():
        xp_ref[0:8, :] = _bf16_round(tail_ref[0])
        st_ref[...] = h0_ref[0]

    if rows < CL:
        xp_ref[8 + rows:, :] = jnp.zeros((CL - rows, SSD_CONV_DIM), F32)
    conv = _conv_step(xbc_ref, xp_ref, cw_ref, cb_ref, rows, CL)
    act_ref[...] = conv * jax.nn.sigmoid(conv)

    row = lax.broadcasted_iota(jnp.int32, (CL, 128), 0)
    lane = lax.broadcasted_iota(jnp.int32, (CL, 128), 1)
    raw = dt_ref[...]
    if rows < CL:
        raw = jnp.concatenate([raw, jnp.zeros((CL - rows, 128), F32)], axis=0)
    dt = _softplus(raw + dtb_ref[...])
    dt = jnp.where((lane < SSD_HEADS) & (row < rows), dt, 0.0)
    da = dt * (-jnp.exp(alog_ref[...]))
    acs = da
    d = 1
    while d < CL:
        acs = acs + jnp.where(row >= d, pltpu.roll(acs, d, 0), 0.0)
        d *= 2
    acs_t = acs.T
    last = acs[CL - 1:CL, :]
    e_acs = jnp.exp(acs)
    to_end = jnp.exp(last - acs)
    cdec = jnp.exp(last)
    causal = row >= lane
    lo_lane = lane < SSD_HEAD_DIM
    lo_row = row < SSD_HEAD_DIM
    dsk = dsk_ref[...]

    def pair_cols(arr, h):
        return jnp.where(lo_lane, arr[:, h:h + 1], arr[:, h + 1:h + 2])

    nt = (((1,), (1,)), ((), ()))
    for g in range(2):
        bm = act_ref[:, SSD_INNER + g * SSD_STATE:SSD_INNER + (g + 1) * SSD_STATE].astype(BF16)
        cm = act_ref[:, SSD_INNER + 256 + g * SSD_STATE:SSD_INNER + 256 + (g + 1) * SSD_STATE].astype(BF16)
        cb = lax.dot_general(cm, bm, nt, preferred_element_type=F32)
        for pp in range(4):
            h = g * 8 + 2 * pp
            sl = slice(h * SSD_HEAD_DIM, h * SSD_HEAD_DIM + 128)
            xs = act_ref[:, sl]
            xdt = xs * pair_cols(dt, h)
            xdt_b = xdt.astype(BF16)
            ys = []
            for hh in (h, h + 1):
                seg = acs[:, hh:hh + 1] - acs_t[hh:hh + 1, :]
                decay = jnp.exp(jnp.where(causal, seg, -jnp.inf))
                ys.append(jnp.dot((cb * decay).astype(BF16), xdt_b, preferred_element_type=F32))
            y_diag = jnp.where(lo_lane, ys[0], ys[1])
            st = st_ref[sl, :]
            y_off = lax.dot_general(cm, st.astype(BF16), nt, preferred_element_type=F32) * pair_cols(e_acs, h)
            d_pair = jnp.where(lo_lane, dsk[:, h:h + 1], dsk[:, h + 1:h + 2])
            ysc_ref[:, sl] = y_diag + y_off + d_pair * xs
            xdte_t = (xdt * pair_cols(to_end, h)).T.astype(BF16)
            s_new = jnp.dot(xdte_t, bm, preferred_element_type=F32)
            dec = jnp.where(lo_row, cdec[:, h:h + 1], cdec[:, h + 1:h + 2])
            st_ref[sl, :] = dec * st + s_new

    zz = z_ref[...]
    yg = ysc_ref[0:rows, :] * (zz * jax.nn.sigmoid(zz))
    gw = SSD_INNER // 2
    for g in range(2):
        part = yg[:, g * gw:(g + 1) * gw]
        ms = jnp.mean(part * part, axis=-1, keepdims=True)
        y_ref[:, g * gw:(g + 1) * gw] = (part * lax.rsqrt(ms + EPS)
                                         * nw_ref[:, g * gw:(g + 1) * gw]).astype(y_ref.dtype)

    @pl.when(c == n_chunks - 1)
    def _():
        hf_ref[0] = st_ref[...]


def _ssd(P, tail, h0, lw, nb, t, rows):
    nc = t // rows
    kern = functools.partial(_ssd_kernel, rows=rows, n_chunks=nc)
    return pl.pallas_call(
        kern,
        grid=(nb, nc),
        in_specs=[pl.BlockSpec((rows, SSD_INNER), lambda b, c: (b * nc + c, C_Z // SSD_INNER)),
                  pl.BlockSpec((rows, SSD_CONV_DIM), lambda b, c: (b * nc + c, C_XBC // SSD_CONV_DIM)),
                  pl.BlockSpec((rows, 128), lambda b, c: (b * nc + c, C_DT // 128)),
                  pl.BlockSpec((1, 8, SSD_CONV_DIM), lambda b, c: (b, 0, 0)),
                  pl.BlockSpec((1, SSD_INNER, SSD_STATE), lambda b, c: (b, 0, 0)),
                  _full((CONV_W, SSD_CONV_DIM)), _full((1, SSD_CONV_DIM)),
                  _full((1, 128)), _full((1, 128)), _full((1, 128)), _full((1, SSD_INNER))],
        out_specs=[pl.BlockSpec((rows, SSD_INNER), lambda b, c: (b * nc + c, 0)),
                   pl.BlockSpec((1, SSD_INNER, SSD_STATE), lambda b, c: (b, 0, 0))],
        out_shape=[jax.ShapeDtypeStruct((nb * t, SSD_INNER), BF16 if rows % 16 == 0 else F32),
                   jax.ShapeDtypeStruct((nb, SSD_INNER, SSD_STATE), F32)],
        scratch_shapes=[pltpu.VMEM((8 + CL, SSD_CONV_DIM), F32),
                        pltpu.VMEM((CL, SSD_CONV_DIM), F32),
                        pltpu.VMEM((SSD_INNER, SSD_STATE), F32),
                        pltpu.VMEM((CL, SSD_INNER), F32)],
        compiler_params=_cparams(("parallel", "arbitrary")),
        name="ssd",
    )(P, P, P, tail, h0, lw['conv_ssd_w'], lw['conv_ssd_b'], lw['dt_bias'], lw['a_log'], lw['d_skip'], lw['ssd_norm_w'])


def _lru_kernel(xr_ref, gr_ref, tail_ref, h0_ref, cw_ref, cb_ref, wr_ref, br_ref, wi_ref, bi_ref, lam_ref,
                y_ref, hl_ref, xp_ref, h_ref, *, rows, n_chunks):
    c = pl.program_id(1)

    @pl.when(c == 0)
    def _():
        xp_ref[0:8, :] = _bf16_round(tail_ref[0])
        h_ref[...] = h0_ref[0]

    x = _conv_step(xr_ref, xp_ref, cw_ref, cb_ref, rows, rows)
    xb = x.astype(BF16)
    rs, is_ = [], []
    for n in range(LRU_BLOCKS):
        blk = xb[:, n * 128:(n + 1) * 128]
        rs.append(jnp.dot(blk, wr_ref[0, n], preferred_element_type=F32))
        is_.append(jnp.dot(blk, wi_ref[0, n], preferred_element_type=F32))
    r_gate = jax.nn.sigmoid(jnp.concatenate(rs, axis=1) + br_ref[...])
    i_gate = jax.nn.sigmoid(jnp.concatenate(is_, axis=1) + bi_ref[...])
    log_a = -LRU_C * r_gate * _softplus(-lam_ref[...])
    a = jnp.exp(log_a)
    th = jnp.tanh(log_a)
    b = jnp.sqrt(-2.0 * th / (1.0 - th)) * (i_gate * x)
    in_group = lax.broadcasted_iota(jnp.int32, (rows, LRU_W), 0) % 8
    for d in (1, 2, 4):
        a_s = jnp.where(in_group >= d, pltpu.roll(a, d, 0), 1.0)
        b_s = jnp.where(in_group >= d, pltpu.roll(b, d, 0), 0.0)
        b = a * b_s + b
        a = a * a_s
    carry = h_ref[...]
    groups = []
    for g in range(rows // 8):
        h_g = b[8 * g:8 * g + 8, :] + a[8 * g:8 * g + 8, :] * carry
        groups.append(h_g)
        carry = h_g[7:8, :]
    h = jnp.concatenate(groups, axis=0) if len(groups) > 1 else groups[0]
    last = carry
    h_ref[...] = last
    y_ref[...] = (h * jax.nn.gelu(gr_ref[...])).astype(y_ref.dtype)

    @pl.when(c == n_chunks - 1)
    def _():
        hl_ref[0] = last


def _lru(P, tail, h0, lw, sw, layer, nb, t, rows):
    nc = t // rows
    kern = functools.partial(_lru_kernel, rows=rows, n_chunks=nc)
    y_dtype = BF16 if rows % 16 == 0 else F32
    return pl.pallas_call(
        kern,
        grid=(nb, nc),
        in_specs=[pl.BlockSpec((rows, LRU_W), lambda b, c: (b * nc + c, C_XR // LRU_W)),
                  pl.BlockSpec((rows, LRU_W), lambda b, c: (b * nc + c, C_GR // LRU_W)),
                  pl.BlockSpec((1, 8, LRU_W), lambda b, c: (b, 0, 0)),
                  pl.BlockSpec((1, 1, LRU_W), lambda b, c: (b, 0, 0)),
                  _full((CONV_W, LRU_W)), _full((1, LRU_W)),
                  _of_layer((LRU_BLOCKS, 128, 128), layer), _full((1, LRU_W)),
                  _of_layer((LRU_BLOCKS, 128, 128), layer), _full((1, LRU_W)), _full((1, LRU_W))],
        out_specs=[pl.BlockSpec((rows, LRU_W), lambda b, c: (b * nc + c, 0)),
                   pl.BlockSpec((1, 1, LRU_W), lambda b, c: (b, 0, 0))],
        out_shape=[jax.ShapeDtypeStruct((nb * t, LRU_W), y_dtype),
                   jax.ShapeDtypeStruct((nb, 1, LRU_W), F32)],
        scratch_shapes=[pltpu.VMEM((8 + rows, LRU_W), F32), pltpu.VMEM((1, LRU_W), F32)],
        compiler_params=_cparams(("parallel", "arbitrary")),
        name="lru",
    )(P, P, tail, h0, lw['conv_lru_w'], lw['conv_lru_b'], sw['lru_wr'], lw['lru_br'], sw['lru_wi'], lw['lru_bi'],
      lw['lru_lambda'])


def _attn_kernel(q_ref, kp_ref, vp_ref, kc_ref, vc_ref, bias_ref, o_ref, lse_ref, *, sub, phases):
    scale = HD ** -0.5
    nt = (((1,), (1,)), ((), ()))
    lane = lax.broadcasted_iota(jnp.int32, (CL, 128), 1)
    key = lax.broadcasted_iota(jnp.int32, (CL, 2 * CL), 1)
    first_ok = (pl.program_id(2) > 0) | (key >= CL)
    for z in range(phases):
        for s in range(sub):
            rows = slice(s * CL, (s + 1) * CL)
            both = slice((s - 1) * CL, (s + 1) * CL)
            lse_all = jnp.zeros((CL, 128), F32)
            for h in range(HPG):
                sl = slice(h * HD, (h + 1) * HD)
                q = q_ref[0, z, rows, sl]
                if s == 0:
                    kk = jnp.concatenate([kp_ref[0, z, :, sl], kc_ref[0, z, rows, sl]], axis=0)
                    vv = jnp.concatenate([vp_ref[0, z, :, sl], vc_ref[0, z, rows, sl]], axis=0)
                else:
                    kk, vv = kc_ref[0, z, both, sl], vc_ref[0, z, both, sl]
                sc = lax.dot_general(q, kk, nt, preferred_element_type=F32) * scale + bias_ref[h]
                if s == 0:
                    sc = jnp.where(first_ok, sc, NEG)
                m = jnp.max(sc, axis=-1, keepdims=True)
                p = jnp.exp(sc - m)
                l = jnp.sum(p, axis=-1, keepdims=True)
                o_ref[rows, z * GW + h * HD:z * GW + (h + 1) * HD] = jnp.dot(
                    (p * (1.0 / l)).astype(BF16), vv, preferred_element_type=F32)
                lse_all = jnp.where(lane == h, m + jnp.log(l), lse_all)
            lse_ref[rows, z * 128:(z + 1) * 128] = lse_all


def _attn_prompt(q, k, v, bias, dil, nb, t):
    n = nb * t
    blocks = 8
    sub = min(blocks, t // dil // CL)
    phases = min(blocks // sub, dil)
    nstep = t // dil // (sub * CL)
    cur = pl.BlockSpec((1, phases, sub * CL, GW), lambda b, p, i: (b, p, i, 0))
    prev = pl.BlockSpec((1, phases, CL, GW), lambda b, p, i: (b, p, jnp.maximum(i * sub - 1, 0), 0))
    return pl.pallas_call(
        functools.partial(_attn_kernel, sub=sub, phases=phases),
        grid=(nb, dil // phases, nstep),
        in_specs=[cur, prev, prev, cur, cur, _full((HPG, CL, 2 * CL))],
        out_specs=[pl.BlockSpec((sub * CL, phases * GW), lambda b, p, i: (b * nstep + i, p)),
                   pl.BlockSpec((sub * CL, phases * 128), lambda b, p, i: (b * nstep + i, p))],
        out_shape=[jax.ShapeDtypeStruct((n // dil, dil * GW), F32),
                   jax.ShapeDtypeStruct((n // dil, dil * 128), F32)],
        compiler_params=_cparams(("parallel", "parallel", "arbitrary")),
        name=f"attn_prompt_d{dil}",
    )(q, k, v, k, v, bias)


def _attn_sample_kernel(q_ref, kv_ref, kb_ref, vb_ref, ba_ref, bb_ref, o_ref, lse_ref):
    scale = HD ** -0.5
    nt = (((1,), (1,)), ((), ()))
    rows = o_ref.shape[0]
    window = kv_ref.shape[0] // (2 * HPG)
    lane = lax.broadcasted_iota(jnp.int32, (rows, 128), 1)
    lse_all = jnp.zeros((rows, 128), F32)
    for h in range(HPG):
        sl = slice(h * HD, (h + 1) * HD)
        q = q_ref[:, sl].astype(BF16)
        k_cache = kv_ref[pl.ds(h, window, stride=2 * HPG), :].astype(BF16)
        v_cache = kv_ref[pl.ds(HPG + h, window, stride=2 * HPG), :].astype(BF16)
        sa = lax.dot_general(q, k_cache, nt, preferred_element_type=F32) * scale + ba_ref[h]
        sb = lax.dot_general(q, kb_ref[:, sl].astype(BF16), nt, preferred_element_type=F32) * scale + bb_ref[h]
        m = jnp.maximum(jnp.max(sa, axis=-1, keepdims=True), jnp.max(sb, axis=-1, keepdims=True))
        pa = jnp.exp(sa - m)
        pb = jnp.exp(sb - m)
        l = jnp.sum(pa, axis=-1, keepdims=True) + jnp.sum(pb, axis=-1, keepdims=True)
        inv = 1.0 / l
        o_ref[:, sl] = (jnp.dot((pa * inv).astype(BF16), v_cache, preferred_element_type=F32)
                        + jnp.dot((pb * inv).astype(BF16), vb_ref[:, sl].astype(BF16), preferred_element_type=F32))
        lse_all = jnp.where(lane == h, m + jnp.log(l), lse_all)
    lse_ref[...] = lse_all


def _attn_sample(Ps, cache_rows, bias_a, bias_b, g, layer, nb, t, window):
    tq, tk, tv = C_Q // GW + g, C_K // GW + g, C_V // GW + g
    out_spec = pl.BlockSpec((t, GW), lambda b: (b, 0))
    return pl.pallas_call(
        _attn_sample_kernel,
        grid=(nb,),
        in_specs=[pl.BlockSpec((t, GW), lambda b: (b, tq)),
                  pl.BlockSpec((window * 2 * HPG, HD), lambda b: (layer * nb + b, 0)),
                  pl.BlockSpec((t, GW), lambda b: (b, tk)),
                  pl.BlockSpec((t, GW), lambda b: (b, tv)),
                  _full((HPG, t, window)), _full((HPG, t, t))],
        out_specs=[out_spec, pl.BlockSpec((t, 128), lambda b: (b, 0))],
        out_shape=[jax.ShapeDtypeStruct((nb * t, GW), F32), jax.ShapeDtypeStruct((nb * t, 128), F32)],
        compiler_params=_cparams(("parallel",)),
        name=f"attn_sample_w{window}",
    )(Ps, cache_rows, Ps, Ps, bias_a, bias_b)


def _mix_kernel(gs_ref, gl_ref, ga_ref, ys_ref, yl_ref, o0_ref, o1_ref, o2_ref, l0_ref, l1_ref, l2_ref,
                wbs_ref, wbl_ref, wba_ref, out_ref, *scratch, dils):
    tm = out_ref.shape[0]
    o_heads, lses = [], []
    scratch = list(scratch)
    for o_ref, l_ref, dil in zip((o0_ref, o1_ref, o2_ref), (l0_ref, l1_ref, l2_ref), dils):
        if dil == 1:
            o_heads.append([o_ref[:, h * HD:(h + 1) * HD] for h in range(HPG)])
            lses.append(l_ref[...])
            continue
        o_scr, l_scr = scratch.pop(0), scratch.pop(0)
        for p in range(dil):
            rows = pl.ds(p, tm // dil, stride=dil)
            for h in range(HPG):
                o_scr[h, rows, :] = o_ref[:, p * GW + h * HD:p * GW + (h + 1) * HD]
            l_scr[rows, :] = l_ref[:, p * 128:(p + 1) * 128]
        o_heads.append([o_scr[h] for h in range(HPG)])
        lses.append(l_scr[...])
    l0, l1, l2 = lses
    m = jnp.maximum(jnp.maximum(l0, l1), l2)
    e0, e1, e2 = jnp.exp(l0 - m), jnp.exp(l1 - m), jnp.exp(l2 - m)
    den = e0 + e1 + e2
    w0, w1, w2 = e0 / den, e1 / den, e2 / den
    heads = []
    for h in range(HPG):
        per_head = lambda w: jnp.broadcast_to(w[:, h:h + 1], (tm, HD))
        heads.append(o_heads[0][h] * per_head(w0) + o_heads[1][h] * per_head(w1) + o_heads[2][h] * per_head(w2))
    ya = jnp.concatenate(heads, axis=1)
    mixed = (gs_ref[...] * jnp.dot(ys_ref[...].astype(BF16), wbs_ref[0], preferred_element_type=F32)
             + gl_ref[...] * jnp.dot(yl_ref[...].astype(BF16), wbl_ref[0], preferred_element_type=F32)
             + ga_ref[...] * jnp.dot(ya.astype(BF16), wba_ref[0], preferred_element_type=F32))
    out_ref[...] = mixed.astype(BF16)


def _mix(P, y_ssd, y_lru, attn, dils, sw, layer, tm):
    n = P.shape[0]
    row = lambda w: pl.BlockSpec((tm, w), lambda i: (i, 0))
    phased = lambda w, d: pl.BlockSpec((tm // d, d * w), lambda i: (i, 0))
    (o0, s0), (o1, s1), (o2, s2) = attn
    scratch = []
    for d in dils:
        if d > 1:
            scratch += [pltpu.VMEM((HPG, tm, HD), F32), pltpu.VMEM((tm, 128), F32)]
    return pl.pallas_call(
        functools.partial(_mix_kernel, dils=dils),
        grid=(n // tm,),
        in_specs=[pl.BlockSpec((tm, D), lambda i: (i, 0)), pl.BlockSpec((tm, D), lambda i: (i, 1)),
                  pl.BlockSpec((tm, D), lambda i: (i, 2)),
                  row(SSD_INNER), row(LRU_W)] + [phased(GW, d) for d in dils] + [phased(128, d) for d in dils] + [
                  _of_layer((SSD_INNER, D), layer), _of_layer((LRU_W, D), layer), _of_layer((GW, D), layer)],
        out_specs=row(D),
        out_shape=jax.ShapeDtypeStruct((n, D), BF16),
        scratch_shapes=scratch,
        compiler_params=_cparams(("parallel",)),
        name="mix",
    )(P, P, P, y_ssd, y_lru, o0, o1, o2, s0, s1, s2, sw['w_br_ssd'], sw['w_br_lru'], sw['w_br_attn'])


def _res_kernel(x_ref, mixed_ref, wo_ref, n2_ref, wr_ref, br_ref, x1_ref, h2_ref, comb_ref, *rest, dispatch):
    x1 = x_ref[...] + jnp.dot(mixed_ref[...], wo_ref[0], preferred_element_type=F32)
    x1_ref[...] = x1
    ms = jnp.mean(x1 * x1, axis=-1, keepdims=True)
    h2 = x1 * lax.rsqrt(ms + EPS) * n2_ref[...]
    h2b = h2.astype(BF16)
    if dispatch:
        lo = pltpu.bitcast(h2b[:, :HALF].astype(F32), jnp.uint32)
        hi = pltpu.bitcast(h2b[:, HALF:].astype(F32), jnp.uint32)
        h2_ref[...] = pltpu.bitcast(lax.shift_right_logical(lo, jnp.uint32(16)) | hi, jnp.int32)
    else:
        h2_ref[...] = h2b
    logits = jnp.dot(h2b, wr_ref[...], preferred_element_type=F32) + br_ref[...]
    lane = lax.broadcasted_iota(jnp.int32, logits.shape, 1).astype(F32)
    big = float(ROUTE_LANES)

    def first_max(vals, ok):
        v = jnp.where(ok, vals, NEG)
        top = jnp.max(v, axis=-1, keepdims=True)
        idx = jnp.min(jnp.where(ok & (v == top), lane, big), axis=-1, keepdims=True)
        return top, idx

    is_g = lane < N_GROUPS
    gmax, gsel = first_max(logits, is_g)
    gp = 1.0 / jnp.sum(jnp.where(is_g, jnp.exp(logits - gmax), 0.0), axis=-1, keepdims=True)
    lo = N_GROUPS + PER_GROUP * gsel
    is_e = (lane >= lo) & (lane < lo + PER_GROUP)
    t1, i1 = first_max(logits, is_e)
    t2, i2 = first_max(logits, is_e & (lane != i1))
    e2 = jnp.exp(t2 - t1)
    w1 = gp / (1.0 + e2)
    w2 = gp * e2 / (1.0 + e2)
    comb_ref[...] = jnp.where(lane == i1, w1, 0.0) + jnp.where(lane == i2, w2, 0.0)

    if dispatch:
        route_ref, cnt_ref, wrow1_ref, wrow2_ref, carry_ref = rest

        @pl.when(pl.program_id(0) == 0)
        def _():
            carry_ref[...] = jnp.zeros(carry_ref.shape, F32)

        tm = x1.shape[0]
        onehot = jnp.where((lane == i1) | (lane == i2), 1.0, 0.0)
        r = lax.broadcasted_iota(jnp.int32, (tm, tm), 0)
        c = lax.broadcasted_iota(jnp.int32, (tm, tm), 1)
        earlier = jnp.where(r > c, 1.0, 0.0).astype(BF16)
        before = jnp.dot(earlier, onehot.astype(BF16), preferred_element_type=F32) + carry_ref[...]
        rank1 = jnp.sum(jnp.where(lane == i1, before, 0.0), axis=-1, keepdims=True)
        rank2 = jnp.sum(jnp.where(lane == i2, before, 0.0), axis=-1, keepdims=True)
        carry_ref[...] += jnp.sum(onehot, axis=0, keepdims=True)
        cnt_ref[...] = carry_ref[...]
        fields = (i1 - N_GROUPS, i2 - N_GROUPS, rank1, rank2)
        route = jnp.zeros(logits.shape, F32)
        for k, val in enumerate(fields):
            route = jnp.where(lane == k, val, route)
        route_ref[...] = route.T[0:8, :]
        wrow1_ref[...] = jnp.broadcast_to(w1, wrow1_ref.shape)
        wrow2_ref[...] = jnp.broadcast_to(w2, wrow2_ref.shape)


def _res(x, mixed, lw, sw, layer, tm, dispatch):
    n = x.shape[0]
    h2_cols, h2_dtype = (HALF, jnp.int32) if dispatch else (D, BF16)
    out_specs = [pl.BlockSpec((tm, D), lambda i: (i, 0)), pl.BlockSpec((tm, h2_cols), lambda i: (i, 0)),
                 pl.BlockSpec((tm, ROUTE_LANES), lambda i: (i, 0))]
    out_shape = [jax.ShapeDtypeStruct((n, D), F32), jax.ShapeDtypeStruct((n, h2_cols), h2_dtype),
                 jax.ShapeDtypeStruct((n, ROUTE_LANES), F32)]
    scratch = []
    if dispatch:
        wide = pl.BlockSpec((tm, 128), lambda i: (i, 0))
        out_specs += [pl.BlockSpec((8, tm), lambda i: (0, i)), _full((1, ROUTE_LANES)), wide, wide]
        out_shape += [jax.ShapeDtypeStruct((8, n), F32), jax.ShapeDtypeStruct((1, ROUTE_LANES), F32),
                      jax.ShapeDtypeStruct((n, 128), F32), jax.ShapeDtypeStruct((n, 128), F32)]
        scratch = [pltpu.VMEM((1, ROUTE_LANES), F32)]
    return pl.pallas_call(
        functools.partial(_res_kernel, dispatch=dispatch),
        grid=(n // tm,),
        in_specs=[pl.BlockSpec((tm, D), lambda i: (i, 0)), pl.BlockSpec((tm, D), lambda i: (i, 0)),
                  _of_layer((D, D), layer), _full((1, D)), _full((D, ROUTE_LANES)), _full((1, ROUTE_LANES))],
        out_specs=out_specs,
        out_shape=out_shape,
        scratch_shapes=scratch,
        compiler_params=_cparams(("arbitrary",)),
        name="res_router",
    )(x, mixed, sw['w_o'], lw['norm2'], lw['w_router'], lw['b_router'])


def _moe_kernel(h2_ref, comb_ref, x1_ref, w1_ref, w3_ref, w2_ref, fn_ref, o_ref, *, final):
    e = pl.program_id(1)

    @pl.when(e == 0)
    def _():
        o_ref[...] = x1_ref[...]

    h = h2_ref[...]
    a = jnp.dot(h, w1_ref[0, 0].astype(BF16), preferred_element_type=F32)
    b = jnp.dot(h, w3_ref[0, 0].astype(BF16), preferred_element_type=F32)
    comb = comb_ref[...]
    lane = lax.broadcasted_iota(jnp.int32, comb.shape, 1)
    w = jnp.sum(jnp.where(lane == e + N_GROUPS, comb, 0.0), axis=-1, keepdims=True)
    act = (a * jax.nn.sigmoid(a)) * b * w
    o_ref[...] += jnp.dot(act.astype(BF16), w2_ref[0, 0].astype(BF16), preferred_element_type=F32)

    if final:
        @pl.when(e == N_EXPERTS - 1)
        def _():
            x = o_ref[...]
            ms = jnp.mean(x * x, axis=-1, keepdims=True)
            o_ref[...] = x * lax.rsqrt(ms + EPS) * fn_ref[...]


def _moe(h2, comb, x1, sw, layer, final_norm, tm, final):
    n = x1.shape[0]
    return pl.pallas_call(
        functools.partial(_moe_kernel, final=final),
        grid=(n // tm, N_EXPERTS),
        in_specs=[pl.BlockSpec((tm, D), lambda i, e: (i, 0)),
                  pl.BlockSpec((tm, ROUTE_LANES), lambda i, e: (i, 0)),
                  pl.BlockSpec((tm, D), lambda i, e: (i, 0)),
                  pl.BlockSpec((1, 1, D, D_EXPERT), lambda i, e: (layer, e, 0, 0)),
                  pl.BlockSpec((1, 1, D, D_EXPERT), lambda i, e: (layer, e, 0, 0)),
                  pl.BlockSpec((1, 1, D_EXPERT, D), lambda i, e: (layer, e, 0, 0)),
                  pl.BlockSpec((1, D), lambda i, e: (0, 0))],
        out_specs=pl.BlockSpec((tm, D), lambda i, e: (i, 0)),
        out_shape=jax.ShapeDtypeStruct((n, D), F32),
        compiler_params=_cparams(("parallel", "arbitrary")),
        name="moe",
    )(h2, comb, x1, sw['w1'], sw['w3'], sw['w2'], final_norm)


FFN_TM = 512
SC_CORES = 2
SC_SUBCORES = 16
SC_WORKERS = SC_CORES * SC_SUBCORES
SC_CHUNK = 32
HALF = D // 2


def _sc_mesh():
    return plsc.VectorSubcoreMesh(core_axis_name="c", subcore_axis_name="s", num_cores=SC_CORES,
                                  num_subcores=SC_SUBCORES)


def _sc_dispatch(x, wrow1, wrow2, dest1, dest2, n_sorted):
    n = x.shape[0]
    per_w = n // SC_WORKERS

    @functools.partial(
        pl.kernel, mesh=_sc_mesh(),
        out_type=(jax.ShapeDtypeStruct((n_sorted, HALF), jnp.int32), jax.ShapeDtypeStruct((n_sorted, 128), F32)),
        scratch_types=[pltpu.VMEM((SC_CHUNK,), jnp.int32), pltpu.VMEM((SC_CHUNK,), jnp.int32),
                       pltpu.VMEM((SC_CHUNK, HALF), jnp.int32), pltpu.VMEM((SC_CHUNK, 128), F32)],
    )
    def k(x_hbm, w1_hbm, w2_hbm, d1_hbm, d2_hbm, out_hbm, wout_hbm, i1_v, i2_v, rows_v, wrows_v):
        base = (lax.axis_index("s") * SC_CORES + lax.axis_index("c")) * per_w

        @pl.loop(0, per_w // SC_CHUNK)
        def _(j):
            off = base + j * SC_CHUNK
            pltpu.sync_copy(d1_hbm.at[pl.ds(off, SC_CHUNK)], i1_v)
            pltpu.sync_copy(d2_hbm.at[pl.ds(off, SC_CHUNK)], i2_v)
            pltpu.sync_copy(x_hbm.at[pl.ds(off, SC_CHUNK)], rows_v)
            pltpu.sync_copy(rows_v, out_hbm.at[i1_v])
            pltpu.sync_copy(rows_v, out_hbm.at[i2_v])
            pltpu.sync_copy(w1_hbm.at[pl.ds(off, SC_CHUNK)], wrows_v)
            pltpu.sync_copy(wrows_v, wout_hbm.at[i1_v])
            pltpu.sync_copy(w2_hbm.at[pl.ds(off, SC_CHUNK)], wrows_v)
            pltpu.sync_copy(wrows_v, wout_hbm.at[i2_v])

    return k(x, wrow1, wrow2, dest1, dest2)


def _sc_collect(y, dest1, dest2, n):
    per_w = n // SC_WORKERS

    @functools.partial(
        pl.kernel, mesh=_sc_mesh(),
        out_type=(jax.ShapeDtypeStruct((n, D), F32), jax.ShapeDtypeStruct((n, D), F32)),
        scratch_types=[pltpu.VMEM((SC_CHUNK,), jnp.int32), pltpu.VMEM((SC_CHUNK, D), F32)],
    )
    def k(y_hbm, d1_hbm, d2_hbm, g1_hbm, g2_hbm, idx_v, rows_v):
        base = (lax.axis_index("s") * SC_CORES + lax.axis_index("c")) * per_w

        @pl.loop(0, per_w // SC_CHUNK)
        def _(j):
            off = base + j * SC_CHUNK
            for d_hbm, g_hbm in ((d1_hbm, g1_hbm), (d2_hbm, g2_hbm)):
                pltpu.sync_copy(d_hbm.at[pl.ds(off, SC_CHUNK)], idx_v)
                pltpu.sync_copy(y_hbm.at[idx_v], rows_v)
                pltpu.sync_copy(rows_v, g_hbm.at[pl.ds(off, SC_CHUNK)])

    return k(y, dest1, dest2)


def _ffn_kernel(te_ref, nt_ref, xs_ref, ws_ref, w1_ref, w3_ref, w2_ref, y_ref, w1b_ref, w3b_ref, w2b_ref):
    k = pl.program_id(0)

    @pl.when(k < nt_ref[0])
    def _():
        @pl.when((k == 0) | (te_ref[k] != te_ref[jnp.maximum(k - 1, 0)]))
        def _():
            w1b_ref[...] = w1_ref[0, 0].astype(BF16)
            w3b_ref[...] = w3_ref[0, 0].astype(BF16)
            w2b_ref[...] = w2_ref[0, 0].astype(BF16)

        words = pltpu.bitcast(xs_ref[...], jnp.uint32)
        lo = pltpu.bitcast(lax.shift_left(words, jnp.uint32(16)), F32)
        hi = pltpu.bitcast(words & jnp.uint32(0xFFFF0000), F32)
        xs = jnp.concatenate([lo, hi], axis=1).astype(BF16)
        a = jnp.dot(xs, w1b_ref[...], preferred_element_type=F32)
        b = jnp.dot(xs, w3b_ref[...], preferred_element_type=F32)
        act = (a * jax.nn.sigmoid(a)) * b * ws_ref[:, 0:1]
        y_ref[...] = jnp.dot(act.astype(BF16), w2b_ref[...], preferred_element_type=F32)


def _ffn(xs, ws, tile_expert, n_tiles_used, sw, layer):
    n_sorted = xs.shape[0]
    w_in_spec = pl.BlockSpec((1, 1, D, D_EXPERT), lambda k, te, nt: (layer, te[k], 0, 0))
    return pl.pallas_call(
        _ffn_kernel,
        grid_spec=pltpu.PrefetchScalarGridSpec(
            num_scalar_prefetch=2,
            grid=(n_sorted // FFN_TM,),
            in_specs=[pl.BlockSpec((FFN_TM, HALF), lambda k, te, nt: (k, 0)),
                      pl.BlockSpec((FFN_TM, 128), lambda k, te, nt: (k, 0)), w_in_spec, w_in_spec,
                      pl.BlockSpec((1, 1, D_EXPERT, D), lambda k, te, nt: (layer, te[k], 0, 0))],
            out_specs=pl.BlockSpec((FFN_TM, D), lambda k, te, nt: (k, 0)),
            scratch_shapes=[pltpu.VMEM((D, D_EXPERT), BF16), pltpu.VMEM((D, D_EXPERT), BF16),
                            pltpu.VMEM((D_EXPERT, D), BF16)],
        ),
        out_shape=jax.ShapeDtypeStruct((n_sorted, D), F32),
        compiler_params=_cparams(("arbitrary",)),
        name="ffn",
    )(tile_expert, n_tiles_used, xs, ws, sw['w1'], sw['w3'], sw['w2'])


def _combine_kernel(x1_ref, g1_ref, g2_ref, fn_ref, o_ref, *, final):
    x = x1_ref[...] + (g1_ref[...] + g2_ref[...])
    if final:
        ms = jnp.mean(x * x, axis=-1, keepdims=True)
        x = x * lax.rsqrt(ms + EPS) * fn_ref[...]
    o_ref[...] = x


def _combine(x1, g1, g2, final_norm, tm, final):
    n = x1.shape[0]
    row = pl.BlockSpec((tm, D), lambda i: (i, 0))
    return pl.pallas_call(
        functools.partial(_combine_kernel, final=final),
        grid=(n // tm,),
        in_specs=[row, row, row, _full((1, D))],
        out_specs=row,
        out_shape=jax.ShapeDtypeStruct((n, D), F32),
        compiler_params=_cparams(("parallel",)),
        name="combine",
    )(x1, g1, g2, final_norm)


def _dest_kernel(seg_ref, route_ref, o_ref):
    route = route_ref[...]
    for s in range(2):
        expert, rank = route[s:s + 1, :], route[2 + s:3 + s, :]
        start = jnp.zeros(expert.shape, jnp.int32)
        for e in range(N_EXPERTS):
            start = jnp.where(expert == float(e), seg_ref[e], start)
        o_ref[s:s + 1, :] = start + rank.astype(jnp.int32)


def _dest(route_t, seg_start, tn):
    n = route_t.shape[1]
    return pl.pallas_call(
        _dest_kernel,
        grid_spec=pltpu.PrefetchScalarGridSpec(
            num_scalar_prefetch=1,
            grid=(n // tn,),
            in_specs=[pl.BlockSpec((8, tn), lambda i, seg: (0, i))],
            out_specs=pl.BlockSpec((2, tn), lambda i, seg: (0, i)),
        ),
        out_shape=jax.ShapeDtypeStruct((2, n), jnp.int32),
        compiler_params=_cparams(("parallel",)),
        name="dest",
    )(seg_start, route_t)


def _after(values, others):
    values, _ = lax.optimization_barrier((values, others))
    return values


def _moe_sparse(h2, route_t, counts, wrow1, wrow2, x1, sw, layer, final_norm, final, during_dispatch,
                during_collect):
    n = x1.shape[0]
    n_sorted = 2 * n + N_EXPERTS * FFN_TM
    cnt = counts[0, N_GROUPS:N_GROUPS + N_EXPERTS].astype(jnp.int32)
    tiles = (cnt + FFN_TM - 1) // FFN_TM
    tile_end = jnp.cumsum(tiles)
    seg_start = (tile_end - tiles) * FFN_TM
    dest = _dest(route_t, seg_start.astype(jnp.int32), 2048)
    dest1, dest2 = dest[0], dest[1]
    tile_ids = jnp.arange(n_sorted // FFN_TM, dtype=jnp.int32)
    tile_expert = jnp.minimum(jnp.sum(tile_end[None, :] <= tile_ids[:, None], axis=1), N_EXPERTS - 1)
    xs, ws = _after(_sc_dispatch(h2, wrow1, wrow2, dest1, dest2, n_sorted), during_dispatch())
    y = _ffn(xs, ws, tile_expert.astype(jnp.int32), tile_end[-1:].astype(jnp.int32), sw, layer)
    g1, g2 = _after(_sc_collect(y, dest1, dest2, n), during_collect())
    return _combine(x1, g1, g2, final_norm, 512, final)


def _t5_buckets(dist):
    max_exact = T5_BUCKETS // 2
    large = max_exact + (np.log(np.maximum(dist, 1) / max_exact) / np.log(T5_MAX_DIST / max_exact)
                         * (T5_BUCKETS - max_exact)).astype(np.int32)
    large = np.minimum(large, T5_BUCKETS - 1)
    return np.where(dist < max_exact, dist, large).astype(np.int32)


def _bias_tables(t5, g, dil, window, t_sample):
    nk = window // dil + 1
    hs = slice(g * HPG, (g + 1) * HPG)
    bias = t5[_t5_buckets(np.arange(nk) * dil)][:, hs].T
    rev = bias[:, ::-1]
    neg = lambda *shape: jnp.full(shape, NEG, F32)

    vec = jnp.concatenate([rev, neg(HPG, CL)], axis=1)
    both = jnp.tile(vec, (1, CL + 1))[:, :CL * 2 * CL].reshape(HPG, CL, 2 * CL)

    rows = []
    for r in range(t_sample):
        shift = r // dil
        per_u = jnp.concatenate([neg(HPG, shift), rev[:, :nk - 1 - shift]], axis=1)
        on_phase = (np.arange(dil) == r % dil)[None, None, :]
        rows.append(jnp.where(on_phase, per_u[:, :, None], NEG).reshape(HPG, window))
    cache_t = jnp.stack(rows, axis=1)
    r = np.arange(t_sample)[:, None]
    c = np.arange(t_sample)[None, :]
    ok = ((r - c) % dil == 0) & (r >= c)
    new_t = jnp.where(jnp.asarray(ok)[None], bias[:, np.clip((r - c) // dil, 0, nk - 1)], NEG)
    return both, cache_t, new_t


def _layer_weights(l, norm1, conv_ssd_w, conv_ssd_b, ssd_dt_bias, ssd_a_log, ssd_d, ssd_norm_w,
                   conv_lru_w, conv_lru_b, lru_br, lru_bi, lru_lambda, b_gate, norm2,
                   w_router_group, b_router_group, w_router_expert, b_router_expert):
    b_all = jnp.concatenate([b_gate[l], jnp.zeros((PW - 3 * D,), F32)])[None]

    def pad128(v):
        return jnp.concatenate([v, jnp.zeros((128 - v.shape[0],), F32)])[None]

    return {
        'norm1': norm1[l][None], 'b_all': b_all,
        'conv_ssd_w': conv_ssd_w[l], 'conv_ssd_b': conv_ssd_b[l][None],
        'dt_bias': pad128(ssd_dt_bias[l]), 'a_log': pad128(ssd_a_log[l]), 'd_skip': pad128(ssd_d[l]),
        'ssd_norm_w': ssd_norm_w[l][None],
        'conv_lru_w': conv_lru_w[l], 'conv_lru_b': conv_lru_b[l][None],
        'lru_br': lru_br[l][None], 'lru_bi': lru_bi[l][None], 'lru_lambda': lru_lambda[l][None],
        'norm2': norm2[l][None],
        'w_router': jnp.concatenate([w_router_group[l], w_router_expert[l],
                                     jnp.zeros((D, ROUTE_LANES - N_GROUPS - N_EXPERTS), F32)],
                                    axis=1).astype(BF16),
        'b_router': pad128(jnp.concatenate([b_router_group[l], b_router_expert[l]])),
    }


def _front_pad(buf):
    return jnp.pad(buf, ((0, 0), (8 - (CONV_W - 1), 0), (0, 0)))


def _cols(P, nb, t, start, width):
    return P.reshape(nb, t, PW)[:, :, start:start + width]


def _kv_rows(P, nb, t, g, n_rows):
    k = _cols(P, nb, t, C_K + g * GW, GW)[:, t - n_rows:].reshape(nb, n_rows, HPG, HD)
    v = _cols(P, nb, t, C_V + g * GW, GW)[:, t - n_rows:].reshape(nb, n_rows, HPG, HD)
    return jnp.stack([k, v], axis=2)


def _layer_front(x, lw, sw, tables, layer, nb, t, rows, tm, tm_mix, tm_res, conv_ssd, st_ssd, conv_lru, st_lru,
                 caches):
    if caches is None:
        P, *qkv = _proj(x, lw['norm1'], sw['w_all'], lw['b_all'], layer, tm, nb, t)
    else:
        (P,) = _proj(x, lw['norm1'], sw['w_all'], lw['b_all'], layer, tm)
    y_ssd, h_ssd = _ssd(P, _front_pad(conv_ssd), st_ssd.reshape(nb, SSD_INNER, SSD_STATE), lw, nb, t, rows)
    y_lru, h_lru = _lru(P, _front_pad(conv_lru), st_lru.reshape(nb, 1, LRU_W), lw, sw, layer, nb, t, rows)
    attn = []
    for g, (window, dil) in enumerate(ATTN_GROUPS):
        both_t, cache_t, new_t = tables[g]
        if caches is None:
            attn.append(_attn_prompt(qkv[g], qkv[3 + g], qkv[6 + g], both_t, dil, nb, t))
        else:
            attn.append(_attn_sample(P, caches[g], cache_t, new_t, g, layer, nb, t, window))
    dils = tuple(dil if caches is None else 1 for _, dil in ATTN_GROUPS)
    mixed = _mix(P, y_ssd, y_lru, attn, dils, sw, layer, tm_mix)
    routed = _res(x, mixed, lw, sw, layer, tm_res, caches is None)
    states = (_cols(P, nb, t, C_XBC, SSD_CONV_DIM)[:, t - 3:],
              h_ssd.reshape(nb, SSD_HEADS, SSD_HEAD_DIM, SSD_STATE),
              _cols(P, nb, t, C_XR, LRU_W)[:, t - 3:],
              h_lru.reshape(nb, LRU_W)) + tuple(
                  _kv_rows(P, nb, t, g, min(w, t)) for g, (w, _) in enumerate(ATTN_GROUPS))
    return routed, states


def kernel(x_prompt, x_sample, cache_conv_ssd, state_ssd, cache_conv_lru, state_lru, cache_kv_w128, cache_kv_w512, cache_kv_w2048, norm1, w_in, conv_ssd_w, conv_ssd_b, ssd_dt_bias, ssd_a_log, ssd_d, ssd_norm_w, conv_lru_w, conv_lru_b, lru_wr, lru_br, lru_wi, lru_bi, lru_lambda, t5_bias, w_br_ssd, w_br_lru, w_br_attn, w_gate, b_gate, w_o, norm2, w_router_group, b_router_group, w_router_expert, b_router_expert, w1, w3, w2, final_norm):
    bp, tp, _ = x_prompt.shape
    bs, ts, _ = x_sample.shape
    xp = x_prompt.reshape(bp * tp, D)
    xs = x_sample.reshape(bs * ts, D)
    fn = final_norm[None]
    tables = [_bias_tables(t5_bias, g, dil, window, ts) for g, (window, dil) in enumerate(ATTN_GROUPS)]
    caches = [c.reshape(-1, HD) for c in (cache_kv_w128, cache_kv_w512, cache_kv_w2048)]
    sw = {name: w.astype(BF16) for name, w in dict(
        lru_wr=lru_wr, lru_wi=lru_wi, w_br_ssd=w_br_ssd, w_br_lru=w_br_lru, w_br_attn=w_br_attn, w_o=w_o).items()}
    sw.update(w1=w1, w3=w3, w2=w2)
    sw['w_all'] = _prep_w(w_gate, w_in)
    outs_p, outs_s = [], []
    for l in range(DEPTH):
        lw = _layer_weights(l, norm1, conv_ssd_w, conv_ssd_b, ssd_dt_bias, ssd_a_log, ssd_d, ssd_norm_w,
                            conv_lru_w, conv_lru_b, lru_br, lru_bi, lru_lambda, b_gate, norm2,
                            w_router_group, b_router_group, w_router_expert, b_router_expert)
        final = l == DEPTH - 1
        (x1, h2, _, route_t, counts, wrow1, wrow2), sp = _layer_front(
            xp, lw, sw, tables, l, bp, tp, CL, 1024, 256, 512,
            jnp.zeros((bp, CONV_W - 1, SSD_CONV_DIM), F32), jnp.zeros((bp, SSD_HEADS, SSD_HEAD_DIM, SSD_STATE), F32),
            jnp.zeros((bp, CONV_W - 1, LRU_W), F32), jnp.zeros((bp, LRU_W), F32), None)
        sample = {}

        def sample_front(l=l, lw=lw, xs=xs):
            sample['front'] = _layer_front(xs, lw, sw, tables, l, bs, ts, ts, bs * ts, bs * ts, bs * ts,
                                           cache_conv_ssd[l], state_ssd[l], cache_conv_lru[l], state_lru[l], caches)
            return sample['front']

        def sample_moe(l=l, final=final):
            (x1_s, h2_s, comb_s), _ = sample['front']
            sample['x'] = _moe(h2_s, comb_s, x1_s, sw, l, fn, bs * ts, final)
            return sample['x']

        xp = _moe_sparse(h2, route_t, counts, wrow1, wrow2, x1, sw, l, fn, final, sample_front, sample_moe)
        xs = sample['x']
        outs_p.append(sp)
        outs_s.append(sample['front'][1])

    def stk(outs, i):
        return jnp.stack([o[i] for o in outs], axis=0)

    return ((xp.reshape(bp, tp, D), xs.reshape(bs, ts, D))
            + tuple(stk(outs_p, i) for i in range(7)) + tuple(stk(outs_s, i) for i in range(7)))
```

```python
import functools

import numpy as np
import jax
import jax.numpy as jnp
from jax import lax
from jax.experimental import pallas as pl
from jax.experimental.pallas import tpu as pltpu
from jax.experimental.pallas import tpu_sc as plsc

F32 = jnp.float32
BF16 = jnp.bfloat16
EPS = 1e-6
NEG = -1e30

D = 2048
DEPTH = 2
PAST_LEN = 16384
CL = 128
CONV_W = 4
SSD_HEADS = 16
SSD_HEAD_DIM = 64
SSD_INNER = 1024
SSD_STATE = 128
SSD_CONV_DIM = 1536
LRU_W = 1024
LRU_BLOCKS = 8
LRU_C = 8.0
ATTN_GROUPS = ((128, 1), (512, 4), (2048, 16))
HPG = 4
HD = 128
GW = HPG * HD
T5_BUCKETS = 32
T5_MAX_DIST = 2048
N_GROUPS = 4
PER_GROUP = 4
N_EXPERTS = 16
D_EXPERT = 512

TILE = 512
C_GATE = 0
C_Z = 6144
C_XR = 7168
C_GR = 8192
C_XBC = 9216
C_Q = 10752
C_K = 12288
C_V = 13824
C_DT = 15360
PW = 15872
N_GATE_TILES = (3 * D) // TILE
ROUTE_LANES = 128
VMEM_LIMIT = 56 * 1024 * 1024


def _cparams(sem):
    return pltpu.CompilerParams(dimension_semantics=sem, vmem_limit_bytes=VMEM_LIMIT)


def _full(shape):
    nd = len(shape)
    return pl.BlockSpec(shape, lambda *_: (0,) * nd)


def _of_layer(shape, layer):
    nd = len(shape)
    return pl.BlockSpec((1,) + shape, lambda *_: (layer,) + (0,) * nd)


_W_IN_STARTS = (0, 512, 2576, 3088, 3600, 4112, 1024, 1536, 2048, 4624, 5136, 5648, 6160, 6672, 7184,
                7696, 8208, 8720, 2560)
DT_SHIFT = SSD_HEADS


def _prep_kernel(blk_ref, shift_ref, width_ref, wg_ref, wa_ref, wb_ref, o_ref):
    j = pl.program_id(1)

    @pl.when(j < N_GATE_TILES)
    def _():
        o_ref[0] = wg_ref[0].astype(BF16)

    @pl.when(j >= N_GATE_TILES)
    def _():
        a = wa_ref[0]
        shifted = jnp.concatenate([a[DT_SHIFT:, :], wb_ref[0]], axis=0)
        val = jnp.where(shift_ref[j] == 0, a, shifted)
        row = lax.broadcasted_iota(jnp.int32, val.shape, 0)
        o_ref[0] = jnp.where(row < width_ref[j], val, 0.0).T.astype(BF16)


def _prep_w(w_gate, w_in):
    starts = (0,) * N_GATE_TILES + _W_IN_STARTS
    blk = jnp.asarray([s // TILE for s in starts], jnp.int32)
    shift = jnp.asarray([s % TILE for s in starts], jnp.int32)
    assert all(s % TILE in (0, DT_SHIFT) for s in starts) and w_in.shape[2] % DT_SHIFT == 0
    width = jnp.asarray([TILE] * (len(starts) - 1) + [SSD_HEADS], jnp.int32)
    per_tile = TILE // DT_SHIFT
    w_in_t = jnp.swapaxes(w_in, 1, 2)
    return pl.pallas_call(
        _prep_kernel,
        grid_spec=pltpu.PrefetchScalarGridSpec(
            num_scalar_prefetch=3,
            grid=(DEPTH, PW // TILE),
            in_specs=[pl.BlockSpec((1, D, TILE), lambda l, j, b, s, w: (l, 0, jnp.minimum(j, N_GATE_TILES - 1))),
                      pl.BlockSpec((1, TILE, D), lambda l, j, b, s, w: (l, b[j], 0)),
                      pl.BlockSpec((1, DT_SHIFT, D), lambda l, j, b, s, w: (l, (b[j] + 1) * per_tile, 0))],
            out_specs=pl.BlockSpec((1, D, TILE), lambda l, j, b, s, w: (l, 0, j)),
        ),
        out_shape=jax.ShapeDtypeStruct((DEPTH, D, PW), BF16),
        compiler_params=_cparams(("parallel", "arbitrary")),
        name="prep_w",
    )(blk, shift, width, w_gate, w_in_t, w_in_t)


def _proj_kernel(x_ref, nw_ref, w_ref, b_ref, o_ref, *rest, tm, phase_major):
    if phase_major:
        qkv_refs, (h_ref, acc_ref, ph_ref) = rest[:9], rest[9:]
    else:
        h_ref, acc_ref = rest
    j = pl.program_id(1)

    @pl.when(j == 0)
    def _():
        x = x_ref[...]
        ms = jnp.mean(x * x, axis=-1, keepdims=True)
        h_ref[...] = (x * lax.rsqrt(ms + EPS) * nw_ref[...]).astype(BF16)
        acc_ref[...] = jnp.zeros(acc_ref.shape, F32)

    prev = acc_ref[...]
    o_ref[...] = jnp.where(j <= N_GATE_TILES, jax.nn.sigmoid(prev), prev)
    acc_ref[...] = jnp.dot(h_ref[...], w_ref[0], preferred_element_type=F32) + b_ref[...]

    if phase_major:
        for part in range(3):
            for g, (_, dil) in enumerate(ATTN_GROUPS):
                ref = qkv_refs[part * 3 + g]

                @pl.when(j - 1 == C_Q // TILE + part * 3 + g)
                def _(ref=ref, dil=dil):
                    if dil == 1:
                        ref[0, 0] = o_ref[...].astype(BF16)
                    else:
                        for c in range(TILE // 128):
                            ph_ref[c] = o_ref[:, c * 128:(c + 1) * 128]
                        for p in range(dil):
                            for c in range(TILE // 128):
                                ref[0, p, :, c * 128:(c + 1) * 128] = (
                                    ph_ref[c, pl.ds(p, tm // dil, stride=dil), :].astype(BF16))


def _proj(x, nw, w_all, b_all, layer, tm, nb=None, t=None):
    n = x.shape[0]
    phase_major = nb is not None
    nt = PW // TILE
    out_specs = [pl.BlockSpec((tm, TILE), lambda i, j: (i, jnp.maximum(j - 1, 0)))]
    out_shape = [jax.ShapeDtypeStruct((n, PW), F32)]
    scratch = [pltpu.VMEM((tm, D), BF16), pltpu.VMEM((tm, TILE), F32)]
    if phase_major:
        tpb = t // tm
        for _ in range(3):
            for _, dil in ATTN_GROUPS:
                out_specs.append(pl.BlockSpec((1, dil, tm // dil, GW), lambda i, j: (i // tpb, 0, i % tpb, 0)))
                out_shape.append(jax.ShapeDtypeStruct((nb, dil, t // dil, GW), BF16))
        scratch.append(pltpu.VMEM((TILE // 128, tm, 128), F32))
    return pl.pallas_call(
        functools.partial(_proj_kernel, tm=tm, phase_major=phase_major),
        grid=(n // tm, nt + 1),
        in_specs=[pl.BlockSpec((tm, D), lambda i, j: (i, 0)),
                  pl.BlockSpec((1, D), lambda i, j: (0, 0)),
                  pl.BlockSpec((1, D, TILE), lambda i, j: (layer, 0, jnp.minimum(j, nt - 1))),
                  pl.BlockSpec((1, TILE), lambda i, j: (0, jnp.minimum(j, nt - 1)))],
        out_specs=out_specs,
        out_shape=out_shape,
        scratch_shapes=scratch,
        compiler_params=_cparams(("parallel", "arbitrary")),
        name="proj",
    )(x, nw, w_all, b_all)


def _conv_step(x_ref, xp_ref, cw_ref, cb_ref, rows, out_rows):
    xp_ref[8:8 + rows, :] = _bf16_round(x_ref[...])
    cw = _bf16_round(cw_ref[...])
    acc = cw[0:1, :] * xp_ref[5:5 + out_rows, :]
    for j in range(1, CONV_W):
        acc = acc + cw[j:j + 1, :] * xp_ref[5 + j:5 + j + out_rows, :]
    tail = xp_ref[rows:rows + 8, :]
    xp_ref[0:8, :] = tail
    return acc + cb_ref[...]


def _bf16_round(x):
    return x.astype(BF16).astype(F32)


def _softplus(x):
    return jnp.maximum(x, 0.0) + jnp.log1p(jnp.exp(-jnp.abs(x)))


def _ssd_part(part, z_ref, xbc_ref, dt_ref, tail_ref, h0_ref, cw_ref, cb_ref, dtb_ref, alog_ref, dsk_ref, nw_ref,
              y_ref, hf_ref, xp_ref, act_ref, st_ref, ysc_ref, *, rows):
    if part == 'init':
        xp_ref[0:8, :] = _bf16_round(tail_ref[0])
        st_ref[...] = h0_ref[0]
        return
    if part == 'final':
        hf_ref[0] = st_ref[...]
        return

    if rows < CL:
        xp_ref[8 + rows:, :] = jnp.zeros((CL - rows, SSD_CONV_DIM), F32)
    conv = _conv_step(xbc_ref, xp_ref, cw_ref, cb_ref, rows, CL)
    act_ref[...] = conv * jax.nn.sigmoid(conv)

    row = lax.broadcasted_iota(jnp.int32, (CL, 128), 0)
    lane = lax.broadcasted_iota(jnp.int32, (CL, 128), 1)
    raw = dt_ref[...]
    if rows < CL:
        raw = jnp.concatenate([raw, jnp.zeros((CL - rows, 128), F32)], axis=0)
    dt = _softplus(raw + dtb_ref[...])
    dt = jnp.where((lane < SSD_HEADS) & (row < rows), dt, 0.0)
    da = dt * (-jnp.exp(alog_ref[...]))
    acs = da
    d = 1
    while d < CL:
        acs = acs + jnp.where(row >= d, pltpu.roll(acs, d, 0), 0.0)
        d *= 2
    acs_t = acs.T
    last = acs[CL - 1:CL, :]
    e_acs = jnp.exp(acs)
    to_end = jnp.exp(last - acs)
    cdec = jnp.exp(last)
    causal = row >= lane
    lo_lane = lane < SSD_HEAD_DIM
    lo_row = row < SSD_HEAD_DIM
    dsk = dsk_ref[...]

    def pair_cols(arr, h):
        return jnp.where(lo_lane, arr[:, h:h + 1], arr[:, h + 1:h + 2])

    nt = (((1,), (1,)), ((), ()))
    for g in range(2):
        bm = act_ref[:, SSD_INNER + g * SSD_STATE:SSD_INNER + (g + 1) * SSD_STATE].astype(BF16)
        cm = act_ref[:, SSD_INNER + 256 + g * SSD_STATE:SSD_INNER + 256 + (g + 1) * SSD_STATE].astype(BF16)
        cb = lax.dot_general(cm, bm, nt, preferred_element_type=F32)
        for pp in range(4):
            h = g * 8 + 2 * pp
            sl = slice(h * SSD_HEAD_DIM, h * SSD_HEAD_DIM + 128)
            xs = act_ref[:, sl]
            xdt = xs * pair_cols(dt, h)
            xdt_b = xdt.astype(BF16)
            ys = []
            for hh in (h, h + 1):
                seg = acs[:, hh:hh + 1] - acs_t[hh:hh + 1, :]
                decay = jnp.exp(jnp.where(causal, seg, -jnp.inf))
                ys.append(jnp.dot((cb * decay).astype(BF16), xdt_b, preferred_element_type=F32))
            y_diag = jnp.where(lo_lane, ys[0], ys[1])
            st = st_ref[sl, :]
            y_off = lax.dot_general(cm, st.astype(BF16), nt, preferred_element_type=F32) * pair_cols(e_acs, h)
            d_pair = jnp.where(lo_lane, dsk[:, h:h + 1], dsk[:, h + 1:h + 2])
            ysc_ref[:, sl] = y_diag + y_off + d_pair * xs
            xdte_t = (xdt * pair_cols(to_end, h)).T.astype(BF16)
            s_new = jnp.dot(xdte_t, bm, preferred_element_type=F32)
            dec = jnp.where(lo_row, cdec[:, h:h + 1], cdec[:, h + 1:h + 2])
            st_ref[sl, :] = dec * st + s_new

    zz = z_ref[...]
    yg = ysc_ref[0:rows, :] * (zz * jax.nn.sigmoid(zz))
    gw = SSD_INNER // 2
    for g in range(2):
        half = yg[:, g * gw:(g + 1) * gw]
        ms = jnp.mean(half * half, axis=-1, keepdims=True)
        y_ref[:, g * gw:(g + 1) * gw] = (half * lax.rsqrt(ms + EPS)
                                         * nw_ref[:, g * gw:(g + 1) * gw]).astype(y_ref.dtype)


def _lru_part(part, xr_ref, gr_ref, tail_ref, h0_ref, cw_ref, cb_ref, wr_ref, br_ref, wi_ref, bi_ref, lam_ref,
              y_ref, hl_ref, xp_ref, h_ref, *, rows):
    if part == 'init':
        xp_ref[0:8, :] = _bf16_round(tail_ref[0])
        h_ref[...] = h0_ref[0]
        return
    if part == 'final':
        hl_ref[0] = h_ref[...]
        return

    x = _conv_step(xr_ref, xp_ref, cw_ref, cb_ref, rows, rows)
    xb = x.astype(BF16)
    rs, is_ = [], []
    for n in range(LRU_BLOCKS):
        blk = xb[:, n * 128:(n + 1) * 128]
        rs.append(jnp.dot(blk, wr_ref[0, n], preferred_element_type=F32))
        is_.append(jnp.dot(blk, wi_ref[0, n], preferred_element_type=F32))
    r_gate = jax.nn.sigmoid(jnp.concatenate(rs, axis=1) + br_ref[...])
    i_gate = jax.nn.sigmoid(jnp.concatenate(is_, axis=1) + bi_ref[...])
    log_a = -LRU_C * r_gate * _softplus(-lam_ref[...])
    a = jnp.exp(log_a)
    th = jnp.tanh(log_a)
    b = jnp.sqrt(-2.0 * th / (1.0 - th)) * (i_gate * x)
    in_group = lax.broadcasted_iota(jnp.int32, (rows, LRU_W), 0) % 8
    for d in (1, 2, 4):
        a_s = jnp.where(in_group >= d, pltpu.roll(a, d, 0), 1.0)
        b_s = jnp.where(in_group >= d, pltpu.roll(b, d, 0), 0.0)
        b = a * b_s + b
        a = a * a_s
    carry = h_ref[...]
    groups = []
    for g in range(rows // 8):
        h_g = b[8 * g:8 * g + 8, :] + a[8 * g:8 * g + 8, :] * carry
        groups.append(h_g)
        carry = h_g[7:8, :]
    h = jnp.concatenate(groups, axis=0) if len(groups) > 1 else groups[0]
    h_ref[...] = carry
    y_ref[...] = (h * jax.nn.gelu(gr_ref[...])).astype(y_ref.dtype)


N_SSD_IN, N_LRU_IN = 11, 11


def _mixers_kernel(*refs, rows, n_chunks):
    ins, rest = refs[:N_SSD_IN + N_LRU_IN], refs[N_SSD_IN + N_LRU_IN:]
    ssd = ins[:N_SSD_IN] + rest[0:2] + rest[4:8]
    lru = ins[N_SSD_IN:] + rest[2:4] + rest[8:10]
    c = pl.program_id(1)

    @pl.when(c == 0)
    def _():
        _ssd_part('init', *ssd, rows=rows)
        _lru_part('init', *lru, rows=rows)

    _ssd_part('main', *ssd, rows=rows)
    _lru_part('main', *lru, rows=rows)

    @pl.when(c == n_chunks - 1)
    def _():
        _ssd_part('final', *ssd, rows=rows)
        _lru_part('final', *lru, rows=rows)


def _mixers(P, ssd_tail, ssd_h0, lru_tail, lru_h0, lw, sw, layer, nb, t, rows):
    nc = t // rows
    y_dtype = BF16 if rows % 16 == 0 else F32
    tok = lambda w, col: pl.BlockSpec((rows, w), lambda b, c: (b * nc + c, col // w))
    seq = lambda *shape: pl.BlockSpec((1,) + shape, lambda b, c: (b,) + (0,) * len(shape))
    return pl.pallas_call(
        functools.partial(_mixers_kernel, rows=rows, n_chunks=nc),
        grid=(nb, nc),
        in_specs=[tok(SSD_INNER, C_Z), tok(SSD_CONV_DIM, C_XBC), tok(128, C_DT),
                  seq(8, SSD_CONV_DIM), seq(SSD_INNER, SSD_STATE),
                  _full((CONV_W, SSD_CONV_DIM)), _full((1, SSD_CONV_DIM)),
                  _full((1, 128)), _full((1, 128)), _full((1, 128)), _full((1, SSD_INNER)),
                  tok(LRU_W, C_XR), tok(LRU_W, C_GR), seq(8, LRU_W), seq(1, LRU_W),
                  _full((CONV_W, LRU_W)), _full((1, LRU_W)),
                  _of_layer((LRU_BLOCKS, 128, 128), layer), _full((1, LRU_W)),
                  _of_layer((LRU_BLOCKS, 128, 128), layer), _full((1, LRU_W)), _full((1, LRU_W))],
        out_specs=[tok(SSD_INNER, 0), seq(SSD_INNER, SSD_STATE), tok(LRU_W, 0), seq(1, LRU_W)],
        out_shape=[jax.ShapeDtypeStruct((nb * t, SSD_INNER), y_dtype),
                   jax.ShapeDtypeStruct((nb, SSD_INNER, SSD_STATE), F32),
                   jax.ShapeDtypeStruct((nb * t, LRU_W), y_dtype),
                   jax.ShapeDtypeStruct((nb, 1, LRU_W), F32)],
        scratch_shapes=[pltpu.VMEM((8 + CL, SSD_CONV_DIM), F32), pltpu.VMEM((CL, SSD_CONV_DIM), F32),
                        pltpu.VMEM((SSD_INNER, SSD_STATE), F32), pltpu.VMEM((CL, SSD_INNER), F32),
                        pltpu.VMEM((8 + rows, LRU_W), F32), pltpu.VMEM((1, LRU_W), F32)],
        compiler_params=_cparams(("parallel", "arbitrary")),
        name="mixers",
    )(P, P, P, ssd_tail, ssd_h0, lw['conv_ssd_w'], lw['conv_ssd_b'], lw['dt_bias'], lw['a_log'], lw['d_skip'],
      lw['ssd_norm_w'], P, P, lru_tail, lru_h0, lw['conv_lru_w'], lw['conv_lru_b'], sw['lru_wr'], lw['lru_br'],
      sw['lru_wi'], lw['lru_bi'], lw['lru_lambda'])


def _attn_kernel(q_ref, kp_ref, vp_ref, kc_ref, vc_ref, bias_ref, o_ref, lse_ref, *, sub, phases):
    scale = HD ** -0.5
    nt = (((1,), (1,)), ((), ()))
    lane = lax.broadcasted_iota(jnp.int32, (CL, 128), 1)
    key = lax.broadcasted_iota(jnp.int32, (CL, 2 * CL), 1)
    first_ok = (pl.program_id(2) > 0) | (key >= CL)
    for z in range(phases):
        for s in range(sub):
            rows = slice(s * CL, (s + 1) * CL)
            both = slice((s - 1) * CL, (s + 1) * CL)
            lse_all = jnp.zeros((CL, 128), F32)
            for h in range(HPG):
                sl = slice(h * HD, (h + 1) * HD)
                q = q_ref[0, z, rows, sl]
                if s == 0:
                    kk = jnp.concatenate([kp_ref[0, z, :, sl], kc_ref[0, z, rows, sl]], axis=0)
                    vv = jnp.concatenate([vp_ref[0, z, :, sl], vc_ref[0, z, rows, sl]], axis=0)
                else:
                    kk, vv = kc_ref[0, z, both, sl], vc_ref[0, z, both, sl]
                sc = lax.dot_general(q, kk, nt, preferred_element_type=F32) * scale + bias_ref[h]
                if s == 0:
                    sc = jnp.where(first_ok, sc, NEG)
                m = jnp.max(sc, axis=-1, keepdims=True)
                p = jnp.exp(sc - m)
                l = jnp.sum(p, axis=-1, keepdims=True)
                o_ref[rows, z * GW + h * HD:z * GW + (h + 1) * HD] = jnp.dot(
                    (p * (1.0 / l)).astype(BF16), vv, preferred_element_type=F32)
                lse_all = jnp.where(lane == h, m + jnp.log(l), lse_all)
            lse_ref[rows, z * 128:(z + 1) * 128] = lse_all


def _attn_prompt(q, k, v, bias, dil, nb, t):
    n = nb * t
    blocks = 8
    sub = min(blocks, t // dil // CL)
    phases = min(blocks // sub, dil)
    nstep = t // dil // (sub * CL)
    cur = pl.BlockSpec((1, phases, sub * CL, GW), lambda b, p, i: (b, p, i, 0))
    prev = pl.BlockSpec((1, phases, CL, GW), lambda b, p, i: (b, p, jnp.maximum(i * sub - 1, 0), 0))
    return pl.pallas_call(
        functools.partial(_attn_kernel, sub=sub, phases=phases),
        grid=(nb, dil // phases, nstep),
        in_specs=[cur, prev, prev, cur, cur, _full((HPG, CL, 2 * CL))],
        out_specs=[pl.BlockSpec((sub * CL, phases * GW), lambda b, p, i: (b * nstep + i, p)),
                   pl.BlockSpec((sub * CL, phases * 128), lambda b, p, i: (b * nstep + i, p))],
        out_shape=[jax.ShapeDtypeStruct((n // dil, dil * GW), F32),
                   jax.ShapeDtypeStruct((n // dil, dil * 128), F32)],
        compiler_params=_cparams(("parallel", "parallel", "arbitrary")),
        name=f"attn_prompt_d{dil}",
    )(q, k, v, k, v, bias)


def _attn_sample_kernel(q_ref, kv_ref, kb_ref, vb_ref, ba_ref, bb_ref, o_ref, lse_ref):
    scale = HD ** -0.5
    nt = (((1,), (1,)), ((), ()))
    rows = o_ref.shape[0]
    window = kv_ref.shape[0] // (2 * HPG)
    lane = lax.broadcasted_iota(jnp.int32, (rows, 128), 1)
    lse_all = jnp.zeros((rows, 128), F32)
    for h in range(HPG):
        sl = slice(h * HD, (h + 1) * HD)
        q = q_ref[:, sl].astype(BF16)
        k_cache = kv_ref[pl.ds(h, window, stride=2 * HPG), :].astype(BF16)
        v_cache = kv_ref[pl.ds(HPG + h, window, stride=2 * HPG), :].astype(BF16)
        sa = lax.dot_general(q, k_cache, nt, preferred_element_type=F32) * scale + ba_ref[h]
        sb = lax.dot_general(q, kb_ref[:, sl].astype(BF16), nt, preferred_element_type=F32) * scale + bb_ref[h]
        m = jnp.maximum(jnp.max(sa, axis=-1, keepdims=True), jnp.max(sb, axis=-1, keepdims=True))
        pa = jnp.exp(sa - m)
        pb = jnp.exp(sb - m)
        l = jnp.sum(pa, axis=-1, keepdims=True) + jnp.sum(pb, axis=-1, keepdims=True)
        inv = 1.0 / l
        o_ref[:, sl] = (jnp.dot((pa * inv).astype(BF16), v_cache, preferred_element_type=F32)
                        + jnp.dot((pb * inv).astype(BF16), vb_ref[:, sl].astype(BF16), preferred_element_type=F32))
        lse_all = jnp.where(lane == h, m + jnp.log(l), lse_all)
    lse_ref[...] = lse_all


def _attn_sample(Ps, cache_rows, bias_a, bias_b, g, layer, nb, t, window):
    tq, tk, tv = C_Q // GW + g, C_K // GW + g, C_V // GW + g
    out_spec = pl.BlockSpec((t, GW), lambda b: (b, 0))
    return pl.pallas_call(
        _attn_sample_kernel,
        grid=(nb,),
        in_specs=[pl.BlockSpec((t, GW), lambda b: (b, tq)),
                  pl.BlockSpec((window * 2 * HPG, HD), lambda b: (layer * nb + b, 0)),
                  pl.BlockSpec((t, GW), lambda b: (b, tk)),
                  pl.BlockSpec((t, GW), lambda b: (b, tv)),
                  _full((HPG, t, window)), _full((HPG, t, t))],
        out_specs=[out_spec, pl.BlockSpec((t, 128), lambda b: (b, 0))],
        out_shape=[jax.ShapeDtypeStruct((nb * t, GW), F32), jax.ShapeDtypeStruct((nb * t, 128), F32)],
        compiler_params=_cparams(("parallel",)),
        name=f"attn_sample_w{window}",
    )(Ps, cache_rows, Ps, Ps, bias_a, bias_b)


def _mix_kernel(gs_ref, gl_ref, ga_ref, ys_ref, yl_ref, o0_ref, o1_ref, o2_ref, l0_ref, l1_ref, l2_ref,
                wbs_ref, wbl_ref, wba_ref, out_ref, *scratch, dils):
    tm = out_ref.shape[0]
    o_heads, lses = [], []
    scratch = list(scratch)
    for o_ref, l_ref, dil in zip((o0_ref, o1_ref, o2_ref), (l0_ref, l1_ref, l2_ref), dils):
        if dil == 1:
            o_heads.append([o_ref[:, h * HD:(h + 1) * HD] for h in range(HPG)])
            lses.append(l_ref[...])
            continue
        o_scr, l_scr = scratch.pop(0), scratch.pop(0)
        for p in range(dil):
            rows = pl.ds(p, tm // dil, stride=dil)
            for h in range(HPG):
                o_scr[h, rows, :] = o_ref[:, p * GW + h * HD:p * GW + (h + 1) * HD]
            l_scr[rows, :] = l_ref[:, p * 128:(p + 1) * 128]
        o_heads.append([o_scr[h] for h in range(HPG)])
        lses.append(l_scr[...])
    l0, l1, l2 = lses
    m = jnp.maximum(jnp.maximum(l0, l1), l2)
    e0, e1, e2 = jnp.exp(l0 - m), jnp.exp(l1 - m), jnp.exp(l2 - m)
    den = e0 + e1 + e2
    w0, w1, w2 = e0 / den, e1 / den, e2 / den
    heads = []
    for h in range(HPG):
        per_head = lambda w: jnp.broadcast_to(w[:, h:h + 1], (tm, HD))
        heads.append(o_heads[0][h] * per_head(w0) + o_heads[1][h] * per_head(w1) + o_heads[2][h] * per_head(w2))
    ya = jnp.concatenate(heads, axis=1)
    mixed = (gs_ref[...] * jnp.dot(ys_ref[...].astype(BF16), wbs_ref[0], preferred_element_type=F32)
             + gl_ref[...] * jnp.dot(yl_ref[...].astype(BF16), wbl_ref[0], preferred_element_type=F32)
             + ga_ref[...] * jnp.dot(ya.astype(BF16), wba_ref[0], preferred_element_type=F32))
    out_ref[...] = mixed.astype(BF16)


def _mix(P, y_ssd, y_lru, attn, dils, sw, layer, tm):
    n = P.shape[0]
    row = lambda w: pl.BlockSpec((tm, w), lambda i: (i, 0))
    phased = lambda w, d: pl.BlockSpec((tm // d, d * w), lambda i: (i, 0))
    (o0, s0), (o1, s1), (o2, s2) = attn
    scratch = []
    for d in dils:
        if d > 1:
            scratch += [pltpu.VMEM((HPG, tm, HD), F32), pltpu.VMEM((tm, 128), F32)]
    return pl.pallas_call(
        functools.partial(_mix_kernel, dils=dils),
        grid=(n // tm,),
        in_specs=[pl.BlockSpec((tm, D), lambda i: (i, 0)), pl.BlockSpec((tm, D), lambda i: (i, 1)),
                  pl.BlockSpec((tm, D), lambda i: (i, 2)),
                  row(SSD_INNER), row(LRU_W)] + [phased(GW, d) for d in dils] + [phased(128, d) for d in dils] + [
                  _of_layer((SSD_INNER, D), layer), _of_layer((LRU_W, D), layer), _of_layer((GW, D), layer)],
        out_specs=row(D),
        out_shape=jax.ShapeDtypeStruct((n, D), BF16),
        scratch_shapes=scratch,
        compiler_params=_cparams(("parallel",)),
        name="mix",
    )(P, P, P, y_ssd, y_lru, o0, o1, o2, s0, s1, s2, sw['w_br_ssd'], sw['w_br_lru'], sw['w_br_attn'])


def _res_kernel(x_ref, mixed_ref, wo_ref, n2_ref, wr_ref, br_ref, x1_ref, h2_ref, comb_ref, *rest, dispatch):
    x1 = x_ref[...] + jnp.dot(mixed_ref[...], wo_ref[0], preferred_element_type=F32)
    x1_ref[...] = x1
    ms = jnp.mean(x1 * x1, axis=-1, keepdims=True)
    h2 = x1 * lax.rsqrt(ms + EPS) * n2_ref[...]
    h2b = h2.astype(BF16)
    if dispatch:
        lo = pltpu.bitcast(h2b[:, :HALF].astype(F32), jnp.uint32)
        hi = pltpu.bitcast(h2b[:, HALF:].astype(F32), jnp.uint32)
        h2_ref[...] = pltpu.bitcast(lax.shift_right_logical(lo, jnp.uint32(16)) | hi, jnp.int32)
    else:
        h2_ref[...] = h2b
    logits = jnp.dot(h2b, wr_ref[...], preferred_element_type=F32) + br_ref[...]
    lane = lax.broadcasted_iota(jnp.int32, logits.shape, 1).astype(F32)
    big = float(ROUTE_LANES)

    def first_max(vals, ok):
        v = jnp.where(ok, vals, NEG)
        top = jnp.max(v, axis=-1, keepdims=True)
        idx = jnp.min(jnp.where(ok & (v == top), lane, big), axis=-1, keepdims=True)
        return top, idx

    is_g = lane < N_GROUPS
    gmax, gsel = first_max(logits, is_g)
    gp = 1.0 / jnp.sum(jnp.where(is_g, jnp.exp(logits - gmax), 0.0), axis=-1, keepdims=True)
    lo = N_GROUPS + PER_GROUP * gsel
    is_e = (lane >= lo) & (lane < lo + PER_GROUP)
    t1, i1 = first_max(logits, is_e)
    t2, i2 = first_max(logits, is_e & (lane != i1))
    e2 = jnp.exp(t2 - t1)
    w1 = gp / (1.0 + e2)
    w2 = gp * e2 / (1.0 + e2)
    comb_ref[...] = jnp.where(lane == i1, w1, 0.0) + jnp.where(lane == i2, w2, 0.0)

    if dispatch:
        route_ref, cnt_ref, wrow1_ref, wrow2_ref, carry_ref = rest

        @pl.when(pl.program_id(0) == 0)
        def _():
            carry_ref[...] = jnp.zeros(carry_ref.shape, F32)

        tm = x1.shape[0]
        onehot = jnp.where((lane == i1) | (lane == i2), 1.0, 0.0)
        r = lax.broadcasted_iota(jnp.int32, (tm, tm), 0)
        c = lax.broadcasted_iota(jnp.int32, (tm, tm), 1)
        earlier = jnp.where(r > c, 1.0, 0.0).astype(BF16)
        before = jnp.dot(earlier, onehot.astype(BF16), preferred_element_type=F32) + carry_ref[...]
        rank1 = jnp.sum(jnp.where(lane == i1, before, 0.0), axis=-1, keepdims=True)
        rank2 = jnp.sum(jnp.where(lane == i2, before, 0.0), axis=-1, keepdims=True)
        carry_ref[...] += jnp.sum(onehot, axis=0, keepdims=True)
        cnt_ref[...] = carry_ref[...]
        fields = (i1 - N_GROUPS, i2 - N_GROUPS, rank1, rank2)
        route = jnp.zeros(logits.shape, F32)
        for k, val in enumerate(fields):
            route = jnp.where(lane == k, val, route)
        route_ref[...] = route.T[0:8, :]
        wrow1_ref[...] = jnp.broadcast_to(w1, wrow1_ref.shape)
        wrow2_ref[...] = jnp.broadcast_to(w2, wrow2_ref.shape)


def _res(x, mixed, lw, sw, layer, tm, dispatch):
    n = x.shape[0]
    h2_cols, h2_dtype = (HALF, jnp.int32) if dispatch else (D, BF16)
    out_specs = [pl.BlockSpec((tm, D), lambda i: (i, 0)), pl.BlockSpec((tm, h2_cols), lambda i: (i, 0)),
                 pl.BlockSpec((tm, ROUTE_LANES), lambda i: (i, 0))]
    out_shape = [jax.ShapeDtypeStruct((n, D), F32), jax.ShapeDtypeStruct((n, h2_cols), h2_dtype),
                 jax.ShapeDtypeStruct((n, ROUTE_LANES), F32)]
    scratch = []
    if dispatch:
        wide = pl.BlockSpec((tm, 128), lambda i: (i, 0))
        out_specs += [pl.BlockSpec((8, tm), lambda i: (0, i)), _full((1, ROUTE_LANES)), wide, wide]
        out_shape += [jax.ShapeDtypeStruct((8, n), F32), jax.ShapeDtypeStruct((1, ROUTE_LANES), F32),
                      jax.ShapeDtypeStruct((n, 128), F32), jax.ShapeDtypeStruct((n, 128), F32)]
        scratch = [pltpu.VMEM((1, ROUTE_LANES), F32)]
    return pl.pallas_call(
        functools.partial(_res_kernel, dispatch=dispatch),
        grid=(n // tm,),
        in_specs=[pl.BlockSpec((tm, D), lambda i: (i, 0)), pl.BlockSpec((tm, D), lambda i: (i, 0)),
                  _of_layer((D, D), layer), _full((1, D)), _full((D, ROUTE_LANES)), _full((1, ROUTE_LANES))],
        out_specs=out_specs,
        out_shape=out_shape,
        scratch_shapes=scratch,
        compiler_params=_cparams(("arbitrary",)),
        name="res_router",
    )(x, mixed, sw['w_o'], lw['norm2'], lw['w_router'], lw['b_router'])


def _moe_kernel(h2_ref, comb_ref, x1_ref, w1_ref, w3_ref, w2_ref, fn_ref, o_ref, *, final):
    e = pl.program_id(1)

    @pl.when(e == 0)
    def _():
        o_ref[...] = x1_ref[...]

    h = h2_ref[...]
    a = jnp.dot(h, w1_ref[0, 0].astype(BF16), preferred_element_type=F32)
    b = jnp.dot(h, w3_ref[0, 0].astype(BF16), preferred_element_type=F32)
    comb = comb_ref[...]
    lane = lax.broadcasted_iota(jnp.int32, comb.shape, 1)
    w = jnp.sum(jnp.where(lane == e + N_GROUPS, comb, 0.0), axis=-1, keepdims=True)
    act = (a * jax.nn.sigmoid(a)) * b * w
    o_ref[...] += jnp.dot(act.astype(BF16), w2_ref[0, 0].astype(BF16), preferred_element_type=F32)

    if final:
        @pl.when(e == N_EXPERTS - 1)
        def _():
            x = o_ref[...]
            ms = jnp.mean(x * x, axis=-1, keepdims=True)
            o_ref[...] = x * lax.rsqrt(ms + EPS) * fn_ref[...]


def _moe(h2, comb, x1, sw, layer, final_norm, tm, final):
    n = x1.shape[0]
    return pl.pallas_call(
        functools.partial(_moe_kernel, final=final),
        grid=(n // tm, N_EXPERTS),
        in_specs=[pl.BlockSpec((tm, D), lambda i, e: (i, 0)),
                  pl.BlockSpec((tm, ROUTE_LANES), lambda i, e: (i, 0)),
                  pl.BlockSpec((tm, D), lambda i, e: (i, 0)),
                  pl.BlockSpec((1, 1, D, D_EXPERT), lambda i, e: (layer, e, 0, 0)),
                  pl.BlockSpec((1, 1, D, D_EXPERT), lambda i, e: (layer, e, 0, 0)),
                  pl.BlockSpec((1, 1, D_EXPERT, D), lambda i, e: (layer, e, 0, 0)),
                  pl.BlockSpec((1, D), lambda i, e: (0, 0))],
        out_specs=pl.BlockSpec((tm, D), lambda i, e: (i, 0)),
        out_shape=jax.ShapeDtypeStruct((n, D), F32),
        compiler_params=_cparams(("parallel", "arbitrary")),
        name="moe",
    )(h2, comb, x1, sw['w1'], sw['w3'], sw['w2'], final_norm)


FFN_TM = 512
SC_CORES = 2
SC_SUBCORES = 16
SC_WORKERS = SC_CORES * SC_SUBCORES
SC_CHUNK = 32
HALF = D // 2


def _sc_mesh():
    return plsc.VectorSubcoreMesh(core_axis_name="c", subcore_axis_name="s", num_cores=SC_CORES,
                                  num_subcores=SC_SUBCORES)


def _sc_dispatch(x, wrow1, wrow2, dest1, dest2, n_sorted):
    n = x.shape[0]
    per_w = n // SC_WORKERS

    @functools.partial(
        pl.kernel, mesh=_sc_mesh(),
        out_type=(jax.ShapeDtypeStruct((n_sorted, HALF), jnp.int32), jax.ShapeDtypeStruct((n_sorted, 128), F32)),
        scratch_types=[pltpu.VMEM((SC_CHUNK,), jnp.int32), pltpu.VMEM((SC_CHUNK,), jnp.int32),
                       pltpu.VMEM((SC_CHUNK, HALF), jnp.int32), pltpu.VMEM((SC_CHUNK, 128), F32)],
    )
    def k(x_hbm, w1_hbm, w2_hbm, d1_hbm, d2_hbm, out_hbm, wout_hbm, i1_v, i2_v, rows_v, wrows_v):
        base = (lax.axis_index("s") * SC_CORES + lax.axis_index("c")) * per_w

        @pl.loop(0, per_w // SC_CHUNK)
        def _(j):
            off = base + j * SC_CHUNK
            pltpu.sync_copy(d1_hbm.at[pl.ds(off, SC_CHUNK)], i1_v)
            pltpu.sync_copy(d2_hbm.at[pl.ds(off, SC_CHUNK)], i2_v)
            pltpu.sync_copy(x_hbm.at[pl.ds(off, SC_CHUNK)], rows_v)
            pltpu.sync_copy(rows_v, out_hbm.at[i1_v])
            pltpu.sync_copy(rows_v, out_hbm.at[i2_v])
            pltpu.sync_copy(w1_hbm.at[pl.ds(off, SC_CHUNK)], wrows_v)
            pltpu.sync_copy(wrows_v, wout_hbm.at[i1_v])
            pltpu.sync_copy(w2_hbm.at[pl.ds(off, SC_CHUNK)], wrows_v)
            pltpu.sync_copy(wrows_v, wout_hbm.at[i2_v])

    return k(x, wrow1, wrow2, dest1, dest2)


def _sc_collect(y, dest1, dest2, n):
    per_w = n // SC_WORKERS

    @functools.partial(
        pl.kernel, mesh=_sc_mesh(),
        out_type=(jax.ShapeDtypeStruct((n, D), F32), jax.ShapeDtypeStruct((n, D), F32)),
        scratch_types=[pltpu.VMEM((SC_CHUNK,), jnp.int32), pltpu.VMEM((SC_CHUNK, D), F32)],
    )
    def k(y_hbm, d1_hbm, d2_hbm, g1_hbm, g2_hbm, idx_v, rows_v):
        base = (lax.axis_index("s") * SC_CORES + lax.axis_index("c")) * per_w

        @pl.loop(0, per_w // SC_CHUNK)
        def _(j):
            off = base + j * SC_CHUNK
            for d_hbm, g_hbm in ((d1_hbm, g1_hbm), (d2_hbm, g2_hbm)):
                pltpu.sync_copy(d_hbm.at[pl.ds(off, SC_CHUNK)], idx_v)
                pltpu.sync_copy(y_hbm.at[idx_v], rows_v)
                pltpu.sync_copy(rows_v, g_hbm.at[pl.ds(off, SC_CHUNK)])

    return k(y, dest1, dest2)


def _ffn_kernel(te_ref, nt_ref, xs_ref, ws_ref, w1_ref, w3_ref, w2_ref, y_ref, w1b_ref, w3b_ref, w2b_ref):
    k = pl.program_id(0)

    @pl.when(k < nt_ref[0])
    def _():
        @pl.when((k == 0) | (te_ref[k] != te_ref[jnp.maximum(k - 1, 0)]))
        def _():
            w1b_ref[...] = w1_ref[0, 0].astype(BF16)
            w3b_ref[...] = w3_ref[0, 0].astype(BF16)
            w2b_ref[...] = w2_ref[0, 0].astype(BF16)

        words = pltpu.bitcast(xs_ref[...], jnp.uint32)
        lo = pltpu.bitcast(lax.shift_left(words, jnp.uint32(16)), F32)
        hi = pltpu.bitcast(words & jnp.uint32(0xFFFF0000), F32)
        xs = jnp.concatenate([lo, hi], axis=1).astype(BF16)
        a = jnp.dot(xs, w1b_ref[...], preferred_element_type=F32)
        b = jnp.dot(xs, w3b_ref[...], preferred_element_type=F32)
        act = (a * jax.nn.sigmoid(a)) * b * ws_ref[:, 0:1]
        y_ref[...] = jnp.dot(act.astype(BF16), w2b_ref[...], preferred_element_type=F32)


def _ffn(xs, ws, tile_expert, n_tiles_used, sw, layer):
    n_sorted = xs.shape[0]
    w_in_spec = pl.BlockSpec((1, 1, D, D_EXPERT), lambda k, te, nt: (layer, te[k], 0, 0))
    return pl.pallas_call(
        _ffn_kernel,
        grid_spec=pltpu.PrefetchScalarGridSpec(
            num_scalar_prefetch=2,
            grid=(n_sorted // FFN_TM,),
            in_specs=[pl.BlockSpec((FFN_TM, HALF), lambda k, te, nt: (k, 0)),
                      pl.BlockSpec((FFN_TM, 128), lambda k, te, nt: (k, 0)), w_in_spec, w_in_spec,
                      pl.BlockSpec((1, 1, D_EXPERT, D), lambda k, te, nt: (layer, te[k], 0, 0))],
            out_specs=pl.BlockSpec((FFN_TM, D), lambda k, te, nt: (k, 0)),
            scratch_shapes=[pltpu.VMEM((D, D_EXPERT), BF16), pltpu.VMEM((D, D_EXPERT), BF16),
                            pltpu.VMEM((D_EXPERT, D), BF16)],
        ),
        out_shape=jax.ShapeDtypeStruct((n_sorted, D), F32),
        compiler_params=_cparams(("arbitrary",)),
        name="ffn",
    )(tile_expert, n_tiles_used, xs, ws, sw['w1'], sw['w3'], sw['w2'])


def _combine_kernel(x1_ref, g1_ref, g2_ref, fn_ref, o_ref, *, final):
    x = x1_ref[...] + (g1_ref[...] + g2_ref[...])
    if final:
        ms = jnp.mean(x * x, axis=-1, keepdims=True)
        x = x * lax.rsqrt(ms + EPS) * fn_ref[...]
    o_ref[...] = x


def _combine(x1, g1, g2, final_norm, tm, final):
    n = x1.shape[0]
    row = pl.BlockSpec((tm, D), lambda i: (i, 0))
    return pl.pallas_call(
        functools.partial(_combine_kernel, final=final),
        grid=(n // tm,),
        in_specs=[row, row, row, _full((1, D))],
        out_specs=row,
        out_shape=jax.ShapeDtypeStruct((n, D), F32),
        compiler_params=_cparams(("parallel",)),
        name="combine",
    )(x1, g1, g2, final_norm)


def _dest_kernel(seg_ref, route_ref, o_ref):
    route = route_ref[...]
    for s in range(2):
        expert, rank = route[s:s + 1, :], route[2 + s:3 + s, :]
        start = jnp.zeros(expert.shape, jnp.int32)
        for e in range(N_EXPERTS):
            start = jnp.where(expert == float(e), seg_ref[e], start)
        o_ref[s:s + 1, :] = start + rank.astype(jnp.int32)


def _dest(route_t, seg_start, tn):
    n = route_t.shape[1]
    return pl.pallas_call(
        _dest_kernel,
        grid_spec=pltpu.PrefetchScalarGridSpec(
            num_scalar_prefetch=1,
            grid=(n // tn,),
            in_specs=[pl.BlockSpec((8, tn), lambda i, seg: (0, i))],
            out_specs=pl.BlockSpec((2, tn), lambda i, seg: (0, i)),
        ),
        out_shape=jax.ShapeDtypeStruct((2, n), jnp.int32),
        compiler_params=_cparams(("parallel",)),
        name="dest",
    )(seg_start, route_t)


def _after(values, others):
    values, _ = lax.optimization_barrier((values, others))
    return values


def _moe_sparse(h2, route_t, counts, wrow1, wrow2, x1, sw, layer, final_norm, final, during_dispatch,
                during_collect):
    n = x1.shape[0]
    n_sorted = 2 * n + N_EXPERTS * FFN_TM
    cnt = counts[0, N_GROUPS:N_GROUPS + N_EXPERTS].astype(jnp.int32)
    tiles = (cnt + FFN_TM - 1) // FFN_TM
    tile_end = jnp.cumsum(tiles)
    seg_start = (tile_end - tiles) * FFN_TM
    dest = _dest(route_t, seg_start.astype(jnp.int32), 2048)
    dest1, dest2 = dest[0], dest[1]
    tile_ids = jnp.arange(n_sorted // FFN_TM, dtype=jnp.int32)
    tile_expert = jnp.minimum(jnp.sum(tile_end[None, :] <= tile_ids[:, None], axis=1), N_EXPERTS - 1)
    xs, ws = _after(_sc_dispatch(h2, wrow1, wrow2, dest1, dest2, n_sorted), during_dispatch())
    y = _ffn(xs, ws, tile_expert.astype(jnp.int32), tile_end[-1:].astype(jnp.int32), sw, layer)
    g1, g2 = _after(_sc_collect(y, dest1, dest2, n), during_collect())
    return _combine(x1, g1, g2, final_norm, 512, final)


def _t5_buckets(dist):
    max_exact = T5_BUCKETS // 2
    large = max_exact + (np.log(np.maximum(dist, 1) / max_exact) / np.log(T5_MAX_DIST / max_exact)
                         * (T5_BUCKETS - max_exact)).astype(np.int32)
    large = np.minimum(large, T5_BUCKETS - 1)
    return np.where(dist < max_exact, dist, large).astype(np.int32)


def _bias_tables(t5, g, dil, window, t_sample):
    nk = window // dil + 1
    hs = slice(g * HPG, (g + 1) * HPG)
    bias = t5[_t5_buckets(np.arange(nk) * dil)][:, hs].T
    rev = bias[:, ::-1]
    neg = lambda *shape: jnp.full(shape, NEG, F32)

    vec = jnp.concatenate([rev, neg(HPG, CL)], axis=1)
    both = jnp.tile(vec, (1, CL + 1))[:, :CL * 2 * CL].reshape(HPG, CL, 2 * CL)

    rows = []
    for r in range(t_sample):
        shift = r // dil
        per_u = jnp.concatenate([neg(HPG, shift), rev[:, :nk - 1 - shift]], axis=1)
        on_phase = (np.arange(dil) == r % dil)[None, None, :]
        rows.append(jnp.where(on_phase, per_u[:, :, None], NEG).reshape(HPG, window))
    cache_t = jnp.stack(rows, axis=1)
    r = np.arange(t_sample)[:, None]
    c = np.arange(t_sample)[None, :]
    ok = ((r - c) % dil == 0) & (r >= c)
    new_t = jnp.where(jnp.asarray(ok)[None], bias[:, np.clip((r - c) // dil, 0, nk - 1)], NEG)
    return both, cache_t, new_t


def _layer_weights(l, norm1, conv_ssd_w, conv_ssd_b, ssd_dt_bias, ssd_a_log, ssd_d, ssd_norm_w,
                   conv_lru_w, conv_lru_b, lru_br, lru_bi, lru_lambda, b_gate, norm2,
                   w_router_group, b_router_group, w_router_expert, b_router_expert):
    b_all = jnp.concatenate([b_gate[l], jnp.zeros((PW - 3 * D,), F32)])[None]

    def pad128(v):
        return jnp.concatenate([v, jnp.zeros((128 - v.shape[0],), F32)])[None]

    return {
        'norm1': norm1[l][None], 'b_all': b_all,
        'conv_ssd_w': conv_ssd_w[l], 'conv_ssd_b': conv_ssd_b[l][None],
        'dt_bias': pad128(ssd_dt_bias[l]), 'a_log': pad128(ssd_a_log[l]), 'd_skip': pad128(ssd_d[l]),
        'ssd_norm_w': ssd_norm_w[l][None],
        'conv_lru_w': conv_lru_w[l], 'conv_lru_b': conv_lru_b[l][None],
        'lru_br': lru_br[l][None], 'lru_bi': lru_bi[l][None], 'lru_lambda': lru_lambda[l][None],
        'norm2': norm2[l][None],
        'w_router': jnp.concatenate([w_router_group[l], w_router_expert[l],
                                     jnp.zeros((D, ROUTE_LANES - N_GROUPS - N_EXPERTS), F32)],
                                    axis=1).astype(BF16),
        'b_router': pad128(jnp.concatenate([b_router_group[l], b_router_expert[l]])),
    }


def _front_pad(buf):
    return jnp.pad(buf, ((0, 0), (8 - (CONV_W - 1), 0), (0, 0)))


def _cols(P, nb, t, start, width):
    return P.reshape(nb, t, PW)[:, :, start:start + width]


def _kv_rows(P, nb, t, g, n_rows):
    k = _cols(P, nb, t, C_K + g * GW, GW)[:, t - n_rows:].reshape(nb, n_rows, HPG, HD)
    v = _cols(P, nb, t, C_V + g * GW, GW)[:, t - n_rows:].reshape(nb, n_rows, HPG, HD)
    return jnp.stack([k, v], axis=2)


def _layer_front(x, lw, sw, tables, layer, nb, t, rows, tm, tm_mix, tm_res, conv_ssd, st_ssd, conv_lru, st_lru,
                 caches):
    if caches is None:
        P, *qkv = _proj(x, lw['norm1'], sw['w_all'], lw['b_all'], layer, tm, nb, t)
    else:
        (P,) = _proj(x, lw['norm1'], sw['w_all'], lw['b_all'], layer, tm)
    y_ssd, h_ssd, y_lru, h_lru = _mixers(P, _front_pad(conv_ssd), st_ssd.reshape(nb, SSD_INNER, SSD_STATE),
                                         _front_pad(conv_lru), st_lru.reshape(nb, 1, LRU_W), lw, sw, layer, nb, t, rows)
    attn = []
    for g, (window, dil) in enumerate(ATTN_GROUPS):
        both_t, cache_t, new_t = tables[g]
        if caches is None:
            attn.append(_attn_prompt(qkv[g], qkv[3 + g], qkv[6 + g], both_t, dil, nb, t))
        else:
            attn.append(_attn_sample(P, caches[g], cache_t, new_t, g, layer, nb, t, window))
    dils = tuple(dil if caches is None else 1 for _, dil in ATTN_GROUPS)
    mixed = _mix(P, y_ssd, y_lru, attn, dils, sw, layer, tm_mix)
    routed = _res(x, mixed, lw, sw, layer, tm_res, caches is None)
    states = (_cols(P, nb, t, C_XBC, SSD_CONV_DIM)[:, t - 3:],
              h_ssd.reshape(nb, SSD_HEADS, SSD_HEAD_DIM, SSD_STATE),
              _cols(P, nb, t, C_XR, LRU_W)[:, t - 3:],
              h_lru.reshape(nb, LRU_W)) + tuple(
                  _kv_rows(P, nb, t, g, min(w, t)) for g, (w, _) in enumerate(ATTN_GROUPS))
    return routed, states


def kernel(x_prompt, x_sample, cache_conv_ssd, state_ssd, cache_conv_lru, state_lru, cache_kv_w128, cache_kv_w512, cache_kv_w2048, norm1, w_in, conv_ssd_w, conv_ssd_b, ssd_dt_bias, ssd_a_log, ssd_d, ssd_norm_w, conv_lru_w, conv_lru_b, lru_wr, lru_br, lru_wi, lru_bi, lru_lambda, t5_bias, w_br_ssd, w_br_lru, w_br_attn, w_gate, b_gate, w_o, norm2, w_router_group, b_router_group, w_router_expert, b_router_expert, w1, w3, w2, final_norm):
    bp, tp, _ = x_prompt.shape
    bs, ts, _ = x_sample.shape
    xp = x_prompt.reshape(bp * tp, D)
    xs = x_sample.reshape(bs * ts, D)
    fn = final_norm[None]
    tables = [_bias_tables(t5_bias, g, dil, window, ts) for g, (window, dil) in enumerate(ATTN_GROUPS)]
    caches = [c.reshape(-1, HD) for c in (cache_kv_w128, cache_kv_w512, cache_kv_w2048)]
    sw = {name: w.astype(BF16) for name, w in dict(
        lru_wr=lru_wr, lru_wi=lru_wi, w_br_ssd=w_br_ssd, w_br_lru=w_br_lru, w_br_attn=w_br_attn, w_o=w_o).items()}
    sw.update(w1=w1, w3=w3, w2=w2)
    sw['w_all'] = _prep_w(w_gate, w_in)
    outs_p, outs_s = [], []
    for l in range(DEPTH):
        lw = _layer_weights(l, norm1, conv_ssd_w, conv_ssd_b, ssd_dt_bias, ssd_a_log, ssd_d, ssd_norm_w,
                            conv_lru_w, conv_lru_b, lru_br, lru_bi, lru_lambda, b_gate, norm2,
                            w_router_group, b_router_group, w_router_expert, b_router_expert)
        final = l == DEPTH - 1
        (x1, h2, _, route_t, counts, wrow1, wrow2), sp = _layer_front(
            xp, lw, sw, tables, l, bp, tp, CL, 1024, 256, 512,
            jnp.zeros((bp, CONV_W - 1, SSD_CONV_DIM), F32), jnp.zeros((bp, SSD_HEADS, SSD_HEAD_DIM, SSD_STATE), F32),
            jnp.zeros((bp, CONV_W - 1, LRU_W), F32), jnp.zeros((bp, LRU_W), F32), None)
        sample = {}

        def sample_front(l=l, lw=lw, xs=xs):
            sample['front'] = _layer_front(xs, lw, sw, tables, l, bs, ts, ts, bs * ts, bs * ts, bs * ts,
                                           cache_conv_ssd[l], state_ssd[l], cache_conv_lru[l], state_lru[l], caches)
            return sample['front']

        def sample_moe(l=l, final=final):
            (x1_s, h2_s, comb_s), _ = sample['front']
            sample['x'] = _moe(h2_s, comb_s, x1_s, sw, l, fn, bs * ts, final)
            return sample['x']

        xp = _moe_sparse(h2, route_t, counts, wrow1, wrow2, x1, sw, l, fn, final, sample_front, sample_moe)
        xs = sample['x']
        outs_p.append(sp)
        outs_s.append(sample['front'][1])

    def stk(outs, i):
        return jnp.stack([o[i] for o in outs], axis=0)

    return ((xp.reshape(bp, tp, D), xs.reshape(bs, ts, D))
            + tuple(stk(outs_p, i) for i in range(7)) + tuple(stk(outs_s, i) for i in range(7)))
```

```python
import functools

import numpy as np
import jax
import jax.numpy as jnp
from jax import lax
from jax.experimental import pallas as pl
from jax.experimental.pallas import tpu as pltpu
from jax.experimental.pallas import tpu_sc as plsc

F32 = jnp.float32
BF16 = jnp.bfloat16
EPS = 1e-6
NEG = -1e30

D = 2048
DEPTH = 2
PAST_LEN = 16384
CL = 128
CONV_W = 4
SSD_HEADS = 16
SSD_HEAD_DIM = 64
SSD_INNER = 1024
SSD_STATE = 128
SSD_CONV_DIM = 1536
LRU_W = 1024
LRU_BLOCKS = 8
LRU_C = 8.0
ATTN_GROUPS = ((128, 1), (512, 4), (2048, 16))
HPG = 4
HD = 128
GW = HPG * HD
T5_BUCKETS = 32
T5_MAX_DIST = 2048
N_GROUPS = 4
PER_GROUP = 4
N_EXPERTS = 16
D_EXPERT = 512

TILE = 512
C_GATE = 0
C_Z = 6144
C_XR = 7168
C_GR = 8192
C_XBC = 9216
C_Q = 10752
C_K = 12288
C_V = 13824
C_DT = 15360
PW = 15872
N_GATE_TILES = (3 * D) // TILE
ROUTE_LANES = 128
VMEM_LIMIT = 56 * 1024 * 1024


def _cparams(sem):
    return pltpu.CompilerParams(dimension_semantics=sem, vmem_limit_bytes=VMEM_LIMIT)


def _full(shape):
    nd = len(shape)
    return pl.BlockSpec(shape, lambda *_: (0,) * nd)


def _of_layer(shape, layer):
    nd = len(shape)
    return pl.BlockSpec((1,) + shape, lambda *_: (layer,) + (0,) * nd)


_W_IN_STARTS = (0, 512, 2576, 3088, 3600, 4112, 1024, 1536, 2048, 4624, 5136, 5648, 6160, 6672, 7184,
                7696, 8208, 8720, 2560)
DT_SHIFT = SSD_HEADS


def _prep_kernel(blk_ref, shift_ref, width_ref, wg_ref, wa_ref, wb_ref, o_ref):
    j = pl.program_id(1)

    @pl.when(j < N_GATE_TILES)
    def _():
        o_ref[0] = wg_ref[0].astype(BF16)

    @pl.when(j >= N_GATE_TILES)
    def _():
        a = wa_ref[0]
        shifted = jnp.concatenate([a[DT_SHIFT:, :], wb_ref[0]], axis=0)
        val = jnp.where(shift_ref[j] == 0, a, shifted)
        row = lax.broadcasted_iota(jnp.int32, val.shape, 0)
        o_ref[0] = jnp.where(row < width_ref[j], val, 0.0).T.astype(BF16)


def _prep_w(w_gate, w_in):
    starts = (0,) * N_GATE_TILES + _W_IN_STARTS
    blk = jnp.asarray([s // TILE for s in starts], jnp.int32)
    shift = jnp.asarray([s % TILE for s in starts], jnp.int32)
    assert all(s % TILE in (0, DT_SHIFT) for s in starts) and w_in.shape[2] % DT_SHIFT == 0
    width = jnp.asarray([TILE] * (len(starts) - 1) + [SSD_HEADS], jnp.int32)
    per_tile = TILE // DT_SHIFT
    w_in_t = jnp.swapaxes(w_in, 1, 2)
    return pl.pallas_call(
        _prep_kernel,
        grid_spec=pltpu.PrefetchScalarGridSpec(
            num_scalar_prefetch=3,
            grid=(DEPTH, PW // TILE),
            in_specs=[pl.BlockSpec((1, D, TILE), lambda l, j, b, s, w: (l, 0, jnp.minimum(j, N_GATE_TILES - 1))),
                      pl.BlockSpec((1, TILE, D), lambda l, j, b, s, w: (l, b[j], 0)),
                      pl.BlockSpec((1, DT_SHIFT, D), lambda l, j, b, s, w: (l, (b[j] + 1) * per_tile, 0))],
            out_specs=pl.BlockSpec((1, D, TILE), lambda l, j, b, s, w: (l, 0, j)),
        ),
        out_shape=jax.ShapeDtypeStruct((DEPTH, D, PW), BF16),
        compiler_params=_cparams(("parallel", "arbitrary")),
        name="prep_w",
    )(blk, shift, width, w_gate, w_in_t, w_in_t)


def _proj_kernel(x_ref, nw_ref, w_ref, b_ref, o_ref, *rest, tm, phase_major):
    if phase_major:
        qkv_refs, (h_ref, acc_ref, ph_ref) = rest[:9], rest[9:]
    else:
        h_ref, acc_ref = rest
    j = pl.program_id(1)

    @pl.when(j == 0)
    def _():
        x = x_ref[...]
        ms = jnp.mean(x * x, axis=-1, keepdims=True)
        h_ref[...] = (x * lax.rsqrt(ms + EPS) * nw_ref[...]).astype(BF16)
        acc_ref[...] = jnp.zeros(acc_ref.shape, F32)

    prev = acc_ref[...]
    o_ref[...] = jnp.where(j <= N_GATE_TILES, jax.nn.sigmoid(prev), prev)
    acc_ref[...] = jnp.dot(h_ref[...], w_ref[0], preferred_element_type=F32) + b_ref[...]

    if phase_major:
        for part in range(3):
            for g, (_, dil) in enumerate(ATTN_GROUPS):
                ref = qkv_refs[part * 3 + g]

                @pl.when(j - 1 == C_Q // TILE + part * 3 + g)
                def _(ref=ref, dil=dil):
                    if dil == 1:
                        ref[0, 0] = o_ref[...].astype(BF16)
                    else:
                        for c in range(TILE // 128):
                            ph_ref[c] = o_ref[:, c * 128:(c + 1) * 128]
                        for p in range(dil):
                            for c in range(TILE // 128):
                                ref[0, p, :, c * 128:(c + 1) * 128] = (
                                    ph_ref[c, pl.ds(p, tm // dil, stride=dil), :].astype(BF16))


def _proj(x, nw, w_all, b_all, layer, tm, nb=None, t=None):
    n = x.shape[0]
    phase_major = nb is not None
    nt = PW // TILE
    out_specs = [pl.BlockSpec((tm, TILE), lambda i, j: (i, jnp.maximum(j - 1, 0)))]
    out_shape = [jax.ShapeDtypeStruct((n, PW), F32)]
    scratch = [pltpu.VMEM((tm, D), BF16), pltpu.VMEM((tm, TILE), F32)]
    if phase_major:
        tpb = t // tm
        for _ in range(3):
            for _, dil in ATTN_GROUPS:
                out_specs.append(pl.BlockSpec((1, dil, tm // dil, GW), lambda i, j: (i // tpb, 0, i % tpb, 0)))
                out_shape.append(jax.ShapeDtypeStruct((nb, dil, t // dil, GW), BF16))
        scratch.append(pltpu.VMEM((TILE // 128, tm, 128), F32))
    return pl.pallas_call(
        functools.partial(_proj_kernel, tm=tm, phase_major=phase_major),
        grid=(n // tm, nt + 1),
        in_specs=[pl.BlockSpec((tm, D), lambda i, j: (i, 0)),
                  pl.BlockSpec((1, D), lambda i, j: (0, 0)),
                  pl.BlockSpec((1, D, TILE), lambda i, j: (layer, 0, jnp.minimum(j, nt - 1))),
                  pl.BlockSpec((1, TILE), lambda i, j: (0, jnp.minimum(j, nt - 1)))],
        out_specs=out_specs,
        out_shape=out_shape,
        scratch_shapes=scratch,
        compiler_params=_cparams(("parallel", "arbitrary")),
        name="proj",
    )(x, nw, w_all, b_all)


def _conv_step(x_ref, xp_ref, cw_ref, cb_ref, rows, out_rows):
    xp_ref[8:8 + rows, :] = _bf16_round(x_ref[...])
    cw = _bf16_round(cw_ref[...])
    acc = cw[0:1, :] * xp_ref[5:5 + out_rows, :]
    for j in range(1, CONV_W):
        acc = acc + cw[j:j + 1, :] * xp_ref[5 + j:5 + j + out_rows, :]
    tail = xp_ref[rows:rows + 8, :]
    xp_ref[0:8, :] = tail
    return acc + cb_ref[...]


def _bf16_round(x):
    return x.astype(BF16).astype(F32)


def _softplus(x):
    return jnp.maximum(x, 0.0) + jnp.log1p(jnp.exp(-jnp.abs(x)))


def _ssd_part(part, z_ref, xbc_ref, dt_ref, tail_ref, h0_ref, cw_ref, cb_ref, dtb_ref, alog_ref, dsk_ref, nw_ref,
              y_ref, hf_ref, xp_ref, act_ref, st_ref, ysc_ref, *, rows):
    if part == 'init':
        xp_ref[0:8, :] = _bf16_round(tail_ref[0])
        st_ref[...] = h0_ref[0]
        return
    if part == 'final':
        hf_ref[0] = st_ref[...]
        return

    if rows < CL:
        xp_ref[8 + rows:, :] = jnp.zeros((CL - rows, SSD_CONV_DIM), F32)
    conv = _conv_step(xbc_ref, xp_ref, cw_ref, cb_ref, rows, CL)
    act_ref[...] = conv * jax.nn.sigmoid(conv)

    row = lax.broadcasted_iota(jnp.int32, (CL, 128), 0)
    lane = lax.broadcasted_iota(jnp.int32, (CL, 128), 1)
    raw = dt_ref[...]
    if rows < CL:
        raw = jnp.concatenate([raw, jnp.zeros((CL - rows, 128), F32)], axis=0)
    dt = _softplus(raw + dtb_ref[...])
    dt = jnp.where((lane < SSD_HEADS) & (row < rows), dt, 0.0)
    da = dt * (-jnp.exp(alog_ref[...]))
    acs = da
    d = 1
    while d < CL:
        acs = acs + jnp.where(row >= d, pltpu.roll(acs, d, 0), 0.0)
        d *= 2
    acs_t = acs.T
    last = acs[CL - 1:CL, :]
    e_acs = jnp.exp(acs)
    to_end = jnp.exp(last - acs)
    cdec = jnp.exp(last)
    causal = row >= lane
    lo_lane = lane < SSD_HEAD_DIM
    lo_row = row < SSD_HEAD_DIM
    dsk = dsk_ref[...]

    def pair_cols(arr, h):
        return jnp.where(lo_lane, arr[:, h:h + 1], arr[:, h + 1:h + 2])

    nt = (((1,), (1,)), ((), ()))
    for g in range(2):
        bm = act_ref[:, SSD_INNER + g * SSD_STATE:SSD_INNER + (g + 1) * SSD_STATE].astype(BF16)
        cm = act_ref[:, SSD_INNER + 256 + g * SSD_STATE:SSD_INNER + 256 + (g + 1) * SSD_STATE].astype(BF16)
        cb = lax.dot_general(cm, bm, nt, preferred_element_type=F32)
        for pp in range(4):
            h = g * 8 + 2 * pp
            sl = slice(h * SSD_HEAD_DIM, h * SSD_HEAD_DIM + 128)
            xs = act_ref[:, sl]
            xdt = xs * pair_cols(dt, h)
            xdt_b = xdt.astype(BF16)
            ys = []
            for hh in (h, h + 1):
                seg = acs[:, hh:hh + 1] - acs_t[hh:hh + 1, :]
                decay = jnp.exp(jnp.where(causal, seg, -jnp.inf))
                ys.append(jnp.dot((cb * decay).astype(BF16), xdt_b, preferred_element_type=F32))
            y_diag = jnp.where(lo_lane, ys[0], ys[1])
            st = st_ref[sl, :]
            y_off = lax.dot_general(cm, st.astype(BF16), nt, preferred_element_type=F32) * pair_cols(e_acs, h)
            d_pair = jnp.where(lo_lane, dsk[:, h:h + 1], dsk[:, h + 1:h + 2])
            ysc_ref[:, sl] = y_diag + y_off + d_pair * xs
            xdte_t = (xdt * pair_cols(to_end, h)).T.astype(BF16)
            s_new = jnp.dot(xdte_t, bm, preferred_element_type=F32)
            dec = jnp.where(lo_row, cdec[:, h:h + 1], cdec[:, h + 1:h + 2])
            st_ref[sl, :] = dec * st + s_new

    zz = z_ref[...]
    yg = ysc_ref[0:rows, :] * (zz * jax.nn.sigmoid(zz))
    gw = SSD_INNER // 2
    for g in range(2):
        half = yg[:, g * gw:(g + 1) * gw]
        ms = jnp.mean(half * half, axis=-1, keepdims=True)
        y_ref[:, g * gw:(g + 1) * gw] = (half * lax.rsqrt(ms + EPS)
                                         * nw_ref[:, g * gw:(g + 1) * gw]).astype(y_ref.dtype)


def _lru_part(part, xr_ref, gr_ref, tail_ref, h0_ref, cw_ref, cb_ref, wr_ref, br_ref, wi_ref, bi_ref, lam_ref,
              y_ref, hl_ref, xp_ref, h_ref, *, rows):
    if part == 'init':
        xp_ref[0:8, :] = _bf16_round(tail_ref[0])
        h_ref[...] = h0_ref[0]
        return
    if part == 'final':
        hl_ref[0] = h_ref[...]
        return

    x = _conv_step(xr_ref, xp_ref, cw_ref, cb_ref, rows, rows)
    xb = x.astype(BF16)
    rs, is_ = [], []
    for n in range(LRU_BLOCKS):
        blk = xb[:, n * 128:(n + 1) * 128]
        rs.append(jnp.dot(blk, wr_ref[0, n], preferred_element_type=F32))
        is_.append(jnp.dot(blk, wi_ref[0, n], preferred_element_type=F32))
    r_gate = jax.nn.sigmoid(jnp.concatenate(rs, axis=1) + br_ref[...])
    i_gate = jax.nn.sigmoid(jnp.concatenate(is_, axis=1) + bi_ref[...])
    log_a = -LRU_C * r_gate * _softplus(-lam_ref[...])
    a = jnp.exp(log_a)
    th = jnp.tanh(log_a)
    b = jnp.sqrt(-2.0 * th / (1.0 - th)) * (i_gate * x)
    in_group = lax.broadcasted_iota(jnp.int32, (rows, LRU_W), 0) % 8
    for d in (1, 2, 4):
        a_s = jnp.where(in_group >= d, pltpu.roll(a, d, 0), 1.0)
        b_s = jnp.where(in_group >= d, pltpu.roll(b, d, 0), 0.0)
        b = a * b_s + b
        a = a * a_s
    carry = h_ref[...]
    groups = []
    for g in range(rows // 8):
        h_g = b[8 * g:8 * g + 8, :] + a[8 * g:8 * g + 8, :] * carry
        groups.append(h_g)
        carry = h_g[7:8, :]
    h = jnp.concatenate(groups, axis=0) if len(groups) > 1 else groups[0]
    h_ref[...] = carry
    y_ref[...] = (h * jax.nn.gelu(gr_ref[...])).astype(y_ref.dtype)


N_SSD_IN, N_LRU_IN = 11, 11


def _mixers_kernel(*refs, rows, n_chunks):
    ins, rest = refs[:N_SSD_IN + N_LRU_IN], refs[N_SSD_IN + N_LRU_IN:]
    ssd = ins[:N_SSD_IN] + rest[0:2] + rest[4:8]
    lru = ins[N_SSD_IN:] + rest[2:4] + rest[8:10]
    c = pl.program_id(1)

    @pl.when(c == 0)
    def _():
        _ssd_part('init', *ssd, rows=rows)
        _lru_part('init', *lru, rows=rows)

    _ssd_part('main', *ssd, rows=rows)
    _lru_part('main', *lru, rows=rows)

    @pl.when(c == n_chunks - 1)
    def _():
        _ssd_part('final', *ssd, rows=rows)
        _lru_part('final', *lru, rows=rows)


def _mixers(P, ssd_tail, ssd_h0, lru_tail, lru_h0, lw, sw, layer, nb, t, rows):
    nc = t // rows
    y_dtype = BF16 if rows % 16 == 0 else F32
    tok = lambda w, col: pl.BlockSpec((rows, w), lambda b, c: (b * nc + c, col // w))
    seq = lambda *shape: pl.BlockSpec((1,) + shape, lambda b, c: (b,) + (0,) * len(shape))
    return pl.pallas_call(
        functools.partial(_mixers_kernel, rows=rows, n_chunks=nc),
        grid=(nb, nc),
        in_specs=[tok(SSD_INNER, C_Z), tok(SSD_CONV_DIM, C_XBC), tok(128, C_DT),
                  seq(8, SSD_CONV_DIM), seq(SSD_INNER, SSD_STATE),
                  _full((CONV_W, SSD_CONV_DIM)), _full((1, SSD_CONV_DIM)),
                  _full((1, 128)), _full((1, 128)), _full((1, 128)), _full((1, SSD_INNER)),
                  tok(LRU_W, C_XR), tok(LRU_W, C_GR), seq(8, LRU_W), seq(1, LRU_W),
                  _full((CONV_W, LRU_W)), _full((1, LRU_W)),
                  _of_layer((LRU_BLOCKS, 128, 128), layer), _full((1, LRU_W)),
                  _of_layer((LRU_BLOCKS, 128, 128), layer), _full((1, LRU_W)), _full((1, LRU_W))],
        out_specs=[tok(SSD_INNER, 0), seq(SSD_INNER, SSD_STATE), tok(LRU_W, 0), seq(1, LRU_W)],
        out_shape=[jax.ShapeDtypeStruct((nb * t, SSD_INNER), y_dtype),
                   jax.ShapeDtypeStruct((nb, SSD_INNER, SSD_STATE), F32),
                   jax.ShapeDtypeStruct((nb * t, LRU_W), y_dtype),
                   jax.ShapeDtypeStruct((nb, 1, LRU_W), F32)],
        scratch_shapes=[pltpu.VMEM((8 + CL, SSD_CONV_DIM), F32), pltpu.VMEM((CL, SSD_CONV_DIM), F32),
                        pltpu.VMEM((SSD_INNER, SSD_STATE), F32), pltpu.VMEM((CL, SSD_INNER), F32),
                        pltpu.VMEM((8 + rows, LRU_W), F32), pltpu.VMEM((1, LRU_W), F32)],
        compiler_params=_cparams(("parallel", "arbitrary")),
        name="mixers",
    )(P, P, P, ssd_tail, ssd_h0, lw['conv_ssd_w'], lw['conv_ssd_b'], lw['dt_bias'], lw['a_log'], lw['d_skip'],
      lw['ssd_norm_w'], P, P, lru_tail, lru_h0, lw['conv_lru_w'], lw['conv_lru_b'], sw['lru_wr'], lw['lru_br'],
      sw['lru_wi'], lw['lru_bi'], lw['lru_lambda'])


def _attn_kernel(q_ref, kp_ref, vp_ref, kc_ref, vc_ref, bias_ref, o_ref, lse_ref, *, sub, phases):
    scale = HD ** -0.5
    nt = (((1,), (1,)), ((), ()))
    lane = lax.broadcasted_iota(jnp.int32, (CL, 128), 1)
    key = lax.broadcasted_iota(jnp.int32, (CL, 2 * CL), 1)
    first_ok = (pl.program_id(2) > 0) | (key >= CL)
    for z in range(phases):
        for s in range(sub):
            rows = slice(s * CL, (s + 1) * CL)
            both = slice((s - 1) * CL, (s + 1) * CL)
            lse_all = jnp.zeros((CL, 128), F32)
            for h in range(HPG):
                sl = slice(h * HD, (h + 1) * HD)
                q = q_ref[0, z, rows, sl]
                if s == 0:
                    kk = jnp.concatenate([kp_ref[0, z, :, sl], kc_ref[0, z, rows, sl]], axis=0)
                    vv = jnp.concatenate([vp_ref[0, z, :, sl], vc_ref[0, z, rows, sl]], axis=0)
                else:
                    kk, vv = kc_ref[0, z, both, sl], vc_ref[0, z, both, sl]
                sc = lax.dot_general(q, kk, nt, preferred_element_type=F32) * scale + bias_ref[h]
                if s == 0:
                    sc = jnp.where(first_ok, sc, NEG)
                m = jnp.max(sc, axis=-1, keepdims=True)
                p = jnp.exp(sc - m)
                l = jnp.sum(p, axis=-1, keepdims=True)
                o_ref[rows, z * GW + h * HD:z * GW + (h + 1) * HD] = jnp.dot(
                    (p * (1.0 / l)).astype(BF16), vv, preferred_element_type=F32)
                lse_all = jnp.where(lane == h, m + jnp.log(l), lse_all)
            lse_ref[rows, z * 128:(z + 1) * 128] = lse_all


def _attn_prompt(q, k, v, bias, dil, nb, t):
    n = nb * t
    blocks = 8
    sub = min(blocks, t // dil // CL)
    phases = min(blocks // sub, dil)
    nstep = t // dil // (sub * CL)
    cur = pl.BlockSpec((1, phases, sub * CL, GW), lambda b, p, i: (b, p, i, 0))
    prev = pl.BlockSpec((1, phases, CL, GW), lambda b, p, i: (b, p, jnp.maximum(i * sub - 1, 0), 0))
    return pl.pallas_call(
        functools.partial(_attn_kernel, sub=sub, phases=phases),
        grid=(nb, dil // phases, nstep),
        in_specs=[cur, prev, prev, cur, cur, _full((HPG, CL, 2 * CL))],
        out_specs=[pl.BlockSpec((sub * CL, phases * GW), lambda b, p, i: (b * nstep + i, p)),
                   pl.BlockSpec((sub * CL, phases * 128), lambda b, p, i: (b * nstep + i, p))],
        out_shape=[jax.ShapeDtypeStruct((n // dil, dil * GW), F32),
                   jax.ShapeDtypeStruct((n // dil, dil * 128), F32)],
        compiler_params=_cparams(("parallel", "parallel", "arbitrary")),
        name=f"attn_prompt_d{dil}",
    )(q, k, v, k, v, bias)


def _attn_sample_kernel(q_ref, kv_ref, kb_ref, vb_ref, ba_ref, bb_ref, o_ref, lse_ref):
    scale = HD ** -0.5
    nt = (((1,), (1,)), ((), ()))
    rows = o_ref.shape[0]
    window = kv_ref.shape[0] // (2 * HPG)
    lane = lax.broadcasted_iota(jnp.int32, (rows, 128), 1)
    lse_all = jnp.zeros((rows, 128), F32)
    for h in range(HPG):
        sl = slice(h * HD, (h + 1) * HD)
        q = q_ref[:, sl].astype(BF16)
        k_cache = kv_ref[pl.ds(h, window, stride=2 * HPG), :].astype(BF16)
        v_cache = kv_ref[pl.ds(HPG + h, window, stride=2 * HPG), :].astype(BF16)
        sa = lax.dot_general(q, k_cache, nt, preferred_element_type=F32) * scale + ba_ref[h]
        sb = lax.dot_general(q, kb_ref[:, sl].astype(BF16), nt, preferred_element_type=F32) * scale + bb_ref[h]
        m = jnp.maximum(jnp.max(sa, axis=-1, keepdims=True), jnp.max(sb, axis=-1, keepdims=True))
        pa = jnp.exp(sa - m)
        pb = jnp.exp(sb - m)
        l = jnp.sum(pa, axis=-1, keepdims=True) + jnp.sum(pb, axis=-1, keepdims=True)
        inv = 1.0 / l
        o_ref[:, sl] = (jnp.dot((pa * inv).astype(BF16), v_cache, preferred_element_type=F32)
                        + jnp.dot((pb * inv).astype(BF16), vb_ref[:, sl].astype(BF16), preferred_element_type=F32))
        lse_all = jnp.where(lane == h, m + jnp.log(l), lse_all)
    lse_ref[...] = lse_all


def _attn_sample(Ps, cache_rows, bias_a, bias_b, g, layer, nb, t, window):
    tq, tk, tv = C_Q // GW + g, C_K // GW + g, C_V // GW + g
    out_spec = pl.BlockSpec((t, GW), lambda b: (b, 0))
    return pl.pallas_call(
        _attn_sample_kernel,
        grid=(nb,),
        in_specs=[pl.BlockSpec((t, GW), lambda b: (b, tq)),
                  pl.BlockSpec((window * 2 * HPG, HD), lambda b: (layer * nb + b, 0)),
                  pl.BlockSpec((t, GW), lambda b: (b, tk)),
                  pl.BlockSpec((t, GW), lambda b: (b, tv)),
                  _full((HPG, t, window)), _full((HPG, t, t))],
        out_specs=[out_spec, pl.BlockSpec((t, 128), lambda b: (b, 0))],
        out_shape=[jax.ShapeDtypeStruct((nb * t, GW), F32), jax.ShapeDtypeStruct((nb * t, 128), F32)],
        compiler_params=_cparams(("parallel",)),
        name=f"attn_sample_w{window}",
    )(Ps, cache_rows, Ps, Ps, bias_a, bias_b)


def _mix_kernel(gs_ref, gl_ref, ga_ref, ys_ref, yl_ref, o0_ref, o1_ref, o2_ref, l0_ref, l1_ref, l2_ref,
                wbs_ref, wbl_ref, wba_ref, out_ref, *scratch, dils):
    tm = out_ref.shape[0]
    o_heads, lses = [], []
    scratch = list(scratch)
    for o_ref, l_ref, dil in zip((o0_ref, o1_ref, o2_ref), (l0_ref, l1_ref, l2_ref), dils):
        if dil == 1:
            o_heads.append([o_ref[:, h * HD:(h + 1) * HD] for h in range(HPG)])
            lses.append(l_ref[...])
            continue
        o_scr, l_scr = scratch.pop(0), scratch.pop(0)
        for p in range(dil):
            rows = pl.ds(p, tm // dil, stride=dil)
            for h in range(HPG):
                o_scr[h, rows, :] = o_ref[:, p * GW + h * HD:p * GW + (h + 1) * HD]
            l_scr[rows, :] = l_ref[:, p * 128:(p + 1) * 128]
        o_heads.append([o_scr[h] for h in range(HPG)])
        lses.append(l_scr[...])
    l0, l1, l2 = lses
    m = jnp.maximum(jnp.maximum(l0, l1), l2)
    e0, e1, e2 = jnp.exp(l0 - m), jnp.exp(l1 - m), jnp.exp(l2 - m)
    den = e0 + e1 + e2
    w0, w1, w2 = e0 / den, e1 / den, e2 / den
    heads = []
    for h in range(HPG):
        per_head = lambda w: jnp.broadcast_to(w[:, h:h + 1], (tm, HD))
        heads.append(o_heads[0][h] * per_head(w0) + o_heads[1][h] * per_head(w1) + o_heads[2][h] * per_head(w2))
    ya = jnp.concatenate(heads, axis=1)
    mixed = (gs_ref[...] * jnp.dot(ys_ref[...].astype(BF16), wbs_ref[0], preferred_element_type=F32)
             + gl_ref[...] * jnp.dot(yl_ref[...].astype(BF16), wbl_ref[0], preferred_element_type=F32)
             + ga_ref[...] * jnp.dot(ya.astype(BF16), wba_ref[0], preferred_element_type=F32))
    out_ref[...] = mixed.astype(BF16)


def _mix(P, y_ssd, y_lru, attn, dils, sw, layer, tm):
    n = P.shape[0]
    row = lambda w: pl.BlockSpec((tm, w), lambda i: (i, 0))
    phased = lambda w, d: pl.BlockSpec((tm // d, d * w), lambda i: (i, 0))
    (o0, s0), (o1, s1), (o2, s2) = attn
    scratch = []
    for d in dils:
        if d > 1:
            scratch += [pltpu.VMEM((HPG, tm, HD), F32), pltpu.VMEM((tm, 128), F32)]
    return pl.pallas_call(
        functools.partial(_mix_kernel, dils=dils),
        grid=(n // tm,),
        in_specs=[pl.BlockSpec((tm, D), lambda i: (i, 0)), pl.BlockSpec((tm, D), lambda i: (i, 1)),
                  pl.BlockSpec((tm, D), lambda i: (i, 2)),
                  row(SSD_INNER), row(LRU_W)] + [phased(GW, d) for d in dils] + [phased(128, d) for d in dils] + [
                  _of_layer((SSD_INNER, D), layer), _of_layer((LRU_W, D), layer), _of_layer((GW, D), layer)],
        out_specs=row(D),
        out_shape=jax.ShapeDtypeStruct((n, D), BF16),
        scratch_shapes=scratch,
        compiler_params=_cparams(("parallel",)),
        name="mix",
    )(P, P, P, y_ssd, y_lru, o0, o1, o2, s0, s1, s2, sw['w_br_ssd'], sw['w_br_lru'], sw['w_br_attn'])


def _res_kernel(x_ref, mixed_ref, wo_ref, n2_ref, wr_ref, br_ref, x1_ref, h2_ref, comb_ref, *rest, dispatch):
    x1 = x_ref[...] + jnp.dot(mixed_ref[...], wo_ref[0], preferred_element_type=F32)
    x1_ref[...] = x1
    ms = jnp.mean(x1 * x1, axis=-1, keepdims=True)
    h2 = x1 * lax.rsqrt(ms + EPS) * n2_ref[...]
    h2b = h2.astype(BF16)
    if dispatch:
        lo = pltpu.bitcast(h2b[:, :HALF].astype(F32), jnp.uint32)
        hi = pltpu.bitcast(h2b[:, HALF:].astype(F32), jnp.uint32)
        h2_ref[...] = pltpu.bitcast(lax.shift_right_logical(lo, jnp.uint32(16)) | hi, jnp.int32)
    else:
        h2_ref[...] = h2b
    logits = jnp.dot(h2b, wr_ref[...], preferred_element_type=F32) + br_ref[...]
    lane = lax.broadcasted_iota(jnp.int32, logits.shape, 1).astype(F32)
    big = float(ROUTE_LANES)

    def first_max(vals, ok):
        v = jnp.where(ok, vals, NEG)
        top = jnp.max(v, axis=-1, keepdims=True)
        idx = jnp.min(jnp.where(ok & (v == top), lane, big), axis=-1, keepdims=True)
        return top, idx

    is_g = lane < N_GROUPS
    gmax, gsel = first_max(logits, is_g)
    gp = 1.0 / jnp.sum(jnp.where(is_g, jnp.exp(logits - gmax), 0.0), axis=-1, keepdims=True)
    lo = N_GROUPS + PER_GROUP * gsel
    is_e = (lane >= lo) & (lane < lo + PER_GROUP)
    t1, i1 = first_max(logits, is_e)
    t2, i2 = first_max(logits, is_e & (lane != i1))
    e2 = jnp.exp(t2 - t1)
    w1 = gp / (1.0 + e2)
    w2 = gp * e2 / (1.0 + e2)
    comb_ref[...] = jnp.where(lane == i1, w1, 0.0) + jnp.where(lane == i2, w2, 0.0)

    if dispatch:
        route_ref, cnt_ref, wrow1_ref, wrow2_ref, carry_ref = rest

        @pl.when(pl.program_id(0) == 0)
        def _():
            carry_ref[...] = jnp.zeros(carry_ref.shape, F32)

        tm = x1.shape[0]
        onehot = jnp.where((lane == i1) | (lane == i2), 1.0, 0.0)
        r = lax.broadcasted_iota(jnp.int32, (tm, tm), 0)
        c = lax.broadcasted_iota(jnp.int32, (tm, tm), 1)
        earlier = jnp.where(r > c, 1.0, 0.0).astype(BF16)
        before = jnp.dot(earlier, onehot.astype(BF16), preferred_element_type=F32) + carry_ref[...]
        rank1 = jnp.sum(jnp.where(lane == i1, before, 0.0), axis=-1, keepdims=True)
        rank2 = jnp.sum(jnp.where(lane == i2, before, 0.0), axis=-1, keepdims=True)
        carry_ref[...] += jnp.sum(onehot, axis=0, keepdims=True)
        cnt_ref[...] = carry_ref[...]
        fields = (i1 - N_GROUPS, i2 - N_GROUPS, rank1, rank2)
        route = jnp.zeros(logits.shape, F32)
        for k, val in enumerate(fields):
            route = jnp.where(lane == k, val, route)
        route_ref[...] = route.T[0:8, :]
        wrow1_ref[...] = jnp.broadcast_to(w1, wrow1_ref.shape)
        wrow2_ref[...] = jnp.broadcast_to(w2, wrow2_ref.shape)


def _res(x, mixed, lw, sw, layer, tm, dispatch):
    n = x.shape[0]
    h2_cols, h2_dtype = (HALF, jnp.int32) if dispatch else (D, BF16)
    out_specs = [pl.BlockSpec((tm, D), lambda i: (i, 0)), pl.BlockSpec((tm, h2_cols), lambda i: (i, 0)),
                 pl.BlockSpec((tm, ROUTE_LANES), lambda i: (i, 0))]
    out_shape = [jax.ShapeDtypeStruct((n, D), F32), jax.ShapeDtypeStruct((n, h2_cols), h2_dtype),
                 jax.ShapeDtypeStruct((n, ROUTE_LANES), F32)]
    scratch = []
    if dispatch:
        wide = pl.BlockSpec((tm, 128), lambda i: (i, 0))
        out_specs += [pl.BlockSpec((8, tm), lambda i: (0, i)), _full((1, ROUTE_LANES)), wide, wide]
        out_shape += [jax.ShapeDtypeStruct((8, n), F32), jax.ShapeDtypeStruct((1, ROUTE_LANES), F32),
                      jax.ShapeDtypeStruct((n, 128), F32), jax.ShapeDtypeStruct((n, 128), F32)]
        scratch = [pltpu.VMEM((1, ROUTE_LANES), F32)]
    return pl.pallas_call(
        functools.partial(_res_kernel, dispatch=dispatch),
        grid=(n // tm,),
        in_specs=[pl.BlockSpec((tm, D), lambda i: (i, 0)), pl.BlockSpec((tm, D), lambda i: (i, 0)),
                  _of_layer((D, D), layer), _full((1, D)), _full((D, ROUTE_LANES)), _full((1, ROUTE_LANES))],
        out_specs=out_specs,
        out_shape=out_shape,
        scratch_shapes=scratch,
        compiler_params=_cparams(("arbitrary",)),
        name="res_router",
    )(x, mixed, sw['w_o'], lw['norm2'], lw['w_router'], lw['b_router'])


def _moe_kernel(h2_ref, comb_ref, x1_ref, w1_ref, w3_ref, w2_ref, fn_ref, o_ref, *, final):
    e = pl.program_id(1)

    @pl.when(e == 0)
    def _():
        o_ref[...] = x1_ref[...]

    h = h2_ref[...]
    a = jnp.dot(h, w1_ref[0, 0].astype(BF16), preferred_element_type=F32)
    b = jnp.dot(h, w3_ref[0, 0].astype(BF16), preferred_element_type=F32)
    comb = comb_ref[...]
    lane = lax.broadcasted_iota(jnp.int32, comb.shape, 1)
    w = jnp.sum(jnp.where(lane == e + N_GROUPS, comb, 0.0), axis=-1, keepdims=True)
    act = (a * jax.nn.sigmoid(a)) * b * w
    o_ref[...] += jnp.dot(act.astype(BF16), w2_ref[0, 0].astype(BF16), preferred_element_type=F32)

    if final:
        @pl.when(e == N_EXPERTS - 1)
        def _():
            x = o_ref[...]
            ms = jnp.mean(x * x, axis=-1, keepdims=True)
            o_ref[...] = x * lax.rsqrt(ms + EPS) * fn_ref[...]


def _moe(h2, comb, x1, sw, layer, final_norm, tm, final):
    n = x1.shape[0]
    return pl.pallas_call(
        functools.partial(_moe_kernel, final=final),
        grid=(n // tm, N_EXPERTS),
        in_specs=[pl.BlockSpec((tm, D), lambda i, e: (i, 0)),
                  pl.BlockSpec((tm, ROUTE_LANES), lambda i, e: (i, 0)),
                  pl.BlockSpec((tm, D), lambda i, e: (i, 0)),
                  pl.BlockSpec((1, 1, D, D_EXPERT), lambda i, e: (layer, e, 0, 0)),
                  pl.BlockSpec((1, 1, D, D_EXPERT), lambda i, e: (layer, e, 0, 0)),
                  pl.BlockSpec((1, 1, D_EXPERT, D), lambda i, e: (layer, e, 0, 0)),
                  pl.BlockSpec((1, D), lambda i, e: (0, 0))],
        out_specs=pl.BlockSpec((tm, D), lambda i, e: (i, 0)),
        out_shape=jax.ShapeDtypeStruct((n, D), F32),
        compiler_params=_cparams(("parallel", "arbitrary")),
        name="moe",
    )(h2, comb, x1, sw['w1'], sw['w3'], sw['w2'], final_norm)


FFN_TM = 512
SC_CORES = 2
SC_SUBCORES = 16
SC_WORKERS = SC_CORES * SC_SUBCORES
SC_CHUNK = 32
HALF = D // 2


def _sc_mesh():
    return plsc.VectorSubcoreMesh(core_axis_name="c", subcore_axis_name="s", num_cores=SC_CORES,
                                  num_subcores=SC_SUBCORES)


def _sc_dispatch(x, wrow1, wrow2, dest1, dest2, n_sorted):
    n = x.shape[0]
    per_w = n // SC_WORKERS

    @functools.partial(
        pl.kernel, mesh=_sc_mesh(),
        out_type=(jax.ShapeDtypeStruct((n_sorted, HALF), jnp.int32), jax.ShapeDtypeStruct((n_sorted, 128), F32)),
        scratch_types=[pltpu.VMEM((SC_CHUNK,), jnp.int32), pltpu.VMEM((SC_CHUNK,), jnp.int32),
                       pltpu.VMEM((SC_CHUNK, HALF), jnp.int32), pltpu.VMEM((SC_CHUNK, 128), F32)],
    )
    def k(x_hbm, w1_hbm, w2_hbm, d1_hbm, d2_hbm, out_hbm, wout_hbm, i1_v, i2_v, rows_v, wrows_v):
        base = (lax.axis_index("s") * SC_CORES + lax.axis_index("c")) * per_w

        @pl.loop(0, per_w // SC_CHUNK)
        def _(j):
            off = base + j * SC_CHUNK
            pltpu.sync_copy(d1_hbm.at[pl.ds(off, SC_CHUNK)], i1_v)
            pltpu.sync_copy(d2_hbm.at[pl.ds(off, SC_CHUNK)], i2_v)
            pltpu.sync_copy(x_hbm.at[pl.ds(off, SC_CHUNK)], rows_v)
            pltpu.sync_copy(rows_v, out_hbm.at[i1_v])
            pltpu.sync_copy(rows_v, out_hbm.at[i2_v])
            pltpu.sync_copy(w1_hbm.at[pl.ds(off, SC_CHUNK)], wrows_v)
            pltpu.sync_copy(wrows_v, wout_hbm.at[i1_v])
            pltpu.sync_copy(w2_hbm.at[pl.ds(off, SC_CHUNK)], wrows_v)
            pltpu.sync_copy(wrows_v, wout_hbm.at[i2_v])

    return k(x, wrow1, wrow2, dest1, dest2)


def _sc_collect(y, dest1, dest2, n):
    per_w = n // SC_WORKERS

    @functools.partial(
        pl.kernel, mesh=_sc_mesh(),
        out_type=(jax.ShapeDtypeStruct((n, D), F32), jax.ShapeDtypeStruct((n, D), F32)),
        scratch_types=[pltpu.VMEM((SC_CHUNK,), jnp.int32), pltpu.VMEM((SC_CHUNK, D), F32)],
    )
    def k(y_hbm, d1_hbm, d2_hbm, g1_hbm, g2_hbm, idx_v, rows_v):
        base = (lax.axis_index("s") * SC_CORES + lax.axis_index("c")) * per_w

        @pl.loop(0, per_w // SC_CHUNK)
        def _(j):
            off = base + j * SC_CHUNK
            for d_hbm, g_hbm in ((d1_hbm, g1_hbm), (d2_hbm, g2_hbm)):
                pltpu.sync_copy(d_hbm.at[pl.ds(off, SC_CHUNK)], idx_v)
                pltpu.sync_copy(y_hbm.at[idx_v], rows_v)
                pltpu.sync_copy(rows_v, g_hbm.at[pl.ds(off, SC_CHUNK)])

    return k(y, dest1, dest2)


def _ffn_kernel(te_ref, nt_ref, xs_ref, ws_ref, w1_ref, w3_ref, w2_ref, y_ref, w1b_ref, w3b_ref, w2b_ref):
    k = pl.program_id(0)

    @pl.when(k < nt_ref[0])
    def _():
        @pl.when((k == 0) | (te_ref[k] != te_ref[jnp.maximum(k - 1, 0)]))
        def _():
            w1b_ref[...] = w1_ref[0, 0].astype(BF16)
            w3b_ref[...] = w3_ref[0, 0].astype(BF16)
            w2b_ref[...] = w2_ref[0, 0].astype(BF16)

        words = pltpu.bitcast(xs_ref[...], jnp.uint32)
        lo = pltpu.bitcast(lax.shift_left(words, jnp.uint32(16)), F32)
        hi = pltpu.bitcast(words & jnp.uint32(0xFFFF0000), F32)
        xs = jnp.concatenate([lo, hi], axis=1).astype(BF16)
        a = jnp.dot(xs, w1b_ref[...], preferred_element_type=F32)
        b = jnp.dot(xs, w3b_ref[...], preferred_element_type=F32)
        act = (a * jax.nn.sigmoid(a)) * b * ws_ref[:, 0:1]
        y_ref[...] = jnp.dot(act.astype(BF16), w2b_ref[...], preferred_element_type=F32)


def _ffn(xs, ws, tile_expert, n_tiles_used, sw, layer):
    n_sorted = xs.shape[0]
    used = lambda k, nt: jnp.minimum(k, nt[0] - 1)
    rows_spec = lambda w: pl.BlockSpec((FFN_TM, w), lambda k, te, nt: (used(k, nt), 0))
    w_in_spec = pl.BlockSpec((1, 1, D, D_EXPERT), lambda k, te, nt: (layer, te[used(k, nt)], 0, 0))
    return pl.pallas_call(
        _ffn_kernel,
        grid_spec=pltpu.PrefetchScalarGridSpec(
            num_scalar_prefetch=2,
            grid=(n_sorted // FFN_TM,),
            in_specs=[rows_spec(HALF), rows_spec(128), w_in_spec, w_in_spec,
                      pl.BlockSpec((1, 1, D_EXPERT, D), lambda k, te, nt: (layer, te[used(k, nt)], 0, 0))],
            out_specs=rows_spec(D),
            scratch_shapes=[pltpu.VMEM((D, D_EXPERT), BF16), pltpu.VMEM((D, D_EXPERT), BF16),
                            pltpu.VMEM((D_EXPERT, D), BF16)],
        ),
        out_shape=jax.ShapeDtypeStruct((n_sorted, D), F32),
        compiler_params=_cparams(("arbitrary",)),
        name="ffn",
    )(tile_expert, n_tiles_used, xs, ws, sw['w1'], sw['w3'], sw['w2'])


def _combine_kernel(x1_ref, g1_ref, g2_ref, fn_ref, o_ref, *, final):
    x = x1_ref[...] + (g1_ref[...] + g2_ref[...])
    if final:
        ms = jnp.mean(x * x, axis=-1, keepdims=True)
        x = x * lax.rsqrt(ms + EPS) * fn_ref[...]
    o_ref[...] = x


def _combine(x1, g1, g2, final_norm, tm, final):
    n = x1.shape[0]
    row = pl.BlockSpec((tm, D), lambda i: (i, 0))
    return pl.pallas_call(
        functools.partial(_combine_kernel, final=final),
        grid=(n // tm,),
        in_specs=[row, row, row, _full((1, D))],
        out_specs=row,
        out_shape=jax.ShapeDtypeStruct((n, D), F32),
        compiler_params=_cparams(("parallel",)),
        name="combine",
    )(x1, g1, g2, final_norm)


def _dest_kernel(seg_ref, route_ref, o_ref):
    route = route_ref[...]
    for s in range(2):
        expert, rank = route[s:s + 1, :], route[2 + s:3 + s, :]
        start = jnp.zeros(expert.shape, jnp.int32)
        for e in range(N_EXPERTS):
            start = jnp.where(expert == float(e), seg_ref[e], start)
        o_ref[s:s + 1, :] = start + rank.astype(jnp.int32)


def _dest(route_t, seg_start, tn):
    n = route_t.shape[1]
    return pl.pallas_call(
        _dest_kernel,
        grid_spec=pltpu.PrefetchScalarGridSpec(
            num_scalar_prefetch=1,
            grid=(n // tn,),
            in_specs=[pl.BlockSpec((8, tn), lambda i, seg: (0, i))],
            out_specs=pl.BlockSpec((2, tn), lambda i, seg: (0, i)),
        ),
        out_shape=jax.ShapeDtypeStruct((2, n), jnp.int32),
        compiler_params=_cparams(("parallel",)),
        name="dest",
    )(seg_start, route_t)


def _after(values, others):
    values, _ = lax.optimization_barrier((values, others))
    return values


def _moe_sparse(h2, route_t, counts, wrow1, wrow2, x1, sw, layer, final_norm, final, during_dispatch,
                during_collect):
    n = x1.shape[0]
    n_sorted = 2 * n + N_EXPERTS * FFN_TM
    cnt = counts[0, N_GROUPS:N_GROUPS + N_EXPERTS].astype(jnp.int32)
    tiles = (cnt + FFN_TM - 1) // FFN_TM
    tile_end = jnp.cumsum(tiles)
    seg_start = (tile_end - tiles) * FFN_TM
    dest = _dest(route_t, seg_start.astype(jnp.int32), 2048)
    dest1, dest2 = dest[0], dest[1]
    tile_ids = jnp.arange(n_sorted // FFN_TM, dtype=jnp.int32)
    tile_expert = jnp.minimum(jnp.sum(tile_end[None, :] <= tile_ids[:, None], axis=1), N_EXPERTS - 1)
    xs, ws = _after(_sc_dispatch(h2, wrow1, wrow2, dest1, dest2, n_sorted), during_dispatch())
    y = _ffn(xs, ws, tile_expert.astype(jnp.int32), tile_end[-1:].astype(jnp.int32), sw, layer)
    g1, g2 = _after(_sc_collect(y, dest1, dest2, n), during_collect())
    return _combine(x1, g1, g2, final_norm, 512, final)


def _t5_buckets(dist):
    max_exact = T5_BUCKETS // 2
    large = max_exact + (np.log(np.maximum(dist, 1) / max_exact) / np.log(T5_MAX_DIST / max_exact)
                         * (T5_BUCKETS - max_exact)).astype(np.int32)
    large = np.minimum(large, T5_BUCKETS - 1)
    return np.where(dist < max_exact, dist, large).astype(np.int32)


def _bias_tables(t5, g, dil, window, t_sample):
    nk = window // dil + 1
    hs = slice(g * HPG, (g + 1) * HPG)
    bias = t5[_t5_buckets(np.arange(nk) * dil)][:, hs].T
    rev = bias[:, ::-1]
    neg = lambda *shape: jnp.full(shape, NEG, F32)

    vec = jnp.concatenate([rev, neg(HPG, CL)], axis=1)
    both = jnp.tile(vec, (1, CL + 1))[:, :CL * 2 * CL].reshape(HPG, CL, 2 * CL)

    rows = []
    for r in range(t_sample):
        shift = r // dil
        per_u = jnp.concatenate([neg(HPG, shift), rev[:, :nk - 1 - shift]], axis=1)
        on_phase = (np.arange(dil) == r % dil)[None, None, :]
        rows.append(jnp.where(on_phase, per_u[:, :, None], NEG).reshape(HPG, window))
    cache_t = jnp.stack(rows, axis=1)
    r = np.arange(t_sample)[:, None]
    c = np.arange(t_sample)[None, :]
    ok = ((r - c) % dil == 0) & (r >= c)
    new_t = jnp.where(jnp.asarray(ok)[None], bias[:, np.clip((r - c) // dil, 0, nk - 1)], NEG)
    return both, cache_t, new_t


def _layer_weights(l, norm1, conv_ssd_w, conv_ssd_b, ssd_dt_bias, ssd_a_log, ssd_d, ssd_norm_w,
                   conv_lru_w, conv_lru_b, lru_br, lru_bi, lru_lambda, b_gate, norm2,
                   w_router_group, b_router_group, w_router_expert, b_router_expert):
    b_all = jnp.concatenate([b_gate[l], jnp.zeros((PW - 3 * D,), F32)])[None]

    def pad128(v):
        return jnp.concatenate([v, jnp.zeros((128 - v.shape[0],), F32)])[None]

    return {
        'norm1': norm1[l][None], 'b_all': b_all,
        'conv_ssd_w': conv_ssd_w[l], 'conv_ssd_b': conv_ssd_b[l][None],
        'dt_bias': pad128(ssd_dt_bias[l]), 'a_log': pad128(ssd_a_log[l]), 'd_skip': pad128(ssd_d[l]),
        'ssd_norm_w': ssd_norm_w[l][None],
        'conv_lru_w': conv_lru_w[l], 'conv_lru_b': conv_lru_b[l][None],
        'lru_br': lru_br[l][None], 'lru_bi': lru_bi[l][None], 'lru_lambda': lru_lambda[l][None],
        'norm2': norm2[l][None],
        'w_router': jnp.concatenate([w_router_group[l], w_router_expert[l],
                                     jnp.zeros((D, ROUTE_LANES - N_GROUPS - N_EXPERTS), F32)],
                                    axis=1).astype(BF16),
        'b_router': pad128(jnp.concatenate([b_router_group[l], b_router_expert[l]])),
    }


def _front_pad(buf):
    return jnp.pad(buf, ((0, 0), (8 - (CONV_W - 1), 0), (0, 0)))


def _cols(P, nb, t, start, width):
    return P.reshape(nb, t, PW)[:, :, start:start + width]


def _kv_rows(P, nb, t, g, n_rows):
    k = _cols(P, nb, t, C_K + g * GW, GW)[:, t - n_rows:].reshape(nb, n_rows, HPG, HD)
    v = _cols(P, nb, t, C_V + g * GW, GW)[:, t - n_rows:].reshape(nb, n_rows, HPG, HD)
    return jnp.stack([k, v], axis=2)


def _layer_front(x, lw, sw, tables, layer, nb, t, rows, tm, tm_mix, tm_res, conv_ssd, st_ssd, conv_lru, st_lru,
                 caches):
    if caches is None:
        P, *qkv = _proj(x, lw['norm1'], sw['w_all'], lw['b_all'], layer, tm, nb, t)
    else:
        (P,) = _proj(x, lw['norm1'], sw['w_all'], lw['b_all'], layer, tm)
    y_ssd, h_ssd, y_lru, h_lru = _mixers(P, _front_pad(conv_ssd), st_ssd.reshape(nb, SSD_INNER, SSD_STATE),
                                         _front_pad(conv_lru), st_lru.reshape(nb, 1, LRU_W), lw, sw, layer, nb, t, rows)
    attn = []
    for g, (window, dil) in enumerate(ATTN_GROUPS):
        both_t, cache_t, new_t = tables[g]
        if caches is None:
            attn.append(_attn_prompt(qkv[g], qkv[3 + g], qkv[6 + g], both_t, dil, nb, t))
        else:
            attn.append(_attn_sample(P, caches[g], cache_t, new_t, g, layer, nb, t, window))
    dils = tuple(dil if caches is None else 1 for _, dil in ATTN_GROUPS)
    mixed = _mix(P, y_ssd, y_lru, attn, dils, sw, layer, tm_mix)
    routed = _res(x, mixed, lw, sw, layer, tm_res, caches is None)
    states = (_cols(P, nb, t, C_XBC, SSD_CONV_DIM)[:, t - 3:],
              h_ssd.reshape(nb, SSD_HEADS, SSD_HEAD_DIM, SSD_STATE),
              _cols(P, nb, t, C_XR, LRU_W)[:, t - 3:],
              h_lru.reshape(nb, LRU_W)) + tuple(
                  _kv_rows(P, nb, t, g, min(w, t)) for g, (w, _) in enumerate(ATTN_GROUPS))
    return routed, states


def kernel(x_prompt, x_sample, cache_conv_ssd, state_ssd, cache_conv_lru, state_lru, cache_kv_w128, cache_kv_w512, cache_kv_w2048, norm1, w_in, conv_ssd_w, conv_ssd_b, ssd_dt_bias, ssd_a_log, ssd_d, ssd_norm_w, conv_lru_w, conv_lru_b, lru_wr, lru_br, lru_wi, lru_bi, lru_lambda, t5_bias, w_br_ssd, w_br_lru, w_br_attn, w_gate, b_gate, w_o, norm2, w_router_group, b_router_group, w_router_expert, b_router_expert, w1, w3, w2, final_norm):
    bp, tp, _ = x_prompt.shape
    bs, ts, _ = x_sample.shape
    xp = x_prompt.reshape(bp * tp, D)
    xs = x_sample.reshape(bs * ts, D)
    fn = final_norm[None]
    tables = [_bias_tables(t5_bias, g, dil, window, ts) for g, (window, dil) in enumerate(ATTN_GROUPS)]
    caches = [c.reshape(-1, HD) for c in (cache_kv_w128, cache_kv_w512, cache_kv_w2048)]
    sw = {name: w.astype(BF16) for name, w in dict(
        lru_wr=lru_wr, lru_wi=lru_wi, w_br_ssd=w_br_ssd, w_br_lru=w_br_lru, w_br_attn=w_br_attn, w_o=w_o).items()}
    sw.update(w1=w1, w3=w3, w2=w2)
    sw['w_all'] = _prep_w(w_gate, w_in)
    outs_p, outs_s = [], []
    for l in range(DEPTH):
        lw = _layer_weights(l, norm1, conv_ssd_w, conv_ssd_b, ssd_dt_bias, ssd_a_log, ssd_d, ssd_norm_w,
                            conv_lru_w, conv_lru_b, lru_br, lru_bi, lru_lambda, b_gate, norm2,
                            w_router_group, b_router_group, w_router_expert, b_router_expert)
        final = l == DEPTH - 1
        (x1, h2, _, route_t, counts, wrow1, wrow2), sp = _layer_front(
            xp, lw, sw, tables, l, bp, tp, CL, 1024, 256, 512,
            jnp.zeros((bp, CONV_W - 1, SSD_CONV_DIM), F32), jnp.zeros((bp, SSD_HEADS, SSD_HEAD_DIM, SSD_STATE), F32),
            jnp.zeros((bp, CONV_W - 1, LRU_W), F32), jnp.zeros((bp, LRU_W), F32), None)
        sample = {}

        def sample_front(l=l, lw=lw, xs=xs):
            sample['front'] = _layer_front(xs, lw, sw, tables, l, bs, ts, ts, bs * ts, bs * ts, bs * ts,
                                           cache_conv_ssd[l], state_ssd[l], cache_conv_lru[l], state_lru[l], caches)
            return sample['front']

        def sample_moe(l=l, final=final):
            (x1_s, h2_s, comb_s), _ = sample['front']
            sample['x'] = _moe(h2_s, comb_s, x1_s, sw, l, fn, bs * ts, final)
            return sample['x']

        xp = _moe_sparse(h2, route_t, counts, wrow1, wrow2, x1, sw, l, fn, final, sample_front, sample_moe)
        xs = sample['x']
        outs_p.append(sp)
        outs_s.append(sample['front'][1])

    def stk(outs, i):
        return jnp.stack([o[i] for o in outs], axis=0)

    return ((xp.reshape(bp, tp, D), xs.reshape(bs, ts, D))
            + tuple(stk(outs_p, i) for i in range(7)) + tuple(stk(outs_s, i) for i in range(7)))
```

```python
import functools

import numpy as np
import jax
import jax.numpy as jnp
from jax import lax
from jax.experimental import pallas as pl
from jax.experimental.pallas import tpu as pltpu
from jax.experimental.pallas import tpu_sc as plsc

F32 = jnp.float32
BF16 = jnp.bfloat16
EPS = 1e-6
NEG = -1e30

D = 2048
DEPTH = 2
PAST_LEN = 16384
CL = 128
CONV_W = 4
SSD_HEADS = 16
SSD_HEAD_DIM = 64
SSD_INNER = 1024
SSD_STATE = 128
SSD_CONV_DIM = 1536
LRU_W = 1024
LRU_BLOCKS = 8
LRU_C = 8.0
ATTN_GROUPS = ((128, 1), (512, 4), (2048, 16))
HPG = 4
HD = 128
GW = HPG * HD
T5_BUCKETS = 32
T5_MAX_DIST = 2048
N_GROUPS = 4
PER_GROUP = 4
N_EXPERTS = 16
D_EXPERT = 512

TILE = 512
C_GATE = 0
C_Z = 6144
C_XR = 7168
C_GR = 8192
C_XBC = 9216
C_Q = 10752
C_K = 12288
C_V = 13824
C_DT = 15360
PW = 15872
N_GATE_TILES = (3 * D) // TILE
ROUTE_LANES = 128
VMEM_LIMIT = 56 * 1024 * 1024


def _cparams(sem):
    return pltpu.CompilerParams(dimension_semantics=sem, vmem_limit_bytes=VMEM_LIMIT)


def _full(shape):
    nd = len(shape)
    return pl.BlockSpec(shape, lambda *_: (0,) * nd)


def _of_layer(shape, layer):
    nd = len(shape)
    return pl.BlockSpec((1,) + shape, lambda *_: (layer,) + (0,) * nd)


_W_IN_PARTS = (('z', SSD_INNER), ('xbc', SSD_CONV_DIM), ('dt', SSD_HEADS), ('xr', LRU_W), ('gr', LRU_W),
               ('q', 3 * GW), ('k', 3 * GW), ('v', 3 * GW))
_W_IN_OFFSET = {name: sum(w for _, w in _W_IN_PARTS[:i]) for i, (name, _) in enumerate(_W_IN_PARTS)}
_W_IN_STARTS = tuple(_W_IN_OFFSET[name] + off for name in ('z', 'xr', 'gr', 'xbc', 'q', 'k', 'v')
                     for off in range(0, dict(_W_IN_PARTS)[name], TILE)) + (_W_IN_OFFSET['dt'],)
DT_SHIFT = SSD_HEADS


def _prep_kernel(blk_ref, shift_ref, width_ref, wg_ref, wa_ref, wb_ref, o_ref):
    j = pl.program_id(1)

    @pl.when(j < N_GATE_TILES)
    def _():
        o_ref[0] = wg_ref[0].astype(BF16)

    @pl.when(j >= N_GATE_TILES)
    def _():
        a = wa_ref[0]
        shifted = jnp.concatenate([a[DT_SHIFT:, :], wb_ref[0]], axis=0)
        val = jnp.where(shift_ref[j] == 0, a, shifted)
        row = lax.broadcasted_iota(jnp.int32, val.shape, 0)
        o_ref[0] = jnp.where(row < width_ref[j], val, 0.0).T.astype(BF16)


def _prep_w(w_gate, w_in):
    starts = (0,) * N_GATE_TILES + _W_IN_STARTS
    blk = jnp.asarray([s // TILE for s in starts], jnp.int32)
    shift = jnp.asarray([s % TILE for s in starts], jnp.int32)
    assert all(s % TILE in (0, DT_SHIFT) for s in starts) and w_in.shape[2] % DT_SHIFT == 0
    width = jnp.asarray([TILE] * (len(starts) - 1) + [SSD_HEADS], jnp.int32)
    per_tile = TILE // DT_SHIFT
    w_in_t = jnp.swapaxes(w_in, 1, 2)
    return pl.pallas_call(
        _prep_kernel,
        grid_spec=pltpu.PrefetchScalarGridSpec(
            num_scalar_prefetch=3,
            grid=(DEPTH, PW // TILE),
            in_specs=[pl.BlockSpec((1, D, TILE), lambda l, j, b, s, w: (l, 0, jnp.minimum(j, N_GATE_TILES - 1))),
                      pl.BlockSpec((1, TILE, D), lambda l, j, b, s, w: (l, b[j], 0)),
                      pl.BlockSpec((1, DT_SHIFT, D), lambda l, j, b, s, w: (l, (b[j] + 1) * per_tile, 0))],
            out_specs=pl.BlockSpec((1, D, TILE), lambda l, j, b, s, w: (l, 0, j)),
        ),
        out_shape=jax.ShapeDtypeStruct((DEPTH, D, PW), BF16),
        compiler_params=_cparams(("parallel", "arbitrary")),
        name="prep_w",
    )(blk, shift, width, w_gate, w_in_t, w_in_t)


def _proj_kernel(x_ref, nw_ref, w_ref, b_ref, o_ref, *rest, tm, phase_major):
    if phase_major:
        qkv_refs, (h_ref, acc_ref, ph_ref) = rest[:9], rest[9:]
    else:
        h_ref, acc_ref = rest
    j = pl.program_id(1)

    @pl.when(j == 0)
    def _():
        x = x_ref[...]
        ms = jnp.mean(x * x, axis=-1, keepdims=True)
        h_ref[...] = (x * lax.rsqrt(ms + EPS) * nw_ref[...]).astype(BF16)
        acc_ref[...] = jnp.zeros(acc_ref.shape, F32)

    prev = acc_ref[...]
    o_ref[...] = jnp.where(j <= N_GATE_TILES, jax.nn.sigmoid(prev), prev)
    acc_ref[...] = jnp.dot(h_ref[...], w_ref[0], preferred_element_type=F32) + b_ref[...]

    if phase_major:
        for part in range(3):
            for g, (_, dil) in enumerate(ATTN_GROUPS):
                ref = qkv_refs[part * 3 + g]

                @pl.when(j - 1 == C_Q // TILE + part * 3 + g)
                def _(ref=ref, dil=dil):
                    if dil == 1:
                        ref[0, 0] = o_ref[...].astype(BF16)
                    else:
                        for c in range(TILE // 128):
                            ph_ref[c] = o_ref[:, c * 128:(c + 1) * 128]
                        for p in range(dil):
                            for c in range(TILE // 128):
                                ref[0, p, :, c * 128:(c + 1) * 128] = (
                                    ph_ref[c, pl.ds(p, tm // dil, stride=dil), :].astype(BF16))


def _proj(x, nw, w_all, b_all, layer, tm, nb=None, t=None):
    n = x.shape[0]
    phase_major = nb is not None
    nt = PW // TILE
    out_specs = [pl.BlockSpec((tm, TILE), lambda i, j: (i, jnp.maximum(j - 1, 0)))]
    out_shape = [jax.ShapeDtypeStruct((n, PW), F32)]
    scratch = [pltpu.VMEM((tm, D), BF16), pltpu.VMEM((tm, TILE), F32)]
    if phase_major:
        tpb = t // tm
        for _ in range(3):
            for _, dil in ATTN_GROUPS:
                out_specs.append(pl.BlockSpec((1, dil, tm // dil, GW), lambda i, j: (i // tpb, 0, i % tpb, 0)))
                out_shape.append(jax.ShapeDtypeStruct((nb, dil, t // dil, GW), BF16))
        scratch.append(pltpu.VMEM((TILE // 128, tm, 128), F32))
    return pl.pallas_call(
        functools.partial(_proj_kernel, tm=tm, phase_major=phase_major),
        grid=(n // tm, nt + 1),
        in_specs=[pl.BlockSpec((tm, D), lambda i, j: (i, 0)),
                  pl.BlockSpec((1, D), lambda i, j: (0, 0)),
                  pl.BlockSpec((1, D, TILE), lambda i, j: (layer, 0, jnp.minimum(j, nt - 1))),
                  pl.BlockSpec((1, TILE), lambda i, j: (0, jnp.minimum(j, nt - 1)))],
        out_specs=out_specs,
        out_shape=out_shape,
        scratch_shapes=scratch,
        compiler_params=_cparams(("parallel", "arbitrary")),
        name="proj",
    )(x, nw, w_all, b_all)


def _conv_step(x_ref, xp_ref, cw_ref, cb_ref, rows, out_rows):
    xp_ref[8:8 + rows, :] = _bf16_round(x_ref[...])
    cw = _bf16_round(cw_ref[...])
    acc = cw[0:1, :] * xp_ref[5:5 + out_rows, :]
    for j in range(1, CONV_W):
        acc = acc + cw[j:j + 1, :] * xp_ref[5 + j:5 + j + out_rows, :]
    tail = xp_ref[rows:rows + 8, :]
    xp_ref[0:8, :] = tail
    return acc + cb_ref[...]


def _bf16_round(x):
    return x.astype(BF16).astype(F32)


def _softplus(x):
    u = jnp.exp(-jnp.abs(x))
    w = 1.0 + u
    log1p_u = jnp.where(w == 1.0, u, jnp.log(w) * (u / jnp.where(w == 1.0, 1.0, w - 1.0)))
    return jnp.maximum(x, 0.0) + log1p_u


def _ssd_part(part, z_ref, xbc_ref, dt_ref, tail_ref, h0_ref, cw_ref, cb_ref, dtb_ref, alog_ref, dsk_ref, nw_ref,
              y_ref, hf_ref, xp_ref, act_ref, st_ref, ysc_ref, *, rows):
    if part == 'init':
        xp_ref[0:8, :] = _bf16_round(tail_ref[0])
        st_ref[...] = h0_ref[0]
        return
    if part == 'final':
        hf_ref[0] = st_ref[...]
        return

    if rows < CL:
        xp_ref[8 + rows:, :] = jnp.zeros((CL - rows, SSD_CONV_DIM), F32)
    conv = _conv_step(xbc_ref, xp_ref, cw_ref, cb_ref, rows, CL)
    act_ref[...] = conv * jax.nn.sigmoid(conv)

    row = lax.broadcasted_iota(jnp.int32, (CL, 128), 0)
    lane = lax.broadcasted_iota(jnp.int32, (CL, 128), 1)
    raw = dt_ref[...]
    if rows < CL:
        raw = jnp.concatenate([raw, jnp.zeros((CL - rows, 128), F32)], axis=0)
    dt = _softplus(raw + dtb_ref[...])
    dt = jnp.where((lane < SSD_HEADS) & (row < rows), dt, 0.0)
    da = dt * (-jnp.exp(alog_ref[...]))
    acs = da
    d = 1
    while d < CL:
        acs = acs + jnp.where(row >= d, pltpu.roll(acs, d, 0), 0.0)
        d *= 2
    acs_t = acs.T
    last = acs[CL - 1:CL, :]
    e_acs = jnp.exp(acs)
    to_end = jnp.exp(last - acs)
    cdec = jnp.exp(last)
    causal = row >= lane
    lo_lane = lane < SSD_HEAD_DIM
    lo_row = row < SSD_HEAD_DIM
    dsk = dsk_ref[...]

    def pair_cols(arr, h):
        return jnp.where(lo_lane, arr[:, h:h + 1], arr[:, h + 1:h + 2])

    nt = (((1,), (1,)), ((), ()))
    for g in range(2):
        bm = act_ref[:, SSD_INNER + g * SSD_STATE:SSD_INNER + (g + 1) * SSD_STATE].astype(BF16)
        cm = act_ref[:, SSD_INNER + 256 + g * SSD_STATE:SSD_INNER + 256 + (g + 1) * SSD_STATE].astype(BF16)
        cb = lax.dot_general(cm, bm, nt, preferred_element_type=F32)
        for pp in range(4):
            h = g * 8 + 2 * pp
            sl = slice(h * SSD_HEAD_DIM, h * SSD_HEAD_DIM + 128)
            xs = act_ref[:, sl]
            xdt = xs * pair_cols(dt, h)
            xdt_b = xdt.astype(BF16)
            ys = []
            for hh in (h, h + 1):
                seg = acs[:, hh:hh + 1] - acs_t[hh:hh + 1, :]
                decay = jnp.exp(jnp.where(causal, seg, -jnp.inf))
                ys.append(jnp.dot((cb * decay).astype(BF16), xdt_b, preferred_element_type=F32))
            y_diag = jnp.where(lo_lane, ys[0], ys[1])
            st = st_ref[sl, :]
            y_off = lax.dot_general(cm, st.astype(BF16), nt, preferred_element_type=F32) * pair_cols(e_acs, h)
            d_pair = jnp.where(lo_lane, dsk[:, h:h + 1], dsk[:, h + 1:h + 2])
            ysc_ref[:, sl] = y_diag + y_off + d_pair * xs
            xdte_t = (xdt * pair_cols(to_end, h)).T.astype(BF16)
            s_new = jnp.dot(xdte_t, bm, preferred_element_type=F32)
            dec = jnp.where(lo_row, cdec[:, h:h + 1], cdec[:, h + 1:h + 2])
            st_ref[sl, :] = dec * st + s_new

    zz = z_ref[...]
    yg = ysc_ref[0:rows, :] * (zz * jax.nn.sigmoid(zz))
    gw = SSD_INNER // 2
    for g in range(2):
        half = yg[:, g * gw:(g + 1) * gw]
        ms = jnp.mean(half * half, axis=-1, keepdims=True)
        y_ref[:, g * gw:(g + 1) * gw] = (half * lax.rsqrt(ms + EPS)
                                         * nw_ref[:, g * gw:(g + 1) * gw]).astype(y_ref.dtype)


def _lru_part(part, xr_ref, gr_ref, tail_ref, h0_ref, cw_ref, cb_ref, wr_ref, br_ref, wi_ref, bi_ref, lam_ref,
              y_ref, hl_ref, xp_ref, h_ref, *, rows):
    if part == 'init':
        xp_ref[0:8, :] = _bf16_round(tail_ref[0])
        h_ref[...] = h0_ref[0]
        return
    if part == 'final':
        hl_ref[0] = h_ref[...]
        return

    x = _conv_step(xr_ref, xp_ref, cw_ref, cb_ref, rows, rows)
    xb = x.astype(BF16)
    rs, is_ = [], []
    for n in range(LRU_BLOCKS):
        blk = xb[:, n * 128:(n + 1) * 128]
        rs.append(jnp.dot(blk, wr_ref[0, n], preferred_element_type=F32))
        is_.append(jnp.dot(blk, wi_ref[0, n], preferred_element_type=F32))
    r_gate = jax.nn.sigmoid(jnp.concatenate(rs, axis=1) + br_ref[...])
    i_gate = jax.nn.sigmoid(jnp.concatenate(is_, axis=1) + bi_ref[...])
    log_a = -LRU_C * r_gate * _softplus(-lam_ref[...])
    a = jnp.exp(log_a)
    th = jnp.tanh(log_a)
    b = jnp.sqrt(-2.0 * th / (1.0 - th)) * (i_gate * x)
    in_group = lax.broadcasted_iota(jnp.int32, (rows, LRU_W), 0) % 8
    for d in (1, 2, 4):
        a_s = jnp.where(in_group >= d, pltpu.roll(a, d, 0), 1.0)
        b_s = jnp.where(in_group >= d, pltpu.roll(b, d, 0), 0.0)
        b = a * b_s + b
        a = a * a_s
    carry = h_ref[...]
    groups = []
    for g in range(rows // 8):
        h_g = b[8 * g:8 * g + 8, :] + a[8 * g:8 * g + 8, :] * carry
        groups.append(h_g)
        carry = h_g[7:8, :]
    h = jnp.concatenate(groups, axis=0) if len(groups) > 1 else groups[0]
    h_ref[...] = carry
    y_ref[...] = (h * jax.nn.gelu(gr_ref[...])).astype(y_ref.dtype)


N_SSD_IN, N_LRU_IN = 11, 11


def _mixers_kernel(*refs, rows, n_chunks):
    ins, rest = refs[:N_SSD_IN + N_LRU_IN], refs[N_SSD_IN + N_LRU_IN:]
    ssd = ins[:N_SSD_IN] + rest[0:2] + rest[4:8]
    lru = ins[N_SSD_IN:] + rest[2:4] + rest[8:10]
    c = pl.program_id(1)

    @pl.when(c == 0)
    def _():
        _ssd_part('init', *ssd, rows=rows)
        _lru_part('init', *lru, rows=rows)

    _ssd_part('main', *ssd, rows=rows)
    _lru_part('main', *lru, rows=rows)

    @pl.when(c == n_chunks - 1)
    def _():
        _ssd_part('final', *ssd, rows=rows)
        _lru_part('final', *lru, rows=rows)


def _mixers(P, ssd_tail, ssd_h0, lru_tail, lru_h0, lw, sw, layer, nb, t, rows):
    nc = t // rows
    y_dtype = BF16 if rows % 16 == 0 else F32
    tok = lambda w, col: pl.BlockSpec((rows, w), lambda b, c: (b * nc + c, col // w))
    seq = lambda *shape: pl.BlockSpec((1,) + shape, lambda b, c: (b,) + (0,) * len(shape))
    return pl.pallas_call(
        functools.partial(_mixers_kernel, rows=rows, n_chunks=nc),
        grid=(nb, nc),
        in_specs=[tok(SSD_INNER, C_Z), tok(SSD_CONV_DIM, C_XBC), tok(128, C_DT),
                  seq(8, SSD_CONV_DIM), seq(SSD_INNER, SSD_STATE),
                  _full((CONV_W, SSD_CONV_DIM)), _full((1, SSD_CONV_DIM)),
                  _full((1, 128)), _full((1, 128)), _full((1, 128)), _full((1, SSD_INNER)),
                  tok(LRU_W, C_XR), tok(LRU_W, C_GR), seq(8, LRU_W), seq(1, LRU_W),
                  _full((CONV_W, LRU_W)), _full((1, LRU_W)),
                  _of_layer((LRU_BLOCKS, 128, 128), layer), _full((1, LRU_W)),
                  _of_layer((LRU_BLOCKS, 128, 128), layer), _full((1, LRU_W)), _full((1, LRU_W))],
        out_specs=[tok(SSD_INNER, 0), seq(SSD_INNER, SSD_STATE), tok(LRU_W, 0), seq(1, LRU_W)],
        out_shape=[jax.ShapeDtypeStruct((nb * t, SSD_INNER), y_dtype),
                   jax.ShapeDtypeStruct((nb, SSD_INNER, SSD_STATE), F32),
                   jax.ShapeDtypeStruct((nb * t, LRU_W), y_dtype),
                   jax.ShapeDtypeStruct((nb, 1, LRU_W), F32)],
        scratch_shapes=[pltpu.VMEM((8 + CL, SSD_CONV_DIM), F32), pltpu.VMEM((CL, SSD_CONV_DIM), F32),
                        pltpu.VMEM((SSD_INNER, SSD_STATE), F32), pltpu.VMEM((CL, SSD_INNER), F32),
                        pltpu.VMEM((8 + rows, LRU_W), F32), pltpu.VMEM((1, LRU_W), F32)],
        compiler_params=_cparams(("parallel", "arbitrary")),
        name="mixers",
    )(P, P, P, ssd_tail, ssd_h0, lw['conv_ssd_w'], lw['conv_ssd_b'], lw['dt_bias'], lw['a_log'], lw['d_skip'],
      lw['ssd_norm_w'], P, P, lru_tail, lru_h0, lw['conv_lru_w'], lw['conv_lru_b'], sw['lru_wr'], lw['lru_br'],
      sw['lru_wi'], lw['lru_bi'], lw['lru_lambda'])


def _attn_kernel(q_ref, kp_ref, vp_ref, kc_ref, vc_ref, bias_ref, o_ref, lse_ref, *, sub, phases):
    scale = HD ** -0.5
    nt = (((1,), (1,)), ((), ()))
    lane = lax.broadcasted_iota(jnp.int32, (CL, 128), 1)
    key = lax.broadcasted_iota(jnp.int32, (CL, 2 * CL), 1)
    first_ok = (pl.program_id(2) > 0) | (key >= CL)
    for z in range(phases):
        for s in range(sub):
            rows = slice(s * CL, (s + 1) * CL)
            both = slice((s - 1) * CL, (s + 1) * CL)
            lse_all = jnp.zeros((CL, 128), F32)
            for h in range(HPG):
                sl = slice(h * HD, (h + 1) * HD)
                q = q_ref[0, z, rows, sl]
                if s == 0:
                    kk = jnp.concatenate([kp_ref[0, z, :, sl], kc_ref[0, z, rows, sl]], axis=0)
                    vv = jnp.concatenate([vp_ref[0, z, :, sl], vc_ref[0, z, rows, sl]], axis=0)
                else:
                    kk, vv = kc_ref[0, z, both, sl], vc_ref[0, z, both, sl]
                sc = lax.dot_general(q, kk, nt, preferred_element_type=F32) * scale + bias_ref[h]
                if s == 0:
                    sc = jnp.where(first_ok, sc, NEG)
                m = jnp.max(sc, axis=-1, keepdims=True)
                p = jnp.exp(sc - m)
                l = jnp.sum(p, axis=-1, keepdims=True)
                o_ref[rows, z * GW + h * HD:z * GW + (h + 1) * HD] = jnp.dot(
                    (p * (1.0 / l)).astype(BF16), vv, preferred_element_type=F32)
                lse_all = jnp.where(lane == h, m + jnp.log(l), lse_all)
            lse_ref[rows, z * 128:(z + 1) * 128] = lse_all


def _attn_prompt(q, k, v, bias, dil, nb, t):
    n = nb * t
    blocks = 8
    sub = min(blocks, t // dil // CL)
    phases = min(blocks // sub, dil)
    nstep = t // dil // (sub * CL)
    cur = pl.BlockSpec((1, phases, sub * CL, GW), lambda b, p, i: (b, p, i, 0))
    prev = pl.BlockSpec((1, phases, CL, GW), lambda b, p, i: (b, p, jnp.maximum(i * sub - 1, 0), 0))
    return pl.pallas_call(
        functools.partial(_attn_kernel, sub=sub, phases=phases),
        grid=(nb, dil // phases, nstep),
        in_specs=[cur, prev, prev, cur, cur, _full((HPG, CL, 2 * CL))],
        out_specs=[pl.BlockSpec((sub * CL, phases * GW), lambda b, p, i: (b * nstep + i, p)),
                   pl.BlockSpec((sub * CL, phases * 128), lambda b, p, i: (b * nstep + i, p))],
        out_shape=[jax.ShapeDtypeStruct((n // dil, dil * GW), F32),
                   jax.ShapeDtypeStruct((n // dil, dil * 128), F32)],
        compiler_params=_cparams(("parallel", "parallel", "arbitrary")),
        name=f"attn_prompt_d{dil}",
    )(q, k, v, k, v, bias)


def _attn_sample_kernel(q_ref, kv_ref, kb_ref, vb_ref, ba_ref, bb_ref, o_ref, lse_ref):
    scale = HD ** -0.5
    nt = (((1,), (1,)), ((), ()))
    rows = o_ref.shape[0]
    window = kv_ref.shape[0] // (2 * HPG)
    lane = lax.broadcasted_iota(jnp.int32, (rows, 128), 1)
    lse_all = jnp.zeros((rows, 128), F32)
    for h in range(HPG):
        sl = slice(h * HD, (h + 1) * HD)
        q = q_ref[:, sl].astype(BF16)
        k_cache = kv_ref[pl.ds(h, window, stride=2 * HPG), :].astype(BF16)
        v_cache = kv_ref[pl.ds(HPG + h, window, stride=2 * HPG), :].astype(BF16)
        sa = lax.dot_general(q, k_cache, nt, preferred_element_type=F32) * scale + ba_ref[h]
        sb = lax.dot_general(q, kb_ref[:, sl].astype(BF16), nt, preferred_element_type=F32) * scale + bb_ref[h]
        m = jnp.maximum(jnp.max(sa, axis=-1, keepdims=True), jnp.max(sb, axis=-1, keepdims=True))
        pa = jnp.exp(sa - m)
        pb = jnp.exp(sb - m)
        l = jnp.sum(pa, axis=-1, keepdims=True) + jnp.sum(pb, axis=-1, keepdims=True)
        inv = 1.0 / l
        o_ref[:, sl] = (jnp.dot((pa * inv).astype(BF16), v_cache, preferred_element_type=F32)
                        + jnp.dot((pb * inv).astype(BF16), vb_ref[:, sl].astype(BF16), preferred_element_type=F32))
        lse_all = jnp.where(lane == h, m + jnp.log(l), lse_all)
    lse_ref[...] = lse_all


def _attn_sample(Ps, cache_rows, bias_a, bias_b, g, layer, nb, t, window):
    tq, tk, tv = C_Q // GW + g, C_K // GW + g, C_V // GW + g
    out_spec = pl.BlockSpec((t, GW), lambda b: (b, 0))
    return pl.pallas_call(
        _attn_sample_kernel,
        grid=(nb,),
        in_specs=[pl.BlockSpec((t, GW), lambda b: (b, tq)),
                  pl.BlockSpec((window * 2 * HPG, HD), lambda b: (layer * nb + b, 0)),
                  pl.BlockSpec((t, GW), lambda b: (b, tk)),
                  pl.BlockSpec((t, GW), lambda b: (b, tv)),
                  _full((HPG, t, window)), _full((HPG, t, t))],
        out_specs=[out_spec, pl.BlockSpec((t, 128), lambda b: (b, 0))],
        out_shape=[jax.ShapeDtypeStruct((nb * t, GW), F32), jax.ShapeDtypeStruct((nb * t, 128), F32)],
        compiler_params=_cparams(("parallel",)),
        name=f"attn_sample_w{window}",
    )(Ps, cache_rows, Ps, Ps, bias_a, bias_b)


def _mix_kernel(gs_ref, gl_ref, ga_ref, ys_ref, yl_ref, o0_ref, o1_ref, o2_ref, l0_ref, l1_ref, l2_ref,
                wbs_ref, wbl_ref, wba_ref, out_ref, *scratch, dils):
    tm = out_ref.shape[0]
    o_heads, lses = [], []
    scratch = list(scratch)
    for o_ref, l_ref, dil in zip((o0_ref, o1_ref, o2_ref), (l0_ref, l1_ref, l2_ref), dils):
        if dil == 1:
            o_heads.append([o_ref[:, h * HD:(h + 1) * HD] for h in range(HPG)])
            lses.append(l_ref[...])
            continue
        o_scr, l_scr = scratch.pop(0), scratch.pop(0)
        for p in range(dil):
            rows = pl.ds(p, tm // dil, stride=dil)
            for h in range(HPG):
                o_scr[h, rows, :] = o_ref[:, p * GW + h * HD:p * GW + (h + 1) * HD]
            l_scr[rows, :] = l_ref[:, p * 128:(p + 1) * 128]
        o_heads.append([o_scr[h] for h in range(HPG)])
        lses.append(l_scr[...])
    l0, l1, l2 = lses
    m = jnp.maximum(jnp.maximum(l0, l1), l2)
    e0, e1, e2 = jnp.exp(l0 - m), jnp.exp(l1 - m), jnp.exp(l2 - m)
    den = e0 + e1 + e2
    w0, w1, w2 = e0 / den, e1 / den, e2 / den
    heads = []
    for h in range(HPG):
        per_head = lambda w: jnp.broadcast_to(w[:, h:h + 1], (tm, HD))
        heads.append(o_heads[0][h] * per_head(w0) + o_heads[1][h] * per_head(w1) + o_heads[2][h] * per_head(w2))
    ya = jnp.concatenate(heads, axis=1)
    mixed = (gs_ref[...] * jnp.dot(ys_ref[...].astype(BF16), wbs_ref[0], preferred_element_type=F32)
             + gl_ref[...] * jnp.dot(yl_ref[...].astype(BF16), wbl_ref[0], preferred_element_type=F32)
             + ga_ref[...] * jnp.dot(ya.astype(BF16), wba_ref[0], preferred_element_type=F32))
    out_ref[...] = mixed.astype(BF16)


def _mix(P, y_ssd, y_lru, attn, dils, sw, layer, tm):
    n = P.shape[0]
    row = lambda w: pl.BlockSpec((tm, w), lambda i: (i, 0))
    phased = lambda w, d: pl.BlockSpec((tm // d, d * w), lambda i: (i, 0))
    (o0, s0), (o1, s1), (o2, s2) = attn
    scratch = []
    for d in dils:
        if d > 1:
            scratch += [pltpu.VMEM((HPG, tm, HD), F32), pltpu.VMEM((tm, 128), F32)]
    return pl.pallas_call(
        functools.partial(_mix_kernel, dils=dils),
        grid=(n // tm,),
        in_specs=[pl.BlockSpec((tm, D), lambda i: (i, 0)), pl.BlockSpec((tm, D), lambda i: (i, 1)),
                  pl.BlockSpec((tm, D), lambda i: (i, 2)),
                  row(SSD_INNER), row(LRU_W)] + [phased(GW, d) for d in dils] + [phased(128, d) for d in dils] + [
                  _of_layer((SSD_INNER, D), layer), _of_layer((LRU_W, D), layer), _of_layer((GW, D), layer)],
        out_specs=row(D),
        out_shape=jax.ShapeDtypeStruct((n, D), BF16),
        scratch_shapes=scratch,
        compiler_params=_cparams(("parallel",)),
        name="mix",
    )(P, P, P, y_ssd, y_lru, o0, o1, o2, s0, s1, s2, sw['w_br_ssd'], sw['w_br_lru'], sw['w_br_attn'])


def _res_kernel(x_ref, mixed_ref, wo_ref, n2_ref, wr_ref, br_ref, x1_ref, h2_ref, comb_ref, *rest, dispatch):
    x1 = x_ref[...] + jnp.dot(mixed_ref[...], wo_ref[0], preferred_element_type=F32)
    x1_ref[...] = x1
    ms = jnp.mean(x1 * x1, axis=-1, keepdims=True)
    h2 = x1 * lax.rsqrt(ms + EPS) * n2_ref[...]
    h2b = h2.astype(BF16)
    if dispatch:
        lo = pltpu.bitcast(h2b[:, :HALF].astype(F32), jnp.uint32)
        hi = pltpu.bitcast(h2b[:, HALF:].astype(F32), jnp.uint32)
        h2_ref[...] = pltpu.bitcast(lax.shift_right_logical(lo, jnp.uint32(16)) | hi, jnp.int32)
    else:
        h2_ref[...] = h2b
    logits = jnp.dot(h2b, wr_ref[...], preferred_element_type=F32) + br_ref[...]
    lane = lax.broadcasted_iota(jnp.int32, logits.shape, 1).astype(F32)
    big = float(ROUTE_LANES)

    def first_max(vals, ok):
        v = jnp.where(ok, vals, NEG)
        top = jnp.max(v, axis=-1, keepdims=True)
        idx = jnp.min(jnp.where(ok & (v == top), lane, big), axis=-1, keepdims=True)
        return top, idx

    is_g = lane < N_GROUPS
    gmax, gsel = first_max(logits, is_g)
    gp = 1.0 / jnp.sum(jnp.where(is_g, jnp.exp(logits - gmax), 0.0), axis=-1, keepdims=True)
    lo = N_GROUPS + PER_GROUP * gsel
    is_e = (lane >= lo) & (lane < lo + PER_GROUP)
    t1, i1 = first_max(logits, is_e)
    t2, i2 = first_max(logits, is_e & (lane != i1))
    e2 = jnp.exp(t2 - t1)
    w1 = gp / (1.0 + e2)
    w2 = gp * e2 / (1.0 + e2)
    comb_ref[...] = jnp.where(lane == i1, w1, 0.0) + jnp.where(lane == i2, w2, 0.0)

    if dispatch:
        route_ref, cnt_ref, wrow1_ref, wrow2_ref, carry_ref = rest

        @pl.when(pl.program_id(0) == 0)
        def _():
            carry_ref[...] = jnp.zeros(carry_ref.shape, F32)

        tm = x1.shape[0]
        onehot = jnp.where((lane == i1) | (lane == i2), 1.0, 0.0)
        r = lax.broadcasted_iota(jnp.int32, (tm, tm), 0)
        c = lax.broadcasted_iota(jnp.int32, (tm, tm), 1)
        earlier = jnp.where(r > c, 1.0, 0.0).astype(BF16)
        before = jnp.dot(earlier, onehot.astype(BF16), preferred_element_type=F32) + carry_ref[...]
        rank1 = jnp.sum(jnp.where(lane == i1, before, 0.0), axis=-1, keepdims=True)
        rank2 = jnp.sum(jnp.where(lane == i2, before, 0.0), axis=-1, keepdims=True)
        carry_ref[...] += jnp.sum(onehot, axis=0, keepdims=True)
        cnt_ref[...] = carry_ref[...]
        fields = (i1 - N_GROUPS, i2 - N_GROUPS, rank1, rank2)
        route = jnp.zeros(logits.shape, F32)
        for k, val in enumerate(fields):
            route = jnp.where(lane == k, val, route)
        route_ref[...] = route.T[0:8, :]
        wrow1_ref[...] = jnp.broadcast_to(w1, wrow1_ref.shape)
        wrow2_ref[...] = jnp.broadcast_to(w2, wrow2_ref.shape)


def _res(x, mixed, lw, sw, layer, tm, dispatch):
    n = x.shape[0]
    h2_cols, h2_dtype = (HALF, jnp.int32) if dispatch else (D, BF16)
    out_specs = [pl.BlockSpec((tm, D), lambda i: (i, 0)), pl.BlockSpec((tm, h2_cols), lambda i: (i, 0)),
                 pl.BlockSpec((tm, ROUTE_LANES), lambda i: (i, 0))]
    out_shape = [jax.ShapeDtypeStruct((n, D), F32), jax.ShapeDtypeStruct((n, h2_cols), h2_dtype),
                 jax.ShapeDtypeStruct((n, ROUTE_LANES), F32)]
    scratch = []
    if dispatch:
        wide = pl.BlockSpec((tm, 128), lambda i: (i, 0))
        out_specs += [pl.BlockSpec((8, tm), lambda i: (0, i)), _full((1, ROUTE_LANES)), wide, wide]
        out_shape += [jax.ShapeDtypeStruct((8, n), F32), jax.ShapeDtypeStruct((1, ROUTE_LANES), F32),
                      jax.ShapeDtypeStruct((n, 128), F32), jax.ShapeDtypeStruct((n, 128), F32)]
        scratch = [pltpu.VMEM((1, ROUTE_LANES), F32)]
    return pl.pallas_call(
        functools.partial(_res_kernel, dispatch=dispatch),
        grid=(n // tm,),
        in_specs=[pl.BlockSpec((tm, D), lambda i: (i, 0)), pl.BlockSpec((tm, D), lambda i: (i, 0)),
                  _of_layer((D, D), layer), _full((1, D)), _full((D, ROUTE_LANES)), _full((1, ROUTE_LANES))],
        out_specs=out_specs,
        out_shape=out_shape,
        scratch_shapes=scratch,
        compiler_params=_cparams(("arbitrary",)),
        name="res_router",
    )(x, mixed, sw['w_o'], lw['norm2'], lw['w_router'], lw['b_router'])


def _moe_kernel(h2_ref, comb_ref, x1_ref, w1_ref, w3_ref, w2_ref, fn_ref, o_ref, *, final):
    e = pl.program_id(1)

    @pl.when(e == 0)
    def _():
        o_ref[...] = x1_ref[...]

    h = h2_ref[...]
    a = jnp.dot(h, w1_ref[0, 0].astype(BF16), preferred_element_type=F32)
    b = jnp.dot(h, w3_ref[0, 0].astype(BF16), preferred_element_type=F32)
    comb = comb_ref[...]
    lane = lax.broadcasted_iota(jnp.int32, comb.shape, 1)
    w = jnp.sum(jnp.where(lane == e + N_GROUPS, comb, 0.0), axis=-1, keepdims=True)
    act = (a * jax.nn.sigmoid(a)) * b * w
    o_ref[...] += jnp.dot(act.astype(BF16), w2_ref[0, 0].astype(BF16), preferred_element_type=F32)

    if final:
        @pl.when(e == N_EXPERTS - 1)
        def _():
            x = o_ref[...]
            ms = jnp.mean(x * x, axis=-1, keepdims=True)
            o_ref[...] = x * lax.rsqrt(ms + EPS) * fn_ref[...]


def _moe(h2, comb, x1, sw, layer, final_norm, tm, final):
    n = x1.shape[0]
    return pl.pallas_call(
        functools.partial(_moe_kernel, final=final),
        grid=(n // tm, N_EXPERTS),
        in_specs=[pl.BlockSpec((tm, D), lambda i, e: (i, 0)),
                  pl.BlockSpec((tm, ROUTE_LANES), lambda i, e: (i, 0)),
                  pl.BlockSpec((tm, D), lambda i, e: (i, 0)),
                  pl.BlockSpec((1, 1, D, D_EXPERT), lambda i, e: (layer, e, 0, 0)),
                  pl.BlockSpec((1, 1, D, D_EXPERT), lambda i, e: (layer, e, 0, 0)),
                  pl.BlockSpec((1, 1, D_EXPERT, D), lambda i, e: (layer, e, 0, 0)),
                  pl.BlockSpec((1, D), lambda i, e: (0, 0))],
        out_specs=pl.BlockSpec((tm, D), lambda i, e: (i, 0)),
        out_shape=jax.ShapeDtypeStruct((n, D), F32),
        compiler_params=_cparams(("parallel", "arbitrary")),
        name="moe",
    )(h2, comb, x1, sw['w1'], sw['w3'], sw['w2'], final_norm)


FFN_TM = 512
SC_CORES = 2
SC_SUBCORES = 16
SC_WORKERS = SC_CORES * SC_SUBCORES
SC_CHUNK = 32
HALF = D // 2


def _sc_mesh():
    return plsc.VectorSubcoreMesh(core_axis_name="c", subcore_axis_name="s", num_cores=SC_CORES,
                                  num_subcores=SC_SUBCORES)


def _sc_dispatch(x, wrow1, wrow2, dest1, dest2, n_sorted):
    n = x.shape[0]
    per_w = n // SC_WORKERS

    @functools.partial(
        pl.kernel, mesh=_sc_mesh(),
        out_type=(jax.ShapeDtypeStruct((n_sorted, HALF), jnp.int32), jax.ShapeDtypeStruct((n_sorted, 128), F32)),
        scratch_types=[pltpu.VMEM((SC_CHUNK,), jnp.int32), pltpu.VMEM((SC_CHUNK,), jnp.int32),
                       pltpu.VMEM((SC_CHUNK, HALF), jnp.int32), pltpu.VMEM((SC_CHUNK, 128), F32)],
    )
    def k(x_hbm, w1_hbm, w2_hbm, d1_hbm, d2_hbm, out_hbm, wout_hbm, i1_v, i2_v, rows_v, wrows_v):
        base = (lax.axis_index("s") * SC_CORES + lax.axis_index("c")) * per_w

        @pl.loop(0, per_w // SC_CHUNK)
        def _(j):
            off = base + j * SC_CHUNK
            pltpu.sync_copy(d1_hbm.at[pl.ds(off, SC_CHUNK)], i1_v)
            pltpu.sync_copy(d2_hbm.at[pl.ds(off, SC_CHUNK)], i2_v)
            pltpu.sync_copy(x_hbm.at[pl.ds(off, SC_CHUNK)], rows_v)
            pltpu.sync_copy(rows_v, out_hbm.at[i1_v])
            pltpu.sync_copy(rows_v, out_hbm.at[i2_v])
            pltpu.sync_copy(w1_hbm.at[pl.ds(off, SC_CHUNK)], wrows_v)
            pltpu.sync_copy(wrows_v, wout_hbm.at[i1_v])
            pltpu.sync_copy(w2_hbm.at[pl.ds(off, SC_CHUNK)], wrows_v)
            pltpu.sync_copy(wrows_v, wout_hbm.at[i2_v])

    return k(x, wrow1, wrow2, dest1, dest2)


def _sc_collect(y, dest1, dest2, n):
    per_w = n // SC_WORKERS

    @functools.partial(
        pl.kernel, mesh=_sc_mesh(),
        out_type=(jax.ShapeDtypeStruct((n, D), F32), jax.ShapeDtypeStruct((n, D), F32)),
        scratch_types=[pltpu.VMEM((SC_CHUNK,), jnp.int32), pltpu.VMEM((SC_CHUNK, D), F32)],
    )
    def k(y_hbm, d1_hbm, d2_hbm, g1_hbm, g2_hbm, idx_v, rows_v):
        base = (lax.axis_index("s") * SC_CORES + lax.axis_index("c")) * per_w

        @pl.loop(0, per_w // SC_CHUNK)
        def _(j):
            off = base + j * SC_CHUNK
            for d_hbm, g_hbm in ((d1_hbm, g1_hbm), (d2_hbm, g2_hbm)):
                pltpu.sync_copy(d_hbm.at[pl.ds(off, SC_CHUNK)], idx_v)
                pltpu.sync_copy(y_hbm.at[idx_v], rows_v)
                pltpu.sync_copy(rows_v, g_hbm.at[pl.ds(off, SC_CHUNK)])

    return k(y, dest1, dest2)


def _ffn_kernel(te_ref, nt_ref, xs_ref, ws_ref, w1_ref, w3_ref, w2_ref, y_ref, w1b_ref, w3b_ref, w2b_ref):
    k = pl.program_id(0)

    @pl.when(k < nt_ref[0])
    def _():
        @pl.when((k == 0) | (te_ref[k] != te_ref[jnp.maximum(k - 1, 0)]))
        def _():
            w1b_ref[...] = w1_ref[0, 0].astype(BF16)
            w3b_ref[...] = w3_ref[0, 0].astype(BF16)
            w2b_ref[...] = w2_ref[0, 0].astype(BF16)

        words = pltpu.bitcast(xs_ref[...], jnp.uint32)
        lo = pltpu.bitcast(lax.shift_left(words, jnp.uint32(16)), F32)
        hi = pltpu.bitcast(words & jnp.uint32(0xFFFF0000), F32)
        xs = jnp.concatenate([lo, hi], axis=1).astype(BF16)
        a = jnp.dot(xs, w1b_ref[...], preferred_element_type=F32)
        b = jnp.dot(xs, w3b_ref[...], preferred_element_type=F32)
        act = (a * jax.nn.sigmoid(a)) * b * ws_ref[:, 0:1]
        y_ref[...] = jnp.dot(act.astype(BF16), w2b_ref[...], preferred_element_type=F32)


def _ffn(xs, ws, tile_expert, n_tiles_used, sw, layer):
    n_sorted = xs.shape[0]
    used = lambda k, nt: jnp.minimum(k, nt[0] - 1)
    rows_spec = lambda w: pl.BlockSpec((FFN_TM, w), lambda k, te, nt: (used(k, nt), 0))
    w_in_spec = pl.BlockSpec((1, 1, D, D_EXPERT), lambda k, te, nt: (layer, te[used(k, nt)], 0, 0))
    return pl.pallas_call(
        _ffn_kernel,
        grid_spec=pltpu.PrefetchScalarGridSpec(
            num_scalar_prefetch=2,
            grid=(n_sorted // FFN_TM,),
            in_specs=[rows_spec(HALF), rows_spec(128), w_in_spec, w_in_spec,
                      pl.BlockSpec((1, 1, D_EXPERT, D), lambda k, te, nt: (layer, te[used(k, nt)], 0, 0))],
            out_specs=rows_spec(D),
            scratch_shapes=[pltpu.VMEM((D, D_EXPERT), BF16), pltpu.VMEM((D, D_EXPERT), BF16),
                            pltpu.VMEM((D_EXPERT, D), BF16)],
        ),
        out_shape=jax.ShapeDtypeStruct((n_sorted, D), F32),
        compiler_params=_cparams(("arbitrary",)),
        name="ffn",
    )(tile_expert, n_tiles_used, xs, ws, sw['w1'], sw['w3'], sw['w2'])


def _combine_kernel(x1_ref, g1_ref, g2_ref, fn_ref, o_ref, *, final):
    x = x1_ref[...] + (g1_ref[...] + g2_ref[...])
    if final:
        ms = jnp.mean(x * x, axis=-1, keepdims=True)
        x = x * lax.rsqrt(ms + EPS) * fn_ref[...]
    o_ref[...] = x


def _combine(x1, g1, g2, final_norm, tm, final):
    n = x1.shape[0]
    row = pl.BlockSpec((tm, D), lambda i: (i, 0))
    return pl.pallas_call(
        functools.partial(_combine_kernel, final=final),
        grid=(n // tm,),
        in_specs=[row, row, row, _full((1, D))],
        out_specs=row,
        out_shape=jax.ShapeDtypeStruct((n, D), F32),
        compiler_params=_cparams(("parallel",)),
        name="combine",
    )(x1, g1, g2, final_norm)


def _dest_kernel(seg_ref, route_ref, o_ref):
    route = route_ref[...]
    for s in range(2):
        expert, rank = route[s:s + 1, :], route[2 + s:3 + s, :]
        start = jnp.zeros(expert.shape, jnp.int32)
        for e in range(N_EXPERTS):
            start = jnp.where(expert == float(e), seg_ref[e], start)
        o_ref[s:s + 1, :] = start + rank.astype(jnp.int32)


def _dest(route_t, seg_start, tn):
    n = route_t.shape[1]
    return pl.pallas_call(
        _dest_kernel,
        grid_spec=pltpu.PrefetchScalarGridSpec(
            num_scalar_prefetch=1,
            grid=(n // tn,),
            in_specs=[pl.BlockSpec((8, tn), lambda i, seg: (0, i))],
            out_specs=pl.BlockSpec((2, tn), lambda i, seg: (0, i)),
        ),
        out_shape=jax.ShapeDtypeStruct((2, n), jnp.int32),
        compiler_params=_cparams(("parallel",)),
        name="dest",
    )(seg_start, route_t)


def _after(values, others):
    values, _ = lax.optimization_barrier((values, others))
    return values


def _moe_sparse(h2, route_t, counts, wrow1, wrow2, x1, sw, layer, final_norm, final, during_dispatch,
                during_collect):
    n = x1.shape[0]
    n_sorted = 2 * n + N_EXPERTS * FFN_TM
    cnt = counts[0, N_GROUPS:N_GROUPS + N_EXPERTS].astype(jnp.int32)
    tiles = (cnt + FFN_TM - 1) // FFN_TM
    tile_end = jnp.cumsum(tiles)
    seg_start = (tile_end - tiles) * FFN_TM
    dest = _dest(route_t, seg_start.astype(jnp.int32), 2048)
    dest1, dest2 = dest[0], dest[1]
    tile_ids = jnp.arange(n_sorted // FFN_TM, dtype=jnp.int32)
    tile_expert = jnp.minimum(jnp.sum(tile_end[None, :] <= tile_ids[:, None], axis=1), N_EXPERTS - 1)
    xs, ws = _after(_sc_dispatch(h2, wrow1, wrow2, dest1, dest2, n_sorted), during_dispatch())
    y = _ffn(xs, ws, tile_expert.astype(jnp.int32), tile_end[-1:].astype(jnp.int32), sw, layer)
    g1, g2 = _after(_sc_collect(y, dest1, dest2, n), during_collect())
    return _combine(x1, g1, g2, final_norm, 512, final)


def _t5_buckets(dist):
    max_exact = T5_BUCKETS // 2
    large = max_exact + (np.log(np.maximum(dist, 1) / max_exact) / np.log(T5_MAX_DIST / max_exact)
                         * (T5_BUCKETS - max_exact)).astype(np.int32)
    large = np.minimum(large, T5_BUCKETS - 1)
    return np.where(dist < max_exact, dist, large).astype(np.int32)


def _bias_tables(t5, g, dil, window, t_sample):
    nk = window // dil + 1
    hs = slice(g * HPG, (g + 1) * HPG)
    bias = t5[_t5_buckets(np.arange(nk) * dil)][:, hs].T
    rev = bias[:, ::-1]
    neg = lambda *shape: jnp.full(shape, NEG, F32)

    vec = jnp.concatenate([rev, neg(HPG, CL)], axis=1)
    both = jnp.tile(vec, (1, CL + 1))[:, :CL * 2 * CL].reshape(HPG, CL, 2 * CL)

    rows = []
    for r in range(t_sample):
        shift = r // dil
        per_u = jnp.concatenate([neg(HPG, shift), rev[:, :nk - 1 - shift]], axis=1)
        on_phase = (np.arange(dil) == r % dil)[None, None, :]
        rows.append(jnp.where(on_phase, per_u[:, :, None], NEG).reshape(HPG, window))
    cache_t = jnp.stack(rows, axis=1)
    r = np.arange(t_sample)[:, None]
    c = np.arange(t_sample)[None, :]
    ok = ((r - c) % dil == 0) & (r >= c)
    new_t = jnp.where(jnp.asarray(ok)[None], bias[:, np.clip((r - c) // dil, 0, nk - 1)], NEG)
    return both, cache_t, new_t


def _layer_weights(l, norm1, conv_ssd_w, conv_ssd_b, ssd_dt_bias, ssd_a_log, ssd_d, ssd_norm_w,
                   conv_lru_w, conv_lru_b, lru_br, lru_bi, lru_lambda, b_gate, norm2,
                   w_router_group, b_router_group, w_router_expert, b_router_expert):
    b_all = jnp.concatenate([b_gate[l], jnp.zeros((PW - 3 * D,), F32)])[None]

    def pad128(v):
        return jnp.concatenate([v, jnp.zeros((128 - v.shape[0],), F32)])[None]

    return {
        'norm1': norm1[l][None], 'b_all': b_all,
        'conv_ssd_w': conv_ssd_w[l], 'conv_ssd_b': conv_ssd_b[l][None],
        'dt_bias': pad128(ssd_dt_bias[l]), 'a_log': pad128(ssd_a_log[l]), 'd_skip': pad128(ssd_d[l]),
        'ssd_norm_w': ssd_norm_w[l][None],
        'conv_lru_w': conv_lru_w[l], 'conv_lru_b': conv_lru_b[l][None],
        'lru_br': lru_br[l][None], 'lru_bi': lru_bi[l][None], 'lru_lambda': lru_lambda[l][None],
        'norm2': norm2[l][None],
        'w_router': jnp.concatenate([w_router_group[l], w_router_expert[l],
                                     jnp.zeros((D, ROUTE_LANES - N_GROUPS - N_EXPERTS), F32)],
                                    axis=1).astype(BF16),
        'b_router': pad128(jnp.concatenate([b_router_group[l], b_router_expert[l]])),
    }


def _front_pad(buf):
    return jnp.pad(buf, ((0, 0), (8 - (CONV_W - 1), 0), (0, 0)))


def _cols(P, nb, t, start, width):
    return P.reshape(nb, t, PW)[:, :, start:start + width]


def _kv_rows(P, nb, t, g, n_rows):
    k = _cols(P, nb, t, C_K + g * GW, GW)[:, t - n_rows:].reshape(nb, n_rows, HPG, HD)
    v = _cols(P, nb, t, C_V + g * GW, GW)[:, t - n_rows:].reshape(nb, n_rows, HPG, HD)
    return jnp.stack([k, v], axis=2)


def _layer_front(x, lw, sw, tables, layer, nb, t, rows, tm, tm_mix, tm_res, conv_ssd, st_ssd, conv_lru, st_lru,
                 caches):
    if caches is None:
        P, *qkv = _proj(x, lw['norm1'], sw['w_all'], lw['b_all'], layer, tm, nb, t)
    else:
        (P,) = _proj(x, lw['norm1'], sw['w_all'], lw['b_all'], layer, tm)
    y_ssd, h_ssd, y_lru, h_lru = _mixers(P, _front_pad(conv_ssd), st_ssd.reshape(nb, SSD_INNER, SSD_STATE),
                                         _front_pad(conv_lru), st_lru.reshape(nb, 1, LRU_W), lw, sw, layer, nb, t, rows)
    attn = []
    for g, (window, dil) in enumerate(ATTN_GROUPS):
        both_t, cache_t, new_t = tables[g]
        if caches is None:
            attn.append(_attn_prompt(qkv[g], qkv[3 + g], qkv[6 + g], both_t, dil, nb, t))
        else:
            attn.append(_attn_sample(P, caches[g], cache_t, new_t, g, layer, nb, t, window))
    dils = tuple(dil if caches is None else 1 for _, dil in ATTN_GROUPS)
    mixed = _mix(P, y_ssd, y_lru, attn, dils, sw, layer, tm_mix)
    routed = _res(x, mixed, lw, sw, layer, tm_res, caches is None)
    states = (_cols(P, nb, t, C_XBC, SSD_CONV_DIM)[:, t - 3:],
              h_ssd.reshape(nb, SSD_HEADS, SSD_HEAD_DIM, SSD_STATE),
              _cols(P, nb, t, C_XR, LRU_W)[:, t - 3:],
              h_lru.reshape(nb, LRU_W)) + tuple(
                  _kv_rows(P, nb, t, g, min(w, t)) for g, (w, _) in enumerate(ATTN_GROUPS))
    return routed, states


def kernel(x_prompt, x_sample, cache_conv_ssd, state_ssd, cache_conv_lru, state_lru, cache_kv_w128, cache_kv_w512, cache_kv_w2048, norm1, w_in, conv_ssd_w, conv_ssd_b, ssd_dt_bias, ssd_a_log, ssd_d, ssd_norm_w, conv_lru_w, conv_lru_b, lru_wr, lru_br, lru_wi, lru_bi, lru_lambda, t5_bias, w_br_ssd, w_br_lru, w_br_attn, w_gate, b_gate, w_o, norm2, w_router_group, b_router_group, w_router_expert, b_router_expert, w1, w3, w2, final_norm):
    bp, tp, _ = x_prompt.shape
    bs, ts, _ = x_sample.shape
    xp = x_prompt.reshape(bp * tp, D)
    xs = x_sample.reshape(bs * ts, D)
    fn = final_norm[None]
    tables = [_bias_tables(t5_bias, g, dil, window, ts) for g, (window, dil) in enumerate(ATTN_GROUPS)]
    caches = [c.reshape(-1, HD) for c in (cache_kv_w128, cache_kv_w512, cache_kv_w2048)]
    sw = {name: w.astype(BF16) for name, w in dict(
        lru_wr=lru_wr, lru_wi=lru_wi, w_br_ssd=w_br_ssd, w_br_lru=w_br_lru, w_br_attn=w_br_attn, w_o=w_o).items()}
    sw.update(w1=w1, w3=w3, w2=w2)
    sw['w_all'] = _prep_w(w_gate, w_in)
    outs_p, outs_s = [], []
    for l in range(DEPTH):
        lw = _layer_weights(l, norm1, conv_ssd_w, conv_ssd_b, ssd_dt_bias, ssd_a_log, ssd_d, ssd_norm_w,
                            conv_lru_w, conv_lru_b, lru_br, lru_bi, lru_lambda, b_gate, norm2,
                            w_router_group, b_router_group, w_router_expert, b_router_expert)
        final = l == DEPTH - 1
        (x1, h2, _, route_t, counts, wrow1, wrow2), sp = _layer_front(
            xp, lw, sw, tables, l, bp, tp, CL, 1024, 256, 512,
            jnp.zeros((bp, CONV_W - 1, SSD_CONV_DIM), F32), jnp.zeros((bp, SSD_HEADS, SSD_HEAD_DIM, SSD_STATE), F32),
            jnp.zeros((bp, CONV_W - 1, LRU_W), F32), jnp.zeros((bp, LRU_W), F32), None)
        sample = {}

        def sample_front(l=l, lw=lw, xs=xs):
            sample['front'] = _layer_front(xs, lw, sw, tables, l, bs, ts, ts, bs * ts, bs * ts, bs * ts,
                                           cache_conv_ssd[l], state_ssd[l], cache_conv_lru[l], state_lru[l], caches)
            return sample['front']

        def sample_moe(l=l, final=final):
            (x1_s, h2_s, comb_s), _ = sample['front']
            sample['x'] = _moe(h2_s, comb_s, x1_s, sw, l, fn, bs * ts, final)
            return sample['x']

        xp = _moe_sparse(h2, route_t, counts, wrow1, wrow2, x1, sw, l, fn, final, sample_front, sample_moe)
        xs = sample['x']
        outs_p.append(sp)
        outs_s.append(sample['front'][1])

    def stk(outs, i):
        return jnp.stack([o[i] for o in outs], axis=0)

    return ((xp.reshape(bp, tp, D), xs.reshape(bs, ts, D))
            + tuple(stk(outs_p, i) for i in range(7)) + tuple(stk(outs_s, i) for i in range(7)))
```
